```python
import math
import jax, jax.numpy as jnp
from jax import lax
import numpy as np

D_MODEL = 1024
BATCH = 8
SEQ = 2048
DEPTH = 4
DEC_BATCH = 128
DEC_SEQ = 4
PAST_LEN = 16384
PAGE_SIZE = 128

EPS = 1e-6
R_WIDTH = D_MODEL
R_HEAD = 64
R_HEADS = R_WIDTH // R_HEAD
R_DECAY_LORA = 64
R_AAA_LORA = 64
R_GATE_LORA = 128
R_COLS = 3 * R_WIDTH + R_DECAY_LORA + R_AAA_LORA + R_GATE_LORA
R_LN_EPS = 64e-5
S5_WIDTH = D_MODEL
S5_GROUP = 16
S5_GROUPS = S5_WIDTH // S5_GROUP
S5_STATE = 64
M_WIDTH = 2 * D_MODEL
M_HEAD = 64
M_HEADS = M_WIDTH // M_HEAD
M_GROUPS = 4
M_HPG = M_HEADS // M_GROUPS
M_STATE = 128
M_CONV = 4
M_CONV_DIM = M_WIDTH + 2 * M_GROUPS * M_STATE
SSD_CHUNK = 128
MEM_LEN = 256
XA_HEADS = 4
XA_HEAD_DIM = D_MODEL // XA_HEADS
D_FF = 4 * D_MODEL
N_BRANCH = 3
IN_SIZES = (N_BRANCH * D_MODEL, R_COLS, S5_WIDTH, M_WIDTH, M_CONV_DIM, M_HEADS)
IN_COLS = sum(IN_SIZES)

kernel_name = 'hybrid_rwkv7_s5_ssd_memxattn_step'


def split_cols(x, sizes):
    idx = np.cumsum(sizes)[:-1].tolist()
    return jnp.split(x, idx, axis=-1)


def rmsnorm(x, g):
    x32 = x.astype(jnp.float32)
    return x32 * lax.rsqrt(jnp.mean(x32 * x32, axis=-1, keepdims=True) + EPS) * g.astype(jnp.float32)


def rwkv7_recurrence(r, w, k, v, a, b, s0):
    def step(s, inp):
        r_t, w_t, k_t, v_t, a_t, b_t = inp
        sa = jnp.einsum('bhij,bhj->bhi', s, a_t)
        s = s * w_t[:, :, None, :] + sa[..., None] * b_t[:, :, None, :] + v_t[..., None] * k_t[:, :, None, :]
        return s, jnp.einsum('bhij,bhj->bhi', s, r_t)
    xs = [jnp.moveaxis(z.astype(jnp.float32), 1, 0) for z in (r, w, k, v, a, b)]
    s_fin, y = lax.scan(step, s0.astype(jnp.float32), xs)
    return jnp.moveaxis(y, 0, 1), s_fin


def rwkv7_branch(p, shift_prev, s0, mu, w0, w2, a0, a2, g2, k_k, k_a, r_k, ln_w, ln_b, w_out):
    bsz, t, _ = p.shape
    prev = jnp.concatenate([shift_prev[:, None].astype(p.dtype), p[:, :-1]], axis=1)
    pm = p + mu * (prev - p)
    r, k, v, xw, xa, xg = split_cols(pm, (R_WIDTH, R_WIDTH, R_WIDTH, R_DECAY_LORA, R_AAA_LORA, R_GATE_LORA))
    w_log = -jax.nn.softplus(-(w0 + jnp.tanh(xw) @ w2)) - 0.5
    decay = jnp.exp(-jnp.exp(w_log.astype(jnp.float32)))
    a = jax.nn.sigmoid(a0 + xa @ a2)
    g = jax.nn.sigmoid(xg) @ g2
    heads = lambda z: z.reshape(bsz, t, R_HEADS, R_HEAD)
    kk = heads((k * k_k).astype(jnp.float32))
    kk = kk * lax.rsqrt(jnp.sum(kk * kk, axis=-1, keepdims=True) + 1e-12)
    k = k * (1.0 + (a - 1.0) * k_a)
    rh, kh, vh, ah = heads(r), heads(k), heads(v), heads(a)
    y, s_fin = rwkv7_recurrence(rh, heads(decay), kh, vh, -kk, kk * ah, s0)
    mean = jnp.mean(y, axis=-1, keepdims=True)
    var = jnp.mean(jnp.square(y - mean), axis=-1, keepdims=True)
    y = ((y - mean) * lax.rsqrt(var + R_LN_EPS)).reshape(bsz, t, R_WIDTH) * ln_w + ln_b
    bonus = jnp.sum(rh * kh * r_k, axis=-1, keepdims=True) * vh
    y = y + bonus.reshape(bsz, t, R_WIDTH)
    return (y * g) @ w_out, p[:, -1], s_fin


def s5_branch(u, h_re0, h_im0, a_re, a_im, log_dt, b_re, b_im, c_re, c_im, d, w_glu, w_out):
    bsz, t, _ = u.shape
    u32 = u.astype(jnp.float32)
    dt = jnp.exp(log_dt.astype(jnp.float32))[:, None]
    a_re = a_re.astype(jnp.float32)
    a_im = a_im.astype(jnp.float32)
    mag = jnp.exp(dt * a_re)
    abar_re = mag * jnp.cos(dt * a_im)
    abar_im = mag * jnp.sin(dt * a_im)
    den = a_re * a_re + a_im * a_im
    nr = abar_re - 1.0
    q_re = (nr * a_re + abar_im * a_im) / den
    q_im = (abar_im * a_re - nr * a_im) / den
    bb_re = q_re[..., None] * b_re - q_im[..., None] * b_im
    bb_im = q_re[..., None] * b_im + q_im[..., None] * b_re
    ug = u32.reshape(bsz, t, S5_GROUPS, S5_GROUP)
    in_re = jnp.einsum('btgh,gph->btgp', ug, bb_re)
    in_im = jnp.einsum('btgh,gph->btgp', ug, bb_im)
    in_re = in_re.at[:, 0].add(abar_re * h_re0 - abar_im * h_im0)
    in_im = in_im.at[:, 0].add(abar_re * h_im0 + abar_im * h_re0)
    ar_t = jnp.broadcast_to(abar_re, (1, t, S5_GROUPS, S5_STATE))
    ai_t = jnp.broadcast_to(abar_im, (1, t, S5_GROUPS, S5_STATE))

    def combine(e1, e2):
        a1r, a1i, b1r, b1i = e1
        a2r, a2i, b2r, b2i = e2
        return (a2r * a1r - a2i * a1i, a2r * a1i + a2i * a1r,
                a2r * b1r - a2i * b1i + b2r, a2r * b1i + a2i * b1r + b2i)

    _, _, h_re, h_im = lax.associative_scan(combine, (ar_t, ai_t, in_re, in_im), axis=1)
    y = jnp.einsum('btgp,ghp->btgh', h_re, c_re) - jnp.einsum('btgp,ghp->btgh', h_im, c_im)
    y = y.reshape(bsz, t, S5_WIDTH) + d * u32
    y = jax.nn.gelu(y)
    y = y * jax.nn.sigmoid(y @ w_glu)
    return y @ w_out, h_re[:, -1], h_im[:, -1]


def ssd_chunked(x, dt, a, bm, cm, h0):
    bsz, t = x.shape[:2]
    chunk = SSD_CHUNK if t % SSD_CHUNK == 0 else t
    nc = t // chunk
    xd = (x * dt[..., None]).reshape(bsz, nc, chunk, M_GROUPS, M_HPG, M_HEAD)
    adt = (dt * a).reshape(bsz, nc, chunk, M_GROUPS, M_HPG)
    bm = bm.reshape(bsz, nc, chunk, M_GROUPS, M_STATE)
    cm = cm.reshape(bsz, nc, chunk, M_GROUPS, M_STATE)
    a_cum = jnp.cumsum(adt, axis=2)
    causal = jnp.tril(jnp.ones((chunk, chunk), dtype=bool))[None, None, :, :, None, None]
    seg = a_cum[:, :, :, None] - a_cum[:, :, None, :]
    decay_ls = jnp.exp(jnp.where(causal, seg, -jnp.inf))
    y_diag = jnp.einsum('bclgn,bcsgn,bclsge,bcsgep->bclgep', cm, bm, decay_ls, xd)
    decay_to_end = jnp.exp(a_cum[:, :, -1:] - a_cum)
    chunk_states = jnp.einsum('bclgn,bclge,bclgep->bcgepn', bm, decay_to_end, xd)
    chunk_decay = jnp.exp(a_cum[:, :, -1])

    def carry_step(h, inp):
        dec_c, st_c = inp
        return h * dec_c[..., None, None] + st_c, h

    h_fin, h_start = lax.scan(carry_step, h0.astype(jnp.float32),
                              (jnp.moveaxis(chunk_decay, 1, 0), jnp.moveaxis(chunk_states, 1, 0)))
    h_start = jnp.moveaxis(h_start, 0, 1)
    y_off = jnp.einsum('bclgn,bcgepn,bclge->bclgep', cm, h_start, jnp.exp(a_cum))
    return (y_diag + y_off).reshape(bsz, t, M_GROUPS, M_HPG, M_HEAD), h_fin


def mamba2_branch(z, xbc, dt_raw, conv_prev, h0, conv_w, conv_b, dt_bias, a_log, d_skip, norm_g, w_out):
    bsz, t, _ = xbc.shape
    padded = jnp.concatenate([conv_prev.astype(jnp.float32), xbc.astype(jnp.float32)], axis=1)
    conv = conv_b + padded[:, 0:t] * conv_w[0]
    for j in range(1, M_CONV):
        conv = conv + padded[:, j:j + t] * conv_w[j]
    conv = jax.nn.silu(conv)
    xs, bm, cm = split_cols(conv, (M_WIDTH, M_GROUPS * M_STATE, M_GROUPS * M_STATE))
    dt = jax.nn.softplus(dt_raw.astype(jnp.float32) + dt_bias)
    a = -jnp.exp(a_log.astype(jnp.float32))
    xh = xs.reshape(bsz, t, M_GROUPS, M_HPG, M_HEAD)
    y, h_fin = ssd_chunked(xh, dt.reshape(bsz, t, M_GROUPS, M_HPG), a.reshape(M_GROUPS, M_HPG),
                           bm.reshape(bsz, t, M_GROUPS, M_STATE), cm.reshape(bsz, t, M_GROUPS, M_STATE),
                           h0.reshape(bsz, M_GROUPS, M_HPG, M_HEAD, M_STATE))
    y = y + d_skip.reshape(M_GROUPS, M_HPG)[..., None] * xh
    y = y.reshape(bsz, t, M_WIDTH) * jax.nn.silu(z)
    yg = y.reshape(bsz, t, M_GROUPS, M_WIDTH // M_GROUPS)
    yg = yg * lax.rsqrt(jnp.mean(yg * yg, axis=-1, keepdims=True) + EPS)
    y = yg.reshape(bsz, t, M_WIDTH) * norm_g
    return y @ w_out, padded[:, t:], h_fin.reshape(bsz, M_HEADS, M_HEAD, M_STATE)


def cross_attend(xn, mem_k, mem_v, w_q, w_o):
    bsz, t, _ = xn.shape
    q = (xn @ w_q).reshape(bsz, t, XA_HEADS, XA_HEAD_DIM)
    s = jnp.einsum('bthd,bmhd->bhtm', q, mem_k).astype(jnp.float32) * (XA_HEAD_DIM ** -0.5)
    p = jax.nn.softmax(s, axis=-1)
    o = jnp.einsum('bhtm,bmhd->bthd', p, mem_v).reshape(bsz, t, D_MODEL)
    return o @ w_o


def setup_inputs(seed: int = 0) -> dict:
    key = jax.random.key(seed)
    keys = jax.random.split(key, 64)
    counter = [0]
    f32 = jnp.float32

    def nk():
        counter[0] += 1
        return keys[counter[0] - 1]

    def nrm(shape, scale=1.0):
        return scale * jax.random.normal(nk(), shape, f32)

    def unif(shape, lo, hi):
        return jax.random.uniform(nk(), shape, f32, lo, hi)

    def gain(shape, s=0.05):
        return 1.0 + nrm(shape, s)

    L, D = DEPTH, D_MODEL
    inp = {}
    inp['x_prompt'] = nrm((BATCH, SEQ, D))
    inp['x_sample'] = nrm((DEC_BATCH, DEC_SEQ, D))
    inp['cache_mem_k'] = nrm((L, DEC_BATCH, MEM_LEN, XA_HEADS, XA_HEAD_DIM))
    inp['cache_mem_v'] = nrm((L, DEC_BATCH, MEM_LEN, XA_HEADS, XA_HEAD_DIM))
    inp['state_rwkv'] = nrm((L, DEC_BATCH, R_HEADS, R_HEAD, R_HEAD), 0.5)
    inp['state_rwkv_shift'] = nrm((L, DEC_BATCH, R_COLS))
    inp['state_s5_re'] = nrm((L, DEC_BATCH, S5_GROUPS, S5_STATE), 0.5)
    inp['state_s5_im'] = nrm((L, DEC_BATCH, S5_GROUPS, S5_STATE), 0.5)
    inp['state_conv'] = nrm((L, DEC_BATCH, M_CONV - 1, M_CONV_DIM))
    inp['state_ssm'] = nrm((L, DEC_BATCH, M_HEADS, M_HEAD, M_STATE), 0.5)
    inp['mem_prompt'] = nrm((BATCH, MEM_LEN, D))
    inp['norm_mix_pre'] = gain((L, D))
    inp['norm_mix_post'] = gain((L, D))
    inp['norm_xa_pre'] = gain((L, D))
    inp['norm_xa_post'] = gain((L, D))
    inp['norm_mlp_pre'] = gain((L, D))
    inp['norm_mlp_post'] = gain((L, D))
    inp['norm_mem'] = gain((L, D))
    inp['w_in'] = nrm((L, D, IN_COLS), D ** -0.5)
    inp['w_out'] = nrm((L, D, D), D ** -0.5)
    inp['rwkv_mu'] = unif((L, R_COLS), 0.0, 1.0)
    inp['rwkv_w0'] = unif((L, R_WIDTH), -6.5, -1.5)
    inp['rwkv_w2'] = nrm((L, R_DECAY_LORA, R_WIDTH), 0.5 * R_DECAY_LORA ** -0.5)
    inp['rwkv_a0'] = nrm((L, R_WIDTH), 0.1)
    inp['rwkv_a2'] = nrm((L, R_AAA_LORA, R_WIDTH), R_AAA_LORA ** -0.5)
    inp['rwkv_g2'] = nrm((L, R_GATE_LORA, R_WIDTH), R_GATE_LORA ** -0.5)
    inp['rwkv_k_k'] = 0.85 + nrm((L, R_WIDTH), 0.05)
    inp['rwkv_k_a'] = gain((L, R_WIDTH))
    inp['rwkv_r_k'] = nrm((L, R_HEADS, R_HEAD), 0.1)
    inp['rwkv_ln_w'] = gain((L, R_WIDTH))
    inp['rwkv_ln_b'] = nrm((L, R_WIDTH), 0.01)
    inp['w_out_rwkv'] = nrm((L, R_WIDTH, D), R_WIDTH ** -0.5)
    inp['s5_a_re'] = -0.5 * jnp.exp(nrm((L, S5_GROUPS, S5_STATE), 0.1))
    inp['s5_a_im'] = math.pi * jnp.arange(S5_STATE, dtype=f32) + nrm((L, S5_GROUPS, S5_STATE), 0.05)
    inp['s5_log_dt'] = unif((L, S5_GROUPS), math.log(1e-3), math.log(1e-1))
    inp['s5_b_re'] = nrm((L, S5_GROUPS, S5_STATE, S5_GROUP), (2 * S5_GROUP) ** -0.5)
    inp['s5_b_im'] = nrm((L, S5_GROUPS, S5_STATE, S5_GROUP), (2 * S5_GROUP) ** -0.5)
    inp['s5_c_re'] = nrm((L, S5_GROUPS, S5_GROUP, S5_STATE), (2 * S5_STATE) ** -0.5)
    inp['s5_c_im'] = nrm((L, S5_GROUPS, S5_GROUP, S5_STATE), (2 * S5_STATE) ** -0.5)
    inp['s5_d'] = nrm((L, S5_WIDTH))
    inp['s5_w_glu'] = nrm((L, S5_WIDTH, S5_WIDTH), S5_WIDTH ** -0.5)
    inp['w_out_s5'] = nrm((L, S5_WIDTH, D), S5_WIDTH ** -0.5)
    inp['m_conv_w'] = nrm((L, M_CONV, M_CONV_DIM), M_CONV ** -0.5)
    inp['m_conv_b'] = nrm((L, M_CONV_DIM), 0.01)
    dt0 = jnp.exp(unif((L, M_HEADS), math.log(1e-3), math.log(1e-1)))
    inp['m_dt_bias'] = dt0 + jnp.log(-jnp.expm1(-dt0))
    inp['m_a_log'] = jnp.log(unif((L, M_HEADS), 1.0, 16.0))
    inp['m_d'] = gain((L, M_HEADS), 0.1)
    inp['m_norm'] = gain((L, M_WIDTH))
    inp['w_out_mamba'] = nrm((L, M_WIDTH, D), M_WIDTH ** -0.5)
    inp['xa_wq'] = nrm((L, D, D), D ** -0.5)
    inp['xa_wk'] = nrm((L, D, D), D ** -0.5)
    inp['xa_wv'] = nrm((L, D, D), D ** -0.5)
    inp['xa_wo'] = nrm((L, D, D), D ** -0.5)
    inp['mlp_w1'] = nrm((L, D, D_FF), D ** -0.5)
    inp['mlp_w2'] = nrm((L, D_FF, D), D_FF ** -0.5)
    return inp


def reference(x_prompt, x_sample, cache_mem_k, cache_mem_v, state_rwkv, state_rwkv_shift,
              state_s5_re, state_s5_im, state_conv, state_ssm, mem_prompt,
              norm_mix_pre, norm_mix_post, norm_xa_pre, norm_xa_post, norm_mlp_pre, norm_mlp_post, norm_mem,
              w_in, w_out,
              rwkv_mu, rwkv_w0, rwkv_w2, rwkv_a0, rwkv_a2, rwkv_g2, rwkv_k_k, rwkv_k_a, rwkv_r_k,
              rwkv_ln_w, rwkv_ln_b, w_out_rwkv,
              s5_a_re, s5_a_im, s5_log_dt, s5_b_re, s5_b_im, s5_c_re, s5_c_im, s5_d, s5_w_glu, w_out_s5,
              m_conv_w, m_conv_b, m_dt_bias, m_a_log, m_d, m_norm, w_out_mamba,
              xa_wq, xa_wk, xa_wv, xa_wo, mlp_w1, mlp_w2):
    f32 = jnp.float32

    def mixer_block(l, xn, rwkv_s, rwkv_shift, s5_re, s5_im, conv_buf, ssm_h):
        proj = xn @ w_in[l]
        gates, p_rwkv, u_s5, z_m, xbc_m, dt_m = split_cols(proj, IN_SIZES)
        gate_r, gate_s, gate_m = split_cols(jax.nn.sigmoid(gates), (D_MODEL, D_MODEL, D_MODEL))
        o_r, rwkv_shift_new, rwkv_s_new = rwkv7_branch(
            p_rwkv, rwkv_shift, rwkv_s, rwkv_mu[l], rwkv_w0[l], rwkv_w2[l], rwkv_a0[l], rwkv_a2[l],
            rwkv_g2[l], rwkv_k_k[l], rwkv_k_a[l], rwkv_r_k[l], rwkv_ln_w[l], rwkv_ln_b[l], w_out_rwkv[l])
        o_s, s5_re_new, s5_im_new = s5_branch(
            u_s5, s5_re, s5_im, s5_a_re[l], s5_a_im[l], s5_log_dt[l], s5_b_re[l], s5_b_im[l],
            s5_c_re[l], s5_c_im[l], s5_d[l], s5_w_glu[l], w_out_s5[l])
        o_m, conv_new, ssm_new = mamba2_branch(
            z_m, xbc_m, dt_m, conv_buf, ssm_h, m_conv_w[l], m_conv_b[l], m_dt_bias[l], m_a_log[l],
            m_d[l], m_norm[l], w_out_mamba[l])
        merged = gate_r * o_r + gate_s * o_s + gate_m * o_m
        return merged @ w_out[l], [rwkv_s_new, rwkv_shift_new, s5_re_new, s5_im_new, conv_new, ssm_new]

    def layer(l, h, mk, mv, st):
        mix, st_new = mixer_block(l, rmsnorm(h, norm_mix_pre[l]), st[0], st[1], st[2], st[3], st[4], st[5])
        h = h + rmsnorm(mix, norm_mix_post[l])
        xa = cross_attend(rmsnorm(h, norm_xa_pre[l]), mk, mv, xa_wq[l], xa_wo[l])
        h = h + rmsnorm(xa, norm_xa_post[l])
        ff = jnp.square(jax.nn.relu(rmsnorm(h, norm_mlp_pre[l]) @ mlp_w1[l])) @ mlp_w2[l]
        h = h + rmsnorm(ff, norm_mlp_post[l])
        return h, st_new

    bp = x_prompt.shape[0]
    zero_states = [jnp.zeros((bp, R_HEADS, R_HEAD, R_HEAD), f32), jnp.zeros((bp, R_COLS), f32),
                   jnp.zeros((bp, S5_GROUPS, S5_STATE), f32), jnp.zeros((bp, S5_GROUPS, S5_STATE), f32),
                   jnp.zeros((bp, M_CONV - 1, M_CONV_DIM), f32), jnp.zeros((bp, M_HEADS, M_HEAD, M_STATE), f32)]
    sample_states = [state_rwkv, state_rwkv_shift, state_s5_re, state_s5_im, state_conv, state_ssm]

    hp, hs = x_prompt, x_sample
    mk_out, mv_out = [], []
    st_p_out = [[] for _ in range(6)]
    st_s_out = [[] for _ in range(6)]
    for l in range(DEPTH):
        mem_n = rmsnorm(mem_prompt, norm_mem[l])
        mk = (mem_n @ xa_wk[l]).reshape(bp, mem_prompt.shape[1], XA_HEADS, XA_HEAD_DIM)
        mv = (mem_n @ xa_wv[l]).reshape(bp, mem_prompt.shape[1], XA_HEADS, XA_HEAD_DIM)
        hp, st_p = layer(l, hp, mk, mv, zero_states)
        hs, st_s = layer(l, hs, cache_mem_k[l], cache_mem_v[l], [s[l] for s in sample_states])
        mk_out.append(mk)
        mv_out.append(mv)
        for i in range(6):
            st_p_out[i].append(st_p[i])
            st_s_out[i].append(st_s[i])

    def stk(xs):
        return jnp.stack(xs).astype(f32)

    return (hp.astype(x_prompt.dtype), hs.astype(x_sample.dtype),
            stk(mk_out), stk(mv_out),
            stk(st_p_out[0]), stk(st_p_out[1]), stk(st_p_out[2]), stk(st_p_out[3]), stk(st_p_out[4]), stk(st_p_out[5]),
            stk(st_s_out[0]), stk(st_s_out[1]), stk(st_s_out[2]), stk(st_s_out[3]), stk(st_s_out[4]), stk(st_s_out[5]))
```

```python
import functools
import math

import jax
import jax.numpy as jnp
from jax import lax
from jax.experimental import pallas as pl
from jax.experimental.pallas import tpu as pltpu

F32 = jnp.float32
_MXU = jnp.bfloat16
EPS = 1e-6
R_LN_EPS = 64e-5
D = 1024
R_HEAD = 64
R_PAIR = 2 * R_HEAD
S5_BLK_CH = 128
S5_BLK_ST = 512
M_HEAD = 64
M_STATE = 128
M_GROUPS = 4
M_HPG = 8
M_GW = M_HPG * M_HEAD
M_WIDTH = 2048
M_CONV_DIM = 3072
XA_HEADS = 4
XA_HD = 256
VMEM_LIMIT = 56 * 1024 * 1024


def _mm(a, b):
    return jnp.dot(a.astype(_MXU), b.astype(_MXU), preferred_element_type=F32)


def _mm_nt(a, b):
    return lax.dot_general(a.astype(_MXU), b.astype(_MXU), (((1,), (1,)), ((), ())),
                           preferred_element_type=F32)


def _split3(x):
    hi = x.astype(_MXU)
    r1 = x - hi.astype(F32)
    mid = r1.astype(_MXU)
    lo = (r1 - mid.astype(F32)).astype(_MXU)
    return hi, mid, lo


def _mm_sel_l(sel, x):
    return sum(jnp.dot(sel, p, preferred_element_type=F32) for p in _split3(x))


def _mm_sel_r(x, sel):
    return sum(jnp.dot(p, sel, preferred_element_type=F32) for p in _split3(x))


def _mm_nt_sel_l(sel, x):
    return sum(lax.dot_general(sel, p, (((1,), (1,)), ((), ())), preferred_element_type=F32)
               for p in _split3(x))


def _sigmoid(x):
    return 1.0 / (1.0 + jnp.exp(-x))


def _softplus(x):
    return jnp.maximum(x, 0.0) + jnp.log(1.0 + jnp.exp(-jnp.abs(x)))


def _rms(x, g):
    return x * lax.rsqrt(jnp.mean(x * x, axis=-1, keepdims=True) + EPS) * g


def _iota(shape, dim):
    return lax.broadcasted_iota(jnp.int32, shape, dim)


def _eye(n, dtype):
    return jnp.where(_iota((n, n), 0) == _iota((n, n), 1), 1.0, 0.0).astype(dtype)


def _cparams(sem):
    return pltpu.CompilerParams(dimension_semantics=sem, vmem_limit_bytes=VMEM_LIMIT)


def _row_tile(n, want):
    t = min(n, want)
    while n % t:
        t //= 2
    return t


def _const_spec(shape):
    nd = len(shape)
    return pl.BlockSpec(shape, lambda *_: (0,) * nd)


def _norm_mm_kernel(x_ref, g_ref, w_ref, o_ref, xn_ref):
    @pl.when(pl.program_id(1) == 0)
    def _():
        xn_ref[...] = _rms(x_ref[...], g_ref[...]).astype(_MXU)

    o_ref[...] = jnp.dot(xn_ref[...], w_ref[...], preferred_element_type=F32)


def norm_matmul(x, g, w, *, tm=1024, tn=512, name="norm_mm"):
    n, d = x.shape
    c = w.shape[1]
    tm = _row_tile(n, tm)
    tn = _row_tile(c, tn)
    return pl.pallas_call(
        _norm_mm_kernel,
        out_shape=jax.ShapeDtypeStruct((n, c), F32),
        grid=(n // tm, c // tn),
        in_specs=[pl.BlockSpec((tm, d), lambda i, j: (i, 0)),
                  pl.BlockSpec((1, d), lambda i, j: (0, 0)),
                  pl.BlockSpec((d, tn), lambda i, j: (0, j))],
        out_specs=pl.BlockSpec((tm, tn), lambda i, j: (i, j)),
        scratch_shapes=[pltpu.VMEM((tm, d), _MXU)],
        compiler_params=_cparams(("parallel", "arbitrary")),
        name=name,
    )(x, g, w)


def _mm_norm_res_kernel(x_ref, w_ref, g_ref, h_ref, o_ref):
    y = _mm(x_ref[...], w_ref[...])
    o_ref[...] = h_ref[...] + _rms(y, g_ref[...])


def matmul_norm_residual(x, w, g, h, *, tm=512, name="mm_norm_res"):
    n, k = x.shape
    d = w.shape[1]
    tm = _row_tile(n, tm)
    return pl.pallas_call(
        _mm_norm_res_kernel,
        out_shape=jax.ShapeDtypeStruct((n, d), F32),
        grid=(n // tm,),
        in_specs=[pl.BlockSpec((tm, k), lambda i: (i, 0)),
                  _const_spec((k, d)),
                  _const_spec((1, d)),
                  pl.BlockSpec((tm, d), lambda i: (i, 0))],
        out_specs=pl.BlockSpec((tm, d), lambda i: (i, 0)),
        compiler_params=_cparams(("parallel",)),
        name=name,
    )(x, w, g, h)


def _mlp_kernel(h_ref, g1_ref, w1_ref, w2_ref, g2_ref, o_ref, *, n_chunks, ck):
    h = h_ref[...]
    xn = _rms(h, g1_ref[...]).astype(_MXU)
    acc = jnp.zeros(h.shape, F32)
    for j in range(n_chunks):
        a = jnp.dot(xn, w1_ref[:, j * ck:(j + 1) * ck], preferred_element_type=F32)
        a = jnp.square(jnp.maximum(a, 0.0))
        acc = acc + jnp.dot(a.astype(_MXU), w2_ref[j * ck:(j + 1) * ck, :], preferred_element_type=F32)
    o_ref[...] = h + _rms(acc, g2_ref[...])


def mlp_block(h, g1, w1, w2, g2, *, tm=512, ck=1024):
    n, d = h.shape
    f = w1.shape[1]
    tm = _row_tile(n, tm)
    return pl.pallas_call(
        functools.partial(_mlp_kernel, n_chunks=f // ck, ck=ck),
        out_shape=jax.ShapeDtypeStruct((n, d), F32),
        grid=(n // tm,),
        in_specs=[pl.BlockSpec((tm, d), lambda i: (i, 0)),
                  _const_spec((1, d)),
                  _const_spec((d, f)),
                  _const_spec((f, d)),
                  _const_spec((1, d))],
        out_specs=pl.BlockSpec((tm, d), lambda i: (i, 0)),
        compiler_params=_cparams(("parallel",)),
        name="mlp",
    )(h, g1, w1, w2, g2)


def _merge_kernel(gates_ref, yr_ref, ys_ref, ym_ref, h_ref, wr_ref, wglu_ref, ws_ref, wm_ref, wo_ref, g_ref,
                  o_ref):
    o_r = _mm(yr_ref[...], wr_ref[...])
    ys = ys_ref[...]
    y3 = ys * _sigmoid(_mm(ys, wglu_ref[...]))
    o_s = _mm(y3, ws_ref[...])
    o_m = _mm(ym_ref[...], wm_ref[...])
    merged = (_sigmoid(gates_ref[:, 0:D]) * o_r + _sigmoid(gates_ref[:, D:2 * D]) * o_s
              + _sigmoid(gates_ref[:, 2 * D:3 * D]) * o_m)
    mix = _mm(merged, wo_ref[...])
    o_ref[...] = h_ref[...] + _rms(mix, g_ref[...])


def mixer_merge(gates, yr, ys, ym, h, wr, wglu, ws, wm, wo, g, *, tm=256):
    n = h.shape[0]
    tm = _row_tile(n, tm)
    row = lambda w: pl.BlockSpec((tm, w), lambda i: (i, 0))
    return pl.pallas_call(
        _merge_kernel,
        out_shape=jax.ShapeDtypeStruct((n, D), F32),
        grid=(n // tm,),
        in_specs=[row(3 * D), row(D), row(D), row(M_WIDTH), row(D),
                  _const_spec((D, D)), _const_spec((D, D)), _const_spec((D, D)),
                  _const_spec((M_WIDTH, D)), _const_spec((D, D)), _const_spec((1, D))],
        out_specs=row(D),
        compiler_params=_cparams(("parallel",)),
        name="mixer_merge",
    )(gates, yr, ys, ym, h, wr, wglu, ws, wm, wo, g)


def _rwkv_prep_kernel(p_ref, pprev_ref, shift_ref, mu_ref, wa_ref, w0_ref, a0_ref, g2_ref, kk_ref, ka_ref,
                      hsum_ref, r_ref, e_ref, k_ref, v_ref, an_ref, bn_ref, g_ref, *, bt):
    i = pl.program_id(0)
    p = p_ref[...]
    tm = p.shape[0]
    first = jnp.where(i == 0, shift_ref[...], pprev_ref[...])
    prev = first if tm == bt else jnp.concatenate([first, p[:tm - bt]], axis=0)
    pm = p + mu_ref[...] * (prev - p)
    r = pm[:, 0:D]
    k = pm[:, D:2 * D]
    v = pm[:, 2 * D:3 * D]
    x_wa = pm[:, 3 * D:3 * D + 128]
    xg = pm[:, 3 * D + 128:3 * D + 256]
    lora_in = jnp.where(_iota(x_wa.shape, 1) < 64, jnp.tanh(x_wa), x_wa)
    lwa = _mm(lora_in, wa_ref[...])
    zw = w0_ref[...] + lwa[:, 0:D]
    w_log = -_softplus(-zw) - 0.5
    a = _sigmoid(a0_ref[...] + lwa[:, D:2 * D])
    g = _mm(_sigmoid(xg), g2_ref[...])
    kk = k * kk_ref[...]
    ss = _mm_sel_r(kk * kk, hsum_ref[...])
    kk = kk * lax.rsqrt(ss + 1e-12)
    r_ref[...] = r
    e_ref[...] = jnp.exp(w_log)
    k_ref[...] = k * (1.0 + (a - 1.0) * ka_ref[...])
    v_ref[...] = v
    an_ref[...] = -kk
    bn_ref[...] = kk * a
    g_ref[...] = g


def rwkv_prep(p, shift_prev, mu, wa, w0, a0, g2, k_k, k_a, hsum, *, bt, tm=256):
    n, rc = p.shape
    tm = _row_tile(n, max(tm, bt))
    nb = tm // bt
    row = lambda: pl.BlockSpec((tm, D), lambda i: (i, 0))
    outs = [jax.ShapeDtypeStruct((n, D), F32)] * 7
    return pl.pallas_call(
        functools.partial(_rwkv_prep_kernel, bt=bt),
        out_shape=outs,
        grid=(n // tm,),
        in_specs=[pl.BlockSpec((tm, rc), lambda i: (i, 0)),
                  pl.BlockSpec((bt, rc), lambda i: (jnp.maximum(i * nb - 1, 0), 0)),
                  _const_spec((bt, rc)),
                  _const_spec((1, rc)),
                  _const_spec((128, 2 * D)),
                  _const_spec((1, D)), _const_spec((1, D)),
                  _const_spec((128, D)),
                  _const_spec((1, D)), _const_spec((1, D)),
                  _const_spec((D, D))],
        out_specs=[row() for _ in range(7)],
        compiler_params=_cparams(("parallel",)),
        name="rwkv_prep",
    )(p, p, shift_prev, mu, wa, w0, a0, g2, k_k, k_a, hsum)


def _rwkv_rec_kernel(r_ref, e_ref, k_ref, v_ref, an_ref, bn_ref, g_ref, s0_ref, rk_ref, lnw_ref, lnb_ref,
                     y_ref, sfin_ref, s_scr, *, C, nck, n_steps):
    step = pl.program_id(2)
    C2 = 2 * C
    lane = _iota((1, R_PAIR), 1)
    m0 = lane < R_HEAD
    ri = _iota((R_PAIR, R_PAIR), 0)
    ci = _iota((R_PAIR, R_PAIR), 1)
    bd = (ri < R_HEAD) == (ci < R_HEAD)
    hsum = jnp.where(bd, 1.0, 0.0).astype(_MXU)
    rt_i = _iota((C2, C2), 0)
    ct_i = _iota((C2, C2), 1)
    same = (rt_i < C) == (ct_i < C)
    tr = jnp.where(rt_i < C, rt_i, rt_i - C)
    ts = jnp.where(ct_i < C, ct_i, ct_i - C)
    m_sl = same & (ts < tr)
    m_li = same & (ts <= tr)
    eye2 = jnp.where(rt_i == ct_i, 1.0, 0.0)
    lvl_masks = []
    m = 1
    while m < C:
        lvl_masks.append(((tr & ~(2 * m - 1)) == (ts & ~(2 * m - 1))) & ((tr & m) != 0) & ((ts & m) == 0))
        m *= 2
    tril = jnp.where(_iota((C, C), 1) <= _iota((C, C), 0), 1.0, 0.0).astype(_MXU)
    eye_p = _eye(R_PAIR, _MXU)

    @pl.when(step == 0)
    def _():
        z = jnp.zeros((R_HEAD, R_HEAD), F32)
        s_scr[...] = jnp.concatenate([jnp.concatenate([s0_ref[0, 0], z], axis=1),
                                      jnp.concatenate([z, s0_ref[0, 1]], axis=1)], axis=0)

    def stack(x):
        return jnp.concatenate([jnp.where(m0, x, 0.0), jnp.where(m0, 0.0, x)], axis=0)

    def fold(xs):
        return xs[:C] + xs[C:]

    S = s_scr[...]
    for q in range(nck):
        sl = slice(q * C, (q + 1) * C)
        r, e, k, v = r_ref[sl, :], e_ref[sl, :], k_ref[sl, :], v_ref[sl, :]
        an, bn, g = an_ref[sl, :], bn_ref[sl, :], g_ref[sl, :]
        cs = _mm_sel_l(tril, e)
        cl = cs[C - 1:C, :]
        at = an * jnp.exp(e - cs)
        rt = r * jnp.exp(-cs)
        ecs = jnp.exp(cs)
        bt_ = bn * ecs
        kt = k * ecs
        ecl = jnp.exp(cs - cl)
        bh = bn * ecl
        kh = k * ecl
        wc = jnp.exp(-cl)
        As, Rs, Vs = stack(at), stack(rt), stack(v)
        b2 = jnp.concatenate([bt_, bt_], axis=0)
        k2 = jnp.concatenate([kt, kt], axis=0)
        N = jnp.where(m_sl, _mm_nt(As, b2), 0.0)
        Ak = jnp.where(m_sl, _mm_nt(As, k2), 0.0)
        Arb = jnp.where(m_li, _mm_nt(Rs, b2), 0.0)
        Ark = jnp.where(m_li, _mm_nt(Rs, k2), 0.0)
        T = eye2 + jnp.where(lvl_masks[0], N, 0.0)
        for lm in lvl_masks[1:]:
            T = T + _mm(T, _mm(jnp.where(lm, N, 0.0), T))
        AkV = _mm(Ak, Vs)
        X = _mm(T, jnp.concatenate([As, AkV], axis=1))
        Z = _mm(Arb, X)
        Rp = rt + fold(Z[:, 0:R_PAIR])
        Y0 = fold(Z[:, R_PAIR:] + _mm(Ark, Vs))
        Ap = fold(X[:, 0:R_PAIR])
        U0 = fold(X[:, R_PAIR:])
        ApT = _mm_nt(eye_p, Ap)
        UVT = _mm_nt(eye_p, jnp.concatenate([U0, v], axis=0))
        P = jnp.where(bd, _mm(ApT, bh), 0.0)
        Q = jnp.where(bd, _mm(UVT, jnp.concatenate([bh, kh], axis=0)), 0.0)
        y = Y0 + _mm_nt(Rp, S)
        S = S * wc + _mm(S, P) + Q
        mean = _mm_sel_r(y, hsum) * (1.0 / R_HEAD)
        dlt = y - mean
        var = _mm_sel_r(dlt * dlt, hsum) * (1.0 / R_HEAD)
        yn = dlt * lax.rsqrt(var + R_LN_EPS) * lnw_ref[...] + lnb_ref[...]
        bonus = _mm_sel_r(r * k * rk_ref[...], hsum) * v
        y_ref[sl, :] = (yn + bonus) * g
    s_scr[...] = S

    @pl.when(step == n_steps - 1)
    def _():
        sfin_ref[0, 0] = S[0:R_HEAD, 0:R_HEAD]
        sfin_ref[0, 1] = S[R_HEAD:, R_HEAD:]


def rwkv_recurrence(r, e, k, v, an, bn, g, s0, l, r_k, ln_w, ln_b, *, T, B, C, nck):
    tb = C * nck
    n_steps = T // tb
    n_pairs = D // R_PAIR
    blk = lambda: pl.BlockSpec((tb, R_PAIR), lambda b, p, s: (s, b * n_pairs + p))
    par = lambda: pl.BlockSpec((1, R_PAIR), lambda b, p, s: (0, p))
    y, sfin = pl.pallas_call(
        functools.partial(_rwkv_rec_kernel, C=C, nck=nck, n_steps=n_steps),
        out_shape=[jax.ShapeDtypeStruct((T, B * D), F32),
                   jax.ShapeDtypeStruct((B, 2 * n_pairs, R_HEAD, R_HEAD), F32)],
        grid=(B, n_pairs, n_steps),
        in_specs=[blk() for _ in range(7)]
        + [pl.BlockSpec((None, 1, 2, R_HEAD, R_HEAD), lambda b, p, s: (l, b, p, 0, 0)),
           par(), par(), par()],
        out_specs=[blk(), pl.BlockSpec((1, 2, R_HEAD, R_HEAD), lambda b, p, s: (b, p, 0, 0))],
        scratch_shapes=[pltpu.VMEM((R_PAIR, R_PAIR), F32)],
        compiler_params=_cparams(("parallel", "parallel", "arbitrary")),
        name="rwkv_rec",
    )(r, e, k, v, an, bn, g, s0, r_k, ln_w, ln_b)
    return y, sfin


def _s5_kernel(u_ref, h0r_ref, h0i_ref, ar_ref, ai_ref, bre_ref, bim_ref, cre_ref, cim_ref, d_ref,
               y_ref, hr_out, hi_out, inr_scr, ini_scr, hr_scr, hi_scr, *, tc, bt, n_steps):
    step = pl.program_id(1)

    @pl.when(step == 0)
    def _():
        hr_scr[...] = h0r_ref[...]
        hi_scr[...] = h0i_ref[...]

    u = u_ref[...]
    ub = u.astype(_MXU)
    inr_scr[...] = jnp.dot(ub, bre_ref[0], preferred_element_type=F32)
    ini_scr[...] = jnp.dot(ub, bim_ref[0], preferred_element_type=F32)
    ar = ar_ref[...]
    ai = ai_ref[...]

    def body(t, carry):
        hr, hi = carry
        rows = pl.ds(pl.multiple_of(t * bt, bt), bt)
        nr = ar * hr - ai * hi + inr_scr[rows, :]
        ni = ar * hi + ai * hr + ini_scr[rows, :]
        inr_scr[rows, :] = nr
        ini_scr[rows, :] = ni
        return nr, ni

    hr, hi = lax.fori_loop(0, tc, body, (hr_scr[...], hi_scr[...]))
    hr_scr[...] = hr
    hi_scr[...] = hi
    y = (jnp.dot(inr_scr[...].astype(_MXU), cre_ref[0], preferred_element_type=F32)
         - jnp.dot(ini_scr[...].astype(_MXU), cim_ref[0], preferred_element_type=F32))
    y = y + d_ref[...] * u
    y_ref[...] = 0.5 * y * (1.0 + jnp.tanh(math.sqrt(2.0 / math.pi) * (y + 0.044715 * (y * y * y))))

    @pl.when(step == n_steps - 1)
    def _():
        hr_out[...] = hr
        hi_out[...] = hi


def s5_scan(u, h0r, h0i, l, ar, ai, bre, bim, cre, cim, d, *, T, bt, tc):
    n = T * bt
    nblk = D // S5_BLK_CH
    n_steps = T // tc
    rows = tc * bt
    st = lambda: pl.BlockSpec((None, bt, S5_BLK_ST), lambda c, s: (l, 0, c))
    vec = lambda w: pl.BlockSpec((1, w), lambda c, s: (0, c))
    return pl.pallas_call(
        functools.partial(_s5_kernel, tc=tc, bt=bt, n_steps=n_steps),
        out_shape=[jax.ShapeDtypeStruct((n, D), F32),
                   jax.ShapeDtypeStruct((bt, nblk * S5_BLK_ST), F32),
                   jax.ShapeDtypeStruct((bt, nblk * S5_BLK_ST), F32)],
        grid=(nblk, n_steps),
        in_specs=[pl.BlockSpec((rows, S5_BLK_CH), lambda c, s: (s, c)),
                  st(), st(), vec(S5_BLK_ST), vec(S5_BLK_ST),
                  pl.BlockSpec((1, S5_BLK_CH, S5_BLK_ST), lambda c, s: (c, 0, 0)),
                  pl.BlockSpec((1, S5_BLK_CH, S5_BLK_ST), lambda c, s: (c, 0, 0)),
                  pl.BlockSpec((1, S5_BLK_ST, S5_BLK_CH), lambda c, s: (c, 0, 0)),
                  pl.BlockSpec((1, S5_BLK_ST, S5_BLK_CH), lambda c, s: (c, 0, 0)),
                  vec(S5_BLK_CH)],
        out_specs=[pl.BlockSpec((rows, S5_BLK_CH), lambda c, s: (s, c)),
                   pl.BlockSpec((bt, S5_BLK_ST), lambda c, s: (0, c)),
                   pl.BlockSpec((bt, S5_BLK_ST), lambda c, s: (0, c))],
        scratch_shapes=[pltpu.VMEM((rows, S5_BLK_ST), F32), pltpu.VMEM((rows, S5_BLK_ST), F32),
                        pltpu.VMEM((bt, S5_BLK_ST), F32), pltpu.VMEM((bt, S5_BLK_ST), F32)],
        compiler_params=_cparams(("parallel", "arbitrary")),
        name="s5_scan",
    )(u, h0r, h0i, ar, ai, bre, bim, cre, cim, d)


def _s5_discretize(a_re, a_im, log_dt, b_re, b_im, c_re, c_im):
    g, p, hch = b_re.shape
    dt = jnp.exp(log_dt.astype(F32))[:, None]
    mag = jnp.exp(dt * a_re)
    abar_re = mag * jnp.cos(dt * a_im)
    abar_im = mag * jnp.sin(dt * a_im)
    den = a_re * a_re + a_im * a_im
    nr = abar_re - 1.0
    q_re = (nr * a_re + abar_im * a_im) / den
    q_im = (abar_im * a_re - nr * a_im) / den
    bb_re = q_re[..., None] * b_re - q_im[..., None] * b_im
    bb_im = q_re[..., None] * b_im + q_im[..., None] * b_re
    nblk = D // S5_BLK_CH
    gl = g // nblk
    eye = jnp.eye(gl, dtype=F32)

    def in_blocks(bb):
        t = jnp.transpose(bb, (0, 2, 1)).reshape(nblk, gl, hch, p)
        return jnp.einsum('cghp,gk->cghkp', t, eye).reshape(nblk, gl * hch, gl * p).astype(_MXU)

    def out_blocks(cc):
        t = jnp.transpose(cc, (0, 2, 1)).reshape(nblk, gl, p, hch)
        return jnp.einsum('cgph,gk->cgpkh', t, eye).reshape(nblk, gl * p, gl * hch).astype(_MXU)

    return (abar_re.reshape(1, g * p), abar_im.reshape(1, g * p), in_blocks(bb_re), in_blocks(bb_im),
            out_blocks(c_re), out_blocks(c_im))


def _ssd_kernel(xbc_ref, z_ref, dt_ref, cprev_ref, h0_ref, cw_ref, cb_ref, dtb_ref, alog_ref, dx_ref, ng_ref,
                eh_ref, y_ref, hfin_ref, ext_scr, h_scr, *, L, t_real, n_steps):
    step = pl.program_id(1)

    @pl.when(step == 0)
    def _():
        ext_scr[0:8, :] = jnp.zeros((8, M_CONV_DIM), F32)
        ext_scr[5:8, :] = cprev_ref[0]
        for gi in range(M_GROUPS):
            h_scr[gi] = h0_ref[0, gi * M_HPG:(gi + 1) * M_HPG].reshape(M_GW, M_STATE)

    ext_scr[8:8 + L, :] = xbc_ref[...]
    conv = cb_ref[...] + cw_ref[0:1, :] * ext_scr[pl.ds(5, L), :]
    for j in range(1, 4):
        conv = conv + cw_ref[j:j + 1, :] * ext_scr[pl.ds(5 + j, L), :]
    tail = ext_scr[pl.ds(L + 5, 3), :]
    ext_scr[5:8, :] = tail
    conv = conv * _sigmoid(conv)
    xs = conv[:, 0:M_WIDTH]
    dt = _softplus(dt_ref[...] + dtb_ref[...])
    if t_real < L:
        dt = jnp.where(_iota(dt.shape, 0) < t_real, dt, 0.0)
    a = -jnp.exp(alog_ref[...])
    adt = dt * a
    tril = jnp.where(_iota((L, L), 1) <= _iota((L, L), 0), 1.0, 0.0).astype(_MXU)
    causal = _iota((L, L), 1) <= _iota((L, L), 0)
    acum = _mm_sel_l(tril, adt)
    acum_t = _mm_nt_sel_l(_eye(128, _MXU), acum)
    eh = eh_ref[...]
    dt_x = _mm_sel_r(dt, eh)
    acum_x = _mm_sel_r(acum, eh)
    acl_x = acum_x[L - 1:L, :]
    xd = xs * dt_x
    xdd = xd * jnp.exp(acl_x - acum_x)
    eacum_x = jnp.exp(acum_x)
    cdec_x = jnp.exp(acl_x)
    cdec8 = jnp.broadcast_to(cdec_x, (8, M_WIDTH))
    eye_gw = _eye(M_GW, _MXU)
    lane_head = _iota((1, M_GW), 1) // M_HEAD
    ys = []
    for gi in range(M_GROUPS):
        gs = slice(gi * M_GW, (gi + 1) * M_GW)
        bg = conv[:, M_WIDTH + gi * M_STATE:M_WIDTH + (gi + 1) * M_STATE]
        cg = conv[:, M_WIDTH + M_GROUPS * M_STATE + gi * M_STATE:M_WIDTH + M_GROUPS * M_STATE + (gi + 1) * M_STATE]
        h = h_scr[gi]
        cbm = _mm_nt(cg, bg)
        xd_g = xd[:, gs]
        y_g = _mm_nt(cg, h) * eacum_x[:, gs]
        for e in range(M_HPG):
            he = gi * M_HPG + e
            seg = acum[:, he:he + 1] - acum_t[he:he + 1, :]
            dec = jnp.exp(jnp.where(causal, seg, -jnp.inf))
            y_g = y_g + _mm(cbm * dec, jnp.where(lane_head == e, xd_g, 0.0))
        ys.append(y_g)
        xdd_t = _mm_nt(eye_gw, xdd[:, gs])
        dcol = _mm_nt_sel_l(eye_gw, cdec8[:, gs])[:, 0:1]
        h_scr[gi] = h * dcol + _mm(xdd_t, bg)
    y = jnp.concatenate(ys, axis=1) + dx_ref[...] * xs
    zz = z_ref[...]
    y = y * (zz * _sigmoid(zz))
    outs = []
    for gi in range(M_GROUPS):
        yg = y[:, gi * M_GW:(gi + 1) * M_GW]
        outs.append(yg * lax.rsqrt(jnp.mean(yg * yg, axis=-1, keepdims=True) + EPS))
    y_ref[...] = jnp.concatenate(outs, axis=1) * ng_ref[...]

    @pl.when(step == n_steps - 1)
    def _():
        for gi in range(M_GROUPS):
            hfin_ref[0, gi * M_HPG:(gi + 1) * M_HPG] = h_scr[gi].reshape(M_HPG, M_HEAD, M_STATE)


def ssd_block(xbc, z, dt, conv_prev, h0, l, conv_w, conv_b, dt_bias, a_log, d_x, norm_g, eh, *, T, B, L, t_real):
    n_steps = T // L
    heads = M_GROUPS * M_HPG
    return pl.pallas_call(
        functools.partial(_ssd_kernel, L=L, t_real=t_real, n_steps=n_steps),
        out_shape=[jax.ShapeDtypeStruct((T, B * M_WIDTH), F32),
                   jax.ShapeDtypeStruct((B, heads, M_HEAD, M_STATE), F32)],
        grid=(B, n_steps),
        in_specs=[pl.BlockSpec((L, M_CONV_DIM), lambda b, s: (s, b)),
                  pl.BlockSpec((L, M_WIDTH), lambda b, s: (s, b)),
                  pl.BlockSpec((L, 128), lambda b, s: (s, b)),
                  pl.BlockSpec((None, 1, 3, M_CONV_DIM), lambda b, s: (l, b, 0, 0)),
                  pl.BlockSpec((None, 1, heads, M_HEAD, M_STATE), lambda b, s: (l, b, 0, 0, 0)),
                  _const_spec((4, M_CONV_DIM)), _const_spec((1, M_CONV_DIM)),
                  _const_spec((1, 128)), _const_spec((1, 128)),
                  _const_spec((1, M_WIDTH)), _const_spec((1, M_WIDTH)),
                  _const_spec((128, M_WIDTH))],
        out_specs=[pl.BlockSpec((L, M_WIDTH), lambda b, s: (s, b)),
                   pl.BlockSpec((1, heads, M_HEAD, M_STATE), lambda b, s: (b, 0, 0, 0))],
        scratch_shapes=[pltpu.VMEM((L + 8, M_CONV_DIM), F32),
                        pltpu.VMEM((M_GROUPS, M_GW, M_STATE), F32)],
        compiler_params=_cparams(("parallel", "arbitrary")),
        name="ssd",
    )(xbc, z, dt, conv_prev, h0, conv_w, conv_b, dt_bias, a_log, d_x, norm_g, eh)


def _attn_kernel(q_ref, k_ref, v_ref, o_ref, *, nb):
    scale = XA_HD ** -0.5
    for j in range(nb):
        outs = []
        for hd in range(XA_HEADS):
            cs = slice(hd * XA_HD, (hd + 1) * XA_HD)
            q = q_ref[:, j * D + hd * XA_HD:j * D + (hd + 1) * XA_HD]
            s = _mm_nt(q, k_ref[j, :, cs]) * scale
            s = s - jnp.max(s, axis=-1, keepdims=True)
            p = jnp.exp(s)
            p = p / jnp.sum(p, axis=-1, keepdims=True)
            outs.append(_mm(p, v_ref[j, :, cs]))
        o_ref[:, j * D:(j + 1) * D] = jnp.concatenate(outs, axis=1)


def cross_attention(q, mk, mv, kv_index, *, T, B, tq, nb):
    m = mk.shape[-2]
    nlead = len(kv_index)
    kv_spec = pl.BlockSpec((None,) * nlead + (nb, m, D), lambda b, s: tuple(kv_index) + (b, 0, 0))
    return pl.pallas_call(
        functools.partial(_attn_kernel, nb=nb),
        out_shape=jax.ShapeDtypeStruct((T, B * D), F32),
        grid=(B // nb, T // tq),
        in_specs=[pl.BlockSpec((tq, nb * D), lambda b, s: (s, b)), kv_spec, kv_spec],
        out_specs=pl.BlockSpec((tq, nb * D), lambda b, s: (s, b)),
        compiler_params=_cparams(("parallel", "arbitrary")),
        name="xattn",
    )(q, mk, mv)


def _layer(l, h, grp, wts, st):
    T, B = grp['T'], grp['B']
    n = T * B
    W = wts
    gates = norm_matmul(h, W['g_mix_pre'], W['w_gates'], name="in_gates")
    p_r = norm_matmul(h, W['g_mix_pre'], W['w_rwkv'], name="in_rwkv")
    u_s5 = norm_matmul(h, W['g_mix_pre'], W['w_s5'], name="in_s5")
    z_m = norm_matmul(h, W['g_mix_pre'], W['w_z'], name="in_z")
    xbc = norm_matmul(h, W['g_mix_pre'], W['w_xbc'], name="in_xbc")
    dt_m = norm_matmul(h, W['g_mix_pre'], W['w_dt'], tn=128, name="in_dt")

    r, e, k, v, an, bn, g = rwkv_prep(p_r, st['shift'][l], W['mu'], W['wa'], W['w0'], W['a0'], W['g2'],
                                      W['k_k'], W['k_a'], W['hsum'], bt=B)
    C = grp['rwkv_C']
    Tp = grp['rwkv_Tpad']
    v2 = lambda x, w: x.reshape(T, B * w)
    seqs = [v2(x, D) for x in (r, e, k, v, an, bn, g)]
    if Tp != T:
        seqs = [jnp.pad(x, ((0, Tp - T), (0, 0))) for x in seqs]
    yr, s_fin = rwkv_recurrence(*seqs, st['rwkv'], l, W['r_k'], W['ln_w'], W['ln_b'],
                                T=Tp, B=B, C=C, nck=grp['rwkv_nck'])
    yr = yr[:T].reshape(n, D)
    shift_new = p_r[n - B:]

    ys, s5r, s5i = s5_scan(u_s5, st['s5r'], st['s5i'], l, W['s5_ar'], W['s5_ai'], W['s5_bre'], W['s5_bim'],
                           W['s5_cre'], W['s5_cim'], W['s5_d'], T=T, bt=B, tc=grp['s5_tc'])

    Lc = grp['ssd_L']
    Tm = grp['ssd_Tpad']
    xbc_v, z_v, dt_v = v2(xbc, M_CONV_DIM), v2(z_m, M_WIDTH), v2(dt_m, 128)
    if Tm != T:
        xbc_v, z_v, dt_v = [jnp.pad(x, ((0, Tm - T), (0, 0))) for x in (xbc_v, z_v, dt_v)]
    ym, ssm_fin = ssd_block(xbc_v, z_v, dt_v, st['conv'], st['ssm'], l, W['conv_w'], W['conv_b'], W['dt_bias'],
                            W['a_log'], W['d_x'], W['m_norm'], W['eh'], T=Tm, B=B, L=Lc, t_real=min(T, Lc))
    ym = ym[:T].reshape(n, M_WIDTH)
    padded = jnp.concatenate([jnp.transpose(st['conv'][l], (1, 0, 2)), xbc.reshape(T, B, M_CONV_DIM)[-3:]], axis=0)
    conv_new = jnp.transpose(padded[-3:], (1, 0, 2))

    h = mixer_merge(gates, yr, ys, ym, h, W['w_out_rwkv'], W['s5_w_glu'], W['w_out_s5'], W['w_out_mamba'],
                    W['w_out'], W['g_mix_post'])

    q = norm_matmul(h, W['g_xa_pre'], W['xa_wq'], name="xa_q")
    o = cross_attention(q.reshape(T, B * D), st['mk'], st['mv'], st['kv_index'](l), T=T, B=B,
                        tq=grp['xa_tq'], nb=grp['xa_nb'])
    h = matmul_norm_residual(o.reshape(n, D), W['xa_wo'], W['g_xa_post'], h, name="xa_out")

    h = mlp_block(h, W['g_mlp_pre'], W['mlp_w1'], W['mlp_w2'], W['g_mlp_post'])
    return h, (s_fin, shift_new, s5r.reshape(B, 64, 64), s5i.reshape(B, 64, 64), conv_new, ssm_fin)


def _group_cfg(T, B):
    cfg = dict(T=T, B=B)
    if T % 64 == 0:
        cfg.update(rwkv_C=64, rwkv_Tpad=T, rwkv_nck=4 if T % 256 == 0 else 1)
    else:
        tp = -(-T // 8) * 8
        cfg.update(rwkv_C=8, rwkv_Tpad=tp, rwkv_nck=tp // 8)
    cfg['s5_tc'] = max(1, min(T, 512 // B))
    while T % cfg['s5_tc']:
        cfg['s5_tc'] -= 1
    if T % 128 == 0:
        cfg.update(ssd_L=128, ssd_Tpad=T)
    else:
        tp = -(-T // 8) * 8
        cfg.update(ssd_L=tp, ssd_Tpad=tp)
    cfg['xa_tq'] = _row_tile(T, 512)
    cfg['xa_nb'] = 1 if T >= 64 else min(B, 8)
    return cfg


def kernel(x_prompt, x_sample, cache_mem_k, cache_mem_v, state_rwkv, state_rwkv_shift, state_s5_re, state_s5_im, state_conv, state_ssm, mem_prompt, norm_mix_pre, norm_mix_post, norm_xa_pre, norm_xa_post, norm_mlp_pre, norm_mlp_post, norm_mem, w_in, w_out, rwkv_mu, rwkv_w0, rwkv_w2, rwkv_a0, rwkv_a2, rwkv_g2, rwkv_k_k, rwkv_k_a, rwkv_r_k, rwkv_ln_w, rwkv_ln_b, w_out_rwkv, s5_a_re, s5_a_im, s5_log_dt, s5_b_re, s5_b_im, s5_c_re, s5_c_im, s5_d, s5_w_glu, w_out_s5, m_conv_w, m_conv_b, m_dt_bias, m_a_log, m_d, m_norm, w_out_mamba, xa_wq, xa_wk, xa_wv, xa_wo, mlp_w1, mlp_w2):
    depth = w_in.shape[0]
    bp, tp, _ = x_prompt.shape
    bs, ts, _ = x_sample.shape
    mlen = mem_prompt.shape[1]
    bf = lambda x: x.astype(_MXU)
    row = lambda x: x.reshape(1, -1).astype(F32)

    ch = jnp.arange(D)
    hsum = (ch[:, None] // R_HEAD == ch[None, :] // R_HEAD).astype(_MXU)
    eh = (jnp.arange(128)[:, None] == (jnp.arange(M_WIDTH)[None, :] // M_HEAD)).astype(_MXU)

    col = [0]
    for sz in (3 * D, 3 * D + 256, D, M_WIDTH, M_CONV_DIM, 32):
        col.append(col[-1] + sz)
    layers = []
    for l in range(depth):
        wl = w_in[l]
        ar, ai, bre, bim, cre, cim = _s5_discretize(s5_a_re[l], s5_a_im[l], s5_log_dt[l], s5_b_re[l], s5_b_im[l],
                                                    s5_c_re[l], s5_c_im[l])
        zero = jnp.zeros((64, D), F32)
        wa = jnp.concatenate([jnp.concatenate([rwkv_w2[l], zero], axis=1),
                              jnp.concatenate([zero, rwkv_a2[l]], axis=1)], axis=0)
        pad32 = lambda x: jnp.pad(x.reshape(1, -1).astype(F32), ((0, 0), (0, 128 - x.shape[-1])))
        layers.append(dict(
            g_mix_pre=row(norm_mix_pre[l]), g_mix_post=row(norm_mix_post[l]),
            g_xa_pre=row(norm_xa_pre[l]), g_xa_post=row(norm_xa_post[l]),
            g_mlp_pre=row(norm_mlp_pre[l]), g_mlp_post=row(norm_mlp_post[l]),
            w_gates=bf(wl[:, col[0]:col[1]]), w_rwkv=bf(wl[:, col[1]:col[2]]), w_s5=bf(wl[:, col[2]:col[3]]),
            w_z=bf(wl[:, col[3]:col[4]]), w_xbc=bf(wl[:, col[4]:col[5]]),
            w_dt=bf(jnp.pad(wl[:, col[5]:col[6]], ((0, 0), (0, 96)))),
            mu=row(rwkv_mu[l]), wa=bf(wa), w0=row(rwkv_w0[l]), a0=row(rwkv_a0[l]), g2=bf(rwkv_g2[l]),
            k_k=row(rwkv_k_k[l]), k_a=row(rwkv_k_a[l]), hsum=hsum, r_k=row(rwkv_r_k[l]),
            ln_w=row(rwkv_ln_w[l]), ln_b=row(rwkv_ln_b[l]),
            s5_ar=ar, s5_ai=ai, s5_bre=bre, s5_bim=bim, s5_cre=cre, s5_cim=cim, s5_d=row(s5_d[l]),
            conv_w=m_conv_w[l].astype(F32), conv_b=row(m_conv_b[l]), dt_bias=pad32(m_dt_bias[l]),
            a_log=pad32(m_a_log[l]), d_x=row(jnp.repeat(m_d[l], M_HEAD)), m_norm=row(m_norm[l]), eh=eh,
            w_out_rwkv=bf(w_out_rwkv[l]), s5_w_glu=bf(s5_w_glu[l]), w_out_s5=bf(w_out_s5[l]),
            w_out_mamba=bf(w_out_mamba[l]), w_out=bf(w_out[l]),
            xa_wq=bf(xa_wq[l]), xa_wk=bf(xa_wk[l]), xa_wv=bf(xa_wv[l]), xa_wo=bf(xa_wo[l]),
            mlp_w1=bf(mlp_w1[l]), mlp_w2=bf(mlp_w2[l]), g_mem=row(norm_mem[l]),
        ))

    hp = jnp.transpose(x_prompt, (1, 0, 2)).reshape(tp * bp, D).astype(F32)
    hs = jnp.transpose(x_sample, (1, 0, 2)).reshape(ts * bs, D).astype(F32)
    mem2 = mem_prompt.reshape(bp * mlen, D).astype(F32)

    cfg_p = _group_cfg(tp, bp)
    cfg_s = _group_cfg(ts, bs)
    zeros_p = dict(
        rwkv=jnp.zeros((depth, bp) + state_rwkv.shape[2:], F32),
        shift=jnp.zeros((depth, bp, state_rwkv_shift.shape[-1]), F32),
        s5r=jnp.zeros((depth, bp, state_s5_re.shape[2] * state_s5_re.shape[3]), F32),
        s5i=jnp.zeros((depth, bp, state_s5_re.shape[2] * state_s5_re.shape[3]), F32),
        conv=jnp.zeros((depth, bp) + state_conv.shape[2:], F32),
        ssm=jnp.zeros((depth, bp) + state_ssm.shape[2:], F32),
    )
    st_s = dict(
        rwkv=state_rwkv, shift=state_rwkv_shift,
        s5r=state_s5_re.reshape(depth, bs, -1), s5i=state_s5_im.reshape(depth, bs, -1),
        conv=state_conv, ssm=state_ssm,
        mk=cache_mem_k.reshape(depth, bs, mlen, D), mv=cache_mem_v.reshape(depth, bs, mlen, D),
        kv_index=lambda l: (l,),
    )

    mk_out, mv_out = [], []
    st_p_out = [[] for _ in range(6)]
    st_s_out = [[] for _ in range(6)]
    for l in range(depth):
        W = layers[l]
        mk = norm_matmul(mem2, W['g_mem'], W['xa_wk'], name="mem_k")
        mv = norm_matmul(mem2, W['g_mem'], W['xa_wv'], name="mem_v")
        st_p = dict(zeros_p, mk=mk.reshape(bp, mlen, D), mv=mv.reshape(bp, mlen, D), kv_index=lambda l: ())
        hp, new_p = _layer(l, hp, cfg_p, W, st_p)
        hs, new_s = _layer(l, hs, cfg_s, W, st_s)
        mk_out.append(mk.reshape(bp, mlen, XA_HEADS, XA_HD))
        mv_out.append(mv.reshape(bp, mlen, XA_HEADS, XA_HD))
        for i in range(6):
            st_p_out[i].append(new_p[i])
            st_s_out[i].append(new_s[i])

    stk = lambda xs: jnp.stack(xs).astype(F32)
    y_p = jnp.transpose(hp.reshape(tp, bp, D), (1, 0, 2)).astype(x_prompt.dtype)
    y_s = jnp.transpose(hs.reshape(ts, bs, D), (1, 0, 2)).astype(x_sample.dtype)
    return (y_p, y_s, stk(mk_out), stk(mv_out),
            *[stk(st_p_out[i]) for i in range(6)],
            *[stk(st_s_out[i]) for i in range(6)])
```

```python
import functools
import math

import numpy as np
import jax
import jax.numpy as jnp
from jax import lax
from jax.experimental import pallas as pl
from jax.experimental.pallas import tpu as pltpu

F32 = jnp.float32
_MXU = jnp.bfloat16
EPS = 1e-6
R_LN_EPS = 64e-5
D = 1024
R_HEAD = 64
R_PAIR = 2 * R_HEAD
S5_BLK_CH = 128
S5_BLK_ST = 512
M_HEAD = 64
M_STATE = 128
M_GROUPS = 4
M_HPG = 8
M_GW = M_HPG * M_HEAD
M_WIDTH = 2048
M_CONV_DIM = 3072
XA_HEADS = 4
XA_HD = 256
VMEM_LIMIT = 56 * 1024 * 1024


def _mm(a, b):
    return jnp.dot(a.astype(_MXU), b.astype(_MXU), preferred_element_type=F32)


def _mm_nt(a, b):
    return lax.dot_general(a.astype(_MXU), b.astype(_MXU), (((1,), (1,)), ((), ())),
                           preferred_element_type=F32)


def _split3(x):
    hi = x.astype(_MXU)
    r1 = x - hi.astype(F32)
    mid = r1.astype(_MXU)
    lo = (r1 - mid.astype(F32)).astype(_MXU)
    return hi, mid, lo


def _mm_sel_l(sel, x):
    return sum(jnp.dot(sel, p, preferred_element_type=F32) for p in _split3(x))


def _mm_sel_r(x, sel):
    return sum(jnp.dot(p, sel, preferred_element_type=F32) for p in _split3(x))


def _mm_nt_sel_l(sel, x):
    return sum(lax.dot_general(sel, p, (((1,), (1,)), ((), ())), preferred_element_type=F32)
               for p in _split3(x))


def _sigmoid(x):
    return 1.0 / (1.0 + jnp.exp(-x))


def _softplus(x):
    return jnp.maximum(x, 0.0) + jnp.log(1.0 + jnp.exp(-jnp.abs(x)))


def _rms(x, g):
    return x * lax.rsqrt(jnp.mean(x * x, axis=-1, keepdims=True) + EPS) * g


def _iota(shape, dim):
    return lax.broadcasted_iota(jnp.int32, shape, dim)


def _eye(n, dtype):
    return jnp.where(_iota((n, n), 0) == _iota((n, n), 1), 1.0, 0.0).astype(dtype)


def _cparams(sem):
    return pltpu.CompilerParams(dimension_semantics=sem, vmem_limit_bytes=VMEM_LIMIT)


def _row_tile(n, want):
    t = min(n, want)
    while n % t:
        t //= 2
    return t


def _const_spec(shape):
    nd = len(shape)
    return pl.BlockSpec(shape, lambda *_: (0,) * nd)


def _norm_mm_kernel(x_ref, g_ref, w_ref, o_ref, xn_ref):
    @pl.when(pl.program_id(1) == 0)
    def _():
        xn_ref[...] = _rms(x_ref[...], g_ref[...]).astype(_MXU)

    o_ref[...] = jnp.dot(xn_ref[...], w_ref[...], preferred_element_type=F32)


def norm_matmul(x, g, w, *, tm=1024, tn=512, name="norm_mm"):
    n, d = x.shape
    c = w.shape[1]
    tm = _row_tile(n, tm)
    tn = _row_tile(c, tn)
    return pl.pallas_call(
        _norm_mm_kernel,
        out_shape=jax.ShapeDtypeStruct((n, c), F32),
        grid=(n // tm, c // tn),
        in_specs=[pl.BlockSpec((tm, d), lambda i, j: (i, 0)),
                  pl.BlockSpec((1, d), lambda i, j: (0, 0)),
                  pl.BlockSpec((d, tn), lambda i, j: (0, j))],
        out_specs=pl.BlockSpec((tm, tn), lambda i, j: (i, j)),
        scratch_shapes=[pltpu.VMEM((tm, d), _MXU)],
        compiler_params=_cparams(("parallel", "arbitrary")),
        name=name,
    )(x, g, w)


def _mm_norm_res_kernel(x_ref, w_ref, g_ref, h_ref, o_ref):
    y = _mm(x_ref[...], w_ref[...])
    o_ref[...] = h_ref[...] + _rms(y, g_ref[...])


def matmul_norm_residual(x, w, g, h, *, tm=512, name="mm_norm_res"):
    n, k = x.shape
    d = w.shape[1]
    tm = _row_tile(n, tm)
    return pl.pallas_call(
        _mm_norm_res_kernel,
        out_shape=jax.ShapeDtypeStruct((n, d), F32),
        grid=(n // tm,),
        in_specs=[pl.BlockSpec((tm, k), lambda i: (i, 0)),
                  _const_spec((k, d)),
                  _const_spec((1, d)),
                  pl.BlockSpec((tm, d), lambda i: (i, 0))],
        out_specs=pl.BlockSpec((tm, d), lambda i: (i, 0)),
        compiler_params=_cparams(("parallel",)),
        name=name,
    )(x, w, g, h)


def _mlp_kernel(h_ref, g1_ref, w1_ref, w2_ref, g2_ref, o_ref, *, n_chunks, ck):
    h = h_ref[...]
    xn = _rms(h, g1_ref[...]).astype(_MXU)
    acc = jnp.zeros(h.shape, F32)
    for j in range(n_chunks):
        a = jnp.dot(xn, w1_ref[:, j * ck:(j + 1) * ck], preferred_element_type=F32)
        a = jnp.square(jnp.maximum(a, 0.0))
        acc = acc + jnp.dot(a.astype(_MXU), w2_ref[j * ck:(j + 1) * ck, :], preferred_element_type=F32)
    o_ref[...] = h + _rms(acc, g2_ref[...])


def mlp_block(h, g1, w1, w2, g2, *, tm=512, ck=1024):
    n, d = h.shape
    f = w1.shape[1]
    tm = _row_tile(n, tm)
    return pl.pallas_call(
        functools.partial(_mlp_kernel, n_chunks=f // ck, ck=ck),
        out_shape=jax.ShapeDtypeStruct((n, d), F32),
        grid=(n // tm,),
        in_specs=[pl.BlockSpec((tm, d), lambda i: (i, 0)),
                  _const_spec((1, d)),
                  _const_spec((d, f)),
                  _const_spec((f, d)),
                  _const_spec((1, d))],
        out_specs=pl.BlockSpec((tm, d), lambda i: (i, 0)),
        compiler_params=_cparams(("parallel",)),
        name="mlp",
    )(h, g1, w1, w2, g2)


def _merge_kernel(gates_ref, yr_ref, ys_ref, ym_ref, h_ref, wr_ref, wglu_ref, ws_ref, wm_ref, wo_ref, g_ref,
                  o_ref):
    o_r = _mm(yr_ref[...], wr_ref[...])
    ys = ys_ref[...]
    y3 = ys * _sigmoid(_mm(ys, wglu_ref[...]))
    o_s = _mm(y3, ws_ref[...])
    o_m = _mm(ym_ref[...], wm_ref[...])
    merged = (_sigmoid(gates_ref[:, 0:D]) * o_r + _sigmoid(gates_ref[:, D:2 * D]) * o_s
              + _sigmoid(gates_ref[:, 2 * D:3 * D]) * o_m)
    mix = _mm(merged, wo_ref[...])
    o_ref[...] = h_ref[...] + _rms(mix, g_ref[...])


def mixer_merge(gates, yr, ys, ym, h, wr, wglu, ws, wm, wo, g, *, tm=256):
    n = h.shape[0]
    tm = _row_tile(n, tm)
    row = lambda w: pl.BlockSpec((tm, w), lambda i: (i, 0))
    return pl.pallas_call(
        _merge_kernel,
        out_shape=jax.ShapeDtypeStruct((n, D), F32),
        grid=(n // tm,),
        in_specs=[row(3 * D), row(D), row(D), row(M_WIDTH), row(D),
                  _const_spec((D, D)), _const_spec((D, D)), _const_spec((D, D)),
                  _const_spec((M_WIDTH, D)), _const_spec((D, D)), _const_spec((1, D))],
        out_specs=row(D),
        compiler_params=_cparams(("parallel",)),
        name="mixer_merge",
    )(gates, yr, ys, ym, h, wr, wglu, ws, wm, wo, g)


def _rwkv_prep_kernel(p_ref, aux_ref, shift_ref, mu_ref, wa_ref, w0_ref, a0_ref, g2_ref, kk_ref, ka_ref,
                      hsum_ref, r_ref, e_ref, k_ref, v_ref, an_ref, bn_ref, g_ref, *, T, tiles_per_seq):
    i = pl.program_id(0)
    p = p_ref[...]
    tm = p.shape[0]
    row = _iota((tm, 1), 0)
    rolled = pltpu.roll(p, 1, axis=0)
    if tiles_per_seq >= 1:
        b = i // tiles_per_seq
        tt = i - b * tiles_per_seq
        first = jnp.where(tt == 0, shift_ref[pl.ds(b % 8, 1), :], aux_ref[7:8, :])
        prev = jnp.where(row == 0, first, rolled)
    else:
        prev = jnp.where((row & (T - 1)) == 0, shift_ref[...], rolled)
    pm = p + mu_ref[...] * (prev - p)
    r = pm[:, 0:D]
    k = pm[:, D:2 * D]
    v = pm[:, 2 * D:3 * D]
    x_wa = pm[:, 3 * D:3 * D + 128]
    xg = pm[:, 3 * D + 128:3 * D + 256]
    lora_in = jnp.where(_iota(x_wa.shape, 1) < 64, jnp.tanh(x_wa), x_wa)
    lwa = _mm(lora_in, wa_ref[...])
    zw = w0_ref[...] + lwa[:, 0:D]
    w_log = -_softplus(-zw) - 0.5
    a = _sigmoid(a0_ref[...] + lwa[:, D:2 * D])
    g = _mm(_sigmoid(xg), g2_ref[...])
    kk = k * kk_ref[...]
    ss = _mm_sel_r(kk * kk, hsum_ref[...])
    kk = kk * lax.rsqrt(ss + 1e-12)
    r_ref[...] = r
    e_ref[...] = jnp.exp(w_log)
    k_ref[...] = k * (1.0 + (a - 1.0) * ka_ref[...])
    v_ref[...] = v
    an_ref[...] = -kk
    bn_ref[...] = kk * a
    g_ref[...] = g


def rwkv_prep(p, shift_prev, mu, wa, w0, a0, g2, k_k, k_a, hsum, *, T, tm=256):
    n, rc = p.shape
    tm = _row_tile(n, tm)
    if tm <= T:
        assert T % tm == 0 and tm % 8 == 0 and shift_prev.shape[0] % 8 == 0
        tps = T // tm
        aux_spec = pl.BlockSpec((8, rc), lambda i: (jnp.maximum(i * (tm // 8) - 1, 0), 0))
        shift_spec = pl.BlockSpec((8, rc), lambda i: ((i // tps) // 8, 0))
        shift_arg = shift_prev
    else:
        assert tm % T == 0 and (T & (T - 1)) == 0
        tps = 0
        aux_spec = pl.BlockSpec((8, rc), lambda i: (0, 0))
        shift_spec = pl.BlockSpec((tm, rc), lambda i: (i, 0))
        shift_arg = jnp.repeat(shift_prev, T, axis=0)
    row = lambda: pl.BlockSpec((tm, D), lambda i: (i, 0))
    outs = [jax.ShapeDtypeStruct((n, D), F32)] * 7
    return pl.pallas_call(
        functools.partial(_rwkv_prep_kernel, T=T, tiles_per_seq=tps),
        out_shape=outs,
        grid=(n // tm,),
        in_specs=[pl.BlockSpec((tm, rc), lambda i: (i, 0)),
                  aux_spec, shift_spec,
                  _const_spec((1, rc)),
                  _const_spec((128, 2 * D)),
                  _const_spec((1, D)), _const_spec((1, D)),
                  _const_spec((128, D)),
                  _const_spec((1, D)), _const_spec((1, D)),
                  _const_spec((D, D))],
        out_specs=[row() for _ in range(7)],
        compiler_params=_cparams(("parallel",)),
        name="rwkv_prep",
    )(p, p, shift_arg, mu, wa, w0, a0, g2, k_k, k_a, hsum)


def _rwkv_rec_kernel(r_ref, e_ref, k_ref, v_ref, an_ref, bn_ref, g_ref, s0_ref, rk_ref, lnw_ref, lnb_ref,
                     y_ref, sfin_ref, s_scr, *, C, G, n_steps):
    step = pl.program_id(2)
    C2 = 2 * C
    rng = range(G)
    lane = _iota((1, R_PAIR), 1)
    m0 = lane < R_HEAD
    ri = _iota((R_PAIR, R_PAIR), 0)
    ci = _iota((R_PAIR, R_PAIR), 1)
    bd = (ri < R_HEAD) == (ci < R_HEAD)
    hsum = jnp.where(bd, 1.0, 0.0).astype(_MXU)
    rt_i = _iota((C2, C2), 0)
    ct_i = _iota((C2, C2), 1)
    same = (rt_i < C) == (ct_i < C)
    tr = jnp.where(rt_i < C, rt_i, rt_i - C)
    ts = jnp.where(ct_i < C, ct_i, ct_i - C)
    m_sl = same & (ts < tr)
    m_li = same & (ts <= tr)
    eye2 = jnp.where(rt_i == ct_i, 1.0, 0.0)
    lvl_masks = []
    m = 1
    while m < C:
        lvl_masks.append(((tr & ~(2 * m - 1)) == (ts & ~(2 * m - 1))) & ((tr & m) != 0) & ((ts & m) == 0))
        m *= 2
    tril = jnp.where(_iota((C, C), 1) <= _iota((C, C), 0), 1.0, 0.0).astype(_MXU)
    eye_p = _eye(R_PAIR, _MXU)

    @pl.when(step == 0)
    def _():
        z = jnp.zeros((R_HEAD, R_HEAD), F32)
        for gi in rng:
            s_scr[gi] = jnp.concatenate([jnp.concatenate([s0_ref[gi, 0], z], axis=1),
                                         jnp.concatenate([z, s0_ref[gi, 1]], axis=1)], axis=0)

    def stack(x):
        return jnp.concatenate([jnp.where(m0, x, 0.0), jnp.where(m0, 0.0, x)], axis=0)

    def fold(xs):
        return xs[:C] + xs[C:]

    def each(f, *lists):
        return [f(*xs) for xs in zip(*lists)]

    r = [r_ref[gi] for gi in rng]
    e = [e_ref[gi] for gi in rng]
    k = [k_ref[gi] for gi in rng]
    v = [v_ref[gi] for gi in rng]
    an = [an_ref[gi] for gi in rng]
    bn = [bn_ref[gi] for gi in rng]
    cs = each(lambda x: _mm_sel_l(tril, x), e)
    cl = each(lambda x: x[C - 1:C, :], cs)
    at = each(lambda a_, e_, c_: a_ * jnp.exp(e_ - c_), an, e, cs)
    rt = each(lambda r_, c_: r_ * jnp.exp(-c_), r, cs)
    ecs = each(jnp.exp, cs)
    bt_ = each(lambda b_, x: b_ * x, bn, ecs)
    kt = each(lambda k_, x: k_ * x, k, ecs)
    ecl = each(lambda c_, l_: jnp.exp(c_ - l_), cs, cl)
    bh = each(lambda b_, x: b_ * x, bn, ecl)
    kh = each(lambda k_, x: k_ * x, k, ecl)
    As, Rs, Vs = each(stack, at), each(stack, rt), each(stack, v)
    b2 = each(lambda x: jnp.concatenate([x, x], axis=0), bt_)
    k2 = each(lambda x: jnp.concatenate([x, x], axis=0), kt)
    N = each(lambda a_, b_: jnp.where(m_sl, _mm_nt(a_, b_), 0.0), As, b2)
    Ak = each(lambda a_, b_: jnp.where(m_sl, _mm_nt(a_, b_), 0.0), As, k2)
    Arb = each(lambda a_, b_: jnp.where(m_li, _mm_nt(a_, b_), 0.0), Rs, b2)
    Ark = each(lambda a_, b_: jnp.where(m_li, _mm_nt(a_, b_), 0.0), Rs, k2)
    T = each(lambda n_: eye2 + jnp.where(lvl_masks[0], n_, 0.0), N)
    for lm in lvl_masks[1:]:
        LT = each(lambda n_, t_: _mm(jnp.where(lm, n_, 0.0), t_), N, T)
        T = each(lambda t_, x: t_ + _mm(t_, x), T, LT)
    AkV = each(_mm, Ak, Vs)
    X = each(lambda t_, a_, u_: _mm(t_, jnp.concatenate([a_, u_], axis=1)), T, As, AkV)
    Z = each(_mm, Arb, X)
    ArkV = each(_mm, Ark, Vs)
    Rp = each(lambda r_, z_: r_ + fold(z_[:, 0:R_PAIR]), rt, Z)
    Y0 = each(lambda z_, a_: fold(z_[:, R_PAIR:] + a_), Z, ArkV)
    Ap = each(lambda x: fold(x[:, 0:R_PAIR]), X)
    U0 = each(lambda x: fold(x[:, R_PAIR:]), X)
    ApT = each(lambda x: _mm_nt(eye_p, x), Ap)
    UVT = each(lambda u_, v_: _mm_nt(eye_p, jnp.concatenate([u_, v_], axis=0)), U0, v)
    P = each(lambda a_, b_: jnp.where(bd, _mm(a_, b_), 0.0), ApT, bh)
    Q = each(lambda u_, b_, k_: jnp.where(bd, _mm(u_, jnp.concatenate([b_, k_], axis=0)), 0.0), UVT, bh, kh)
    S = [s_scr[gi] for gi in rng]
    y = each(lambda y_, r_, s_: y_ + _mm_nt(r_, s_), Y0, Rp, S)
    S = each(lambda s_, l_, p_, q_: s_ * jnp.exp(-l_) + _mm(s_, p_) + q_, S, cl, P, Q)
    for gi in rng:
        s_scr[gi] = S[gi]
    rk = rk_ref[...]
    stat = each(lambda y_, r_, k_: _mm_sel_r(jnp.concatenate([y_, r_ * k_ * rk], axis=0), hsum), y, r, k)
    dlt = each(lambda y_, s_: y_ - s_[:C] * (1.0 / R_HEAD), y, stat)
    var = each(lambda d_: _mm_sel_r(d_ * d_, hsum) * (1.0 / R_HEAD), dlt)
    lnw = lnw_ref[...]
    lnb = lnb_ref[...]
    for gi in rng:
        yn = dlt[gi] * lax.rsqrt(var[gi] + R_LN_EPS) * lnw + lnb
        y_ref[gi] = (yn + stat[gi][C:] * v[gi]) * g_ref[gi]

    @pl.when(step == n_steps - 1)
    def _():
        for gi in rng:
            sfin_ref[gi, 0] = S[gi][0:R_HEAD, 0:R_HEAD]
            sfin_ref[gi, 1] = S[gi][R_HEAD:, R_HEAD:]


def rwkv_recurrence(r, e, k, v, an, bn, g, s0, l, r_k, ln_w, ln_b, *, C, G):
    B, T, _ = r.shape
    n_steps = T // C
    n_pairs = D // R_PAIR
    blk = lambda: pl.BlockSpec((G, C, R_PAIR), lambda b, p, s: (b, s, p))
    par = lambda: pl.BlockSpec((1, R_PAIR), lambda b, p, s: (0, p))
    y, sfin = pl.pallas_call(
        functools.partial(_rwkv_rec_kernel, C=C, G=G, n_steps=n_steps),
        out_shape=[jax.ShapeDtypeStruct((B, T, D), F32),
                   jax.ShapeDtypeStruct((B, 2 * n_pairs, R_HEAD, R_HEAD), F32)],
        grid=(B // G, n_pairs, n_steps),
        in_specs=[blk() for _ in range(7)]
        + [pl.BlockSpec((None, G, 2, R_HEAD, R_HEAD), lambda b, p, s: (l, b, p, 0, 0)),
           par(), par(), par()],
        out_specs=[blk(), pl.BlockSpec((G, 2, R_HEAD, R_HEAD), lambda b, p, s: (b, p, 0, 0))],
        scratch_shapes=[pltpu.VMEM((G, R_PAIR, R_PAIR), F32)],
        compiler_params=_cparams(("parallel", "parallel", "arbitrary")),
        name="rwkv_rec",
    )(r, e, k, v, an, bn, g, s0, r_k, ln_w, ln_b)
    return y, sfin


def _s5_kernel(u_ref, h0r_ref, h0i_ref, ar_ref, ai_ref, bre_ref, bim_ref, cre_ref, cim_ref, d_ref, perm_ref,
               permt_ref, y_ref, hr_out, hi_out, inr_scr, ini_scr, hr_scr, hi_scr, *, tc, nb, n_steps):
    step = pl.program_id(1)
    rows = nb * tc

    @pl.when(step == 0)
    def _():
        hr_scr[...] = h0r_ref[...]
        hi_scr[...] = h0i_ref[...]

    u = u_ref[...].reshape(rows, S5_BLK_CH)
    ut = jnp.dot(perm_ref[...], u.astype(_MXU), preferred_element_type=F32).astype(_MXU)
    inr_scr[...] = jnp.dot(ut, bre_ref[0], preferred_element_type=F32)
    ini_scr[...] = jnp.dot(ut, bim_ref[0], preferred_element_type=F32)
    ar = ar_ref[...]
    ai = ai_ref[...]

    def body(t, carry):
        hr, hi = carry
        sl = pl.ds(pl.multiple_of(t * nb, nb), nb)
        nr = ar * hr - ai * hi + inr_scr[sl, :]
        ni = ar * hi + ai * hr + ini_scr[sl, :]
        inr_scr[sl, :] = nr
        ini_scr[sl, :] = ni
        return nr, ni

    hr, hi = lax.fori_loop(0, tc, body, (hr_scr[...], hi_scr[...]))
    hr_scr[...] = hr
    hi_scr[...] = hi
    yt = (jnp.dot(inr_scr[...].astype(_MXU), cre_ref[0], preferred_element_type=F32)
          - jnp.dot(ini_scr[...].astype(_MXU), cim_ref[0], preferred_element_type=F32))
    y = _mm_sel_l(permt_ref[...], yt) + d_ref[...] * u
    y = 0.5 * y * (1.0 + jnp.tanh(math.sqrt(2.0 / math.pi) * (y + 0.044715 * (y * y * y))))
    y_ref[...] = y.reshape(y_ref.shape)

    @pl.when(step == n_steps - 1)
    def _():
        hr_out[...] = hr
        hi_out[...] = hi


def s5_scan(u3, h0r, h0i, l, ar, ai, bre, bim, cre, cim, d, *, nb, tc, blk):
    A, R, _ = u3.shape
    ga, tr = blk
    assert A == ga and ga * tr == nb * tc
    nblk = D // S5_BLK_CH
    n_steps = R // tr
    rows = nb * tc
    j = np.arange(rows)
    perm_np = np.zeros((rows, rows), np.float32)
    perm_np[(j % tc) * nb + j // tc, j] = 1.0
    perm = jnp.asarray(perm_np, _MXU)
    permt = jnp.asarray(perm_np.T, _MXU)
    st = lambda: pl.BlockSpec((None, nb, S5_BLK_ST), lambda c, s: (l, 0, c))
    vec = lambda w: pl.BlockSpec((1, w), lambda c, s: (0, c))
    mat = lambda a, b: pl.BlockSpec((1, a, b), lambda c, s: (c, 0, 0))
    return pl.pallas_call(
        functools.partial(_s5_kernel, tc=tc, nb=nb, n_steps=n_steps),
        out_shape=[jax.ShapeDtypeStruct(u3.shape, F32),
                   jax.ShapeDtypeStruct((nb, nblk * S5_BLK_ST), F32),
                   jax.ShapeDtypeStruct((nb, nblk * S5_BLK_ST), F32)],
        grid=(nblk, n_steps),
        in_specs=[pl.BlockSpec((ga, tr, S5_BLK_CH), lambda c, s: (0, s, c)),
                  st(), st(), vec(S5_BLK_ST), vec(S5_BLK_ST),
                  mat(S5_BLK_CH, S5_BLK_ST), mat(S5_BLK_CH, S5_BLK_ST),
                  mat(S5_BLK_ST, S5_BLK_CH), mat(S5_BLK_ST, S5_BLK_CH),
                  vec(S5_BLK_CH), _const_spec((rows, rows)), _const_spec((rows, rows))],
        out_specs=[pl.BlockSpec((ga, tr, S5_BLK_CH), lambda c, s: (0, s, c)),
                   pl.BlockSpec((nb, S5_BLK_ST), lambda c, s: (0, c)),
                   pl.BlockSpec((nb, S5_BLK_ST), lambda c, s: (0, c))],
        scratch_shapes=[pltpu.VMEM((rows, S5_BLK_ST), F32), pltpu.VMEM((rows, S5_BLK_ST), F32),
                        pltpu.VMEM((nb, S5_BLK_ST), F32), pltpu.VMEM((nb, S5_BLK_ST), F32)],
        compiler_params=_cparams(("parallel", "arbitrary")),
        name="s5_scan",
    )(u3, h0r, h0i, ar, ai, bre, bim, cre, cim, d, perm, permt)


def _s5_discretize(a_re, a_im, log_dt, b_re, b_im, c_re, c_im):
    g, p, hch = b_re.shape
    dt = jnp.exp(log_dt.astype(F32))[:, None]
    mag = jnp.exp(dt * a_re)
    abar_re = mag * jnp.cos(dt * a_im)
    abar_im = mag * jnp.sin(dt * a_im)
    den = a_re * a_re + a_im * a_im
    nr = abar_re - 1.0
    q_re = (nr * a_re + abar_im * a_im) / den
    q_im = (abar_im * a_re - nr * a_im) / den
    bb_re = q_re[..., None] * b_re - q_im[..., None] * b_im
    bb_im = q_re[..., None] * b_im + q_im[..., None] * b_re
    nblk = D // S5_BLK_CH
    gl = g // nblk
    eye = jnp.eye(gl, dtype=F32)

    def in_blocks(bb):
        t = jnp.transpose(bb, (0, 2, 1)).reshape(nblk, gl, hch, p)
        return jnp.einsum('cghp,gk->cghkp', t, eye).reshape(nblk, gl * hch, gl * p).astype(_MXU)

    def out_blocks(cc):
        t = jnp.transpose(cc, (0, 2, 1)).reshape(nblk, gl, p, hch)
        return jnp.einsum('cgph,gk->cgpkh', t, eye).reshape(nblk, gl * p, gl * hch).astype(_MXU)

    return (abar_re.reshape(1, g * p), abar_im.reshape(1, g * p), in_blocks(bb_re), in_blocks(bb_im),
            out_blocks(c_re), out_blocks(c_im))


def _ssd_kernel(xbc_ref, z_ref, dt_ref, cprev_ref, h0_ref, cw_ref, cb_ref, dtb_ref, alog_ref, dx_ref, ng_ref,
                eh_ref, y_ref, hfin_ref, ext_scr, h_scr, *, L, t_real, n_steps):
    step = pl.program_id(1)

    @pl.when(step == 0)
    def _():
        ext_scr[0:8, :] = jnp.zeros((8, M_CONV_DIM), F32)
        ext_scr[5:8, :] = cprev_ref[0]
        for gi in range(M_GROUPS):
            h_scr[gi] = h0_ref[0, gi * M_HPG:(gi + 1) * M_HPG].reshape(M_GW, M_STATE)

    ext_scr[8:8 + L, :] = xbc_ref[...]
    conv = cb_ref[...] + cw_ref[0:1, :] * ext_scr[pl.ds(5, L), :]
    for j in range(1, 4):
        conv = conv + cw_ref[j:j + 1, :] * ext_scr[pl.ds(5 + j, L), :]
    tail = ext_scr[pl.ds(L + 5, 3), :]
    ext_scr[5:8, :] = tail
    conv = conv * _sigmoid(conv)
    xs = conv[:, 0:M_WIDTH]
    dt = _softplus(dt_ref[...] + dtb_ref[...])
    if t_real < L:
        dt = jnp.where(_iota(dt.shape, 0) < t_real, dt, 0.0)
    a = -jnp.exp(alog_ref[...])
    adt = dt * a
    tril = jnp.where(_iota((L, L), 1) <= _iota((L, L), 0), 1.0, 0.0).astype(_MXU)
    causal = _iota((L, L), 1) <= _iota((L, L), 0)
    acum = _mm_sel_l(tril, adt)
    acum_t = _mm_nt_sel_l(_eye(128, _MXU), acum)
    eh = eh_ref[...]
    dt_x = _mm_sel_r(dt, eh)
    acum_x = _mm_sel_r(acum, eh)
    acl_x = acum_x[L - 1:L, :]
    xd = xs * dt_x
    xdd = xd * jnp.exp(acl_x - acum_x)
    eacum_x = jnp.exp(acum_x)
    cdec_x = jnp.exp(acl_x)
    cdec8 = jnp.broadcast_to(cdec_x, (8, M_WIDTH))
    eye_gw = _eye(M_GW, _MXU)
    lane_head = _iota((1, M_GW), 1) // M_HEAD
    ys = []
    for gi in range(M_GROUPS):
        gs = slice(gi * M_GW, (gi + 1) * M_GW)
        bg = conv[:, M_WIDTH + gi * M_STATE:M_WIDTH + (gi + 1) * M_STATE]
        cg = conv[:, M_WIDTH + M_GROUPS * M_STATE + gi * M_STATE:M_WIDTH + M_GROUPS * M_STATE + (gi + 1) * M_STATE]
        h = h_scr[gi]
        cbm = _mm_nt(cg, bg)
        xd_g = xd[:, gs]
        y_g = _mm_nt(cg, h) * eacum_x[:, gs]
        for e in range(M_HPG):
            he = gi * M_HPG + e
            seg = acum[:, he:he + 1] - acum_t[he:he + 1, :]
            dec = jnp.exp(jnp.where(causal, seg, -jnp.inf))
            y_g = y_g + _mm(cbm * dec, jnp.where(lane_head == e, xd_g, 0.0))
        ys.append(y_g)
        xdd_t = _mm_nt(eye_gw, xdd[:, gs])
        dcol = _mm_nt_sel_l(eye_gw, cdec8[:, gs])[:, 0:1]
        h_scr[gi] = h * dcol + _mm(xdd_t, bg)
    y = jnp.concatenate(ys, axis=1) + dx_ref[...] * xs
    zz = z_ref[...]
    y = y * (zz * _sigmoid(zz))
    outs = []
    for gi in range(M_GROUPS):
        yg = y[:, gi * M_GW:(gi + 1) * M_GW]
        outs.append(yg * lax.rsqrt(jnp.mean(yg * yg, axis=-1, keepdims=True) + EPS))
    y_ref[...] = jnp.concatenate(outs, axis=1) * ng_ref[...]

    @pl.when(step == n_steps - 1)
    def _():
        for gi in range(M_GROUPS):
            hfin_ref[0, gi * M_HPG:(gi + 1) * M_HPG] = h_scr[gi].reshape(M_HPG, M_HEAD, M_STATE)


def ssd_block(xbc, z, dt, conv_prev, h0, l, conv_w, conv_b, dt_bias, a_log, d_x, norm_g, eh, *, B, T, L, t_real):
    n_steps = T // L
    heads = M_GROUPS * M_HPG
    seq = lambda w: pl.BlockSpec((L, w), lambda b, s: (b * n_steps + s, 0))
    return pl.pallas_call(
        functools.partial(_ssd_kernel, L=L, t_real=t_real, n_steps=n_steps),
        out_shape=[jax.ShapeDtypeStruct((B * T, M_WIDTH), F32),
                   jax.ShapeDtypeStruct((B, heads, M_HEAD, M_STATE), F32)],
        grid=(B, n_steps),
        in_specs=[seq(M_CONV_DIM), seq(M_WIDTH), seq(128),
                  pl.BlockSpec((None, 1, 3, M_CONV_DIM), lambda b, s: (l, b, 0, 0)),
                  pl.BlockSpec((None, 1, heads, M_HEAD, M_STATE), lambda b, s: (l, b, 0, 0, 0)),
                  _const_spec((4, M_CONV_DIM)), _const_spec((1, M_CONV_DIM)),
                  _const_spec((1, 128)), _const_spec((1, 128)),
                  _const_spec((1, M_WIDTH)), _const_spec((1, M_WIDTH)),
                  _const_spec((128, M_WIDTH))],
        out_specs=[seq(M_WIDTH),
                   pl.BlockSpec((1, heads, M_HEAD, M_STATE), lambda b, s: (b, 0, 0, 0))],
        scratch_shapes=[pltpu.VMEM((L + 8, M_CONV_DIM), F32),
                        pltpu.VMEM((M_GROUPS, M_GW, M_STATE), F32)],
        compiler_params=_cparams(("parallel", "arbitrary")),
        name="ssd",
    )(xbc, z, dt, conv_prev, h0, conv_w, conv_b, dt_bias, a_log, d_x, norm_g, eh)


def _attn_kernel(q_ref, k_ref, v_ref, o_ref, *, nb, tq):
    scale = XA_HD ** -0.5
    for j in range(nb):
        outs = []
        for hd in range(XA_HEADS):
            cs = slice(hd * XA_HD, (hd + 1) * XA_HD)
            q = q_ref[j * tq:(j + 1) * tq, cs]
            s = _mm_nt(q, k_ref[j, :, cs]) * scale
            s = s - jnp.max(s, axis=-1, keepdims=True)
            p = jnp.exp(s)
            p = p / jnp.sum(p, axis=-1, keepdims=True)
            outs.append(_mm(p, v_ref[j, :, cs]))
        o_ref[j * tq:(j + 1) * tq, :] = jnp.concatenate(outs, axis=1)


def cross_attention(q, mk, mv, kv_index, *, B, T, tq, nb):
    m = mk.shape[-2]
    nlead = len(kv_index)
    n_t = T // tq
    kv_spec = pl.BlockSpec((None,) * nlead + (nb, m, D), lambda b, s: tuple(kv_index) + (b, 0, 0))
    q_spec = pl.BlockSpec((nb * tq, D), lambda b, s: (b * n_t + s, 0))
    return pl.pallas_call(
        functools.partial(_attn_kernel, nb=nb, tq=tq),
        out_shape=jax.ShapeDtypeStruct((B * T, D), F32),
        grid=(B // nb, n_t),
        in_specs=[q_spec, kv_spec, kv_spec],
        out_specs=q_spec,
        compiler_params=_cparams(("parallel", "arbitrary")),
        name="xattn",
    )(q, mk, mv)


def _pad_time(x2, B, T, Tp):
    if Tp == T:
        return x2
    w = x2.shape[-1]
    return jnp.pad(x2.reshape(B, T, w), ((0, 0), (0, Tp - T), (0, 0))).reshape(B * Tp, w)


def _layer(l, h, grp, W, st):
    T, B = grp['T'], grp['B']
    n = T * B
    gates = norm_matmul(h, W['g_mix_pre'], W['w_gates'], name="in_gates")
    p_r = norm_matmul(h, W['g_mix_pre'], W['w_rwkv'], name="in_rwkv")
    u_s5 = norm_matmul(h, W['g_mix_pre'], W['w_s5'], name="in_s5")
    z_m = norm_matmul(h, W['g_mix_pre'], W['w_z'], name="in_z")
    xbc = norm_matmul(h, W['g_mix_pre'], W['w_xbc'], name="in_xbc")
    dt_m = norm_matmul(h, W['g_mix_pre'], W['w_dt'], tn=128, name="in_dt")

    seqs = rwkv_prep(p_r, st['shift'][l], W['mu'], W['wa'], W['w0'], W['a0'], W['g2'],
                     W['k_k'], W['k_a'], W['hsum'], T=T)
    Tp = grp['rwkv_Tpad']
    seqs = [_pad_time(x, B, T, Tp).reshape(B, Tp, D) for x in seqs]
    yr, s_fin = rwkv_recurrence(*seqs, st['rwkv'], l, W['r_k'], W['ln_w'], W['ln_b'],
                                C=grp['rwkv_C'], G=grp['rwkv_G'])
    yr = yr[:, :T].reshape(n, D)
    shift_new = p_r.reshape(B, T, -1)[:, T - 1]

    u3 = u_s5.reshape(grp['s5_view'])
    ys, s5r, s5i = s5_scan(u3, st['s5r'], st['s5i'], l, W['s5_ar'], W['s5_ai'], W['s5_bre'], W['s5_bim'],
                           W['s5_cre'], W['s5_cim'], W['s5_d'], nb=B, tc=grp['s5_tc'], blk=grp['s5_blk'])
    ys = ys.reshape(n, D)

    Lc = grp['ssd_L']
    Tm = grp['ssd_Tpad']
    ym, ssm_fin = ssd_block(_pad_time(xbc, B, T, Tm), _pad_time(z_m, B, T, Tm), _pad_time(dt_m, B, T, Tm),
                            st['conv'], st['ssm'], l, W['conv_w'], W['conv_b'], W['dt_bias'],
                            W['a_log'], W['d_x'], W['m_norm'], W['eh'], B=B, T=Tm, L=Lc, t_real=min(T, Lc))
    ym = ym.reshape(B, Tm, M_WIDTH)[:, :T].reshape(n, M_WIDTH)
    conv_new = jnp.concatenate([st['conv'][l], xbc.reshape(B, T, M_CONV_DIM)[:, max(T - 3, 0):]], axis=1)[:, -3:]

    h = mixer_merge(gates, yr, ys, ym, h, W['w_out_rwkv'], W['s5_w_glu'], W['w_out_s5'], W['w_out_mamba'],
                    W['w_out'], W['g_mix_post'])

    q = norm_matmul(h, W['g_xa_pre'], W['xa_wq'], name="xa_q")
    o = cross_attention(q, st['mk'], st['mv'], st['kv_index'](l), B=B, T=T, tq=grp['xa_tq'], nb=grp['xa_nb'])
    h = matmul_norm_residual(o, W['xa_wo'], W['g_xa_post'], h, name="xa_out")

    h = mlp_block(h, W['g_mlp_pre'], W['mlp_w1'], W['mlp_w2'], W['g_mlp_post'])
    return h, (s_fin, shift_new, s5r.reshape(B, 64, 64), s5i.reshape(B, 64, 64), conv_new, ssm_fin)


def _group_cfg(T, B):
    cfg = dict(T=T, B=B)
    g = 8 if B % 8 == 0 else 1
    if T % 64 == 0:
        cfg.update(rwkv_C=64, rwkv_Tpad=T, rwkv_G=g)
    else:
        tp = -(-T // 8) * 8
        cfg.update(rwkv_C=tp, rwkv_Tpad=tp, rwkv_G=16 if B % 16 == 0 else g)
    if T % 8 == 0 and (512 // B) >= 8 and T % (512 // B) == 0:
        tc = 512 // B
        cfg.update(s5_view=(B, T, D), s5_tc=tc, s5_blk=(B, tc))
    else:
        cfg.update(s5_view=(1, B * T, D), s5_tc=T, s5_blk=(1, B * T))
    if T % 128 == 0:
        cfg.update(ssd_L=128, ssd_Tpad=T)
    else:
        tp = -(-T // 8) * 8
        cfg.update(ssd_L=tp, ssd_Tpad=tp)
    if T >= 64:
        cfg.update(xa_tq=_row_tile(T, 512), xa_nb=1)
    else:
        cfg.update(xa_tq=T, xa_nb=min(B, 8))
    return cfg


def kernel(x_prompt, x_sample, cache_mem_k, cache_mem_v, state_rwkv, state_rwkv_shift, state_s5_re, state_s5_im, state_conv, state_ssm, mem_prompt, norm_mix_pre, norm_mix_post, norm_xa_pre, norm_xa_post, norm_mlp_pre, norm_mlp_post, norm_mem, w_in, w_out, rwkv_mu, rwkv_w0, rwkv_w2, rwkv_a0, rwkv_a2, rwkv_g2, rwkv_k_k, rwkv_k_a, rwkv_r_k, rwkv_ln_w, rwkv_ln_b, w_out_rwkv, s5_a_re, s5_a_im, s5_log_dt, s5_b_re, s5_b_im, s5_c_re, s5_c_im, s5_d, s5_w_glu, w_out_s5, m_conv_w, m_conv_b, m_dt_bias, m_a_log, m_d, m_norm, w_out_mamba, xa_wq, xa_wk, xa_wv, xa_wo, mlp_w1, mlp_w2):
    depth = w_in.shape[0]
    bp, tp, _ = x_prompt.shape
    bs, ts, _ = x_sample.shape
    mlen = mem_prompt.shape[1]
    bf = lambda x: x.astype(_MXU)
    row = lambda x: x.reshape(1, -1).astype(F32)

    ch = jnp.arange(D)
    hsum = (ch[:, None] // R_HEAD == ch[None, :] // R_HEAD).astype(_MXU)
    eh = (jnp.arange(128)[:, None] == (jnp.arange(M_WIDTH)[None, :] // M_HEAD)).astype(_MXU)

    col = [0]
    for sz in (3 * D, 3 * D + 256, D, M_WIDTH, M_CONV_DIM, 32):
        col.append(col[-1] + sz)
    layers = []
    for l in range(depth):
        wl = w_in[l]
        ar, ai, bre, bim, cre, cim = _s5_discretize(s5_a_re[l], s5_a_im[l], s5_log_dt[l], s5_b_re[l], s5_b_im[l],
                                                    s5_c_re[l], s5_c_im[l])
        zero = jnp.zeros((64, D), F32)
        wa = jnp.concatenate([jnp.concatenate([rwkv_w2[l], zero], axis=1),
                              jnp.concatenate([zero, rwkv_a2[l]], axis=1)], axis=0)
        pad32 = lambda x: jnp.pad(x.reshape(1, -1).astype(F32), ((0, 0), (0, 128 - x.shape[-1])))
        layers.append(dict(
            g_mix_pre=row(norm_mix_pre[l]), g_mix_post=row(norm_mix_post[l]),
            g_xa_pre=row(norm_xa_pre[l]), g_xa_post=row(norm_xa_post[l]),
            g_mlp_pre=row(norm_mlp_pre[l]), g_mlp_post=row(norm_mlp_post[l]),
            w_gates=bf(wl[:, col[0]:col[1]]), w_rwkv=bf(wl[:, col[1]:col[2]]), w_s5=bf(wl[:, col[2]:col[3]]),
            w_z=bf(wl[:, col[3]:col[4]]), w_xbc=bf(wl[:, col[4]:col[5]]),
            w_dt=bf(jnp.pad(wl[:, col[5]:col[6]], ((0, 0), (0, 96)))),
            mu=row(rwkv_mu[l]), wa=bf(wa), w0=row(rwkv_w0[l]), a0=row(rwkv_a0[l]), g2=bf(rwkv_g2[l]),
            k_k=row(rwkv_k_k[l]), k_a=row(rwkv_k_a[l]), hsum=hsum, r_k=row(rwkv_r_k[l]),
            ln_w=row(rwkv_ln_w[l]), ln_b=row(rwkv_ln_b[l]),
            s5_ar=ar, s5_ai=ai, s5_bre=bre, s5_bim=bim, s5_cre=cre, s5_cim=cim, s5_d=row(s5_d[l]),
            conv_w=m_conv_w[l].astype(F32), conv_b=row(m_conv_b[l]), dt_bias=pad32(m_dt_bias[l]),
            a_log=pad32(m_a_log[l]), d_x=row(jnp.repeat(m_d[l], M_HEAD)), m_norm=row(m_norm[l]), eh=eh,
            w_out_rwkv=bf(w_out_rwkv[l]), s5_w_glu=bf(s5_w_glu[l]), w_out_s5=bf(w_out_s5[l]),
            w_out_mamba=bf(w_out_mamba[l]), w_out=bf(w_out[l]),
            xa_wq=bf(xa_wq[l]), xa_wk=bf(xa_wk[l]), xa_wv=bf(xa_wv[l]), xa_wo=bf(xa_wo[l]),
            mlp_w1=bf(mlp_w1[l]), mlp_w2=bf(mlp_w2[l]), g_mem=row(norm_mem[l]),
        ))

    hp = x_prompt.reshape(bp * tp, D).astype(F32)
    hs = x_sample.reshape(bs * ts, D).astype(F32)
    mem2 = mem_prompt.reshape(bp * mlen, D).astype(F32)

    cfg_p = _group_cfg(tp, bp)
    cfg_s = _group_cfg(ts, bs)
    zeros_p = dict(
        rwkv=jnp.zeros((depth, bp) + state_rwkv.shape[2:], F32),
        shift=jnp.zeros((depth, bp, state_rwkv_shift.shape[-1]), F32),
        s5r=jnp.zeros((depth, bp, state_s5_re.shape[2] * state_s5_re.shape[3]), F32),
        s5i=jnp.zeros((depth, bp, state_s5_re.shape[2] * state_s5_re.shape[3]), F32),
        conv=jnp.zeros((depth, bp) + state_conv.shape[2:], F32),
        ssm=jnp.zeros((depth, bp) + state_ssm.shape[2:], F32),
    )
    st_s = dict(
        rwkv=state_rwkv, shift=state_rwkv_shift,
        s5r=state_s5_re.reshape(depth, bs, -1), s5i=state_s5_im.reshape(depth, bs, -1),
        conv=state_conv, ssm=state_ssm,
        mk=cache_mem_k.reshape(depth, bs, mlen, D), mv=cache_mem_v.reshape(depth, bs, mlen, D),
        kv_index=lambda l: (l,),
    )

    mk_out, mv_out = [], []
    st_p_out = [[] for _ in range(6)]
    st_s_out = [[] for _ in range(6)]
    for l in range(depth):
        W = layers[l]
        mk = norm_matmul(mem2, W['g_mem'], W['xa_wk'], name="mem_k")
        mv = norm_matmul(mem2, W['g_mem'], W['xa_wv'], name="mem_v")
        st_p = dict(zeros_p, mk=mk.reshape(bp, mlen, D), mv=mv.reshape(bp, mlen, D), kv_index=lambda l: ())
        hp, new_p = _layer(l, hp, cfg_p, W, st_p)
        hs, new_s = _layer(l, hs, cfg_s, W, st_s)
        mk_out.append(mk.reshape(bp, mlen, XA_HEADS, XA_HD))
        mv_out.append(mv.reshape(bp, mlen, XA_HEADS, XA_HD))
        for i in range(6):
            st_p_out[i].append(new_p[i])
            st_s_out[i].append(new_s[i])

    stk = lambda xs: jnp.stack(xs).astype(F32)
    y_p = hp.reshape(bp, tp, D).astype(x_prompt.dtype)
    y_s = hs.reshape(bs, ts, D).astype(x_sample.dtype)
    return (y_p, y_s, stk(mk_out), stk(mv_out),
            *[stk(st_p_out[i]) for i in range(6)],
            *[stk(st_s_out[i]) for i in range(6)])
```

```python
import functools
import math

import numpy as np
import jax
import jax.numpy as jnp
from jax import lax
from jax.experimental import pallas as pl
from jax.experimental.pallas import tpu as pltpu

F32 = jnp.float32
_MXU = jnp.bfloat16
EPS = 1e-6
R_LN_EPS = 64e-5
D = 1024
R_HEAD = 64
R_PAIR = 2 * R_HEAD
S5_BLK_CH = 128
S5_BLK_ST = 512
M_HEAD = 64
M_STATE = 128
M_GROUPS = 4
M_HPG = 8
M_GW = M_HPG * M_HEAD
M_WIDTH = 2048
M_CONV_DIM = 3072
XA_HEADS = 4
XA_HD = 256
VMEM_LIMIT = 56 * 1024 * 1024


def _mm(a, b):
    return jnp.dot(a.astype(_MXU), b.astype(_MXU), preferred_element_type=F32)


def _mm_nt(a, b):
    return lax.dot_general(a.astype(_MXU), b.astype(_MXU), (((1,), (1,)), ((), ())),
                           preferred_element_type=F32)


def _mm_tn(a, b):
    return lax.dot_general(a.astype(_MXU), b.astype(_MXU), (((0,), (0,)), ((), ())),
                           preferred_element_type=F32)


def _split3(x):
    hi = x.astype(_MXU)
    r1 = x - hi.astype(F32)
    mid = r1.astype(_MXU)
    lo = (r1 - mid.astype(F32)).astype(_MXU)
    return hi, mid, lo


def _mm_sel_l(sel, x):
    return sum(jnp.dot(sel, p, preferred_element_type=F32) for p in _split3(x))


def _mm_sel_r(x, sel):
    return sum(jnp.dot(p, sel, preferred_element_type=F32) for p in _split3(x))


def _mm_nt_sel_l(sel, x):
    return sum(lax.dot_general(sel, p, (((1,), (1,)), ((), ())), preferred_element_type=F32)
               for p in _split3(x))


def _sigmoid(x):
    return 1.0 / (1.0 + jnp.exp(-x))


def _softplus(x):
    return jnp.maximum(x, 0.0) + jnp.log(1.0 + jnp.exp(-jnp.abs(x)))


def _rms(x, g):
    return x * lax.rsqrt(jnp.mean(x * x, axis=-1, keepdims=True) + EPS) * g


def _iota(shape, dim):
    return lax.broadcasted_iota(jnp.int32, shape, dim)


def _eye(n, dtype):
    return jnp.where(_iota((n, n), 0) == _iota((n, n), 1), 1.0, 0.0).astype(dtype)


def _cparams(sem):
    return pltpu.CompilerParams(dimension_semantics=sem, vmem_limit_bytes=VMEM_LIMIT)


def _row_tile(n, want):
    t = min(n, want)
    while n % t:
        t //= 2
    return t


def _const_spec(shape):
    nd = len(shape)
    return pl.BlockSpec(shape, lambda *_: (0,) * nd)


def _norm_mm_kernel(x_ref, g_ref, w_ref, o_ref, xn_ref):
    @pl.when(pl.program_id(1) == 0)
    def _():
        xn_ref[...] = _rms(x_ref[...], g_ref[...]).astype(_MXU)

    o_ref[...] = jnp.dot(xn_ref[...], w_ref[...], preferred_element_type=F32)


def norm_matmul(x, g, w, *, tm=1024, tn=512, name="norm_mm"):
    n, d = x.shape
    c = w.shape[1]
    tm = _row_tile(n, tm)
    tn = _row_tile(c, tn)
    return pl.pallas_call(
        _norm_mm_kernel,
        out_shape=jax.ShapeDtypeStruct((n, c), F32),
        grid=(n // tm, c // tn),
        in_specs=[pl.BlockSpec((tm, d), lambda i, j: (i, 0)),
                  pl.BlockSpec((1, d), lambda i, j: (0, 0)),
                  pl.BlockSpec((d, tn), lambda i, j: (0, j))],
        out_specs=pl.BlockSpec((tm, tn), lambda i, j: (i, j)),
        scratch_shapes=[pltpu.VMEM((tm, d), _MXU)],
        compiler_params=_cparams(("parallel", "arbitrary")),
        name=name,
    )(x, g, w)


def _mm_norm_res_kernel(x_ref, w_ref, g_ref, h_ref, o_ref):
    y = _mm(x_ref[...], w_ref[...])
    o_ref[...] = h_ref[...] + _rms(y, g_ref[...])


def matmul_norm_residual(x, w, g, h, *, tm=512, name="mm_norm_res"):
    n, k = x.shape
    d = w.shape[1]
    tm = _row_tile(n, tm)
    return pl.pallas_call(
        _mm_norm_res_kernel,
        out_shape=jax.ShapeDtypeStruct((n, d), F32),
        grid=(n // tm,),
        in_specs=[pl.BlockSpec((tm, k), lambda i: (i, 0)),
                  _const_spec((k, d)),
                  _const_spec((1, d)),
                  pl.BlockSpec((tm, d), lambda i: (i, 0))],
        out_specs=pl.BlockSpec((tm, d), lambda i: (i, 0)),
        compiler_params=_cparams(("parallel",)),
        name=name,
    )(x, w, g, h)


def _mlp_kernel(h_ref, g1_ref, w1_ref, w2_ref, g2_ref, o_ref, *, n_chunks, ck):
    h = h_ref[...]
    xn = _rms(h, g1_ref[...]).astype(_MXU)
    acc = jnp.zeros(h.shape, F32)
    for j in range(n_chunks):
        a = jnp.dot(xn, w1_ref[:, j * ck:(j + 1) * ck], preferred_element_type=F32)
        a = jnp.square(jnp.maximum(a, 0.0))
        acc = acc + jnp.dot(a.astype(_MXU), w2_ref[j * ck:(j + 1) * ck, :], preferred_element_type=F32)
    o_ref[...] = h + _rms(acc, g2_ref[...])


def mlp_block(h, g1, w1, w2, g2, *, tm=512, ck=1024):
    n, d = h.shape
    f = w1.shape[1]
    tm = _row_tile(n, tm)
    return pl.pallas_call(
        functools.partial(_mlp_kernel, n_chunks=f // ck, ck=ck),
        out_shape=jax.ShapeDtypeStruct((n, d), F32),
        grid=(n // tm,),
        in_specs=[pl.BlockSpec((tm, d), lambda i: (i, 0)),
                  _const_spec((1, d)),
                  _const_spec((d, f)),
                  _const_spec((f, d)),
                  _const_spec((1, d))],
        out_specs=pl.BlockSpec((tm, d), lambda i: (i, 0)),
        compiler_params=_cparams(("parallel",)),
        name="mlp",
    )(h, g1, w1, w2, g2)


def _merge_kernel(gates_ref, yr_ref, ys_ref, ym_ref, h_ref, wr_ref, wglu_ref, ws_ref, wm_ref, wo_ref, g_ref,
                  o_ref):
    o_r = _mm(yr_ref[...], wr_ref[...])
    ys = ys_ref[...]
    y3 = ys * _sigmoid(_mm(ys, wglu_ref[...]))
    o_s = _mm(y3, ws_ref[...])
    o_m = _mm(ym_ref[...], wm_ref[...])
    merged = (_sigmoid(gates_ref[:, 0:D]) * o_r + _sigmoid(gates_ref[:, D:2 * D]) * o_s
              + _sigmoid(gates_ref[:, 2 * D:3 * D]) * o_m)
    mix = _mm(merged, wo_ref[...])
    o_ref[...] = h_ref[...] + _rms(mix, g_ref[...])


def mixer_merge(gates, yr, ys, ym, h, wr, wglu, ws, wm, wo, g, *, tm=256):
    n = h.shape[0]
    tm = _row_tile(n, tm)
    row = lambda w: pl.BlockSpec((tm, w), lambda i: (i, 0))
    return pl.pallas_call(
        _merge_kernel,
        out_shape=jax.ShapeDtypeStruct((n, D), F32),
        grid=(n // tm,),
        in_specs=[row(3 * D), row(D), row(D), row(M_WIDTH), row(D),
                  _const_spec((D, D)), _const_spec((D, D)), _const_spec((D, D)),
                  _const_spec((M_WIDTH, D)), _const_spec((D, D)), _const_spec((1, D))],
        out_specs=row(D),
        compiler_params=_cparams(("parallel",)),
        name="mixer_merge",
    )(gates, yr, ys, ym, h, wr, wglu, ws, wm, wo, g)


def _rwkv_prep_kernel(p_ref, aux_ref, shift_ref, mu_ref, wa_ref, w0_ref, a0_ref, g2_ref, kk_ref, ka_ref,
                      r_ref, e_ref, k_ref, v_ref, kk_out_ref, a_out_ref, g_ref, *, T, tiles_per_seq):
    i = pl.program_id(0)
    p = p_ref[...]
    tm = p.shape[0]
    row = _iota((tm, 1), 0)
    rolled = pltpu.roll(p, 1, axis=0)
    if tiles_per_seq >= 1:
        b = i // tiles_per_seq
        tt = i - b * tiles_per_seq
        first = jnp.where(tt == 0, shift_ref[pl.ds(b % 8, 1), :], aux_ref[7:8, :])
        prev = jnp.where(row == 0, first, rolled)
    else:
        prev = jnp.where((row & (T - 1)) == 0, shift_ref[...], rolled)
    pm = p + mu_ref[...] * (prev - p)
    r = pm[:, 0:D]
    k = pm[:, D:2 * D]
    v = pm[:, 2 * D:3 * D]
    x_wa = pm[:, 3 * D:3 * D + 128]
    xg = pm[:, 3 * D + 128:3 * D + 256]
    lora_in = jnp.where(_iota(x_wa.shape, 1) < 64, jnp.tanh(x_wa), x_wa)
    lwa = _mm(lora_in, wa_ref[...])
    zw = w0_ref[...] + lwa[:, 0:D]
    w_log = -_softplus(-zw) - 0.5
    a = _sigmoid(a0_ref[...] + lwa[:, D:2 * D])
    g = _mm(_sigmoid(xg), g2_ref[...])
    r_ref[...] = r
    e_ref[...] = jnp.exp(w_log)
    k_ref[...] = k * (1.0 + (a - 1.0) * ka_ref[...])
    v_ref[...] = v
    kk_out_ref[...] = k * kk_ref[...]
    a_out_ref[...] = a
    g_ref[...] = g


def rwkv_prep(p, shift_prev, mu, wa, w0, a0, g2, k_k, k_a, *, T, tm=256):
    n, rc = p.shape
    tm = _row_tile(n, tm)
    if tm <= T:
        assert T % tm == 0 and tm % 8 == 0 and shift_prev.shape[0] % 8 == 0
        tps = T // tm
        aux_spec = pl.BlockSpec((8, rc), lambda i: (jnp.maximum(i * (tm // 8) - 1, 0), 0))
        shift_spec = pl.BlockSpec((8, rc), lambda i: ((i // tps) // 8, 0))
        shift_arg = shift_prev
    else:
        assert tm % T == 0 and (T & (T - 1)) == 0
        tps = 0
        aux_spec = pl.BlockSpec((8, rc), lambda i: (0, 0))
        shift_spec = pl.BlockSpec((tm, rc), lambda i: (i, 0))
        shift_arg = jnp.repeat(shift_prev, T, axis=0)
    row = lambda: pl.BlockSpec((tm, D), lambda i: (i, 0))
    outs = [jax.ShapeDtypeStruct((n, D), F32)] * 7
    return pl.pallas_call(
        functools.partial(_rwkv_prep_kernel, T=T, tiles_per_seq=tps),
        out_shape=outs,
        grid=(n // tm,),
        in_specs=[pl.BlockSpec((tm, rc), lambda i: (i, 0)),
                  aux_spec, shift_spec,
                  _const_spec((1, rc)),
                  _const_spec((128, 2 * D)),
                  _const_spec((1, D)), _const_spec((1, D)),
                  _const_spec((128, D)),
                  _const_spec((1, D)), _const_spec((1, D))],
        out_specs=[row() for _ in range(7)],
        compiler_params=_cparams(("parallel",)),
        name="rwkv_prep",
    )(p, p, shift_arg, mu, wa, w0, a0, g2, k_k, k_a)


def _rwkv_rec_kernel(r_ref, e_ref, k_ref, v_ref, kk_ref, a_ref, g_ref, s0_ref, rk_ref, lnw_ref, lnb_ref,
                     y_ref, sfin_ref, s_scr, *, C, G, n_steps):
    step = pl.program_id(2)
    C2 = 2 * C
    rng = range(G)
    lane = _iota((1, R_PAIR), 1)
    m0 = lane < R_HEAD
    ri = _iota((R_PAIR, R_PAIR), 0)
    ci = _iota((R_PAIR, R_PAIR), 1)
    bd = (ri < R_HEAD) == (ci < R_HEAD)
    rt_i = _iota((C2, C2), 0)
    ct_i = _iota((C2, C2), 1)
    same = (rt_i < C) == (ct_i < C)
    tr = jnp.where(rt_i < C, rt_i, rt_i - C)
    ts = jnp.where(ct_i < C, ct_i, ct_i - C)
    m_sl = same & (ts < tr)
    m_li = same & (ts <= tr)
    eye2 = jnp.where(rt_i == ct_i, 1.0, 0.0)
    lvl_masks = []
    m = 1
    while m < C:
        lvl_masks.append(((tr & ~(2 * m - 1)) == (ts & ~(2 * m - 1))) & ((tr & m) != 0) & ((ts & m) == 0))
        m *= 2
    tril = jnp.where(_iota((C, C), 1) <= _iota((C, C), 0), 1.0, 0.0).astype(_MXU)

    @pl.when(step == 0)
    def _():
        z = jnp.zeros((R_HEAD, R_HEAD), F32)
        for gi in rng:
            s_scr[gi] = jnp.concatenate([jnp.concatenate([s0_ref[gi, 0], z], axis=1),
                                         jnp.concatenate([z, s0_ref[gi, 1]], axis=1)], axis=0)

    def stack(x):
        return jnp.concatenate([jnp.where(m0, x, 0.0), jnp.where(m0, 0.0, x)], axis=0)

    def fold(xs):
        return xs[:C] + xs[C:]

    def each(f, *lists):
        return [f(*xs) for xs in zip(*lists)]

    def head_sum(x):
        s0 = jnp.sum(jnp.where(m0, x, 0.0), axis=-1, keepdims=True)
        s1 = jnp.sum(jnp.where(m0, 0.0, x), axis=-1, keepdims=True)
        return jnp.where(m0, s0, s1)

    r = [r_ref[gi] for gi in rng]
    e = [e_ref[gi] for gi in rng]
    k = [k_ref[gi] for gi in rng]
    v = [v_ref[gi] for gi in rng]
    kk = [kk_ref[gi] for gi in rng]
    an = each(lambda x: -x * lax.rsqrt(head_sum(x * x) + 1e-12), kk)
    bn = each(lambda n_, gi: -n_ * a_ref[gi], an, rng)
    cs = each(lambda x: _mm_sel_l(tril, x), e)
    cl = each(lambda x: x[C - 1:C, :], cs)
    at = each(lambda a_, e_, c_: a_ * jnp.exp(e_ - c_), an, e, cs)
    rt = each(lambda r_, c_: r_ * jnp.exp(-c_), r, cs)
    ecs = each(jnp.exp, cs)
    bt_ = each(lambda b_, x: b_ * x, bn, ecs)
    kt = each(lambda k_, x: k_ * x, k, ecs)
    ecl = each(lambda c_, l_: jnp.exp(c_ - l_), cs, cl)
    bh = each(lambda b_, x: b_ * x, bn, ecl)
    kh = each(lambda k_, x: k_ * x, k, ecl)
    As, Rs, Vs = each(stack, at), each(stack, rt), each(stack, v)
    b2 = each(lambda x: jnp.concatenate([x, x], axis=0), bt_)
    k2 = each(lambda x: jnp.concatenate([x, x], axis=0), kt)
    N = each(lambda a_, b_: jnp.where(m_sl, _mm_nt(a_, b_), 0.0), As, b2)
    Ak = each(lambda a_, b_: jnp.where(m_sl, _mm_nt(a_, b_), 0.0), As, k2)
    Arb = each(lambda a_, b_: jnp.where(m_li, _mm_nt(a_, b_), 0.0), Rs, b2)
    Ark = each(lambda a_, b_: jnp.where(m_li, _mm_nt(a_, b_), 0.0), Rs, k2)
    T = each(lambda n_: eye2 + jnp.where(lvl_masks[0], n_, 0.0), N)
    for lm in lvl_masks[1:]:
        LT = each(lambda n_, t_: _mm(jnp.where(lm, n_, 0.0), t_), N, T)
        T = each(lambda t_, x: t_ + _mm(t_, x), T, LT)
    AkV = each(_mm, Ak, Vs)
    X = each(lambda t_, a_, u_: _mm(t_, jnp.concatenate([a_, u_], axis=1)), T, As, AkV)
    Z = each(_mm, Arb, X)
    ArkV = each(_mm, Ark, Vs)
    Rp = each(lambda r_, z_: r_ + fold(z_[:, 0:R_PAIR]), rt, Z)
    Y0 = each(lambda z_, a_: fold(z_[:, R_PAIR:] + a_), Z, ArkV)
    Ap = each(lambda x: fold(x[:, 0:R_PAIR]), X)
    U0 = each(lambda x: fold(x[:, R_PAIR:]), X)
    P = each(lambda a_, b_: jnp.where(bd, _mm_tn(a_, b_), 0.0), Ap, bh)
    Q = each(lambda u_, v_, b_, k_: jnp.where(bd, _mm_tn(jnp.concatenate([u_, v_], axis=0),
                                                         jnp.concatenate([b_, k_], axis=0)), 0.0), U0, v, bh, kh)
    S = [s_scr[gi] for gi in rng]
    y = each(lambda y_, r_, s_: y_ + _mm_nt(r_, s_), Y0, Rp, S)
    S = each(lambda s_, l_, p_, q_: s_ * jnp.exp(-l_) + _mm(s_, p_) + q_, S, cl, P, Q)
    for gi in rng:
        s_scr[gi] = S[gi]
    rk = rk_ref[...]
    dlt = each(lambda y_: y_ - head_sum(y_) * (1.0 / R_HEAD), y)
    var = each(lambda d_: head_sum(d_ * d_) * (1.0 / R_HEAD), dlt)
    bonus = each(lambda r_, k_, v_: head_sum(r_ * k_ * rk) * v_, r, k, v)
    lnw = lnw_ref[...]
    lnb = lnb_ref[...]
    for gi in rng:
        yn = dlt[gi] * lax.rsqrt(var[gi] + R_LN_EPS) * lnw + lnb
        y_ref[gi] = (yn + bonus[gi]) * g_ref[gi]

    @pl.when(step == n_steps - 1)
    def _():
        for gi in rng:
            sfin_ref[gi, 0] = S[gi][0:R_HEAD, 0:R_HEAD]
            sfin_ref[gi, 1] = S[gi][R_HEAD:, R_HEAD:]


def rwkv_recurrence(r, e, k, v, kk, a, g, s0, l, r_k, ln_w, ln_b, *, C, G):
    B, T, _ = r.shape
    n_steps = T // C
    n_pairs = D // R_PAIR
    blk = lambda: pl.BlockSpec((G, C, R_PAIR), lambda b, p, s: (b, s, p))
    par = lambda: pl.BlockSpec((1, R_PAIR), lambda b, p, s: (0, p))
    y, sfin = pl.pallas_call(
        functools.partial(_rwkv_rec_kernel, C=C, G=G, n_steps=n_steps),
        out_shape=[jax.ShapeDtypeStruct((B, T, D), F32),
                   jax.ShapeDtypeStruct((B, 2 * n_pairs, R_HEAD, R_HEAD), F32)],
        grid=(B // G, n_pairs, n_steps),
        in_specs=[blk() for _ in range(7)]
        + [pl.BlockSpec((None, G, 2, R_HEAD, R_HEAD), lambda b, p, s: (l, b, p, 0, 0)),
           par(), par(), par()],
        out_specs=[blk(), pl.BlockSpec((G, 2, R_HEAD, R_HEAD), lambda b, p, s: (b, p, 0, 0))],
        scratch_shapes=[pltpu.VMEM((G, R_PAIR, R_PAIR), F32)],
        compiler_params=_cparams(("parallel", "parallel", "arbitrary")),
        name="rwkv_rec",
    )(r, e, k, v, kk, a, g, s0, r_k, ln_w, ln_b)
    return y, sfin


def _s5_kernel(u_ref, h0r_ref, h0i_ref, ar_ref, ai_ref, bre_ref, bim_ref, cre_ref, cim_ref, d_ref, perm_ref,
               permt_ref, y_ref, hr_out, hi_out, inr_scr, ini_scr, hr_scr, hi_scr, *, tc, nb, n_steps):
    step = pl.program_id(1)
    rows = nb * tc

    @pl.when(step == 0)
    def _():
        hr_scr[...] = h0r_ref[...]
        hi_scr[...] = h0i_ref[...]

    u = u_ref[...].reshape(rows, S5_BLK_CH)
    ut = jnp.dot(perm_ref[...], u.astype(_MXU), preferred_element_type=F32).astype(_MXU)
    inr_scr[...] = jnp.dot(ut, bre_ref[0], preferred_element_type=F32)
    ini_scr[...] = jnp.dot(ut, bim_ref[0], preferred_element_type=F32)
    ar = ar_ref[...]
    ai = ai_ref[...]

    def body(t, carry):
        hr, hi = carry
        sl = pl.ds(pl.multiple_of(t * nb, nb), nb)
        nr = ar * hr - ai * hi + inr_scr[sl, :]
        ni = ar * hi + ai * hr + ini_scr[sl, :]
        inr_scr[sl, :] = nr
        ini_scr[sl, :] = ni
        return nr, ni

    hr, hi = lax.fori_loop(0, tc, body, (hr_scr[...], hi_scr[...]))
    hr_scr[...] = hr
    hi_scr[...] = hi
    yt = (jnp.dot(inr_scr[...].astype(_MXU), cre_ref[0], preferred_element_type=F32)
          - jnp.dot(ini_scr[...].astype(_MXU), cim_ref[0], preferred_element_type=F32))
    yt_hi = yt.astype(_MXU)
    yt_lo = (yt - yt_hi.astype(F32)).astype(_MXU)
    y = (jnp.dot(permt_ref[...], yt_hi, preferred_element_type=F32)
         + jnp.dot(permt_ref[...], yt_lo, preferred_element_type=F32)) + d_ref[...] * u
    y = 0.5 * y * (1.0 + jnp.tanh(math.sqrt(2.0 / math.pi) * (y + 0.044715 * (y * y * y))))
    y_ref[...] = y.reshape(y_ref.shape)

    @pl.when(step == n_steps - 1)
    def _():
        hr_out[...] = hr
        hi_out[...] = hi


def s5_scan(u3, h0r, h0i, l, ar, ai, bre, bim, cre, cim, d, *, nb, tc, blk):
    A, R, _ = u3.shape
    ga, tr = blk
    assert A == ga and ga * tr == nb * tc
    nblk = D // S5_BLK_CH
    n_steps = R // tr
    rows = nb * tc
    j = np.arange(rows)
    perm_np = np.zeros((rows, rows), np.float32)
    perm_np[(j % tc) * nb + j // tc, j] = 1.0
    perm = jnp.asarray(perm_np, _MXU)
    permt = jnp.asarray(perm_np.T, _MXU)
    st = lambda: pl.BlockSpec((None, nb, S5_BLK_ST), lambda c, s: (l, 0, c))
    vec = lambda w: pl.BlockSpec((1, w), lambda c, s: (0, c))
    mat = lambda a, b: pl.BlockSpec((1, a, b), lambda c, s: (c, 0, 0))
    return pl.pallas_call(
        functools.partial(_s5_kernel, tc=tc, nb=nb, n_steps=n_steps),
        out_shape=[jax.ShapeDtypeStruct(u3.shape, F32),
                   jax.ShapeDtypeStruct((nb, nblk * S5_BLK_ST), F32),
                   jax.ShapeDtypeStruct((nb, nblk * S5_BLK_ST), F32)],
        grid=(nblk, n_steps),
        in_specs=[pl.BlockSpec((ga, tr, S5_BLK_CH), lambda c, s: (0, s, c)),
                  st(), st(), vec(S5_BLK_ST), vec(S5_BLK_ST),
                  mat(S5_BLK_CH, S5_BLK_ST), mat(S5_BLK_CH, S5_BLK_ST),
                  mat(S5_BLK_ST, S5_BLK_CH), mat(S5_BLK_ST, S5_BLK_CH),
                  vec(S5_BLK_CH), _const_spec((rows, rows)), _const_spec((rows, rows))],
        out_specs=[pl.BlockSpec((ga, tr, S5_BLK_CH), lambda c, s: (0, s, c)),
                   pl.BlockSpec((nb, S5_BLK_ST), lambda c, s: (0, c)),
                   pl.BlockSpec((nb, S5_BLK_ST), lambda c, s: (0, c))],
        scratch_shapes=[pltpu.VMEM((rows, S5_BLK_ST), F32), pltpu.VMEM((rows, S5_BLK_ST), F32),
                        pltpu.VMEM((nb, S5_BLK_ST), F32), pltpu.VMEM((nb, S5_BLK_ST), F32)],
        compiler_params=_cparams(("parallel", "arbitrary")),
        name="s5_scan",
    )(u3, h0r, h0i, ar, ai, bre, bim, cre, cim, d, perm, permt)


def _s5_discretize(a_re, a_im, log_dt, b_re, b_im, c_re, c_im):
    g, p, hch = b_re.shape
    dt = jnp.exp(log_dt.astype(F32))[:, None]
    mag = jnp.exp(dt * a_re)
    abar_re = mag * jnp.cos(dt * a_im)
    abar_im = mag * jnp.sin(dt * a_im)
    den = a_re * a_re + a_im * a_im
    nr = abar_re - 1.0
    q_re = (nr * a_re + abar_im * a_im) / den
    q_im = (abar_im * a_re - nr * a_im) / den
    bb_re = q_re[..., None] * b_re - q_im[..., None] * b_im
    bb_im = q_re[..., None] * b_im + q_im[..., None] * b_re
    nblk = D // S5_BLK_CH
    gl = g // nblk
    eye = jnp.eye(gl, dtype=F32)

    def in_blocks(bb):
        t = jnp.transpose(bb, (0, 2, 1)).reshape(nblk, gl, hch, p)
        return jnp.einsum('cghp,gk->cghkp', t, eye).reshape(nblk, gl * hch, gl * p).astype(_MXU)

    def out_blocks(cc):
        t = jnp.transpose(cc, (0, 2, 1)).reshape(nblk, gl, p, hch)
        return jnp.einsum('cgph,gk->cgpkh', t, eye).reshape(nblk, gl * p, gl * hch).astype(_MXU)

    return (abar_re.reshape(1, g * p), abar_im.reshape(1, g * p), in_blocks(bb_re), in_blocks(bb_im),
            out_blocks(c_re), out_blocks(c_im))


def _ssd_kernel(xbc_ref, z_ref, dt_ref, cprev_ref, h0_ref, cw_ref, cb_ref, dtb_ref, alog_ref, dx_ref, ng_ref,
                eh_ref, y_ref, hfin_ref, ext_scr, h_scr, *, L, G, t_real, n_steps):
    step = pl.program_id(1)
    rng = range(G)
    NB = M_GROUPS * M_STATE

    def each(f, *lists):
        return [f(*xs) for xs in zip(*lists)]

    @pl.when(step == 0)
    def _():
        for s in rng:
            ext_scr[s] = jnp.zeros((8, M_CONV_DIM), F32)
            ext_scr[s, 5:8, :] = cprev_ref[s]
            for gi in range(M_GROUPS):
                h_scr[s, gi] = h0_ref[s, gi * M_HPG:(gi + 1) * M_HPG].reshape(M_GW, M_STATE).T

    tril = jnp.where(_iota((L, L), 1) <= _iota((L, L), 0), 1.0, 0.0).astype(_MXU)
    causal = _iota((L, L), 1) <= _iota((L, L), 0)
    eye_h = _eye(128, _MXU)
    eh = eh_ref[...]
    lane_head = _iota((1, M_GW), 1) // M_HEAD
    row8 = _iota((8, 1), 0)
    a = -jnp.exp(alog_ref[...])
    conv, dts = [], []
    for s in rng:
        x = xbc_ref[s * L:(s + 1) * L, :]
        prev8 = ext_scr[s]
        c = cb_ref[...] + cw_ref[3:4, :] * x
        for j in range(1, 4):
            xr = pltpu.roll(x, j, axis=0)
            head = jnp.where(row8 < j, pltpu.roll(prev8, j, axis=0), xr[0:8])
            xr = head if L == 8 else jnp.concatenate([head, xr[8:]], axis=0)
            c = c + cw_ref[3 - j:4 - j, :] * xr
        ext_scr[s] = x[L - 8:L]
        conv.append(c * _sigmoid(c))
        dt = _softplus(dt_ref[s * L:(s + 1) * L, :] + dtb_ref[...])
        if t_real < L:
            dt = jnp.where(_iota(dt.shape, 0) < t_real, dt, 0.0)
        dts.append(dt)
    xs = each(lambda c: c[:, 0:M_WIDTH], conv)
    acum = each(lambda d_: _mm_sel_l(tril, d_ * a), dts)
    acum_t = each(lambda x: _mm_nt_sel_l(eye_h, x), acum)
    dt_x = each(lambda d_: _mm_sel_r(d_, eh), dts)
    acum_x = each(lambda x: _mm_sel_r(x, eh), acum)
    acl_x = each(lambda x: x[L - 1:L, :], acum_x)
    xd = each(lambda x, d_: x * d_, xs, dt_x)
    xdd = each(lambda x, l_, c_: x * jnp.exp(l_ - c_), xd, acl_x, acum_x)
    eacum_x = each(jnp.exp, acum_x)
    cdec_x = each(jnp.exp, acl_x)
    ys = [[] for _ in rng]
    for gi in range(M_GROUPS):
        gs = slice(gi * M_GW, (gi + 1) * M_GW)
        bg = each(lambda c: c[:, M_WIDTH + gi * M_STATE:M_WIDTH + (gi + 1) * M_STATE], conv)
        cg = each(lambda c: c[:, M_WIDTH + NB + gi * M_STATE:M_WIDTH + NB + (gi + 1) * M_STATE], conv)
        h = [h_scr[s, gi] for s in rng]
        cbm = each(_mm_nt, cg, bg)
        bgt = each(lambda b_: _mm_nt(eye_h, b_), bg)
        y_off = each(lambda c_, h_, e_: _mm(c_, h_) * e_[:, gs], cg, h, eacum_x)
        ms = []
        for e in range(M_HPG):
            he = gi * M_HPG + e
            ms.append(each(lambda c_, a_, t_: c_ * jnp.exp(jnp.where(causal, a_[:, he:he + 1] - t_[he:he + 1, :],
                                                                     -jnp.inf)), cbm, acum, acum_t))
        if L % 128 == 0:
            mcat = [jnp.concatenate([ms[e][s] for e in range(M_HPG)], axis=1) for s in rng]
            xst = each(lambda x: jnp.concatenate([jnp.where(lane_head == e, x[:, gs], 0.0).astype(_MXU)
                                                  for e in range(M_HPG)], axis=0), xd)
            y_dg = each(_mm, mcat, xst)
        else:
            y_dg = [sum(_mm(ms[e][s], jnp.where(lane_head == e, xd[s][:, gs], 0.0)) for e in range(M_HPG))
                    for s in rng]
        hn = each(lambda h_, d_, b_, x: h_ * d_[:, gs] + _mm(b_, x[:, gs]), h, cdec_x, bgt, xdd)
        for s in rng:
            h_scr[s, gi] = hn[s]
            ys[s].append(y_off[s] + y_dg[s])
    for s in rng:
        y = jnp.concatenate(ys[s], axis=1) + dx_ref[...] * xs[s]
        zz = z_ref[s * L:(s + 1) * L, :]
        y = y * (zz * _sigmoid(zz))
        outs = []
        for gi in range(M_GROUPS):
            yg = y[:, gi * M_GW:(gi + 1) * M_GW]
            outs.append(yg * lax.rsqrt(jnp.mean(yg * yg, axis=-1, keepdims=True) + EPS))
        y_ref[s * L:(s + 1) * L, :] = jnp.concatenate(outs, axis=1) * ng_ref[...]

    @pl.when(step == n_steps - 1)
    def _():
        for s in rng:
            for gi in range(M_GROUPS):
                hfin_ref[s, gi * M_HPG:(gi + 1) * M_HPG] = h_scr[s, gi].T.reshape(M_HPG, M_HEAD, M_STATE)


def ssd_block(xbc, z, dt, conv_prev, h0, l, conv_w, conv_b, dt_bias, a_log, d_x, norm_g, eh, *, B, T, L, G,
              t_real):
    n_steps = T // L
    assert G == 1 or n_steps == 1
    heads = M_GROUPS * M_HPG
    seq = lambda w: pl.BlockSpec((G * L, w), lambda b, s: (b * n_steps + s, 0))
    return pl.pallas_call(
        functools.partial(_ssd_kernel, L=L, G=G, t_real=t_real, n_steps=n_steps),
        out_shape=[jax.ShapeDtypeStruct((B * T, M_WIDTH), F32),
                   jax.ShapeDtypeStruct((B, heads, M_HEAD, M_STATE), F32)],
        grid=(B // G, n_steps),
        in_specs=[seq(M_CONV_DIM), seq(M_WIDTH), seq(128),
                  pl.BlockSpec((None, G, 3, M_CONV_DIM), lambda b, s: (l, b, 0, 0)),
                  pl.BlockSpec((None, G, heads, M_HEAD, M_STATE), lambda b, s: (l, b, 0, 0, 0)),
                  _const_spec((4, M_CONV_DIM)), _const_spec((1, M_CONV_DIM)),
                  _const_spec((1, 128)), _const_spec((1, 128)),
                  _const_spec((1, M_WIDTH)), _const_spec((1, M_WIDTH)),
                  _const_spec((128, M_WIDTH))],
        out_specs=[seq(M_WIDTH),
                   pl.BlockSpec((G, heads, M_HEAD, M_STATE), lambda b, s: (b, 0, 0, 0))],
        scratch_shapes=[pltpu.VMEM((G, 8, M_CONV_DIM), F32),
                        pltpu.VMEM((G, M_GROUPS, M_STATE, M_GW), F32)],
        compiler_params=_cparams(("parallel", "arbitrary")),
        name="ssd",
    )(xbc, z, dt, conv_prev, h0, conv_w, conv_b, dt_bias, a_log, d_x, norm_g, eh)


def _attn_kernel(q_ref, k_ref, v_ref, o_ref, *, nb, tq):
    scale = XA_HD ** -0.5
    for j in range(nb):
        outs = []
        for hd in range(XA_HEADS):
            cs = slice(hd * XA_HD, (hd + 1) * XA_HD)
            q = q_ref[j * tq:(j + 1) * tq, cs]
            s = _mm_nt(q, k_ref[j, :, cs]) * scale
            s = s - jnp.max(s, axis=-1, keepdims=True)
            p = jnp.exp(s)
            p = p / jnp.sum(p, axis=-1, keepdims=True)
            outs.append(_mm(p, v_ref[j, :, cs]))
        o_ref[j * tq:(j + 1) * tq, :] = jnp.concatenate(outs, axis=1)


def cross_attention(q, mk, mv, kv_index, *, B, T, tq, nb):
    m = mk.shape[-2]
    nlead = len(kv_index)
    n_t = T // tq
    kv_spec = pl.BlockSpec((None,) * nlead + (nb, m, D), lambda b, s: tuple(kv_index) + (b, 0, 0))
    q_spec = pl.BlockSpec((nb * tq, D), lambda b, s: (b * n_t + s, 0))
    return pl.pallas_call(
        functools.partial(_attn_kernel, nb=nb, tq=tq),
        out_shape=jax.ShapeDtypeStruct((B * T, D), F32),
        grid=(B // nb, n_t),
        in_specs=[q_spec, kv_spec, kv_spec],
        out_specs=q_spec,
        compiler_params=_cparams(("parallel", "arbitrary")),
        name="xattn",
    )(q, mk, mv)


def _pad_time(x2, B, T, Tp):
    if Tp == T:
        return x2
    w = x2.shape[-1]
    return jnp.pad(x2.reshape(B, T, w), ((0, 0), (0, Tp - T), (0, 0))).reshape(B * Tp, w)


def _layer(l, h, grp, W, st):
    T, B = grp['T'], grp['B']
    n = T * B
    gates = norm_matmul(h, W['g_mix_pre'], W['w_gates'], name="in_gates")
    p_r = norm_matmul(h, W['g_mix_pre'], W['w_rwkv'], name="in_rwkv")
    u_s5 = norm_matmul(h, W['g_mix_pre'], W['w_s5'], name="in_s5")
    z_m = norm_matmul(h, W['g_mix_pre'], W['w_z'], name="in_z")
    xbc = norm_matmul(h, W['g_mix_pre'], W['w_xbc'], name="in_xbc")
    dt_m = norm_matmul(h, W['g_mix_pre'], W['w_dt'], tn=128, name="in_dt")

    seqs = rwkv_prep(p_r, st['shift'][l], W['mu'], W['wa'], W['w0'], W['a0'], W['g2'],
                     W['k_k'], W['k_a'], T=T)
    Tp = grp['rwkv_Tpad']
    seqs = [_pad_time(x, B, T, Tp).reshape(B, Tp, D) for x in seqs]
    yr, s_fin = rwkv_recurrence(*seqs, st['rwkv'], l, W['r_k'], W['ln_w'], W['ln_b'],
                                C=grp['rwkv_C'], G=grp['rwkv_G'])
    yr = yr[:, :T].reshape(n, D)
    shift_new = p_r.reshape(B, T, -1)[:, T - 1]

    u3 = u_s5.reshape(grp['s5_view'])
    ys, s5r, s5i = s5_scan(u3, st['s5r'], st['s5i'], l, W['s5_ar'], W['s5_ai'], W['s5_bre'], W['s5_bim'],
                           W['s5_cre'], W['s5_cim'], W['s5_d'], nb=B, tc=grp['s5_tc'], blk=grp['s5_blk'])
    ys = ys.reshape(n, D)

    Lc = grp['ssd_L']
    Tm = grp['ssd_Tpad']
    ym, ssm_fin = ssd_block(_pad_time(xbc, B, T, Tm), _pad_time(z_m, B, T, Tm), _pad_time(dt_m, B, T, Tm),
                            st['conv'], st['ssm'], l, W['conv_w'], W['conv_b'], W['dt_bias'],
                            W['a_log'], W['d_x'], W['m_norm'], W['eh'], B=B, T=Tm, L=Lc, G=grp['ssd_G'],
                            t_real=min(T, Lc))
    ym = ym.reshape(B, Tm, M_WIDTH)[:, :T].reshape(n, M_WIDTH)
    conv_new = jnp.concatenate([st['conv'][l], xbc.reshape(B, T, M_CONV_DIM)[:, max(T - 3, 0):]], axis=1)[:, -3:]

    h = mixer_merge(gates, yr, ys, ym, h, W['w_out_rwkv'], W['s5_w_glu'], W['w_out_s5'], W['w_out_mamba'],
                    W['w_out'], W['g_mix_post'])

    q = norm_matmul(h, W['g_xa_pre'], W['xa_wq'], name="xa_q")
    o = cross_attention(q, st['mk'], st['mv'], st['kv_index'](l), B=B, T=T, tq=grp['xa_tq'], nb=grp['xa_nb'])
    h = matmul_norm_residual(o, W['xa_wo'], W['g_xa_post'], h, name="xa_out")

    h = mlp_block(h, W['g_mlp_pre'], W['mlp_w1'], W['mlp_w2'], W['g_mlp_post'])
    return h, (s_fin, shift_new, s5r.reshape(B, 64, 64), s5i.reshape(B, 64, 64), conv_new, ssm_fin)


def _group_cfg(T, B):
    cfg = dict(T=T, B=B)
    g = 8 if B % 8 == 0 else 1
    if T % 64 == 0:
        cfg.update(rwkv_C=64, rwkv_Tpad=T, rwkv_G=g)
    else:
        tp = -(-T // 8) * 8
        cfg.update(rwkv_C=tp, rwkv_Tpad=tp, rwkv_G=16 if B % 16 == 0 else g)
    if T % 8 == 0 and (512 // B) >= 8 and T % (512 // B) == 0:
        tc = 512 // B
        cfg.update(s5_view=(B, T, D), s5_tc=tc, s5_blk=(B, tc))
    else:
        cfg.update(s5_view=(1, B * T, D), s5_tc=T, s5_blk=(1, B * T))
    if T % 128 == 0:
        cfg.update(ssd_L=128, ssd_Tpad=T, ssd_G=1)
    else:
        tp = -(-T // 8) * 8
        cfg.update(ssd_L=tp, ssd_Tpad=tp, ssd_G=4 if B % 4 == 0 else 1)
    if T >= 64:
        cfg.update(xa_tq=_row_tile(T, 512), xa_nb=1)
    else:
        cfg.update(xa_tq=T, xa_nb=min(B, 8))
    return cfg


def kernel(x_prompt, x_sample, cache_mem_k, cache_mem_v, state_rwkv, state_rwkv_shift, state_s5_re, state_s5_im, state_conv, state_ssm, mem_prompt, norm_mix_pre, norm_mix_post, norm_xa_pre, norm_xa_post, norm_mlp_pre, norm_mlp_post, norm_mem, w_in, w_out, rwkv_mu, rwkv_w0, rwkv_w2, rwkv_a0, rwkv_a2, rwkv_g2, rwkv_k_k, rwkv_k_a, rwkv_r_k, rwkv_ln_w, rwkv_ln_b, w_out_rwkv, s5_a_re, s5_a_im, s5_log_dt, s5_b_re, s5_b_im, s5_c_re, s5_c_im, s5_d, s5_w_glu, w_out_s5, m_conv_w, m_conv_b, m_dt_bias, m_a_log, m_d, m_norm, w_out_mamba, xa_wq, xa_wk, xa_wv, xa_wo, mlp_w1, mlp_w2):
    depth = w_in.shape[0]
    bp, tp, _ = x_prompt.shape
    bs, ts, _ = x_sample.shape
    mlen = mem_prompt.shape[1]
    bf = lambda x: x.astype(_MXU)
    row = lambda x: x.reshape(1, -1).astype(F32)

    eh = (jnp.arange(128)[:, None] == (jnp.arange(M_WIDTH)[None, :] // M_HEAD)).astype(_MXU)

    col = [0]
    for sz in (3 * D, 3 * D + 256, D, M_WIDTH, M_CONV_DIM, 32):
        col.append(col[-1] + sz)
    layers = []
    for l in range(depth):
        wl = w_in[l]
        ar, ai, bre, bim, cre, cim = _s5_discretize(s5_a_re[l], s5_a_im[l], s5_log_dt[l], s5_b_re[l], s5_b_im[l],
                                                    s5_c_re[l], s5_c_im[l])
        zero = jnp.zeros((64, D), F32)
        wa = jnp.concatenate([jnp.concatenate([rwkv_w2[l], zero], axis=1),
                              jnp.concatenate([zero, rwkv_a2[l]], axis=1)], axis=0)
        pad32 = lambda x: jnp.pad(x.reshape(1, -1).astype(F32), ((0, 0), (0, 128 - x.shape[-1])))
        layers.append(dict(
            g_mix_pre=row(norm_mix_pre[l]), g_mix_post=row(norm_mix_post[l]),
            g_xa_pre=row(norm_xa_pre[l]), g_xa_post=row(norm_xa_post[l]),
            g_mlp_pre=row(norm_mlp_pre[l]), g_mlp_post=row(norm_mlp_post[l]),
            w_gates=bf(wl[:, col[0]:col[1]]), w_rwkv=bf(wl[:, col[1]:col[2]]), w_s5=bf(wl[:, col[2]:col[3]]),
            w_z=bf(wl[:, col[3]:col[4]]), w_xbc=bf(wl[:, col[4]:col[5]]),
            w_dt=bf(jnp.pad(wl[:, col[5]:col[6]], ((0, 0), (0, 96)))),
            mu=row(rwkv_mu[l]), wa=bf(wa), w0=row(rwkv_w0[l]), a0=row(rwkv_a0[l]), g2=bf(rwkv_g2[l]),
            k_k=row(rwkv_k_k[l]), k_a=row(rwkv_k_a[l]), r_k=row(rwkv_r_k[l]),
            ln_w=row(rwkv_ln_w[l]), ln_b=row(rwkv_ln_b[l]),
            s5_ar=ar, s5_ai=ai, s5_bre=bre, s5_bim=bim, s5_cre=cre, s5_cim=cim, s5_d=row(s5_d[l]),
            conv_w=m_conv_w[l].astype(F32), conv_b=row(m_conv_b[l]), dt_bias=pad32(m_dt_bias[l]),
            a_log=pad32(m_a_log[l]), d_x=row(jnp.repeat(m_d[l], M_HEAD)), m_norm=row(m_norm[l]), eh=eh,
            w_out_rwkv=bf(w_out_rwkv[l]), s5_w_glu=bf(s5_w_glu[l]), w_out_s5=bf(w_out_s5[l]),
            w_out_mamba=bf(w_out_mamba[l]), w_out=bf(w_out[l]),
            xa_wq=bf(xa_wq[l]), xa_wk=bf(xa_wk[l]), xa_wv=bf(xa_wv[l]), xa_wo=bf(xa_wo[l]),
            mlp_w1=bf(mlp_w1[l]), mlp_w2=bf(mlp_w2[l]), g_mem=row(norm_mem[l]),
        ))

    hp = x_prompt.reshape(bp * tp, D).astype(F32)
    hs = x_sample.reshape(bs * ts, D).astype(F32)
    mem2 = mem_prompt.reshape(bp * mlen, D).astype(F32)

    cfg_p = _group_cfg(tp, bp)
    cfg_s = _group_cfg(ts, bs)
    zeros_p = dict(
        rwkv=jnp.zeros((depth, bp) + state_rwkv.shape[2:], F32),
        shift=jnp.zeros((depth, bp, state_rwkv_shift.shape[-1]), F32),
        s5r=jnp.zeros((depth, bp, state_s5_re.shape[2] * state_s5_re.shape[3]), F32),
        s5i=jnp.zeros((depth, bp, state_s5_re.shape[2] * state_s5_re.shape[3]), F32),
        conv=jnp.zeros((depth, bp) + state_conv.shape[2:], F32),
        ssm=jnp.zeros((depth, bp) + state_ssm.shape[2:], F32),
    )
    st_s = dict(
        rwkv=state_rwkv, shift=state_rwkv_shift,
        s5r=state_s5_re.reshape(depth, bs, -1), s5i=state_s5_im.reshape(depth, bs, -1),
        conv=state_conv, ssm=state_ssm,
        mk=cache_mem_k.reshape(depth, bs, mlen, D), mv=cache_mem_v.reshape(depth, bs, mlen, D),
        kv_index=lambda l: (l,),
    )

    mk_out, mv_out = [], []
    st_p_out = [[] for _ in range(6)]
    st_s_out = [[] for _ in range(6)]
    for l in range(depth):
        W = layers[l]
        mk = norm_matmul(mem2, W['g_mem'], W['xa_wk'], name="mem_k")
        mv = norm_matmul(mem2, W['g_mem'], W['xa_wv'], name="mem_v")
        st_p = dict(zeros_p, mk=mk.reshape(bp, mlen, D), mv=mv.reshape(bp, mlen, D), kv_index=lambda l: ())
        hp, new_p = _layer(l, hp, cfg_p, W, st_p)
        hs, new_s = _layer(l, hs, cfg_s, W, st_s)
        mk_out.append(mk.reshape(bp, mlen, XA_HEADS, XA_HD))
        mv_out.append(mv.reshape(bp, mlen, XA_HEADS, XA_HD))
        for i in range(6):
            st_p_out[i].append(new_p[i])
            st_s_out[i].append(new_s[i])

    stk = lambda xs: jnp.stack(xs).astype(F32)
    y_p = hp.reshape(bp, tp, D).astype(x_prompt.dtype)
    y_s = hs.reshape(bs, ts, D).astype(x_sample.dtype)
    return (y_p, y_s, stk(mk_out), stk(mv_out),
            *[stk(st_p_out[i]) for i in range(6)],
            *[stk(st_s_out[i]) for i in range(6)])
```

```python
import functools
import math

import numpy as np
import jax
import jax.numpy as jnp
from jax import lax
from jax.experimental import pallas as pl
from jax.experimental.pallas import tpu as pltpu

F32 = jnp.float32
_MXU = jnp.bfloat16
EPS = 1e-6
R_LN_EPS = 64e-5
D = 1024
R_HEAD = 64
R_PAIR = 2 * R_HEAD
S5_BLK_CH = 128
S5_BLK_ST = 512
M_HEAD = 64
M_STATE = 128
M_GROUPS = 4
M_HPG = 8
M_GW = M_HPG * M_HEAD
M_WIDTH = 2048
M_CONV_DIM = 3072
XA_HEADS = 4
XA_HD = 256
VMEM_LIMIT = 56 * 1024 * 1024


def _mm(a, b):
    return jnp.dot(a.astype(_MXU), b.astype(_MXU), preferred_element_type=F32)


def _mm_nt(a, b):
    return lax.dot_general(a.astype(_MXU), b.astype(_MXU), (((1,), (1,)), ((), ())),
                           preferred_element_type=F32)


def _mm_tn(a, b):
    return lax.dot_general(a.astype(_MXU), b.astype(_MXU), (((0,), (0,)), ((), ())),
                           preferred_element_type=F32)


def _split3(x):
    hi = x.astype(_MXU)
    r1 = x - hi.astype(F32)
    mid = r1.astype(_MXU)
    lo = (r1 - mid.astype(F32)).astype(_MXU)
    return hi, mid, lo


def _mm_sel_l(sel, x):
    return sum(jnp.dot(sel, p, preferred_element_type=F32) for p in _split3(x))


def _mm_sel_r(x, sel):
    return sum(jnp.dot(p, sel, preferred_element_type=F32) for p in _split3(x))


def _mm_nt_sel_l(sel, x):
    return sum(lax.dot_general(sel, p, (((1,), (1,)), ((), ())), preferred_element_type=F32)
               for p in _split3(x))


def _sigmoid(x):
    return 1.0 / (1.0 + jnp.exp(-x))


def _softplus(x):
    return jnp.maximum(x, 0.0) + jnp.log(1.0 + jnp.exp(-jnp.abs(x)))


def _rms(x, g):
    return x * lax.rsqrt(jnp.mean(x * x, axis=-1, keepdims=True) + EPS) * g


def _iota(shape, dim):
    return lax.broadcasted_iota(jnp.int32, shape, dim)


def _eye(n, dtype):
    return jnp.where(_iota((n, n), 0) == _iota((n, n), 1), 1.0, 0.0).astype(dtype)


def _cparams(sem):
    return pltpu.CompilerParams(dimension_semantics=sem, vmem_limit_bytes=VMEM_LIMIT)


def _row_tile(n, want):
    t = min(n, want)
    while n % t:
        t //= 2
    return t


def _col_tile(c, cap):
    best = 128
    for t in range(128, min(c, cap) + 1, 128):
        if c % t == 0:
            best = t
    return best


def _const_spec(shape):
    nd = len(shape)
    return pl.BlockSpec(shape, lambda *_: (0,) * nd, pipeline_mode=pl.Buffered(1))


def _norm_mm_kernel(x_ref, g_ref, w_ref, o_ref, xn_ref):
    @pl.when(pl.program_id(1) == 0)
    def _():
        xn_ref[...] = _rms(x_ref[...], g_ref[...]).astype(_MXU)

    o_ref[...] = jnp.dot(xn_ref[...], w_ref[...], preferred_element_type=F32)


def norm_matmul(x, g, w, *, tm=1024, tn=1664, name="norm_mm"):
    n, d = x.shape
    c = w.shape[1]
    tm = _row_tile(n, tm)
    tn = _col_tile(c, tn)
    return pl.pallas_call(
        _norm_mm_kernel,
        out_shape=jax.ShapeDtypeStruct((n, c), F32),
        grid=(n // tm, c // tn),
        in_specs=[pl.BlockSpec((tm, d), lambda i, j: (i, 0)),
                  pl.BlockSpec((1, d), lambda i, j: (0, 0)),
                  pl.BlockSpec((d, tn), lambda i, j: (0, j))],
        out_specs=pl.BlockSpec((tm, tn), lambda i, j: (i, j)),
        scratch_shapes=[pltpu.VMEM((tm, d), _MXU)],
        compiler_params=_cparams(("parallel", "arbitrary")),
        name=name,
    )(x, g, w)


def _mm_norm_res_kernel(x_ref, w_ref, g_ref, h_ref, o_ref):
    y = _mm(x_ref[...], w_ref[...])
    o_ref[...] = h_ref[...] + _rms(y, g_ref[...])


def matmul_norm_residual(x, w, g, h, *, tm=512, name="mm_norm_res"):
    n, k = x.shape
    d = w.shape[1]
    tm = _row_tile(n, tm)
    return pl.pallas_call(
        _mm_norm_res_kernel,
        out_shape=jax.ShapeDtypeStruct((n, d), F32),
        grid=(n // tm,),
        in_specs=[pl.BlockSpec((tm, k), lambda i: (i, 0)),
                  _const_spec((k, d)),
                  _const_spec((1, d)),
                  pl.BlockSpec((tm, d), lambda i: (i, 0))],
        out_specs=pl.BlockSpec((tm, d), lambda i: (i, 0)),
        compiler_params=_cparams(("parallel",)),
        name=name,
    )(x, w, g, h)


def _mlp_kernel(h_ref, g1_ref, w1_ref, w2_ref, g2_ref, o_ref, *, n_chunks, ck):
    h = h_ref[...]
    xn = _rms(h, g1_ref[...]).astype(_MXU)
    acc = jnp.zeros(h.shape, F32)
    for j in range(n_chunks):
        a = jnp.dot(xn, w1_ref[:, j * ck:(j + 1) * ck], preferred_element_type=F32)
        a = jnp.square(jnp.maximum(a, 0.0))
        acc = acc + jnp.dot(a.astype(_MXU), w2_ref[j * ck:(j + 1) * ck, :], preferred_element_type=F32)
    o_ref[...] = h + _rms(acc, g2_ref[...])


def mlp_block(h, g1, w1, w2, g2, *, tm=512, ck=1024):
    n, d = h.shape
    f = w1.shape[1]
    tm = _row_tile(n, tm)
    return pl.pallas_call(
        functools.partial(_mlp_kernel, n_chunks=f // ck, ck=ck),
        out_shape=jax.ShapeDtypeStruct((n, d), F32),
        grid=(n // tm,),
        in_specs=[pl.BlockSpec((tm, d), lambda i: (i, 0)),
                  _const_spec((1, d)),
                  _const_spec((d, f)),
                  _const_spec((f, d)),
                  _const_spec((1, d))],
        out_specs=pl.BlockSpec((tm, d), lambda i: (i, 0)),
        compiler_params=_cparams(("parallel",)),
        name="mlp",
    )(h, g1, w1, w2, g2)


def _merge_kernel(gates_ref, yr_ref, ys_ref, ym_ref, h_ref, wr_ref, wglu_ref, ws_ref, wm_ref, wo_ref, g_ref,
                  o_ref):
    o_r = _mm(yr_ref[...], wr_ref[...])
    ys = ys_ref[...]
    y3 = ys * _sigmoid(_mm(ys, wglu_ref[...]))
    o_s = _mm(y3, ws_ref[...])
    o_m = _mm(ym_ref[...], wm_ref[...])
    merged = (_sigmoid(gates_ref[:, 0:D]) * o_r + _sigmoid(gates_ref[:, D:2 * D]) * o_s
              + _sigmoid(gates_ref[:, 2 * D:3 * D]) * o_m)
    mix = _mm(merged, wo_ref[...])
    o_ref[...] = h_ref[...] + _rms(mix, g_ref[...])


def mixer_merge(gates, yr, ys, ym, h, wr, wglu, ws, wm, wo, g, *, tm=256):
    n = h.shape[0]
    tm = _row_tile(n, tm)
    row = lambda w: pl.BlockSpec((tm, w), lambda i: (i, 0))
    return pl.pallas_call(
        _merge_kernel,
        out_shape=jax.ShapeDtypeStruct((n, D), F32),
        grid=(n // tm,),
        in_specs=[row(3 * D), row(D), row(D), row(M_WIDTH), row(D),
                  _const_spec((D, D)), _const_spec((D, D)), _const_spec((D, D)),
                  _const_spec((M_WIDTH, D)), _const_spec((D, D)), _const_spec((1, D))],
        out_specs=row(D),
        compiler_params=_cparams(("parallel",)),
        name="mixer_merge",
    )(gates, yr, ys, ym, h, wr, wglu, ws, wm, wo, g)


def _rwkv_prep_kernel(p_ref, aux_ref, shift_ref, mu_ref, wa_ref, w0_ref, a0_ref, g2_ref, kk_ref, ka_ref,
                      r_ref, e_ref, k_ref, v_ref, kk_out_ref, a_out_ref, g_ref, *, T, tiles_per_seq):
    i = pl.program_id(0)
    p = p_ref[...]
    tm = p.shape[0]
    row = _iota((tm, 1), 0)
    rolled = pltpu.roll(p, 1, axis=0)
    if tiles_per_seq >= 1:
        b = i // tiles_per_seq
        tt = i - b * tiles_per_seq
        first = jnp.where(tt == 0, shift_ref[pl.ds(b % 8, 1), :], aux_ref[7:8, :])
        prev = jnp.where(row == 0, first, rolled)
    else:
        prev = jnp.where((row & (T - 1)) == 0, shift_ref[...], rolled)
    pm = p + mu_ref[...] * (prev - p)
    r = pm[:, 0:D]
    k = pm[:, D:2 * D]
    v = pm[:, 2 * D:3 * D]
    x_wa = pm[:, 3 * D:3 * D + 128]
    xg = pm[:, 3 * D + 128:3 * D + 256]
    lora_in = jnp.where(_iota(x_wa.shape, 1) < 64, jnp.tanh(x_wa), x_wa)
    lwa = _mm(lora_in, wa_ref[...])
    zw = w0_ref[...] + lwa[:, 0:D]
    w_log = -_softplus(-zw) - 0.5
    a = _sigmoid(a0_ref[...] + lwa[:, D:2 * D])
    g = _mm(_sigmoid(xg), g2_ref[...])
    r_ref[...] = r
    e_ref[...] = jnp.exp(w_log)
    k_ref[...] = k * (1.0 + (a - 1.0) * ka_ref[...])
    v_ref[...] = v
    kk_out_ref[...] = k * kk_ref[...]
    a_out_ref[...] = a
    g_ref[...] = g


def rwkv_prep(p, shift_prev, mu, wa, w0, a0, g2, k_k, k_a, *, T, tm=256):
    n, rc = p.shape
    tm = _row_tile(n, tm)
    if tm <= T:
        assert T % tm == 0 and tm % 8 == 0 and shift_prev.shape[0] % 8 == 0
        tps = T // tm
        aux_spec = pl.BlockSpec((8, rc), lambda i: (jnp.maximum(i * (tm // 8) - 1, 0), 0))
        shift_spec = pl.BlockSpec((8, rc), lambda i: ((i // tps) // 8, 0))
        shift_arg = shift_prev
    else:
        assert tm % T == 0 and (T & (T - 1)) == 0
        tps = 0
        aux_spec = pl.BlockSpec((8, rc), lambda i: (0, 0))
        shift_spec = pl.BlockSpec((tm, rc), lambda i: (i, 0))
        shift_arg = jnp.repeat(shift_prev, T, axis=0)
    row = lambda: pl.BlockSpec((tm, D), lambda i: (i, 0))
    outs = [jax.ShapeDtypeStruct((n, D), F32)] * 7
    return pl.pallas_call(
        functools.partial(_rwkv_prep_kernel, T=T, tiles_per_seq=tps),
        out_shape=outs,
        grid=(n // tm,),
        in_specs=[pl.BlockSpec((tm, rc), lambda i: (i, 0)),
                  aux_spec, shift_spec,
                  _const_spec((1, rc)),
                  _const_spec((128, 2 * D)),
                  _const_spec((1, D)), _const_spec((1, D)),
                  _const_spec((128, D)),
                  _const_spec((1, D)), _const_spec((1, D))],
        out_specs=[row() for _ in range(7)],
        compiler_params=_cparams(("parallel",)),
        name="rwkv_prep",
    )(p, p, shift_arg, mu, wa, w0, a0, g2, k_k, k_a)


def _rwkv_rec_kernel(r_ref, e_ref, k_ref, v_ref, kk_ref, a_ref, g_ref, s0_ref, rk_ref, lnw_ref, lnb_ref,
                     y_ref, sfin_ref, s_scr, *, C, G, n_steps):
    step = pl.program_id(2)
    C2 = 2 * C
    rng = range(G)
    lane = _iota((1, R_PAIR), 1)
    m0 = lane < R_HEAD
    ri = _iota((R_PAIR, R_PAIR), 0)
    ci = _iota((R_PAIR, R_PAIR), 1)
    bd = (ri < R_HEAD) == (ci < R_HEAD)
    rt_i = _iota((C2, C2), 0)
    ct_i = _iota((C2, C2), 1)
    same = (rt_i < C) == (ct_i < C)
    tr = jnp.where(rt_i < C, rt_i, rt_i - C)
    ts = jnp.where(ct_i < C, ct_i, ct_i - C)
    m_sl = same & (ts < tr)
    m_li = same & (ts <= tr)
    eye2 = jnp.where(rt_i == ct_i, 1.0, 0.0)
    lvl_masks = []
    m = 1
    while m < C:
        lvl_masks.append(((tr & ~(2 * m - 1)) == (ts & ~(2 * m - 1))) & ((tr & m) != 0) & ((ts & m) == 0))
        m *= 2
    tril = jnp.where(_iota((C, C), 1) <= _iota((C, C), 0), 1.0, 0.0).astype(_MXU)

    @pl.when(step == 0)
    def _():
        z = jnp.zeros((R_HEAD, R_HEAD), F32)
        for gi in rng:
            s_scr[gi] = jnp.concatenate([jnp.concatenate([s0_ref[gi, 0], z], axis=1),
                                         jnp.concatenate([z, s0_ref[gi, 1]], axis=1)], axis=0)

    def stack(x):
        return jnp.concatenate([jnp.where(m0, x, 0.0), jnp.where(m0, 0.0, x)], axis=0)

    def fold(xs):
        return xs[:C] + xs[C:]

    def each(f, *lists):
        return [f(*xs) for xs in zip(*lists)]

    def head_sum(x):
        s0 = jnp.sum(jnp.where(m0, x, 0.0), axis=-1, keepdims=True)
        s1 = jnp.sum(jnp.where(m0, 0.0, x), axis=-1, keepdims=True)
        return jnp.where(m0, s0, s1)

    r = [r_ref[gi] for gi in rng]
    e = [e_ref[gi] for gi in rng]
    k = [k_ref[gi] for gi in rng]
    v = [v_ref[gi] for gi in rng]
    kk = [kk_ref[gi] for gi in rng]
    an = each(lambda x: -x * lax.rsqrt(head_sum(x * x) + 1e-12), kk)
    bn = each(lambda n_, gi: -n_ * a_ref[gi], an, rng)
    cs = each(lambda x: _mm_sel_l(tril, x), e)
    cl = each(lambda x: x[C - 1:C, :], cs)
    at = each(lambda a_, e_, c_: a_ * jnp.exp(e_ - c_), an, e, cs)
    rt = each(lambda r_, c_: r_ * jnp.exp(-c_), r, cs)
    ecs = each(jnp.exp, cs)
    bt_ = each(lambda b_, x: b_ * x, bn, ecs)
    kt = each(lambda k_, x: k_ * x, k, ecs)
    ecl = each(lambda c_, l_: jnp.exp(c_ - l_), cs, cl)
    bh = each(lambda b_, x: b_ * x, bn, ecl)
    kh = each(lambda k_, x: k_ * x, k, ecl)
    As, Rs, Vs = each(stack, at), each(stack, rt), each(stack, v)
    b2 = each(lambda x: jnp.concatenate([x, x], axis=0), bt_)
    k2 = each(lambda x: jnp.concatenate([x, x], axis=0), kt)
    N = each(lambda a_, b_: jnp.where(m_sl, _mm_nt(a_, b_), 0.0), As, b2)
    Ak = each(lambda a_, b_: jnp.where(m_sl, _mm_nt(a_, b_), 0.0), As, k2)
    Arb = each(lambda a_, b_: jnp.where(m_li, _mm_nt(a_, b_), 0.0), Rs, b2)
    Ark = each(lambda a_, b_: jnp.where(m_li, _mm_nt(a_, b_), 0.0), Rs, k2)
    T = each(lambda n_: eye2 + jnp.where(lvl_masks[0], n_, 0.0), N)
    for lm in lvl_masks[1:]:
        LT = each(lambda n_, t_: _mm(jnp.where(lm, n_, 0.0), t_), N, T)
        T = each(lambda t_, x: t_ + _mm(t_, x), T, LT)
    AkV = each(_mm, Ak, Vs)
    X = each(lambda t_, a_, u_: _mm(t_, jnp.concatenate([a_, u_], axis=1)), T, As, AkV)
    Z = each(_mm, Arb, X)
    ArkV = each(_mm, Ark, Vs)
    Rp = each(lambda r_, z_: r_ + fold(z_[:, 0:R_PAIR]), rt, Z)
    Y0 = each(lambda z_, a_: fold(z_[:, R_PAIR:] + a_), Z, ArkV)
    Ap = each(lambda x: fold(x[:, 0:R_PAIR]), X)
    U0 = each(lambda x: fold(x[:, R_PAIR:]), X)
    P = each(lambda a_, b_: jnp.where(bd, _mm_tn(a_, b_), 0.0), Ap, bh)
    Q = each(lambda u_, v_, b_, k_: jnp.where(bd, _mm_tn(jnp.concatenate([u_, v_], axis=0),
                                                         jnp.concatenate([b_, k_], axis=0)), 0.0), U0, v, bh, kh)
    S = [s_scr[gi] for gi in rng]
    y = each(lambda y_, r_, s_: y_ + _mm_nt(r_, s_), Y0, Rp, S)
    S = each(lambda s_, l_, p_, q_: s_ * jnp.exp(-l_) + _mm(s_, p_) + q_, S, cl, P, Q)
    for gi in rng:
        s_scr[gi] = S[gi]
    rk = rk_ref[...]
    dlt = each(lambda y_: y_ - head_sum(y_) * (1.0 / R_HEAD), y)
    var = each(lambda d_: head_sum(d_ * d_) * (1.0 / R_HEAD), dlt)
    bonus = each(lambda r_, k_, v_: head_sum(r_ * k_ * rk) * v_, r, k, v)
    lnw = lnw_ref[...]
    lnb = lnb_ref[...]
    for gi in rng:
        yn = dlt[gi] * lax.rsqrt(var[gi] + R_LN_EPS) * lnw + lnb
        y_ref[gi] = (yn + bonus[gi]) * g_ref[gi]

    @pl.when(step == n_steps - 1)
    def _():
        for gi in rng:
            sfin_ref[gi, 0] = S[gi][0:R_HEAD, 0:R_HEAD]
            sfin_ref[gi, 1] = S[gi][R_HEAD:, R_HEAD:]


def _stacked_out(kernel_fn, in_specs, args, out_index, stack):
    if stack is None:
        return kernel_fn, in_specs, args, {}
    pos = len(args)

    def with_alias(*refs):
        return kernel_fn(*refs[:pos], *refs[pos + 1:])

    return with_alias, in_specs + [pl.BlockSpec(memory_space=pl.ANY)], args + [stack], {pos: out_index}


def rwkv_recurrence(r, e, k, v, kk, a, g, s0, l, r_k, ln_w, ln_b, stack, *, C, G):
    B, T, _ = r.shape
    n_steps = T // C
    n_pairs = D // R_PAIR
    blk = lambda: pl.BlockSpec((G, C, R_PAIR), lambda b, p, s: (b, s, p))
    par = lambda: pl.BlockSpec((1, R_PAIR), lambda b, p, s: (0, p))
    in_specs = ([blk() for _ in range(7)]
                + [pl.BlockSpec((None, G, 2, R_HEAD, R_HEAD), lambda b, p, s: (l, b, p, 0, 0)),
                   par(), par(), par()])
    kern, in_specs, args, aliases = _stacked_out(
        functools.partial(_rwkv_rec_kernel, C=C, G=G, n_steps=n_steps), in_specs,
        [r, e, k, v, kk, a, g, s0, r_k, ln_w, ln_b], 1, stack)
    y, sfin = pl.pallas_call(
        kern,
        out_shape=[jax.ShapeDtypeStruct((B, T, D), F32),
                   jax.ShapeDtypeStruct((s0.shape[0], B, 2 * n_pairs, R_HEAD, R_HEAD), F32)],
        grid=(B // G, n_pairs, n_steps),
        in_specs=in_specs,
        out_specs=[blk(), pl.BlockSpec((None, G, 2, R_HEAD, R_HEAD), lambda b, p, s: (l, b, p, 0, 0))],
        scratch_shapes=[pltpu.VMEM((G, R_PAIR, R_PAIR), F32)],
        input_output_aliases=aliases,
        compiler_params=_cparams(("parallel", "parallel", "arbitrary")),
        name="rwkv_rec",
    )(*args)
    return y, sfin


def _s5_kernel(u_ref, h0r_ref, h0i_ref, ar_ref, ai_ref, bre_ref, bim_ref, cre_ref, cim_ref, d_ref, perm_ref,
               permt_ref, y_ref, hr_out, hi_out, inr_scr, ini_scr, hr_scr, hi_scr, *, tc, nb, n_steps):
    step = pl.program_id(1)
    rows = nb * tc

    @pl.when(step == 0)
    def _():
        hr_scr[...] = h0r_ref[...]
        hi_scr[...] = h0i_ref[...]

    u = u_ref[...].reshape(rows, S5_BLK_CH)
    ut = jnp.dot(perm_ref[...], u.astype(_MXU), preferred_element_type=F32).astype(_MXU)
    inr_scr[...] = jnp.dot(ut, bre_ref[0], preferred_element_type=F32)
    ini_scr[...] = jnp.dot(ut, bim_ref[0], preferred_element_type=F32)
    ar = ar_ref[...]
    ai = ai_ref[...]

    def body(t, carry):
        hr, hi = carry
        sl = pl.ds(pl.multiple_of(t * nb, nb), nb)
        nr = ar * hr - ai * hi + inr_scr[sl, :]
        ni = ar * hi + ai * hr + ini_scr[sl, :]
        inr_scr[sl, :] = nr
        ini_scr[sl, :] = ni
        return nr, ni

    hr, hi = lax.fori_loop(0, tc, body, (hr_scr[...], hi_scr[...]))
    hr_scr[...] = hr
    hi_scr[...] = hi
    yt = (jnp.dot(inr_scr[...].astype(_MXU), cre_ref[0], preferred_element_type=F32)
          - jnp.dot(ini_scr[...].astype(_MXU), cim_ref[0], preferred_element_type=F32))
    yt_hi = yt.astype(_MXU)
    yt_lo = (yt - yt_hi.astype(F32)).astype(_MXU)
    y = (jnp.dot(permt_ref[...], yt_hi, preferred_element_type=F32)
         + jnp.dot(permt_ref[...], yt_lo, preferred_element_type=F32)) + d_ref[...] * u
    y = 0.5 * y * (1.0 + jnp.tanh(math.sqrt(2.0 / math.pi) * (y + 0.044715 * (y * y * y))))
    y_ref[...] = y.reshape(y_ref.shape)

    @pl.when(step == n_steps - 1)
    def _():
        hr_out[...] = hr
        hi_out[...] = hi


def s5_scan(u3, h0r, h0i, l, ar, ai, bre, bim, cre, cim, d, *, nb, tc, blk):
    A, R, _ = u3.shape
    ga, tr = blk
    assert A == ga and ga * tr == nb * tc
    nblk = D // S5_BLK_CH
    n_steps = R // tr
    rows = nb * tc
    j = np.arange(rows)
    perm_np = np.zeros((rows, rows), np.float32)
    perm_np[(j % tc) * nb + j // tc, j] = 1.0
    perm = jnp.asarray(perm_np, _MXU)
    permt = jnp.asarray(perm_np.T, _MXU)
    st = lambda: pl.BlockSpec((None, nb, S5_BLK_ST), lambda c, s: (l, 0, c))
    vec = lambda w: pl.BlockSpec((1, w), lambda c, s: (0, c))
    mat = lambda a, b: pl.BlockSpec((1, a, b), lambda c, s: (c, 0, 0))
    return pl.pallas_call(
        functools.partial(_s5_kernel, tc=tc, nb=nb, n_steps=n_steps),
        out_shape=[jax.ShapeDtypeStruct(u3.shape, F32),
                   jax.ShapeDtypeStruct((nb, nblk * S5_BLK_ST), F32),
                   jax.ShapeDtypeStruct((nb, nblk * S5_BLK_ST), F32)],
        grid=(nblk, n_steps),
        in_specs=[pl.BlockSpec((ga, tr, S5_BLK_CH), lambda c, s: (0, s, c)),
                  st(), st(), vec(S5_BLK_ST), vec(S5_BLK_ST),
                  mat(S5_BLK_CH, S5_BLK_ST), mat(S5_BLK_CH, S5_BLK_ST),
                  mat(S5_BLK_ST, S5_BLK_CH), mat(S5_BLK_ST, S5_BLK_CH),
                  vec(S5_BLK_CH), _const_spec((rows, rows)), _const_spec((rows, rows))],
        out_specs=[pl.BlockSpec((ga, tr, S5_BLK_CH), lambda c, s: (0, s, c)),
                   pl.BlockSpec((nb, S5_BLK_ST), lambda c, s: (0, c)),
                   pl.BlockSpec((nb, S5_BLK_ST), lambda c, s: (0, c))],
        scratch_shapes=[pltpu.VMEM((rows, S5_BLK_ST), F32), pltpu.VMEM((rows, S5_BLK_ST), F32),
                        pltpu.VMEM((nb, S5_BLK_ST), F32), pltpu.VMEM((nb, S5_BLK_ST), F32)],
        compiler_params=_cparams(("parallel", "arbitrary")),
        name="s5_scan",
    )(u3, h0r, h0i, ar, ai, bre, bim, cre, cim, d, perm, permt)


def _s5_discretize(a_re, a_im, log_dt, b_re, b_im, c_re, c_im):
    g, p, hch = b_re.shape
    dt = jnp.exp(log_dt.astype(F32))[:, None]
    mag = jnp.exp(dt * a_re)
    abar_re = mag * jnp.cos(dt * a_im)
    abar_im = mag * jnp.sin(dt * a_im)
    den = a_re * a_re + a_im * a_im
    nr = abar_re - 1.0
    q_re = (nr * a_re + abar_im * a_im) / den
    q_im = (abar_im * a_re - nr * a_im) / den
    bb_re = q_re[..., None] * b_re - q_im[..., None] * b_im
    bb_im = q_re[..., None] * b_im + q_im[..., None] * b_re
    nblk = D // S5_BLK_CH
    gl = g // nblk
    eye = jnp.eye(gl, dtype=F32)

    def in_blocks(bb):
        t = jnp.transpose(bb, (0, 2, 1)).reshape(nblk, gl, hch, p)
        return jnp.einsum('cghp,gk->cghkp', t, eye).reshape(nblk, gl * hch, gl * p).astype(_MXU)

    def out_blocks(cc):
        t = jnp.transpose(cc, (0, 2, 1)).reshape(nblk, gl, p, hch)
        return jnp.einsum('cgph,gk->cgpkh', t, eye).reshape(nblk, gl * p, gl * hch).astype(_MXU)

    return (abar_re.reshape(1, g * p), abar_im.reshape(1, g * p), in_blocks(bb_re), in_blocks(bb_im),
            out_blocks(c_re), out_blocks(c_im))


def _ssd_kernel(xbc_ref, z_ref, dt_ref, cprev_ref, h0_ref, cw_ref, cb_ref, dtb_ref, alog_ref, dx_ref, ng_ref,
                eh_ref, y_ref, hfin_ref, ext_scr, h_scr, *, L, G, t_real, n_steps):
    step = pl.program_id(1)
    rng = range(G)
    NB = M_GROUPS * M_STATE

    def each(f, *lists):
        return [f(*xs) for xs in zip(*lists)]

    @pl.when(step == 0)
    def _():
        for s in rng:
            ext_scr[s] = jnp.zeros((8, M_CONV_DIM), F32)
            ext_scr[s, 5:8, :] = cprev_ref[s]
            for gi in range(M_GROUPS):
                h_scr[s, gi] = h0_ref[s, gi * M_HPG:(gi + 1) * M_HPG].reshape(M_GW, M_STATE).T

    tril = jnp.where(_iota((L, L), 1) <= _iota((L, L), 0), 1.0, 0.0).astype(_MXU)
    causal = _iota((L, L), 1) <= _iota((L, L), 0)
    eye_h = _eye(128, _MXU)
    eh = eh_ref[...]
    lane_head = _iota((1, M_GW), 1) // M_HEAD
    row8 = _iota((8, 1), 0)
    a = -jnp.exp(alog_ref[...])
    conv, dts = [], []
    for s in rng:
        x = xbc_ref[s * L:(s + 1) * L, :]
        prev8 = ext_scr[s]
        c = cb_ref[...] + cw_ref[3:4, :] * x
        for j in range(1, 4):
            xr = pltpu.roll(x, j, axis=0)
            head = jnp.where(row8 < j, pltpu.roll(prev8, j, axis=0), xr[0:8])
            xr = head if L == 8 else jnp.concatenate([head, xr[8:]], axis=0)
            c = c + cw_ref[3 - j:4 - j, :] * xr
        ext_scr[s] = x[L - 8:L]
        conv.append(c * _sigmoid(c))
        dt = _softplus(dt_ref[s * L:(s + 1) * L, :] + dtb_ref[...])
        if t_real < L:
            dt = jnp.where(_iota(dt.shape, 0) < t_real, dt, 0.0)
        dts.append(dt)
    xs = each(lambda c: c[:, 0:M_WIDTH], conv)
    acum = each(lambda d_: _mm_sel_l(tril, d_ * a), dts)
    acum_t = each(lambda x: _mm_nt_sel_l(eye_h, x), acum)
    dt_x = each(lambda d_: _mm_sel_r(d_, eh), dts)
    acum_x = each(lambda x: _mm_sel_r(x, eh), acum)
    acl_x = each(lambda x: x[L - 1:L, :], acum_x)
    xd = each(lambda x, d_: x * d_, xs, dt_x)
    xdd = each(lambda x, l_, c_: x * jnp.exp(l_ - c_), xd, acl_x, acum_x)
    eacum_x = each(jnp.exp, acum_x)
    cdec_x = each(jnp.exp, acl_x)
    ys = [[] for _ in rng]
    for gi in range(M_GROUPS):
        gs = slice(gi * M_GW, (gi + 1) * M_GW)
        bg = each(lambda c: c[:, M_WIDTH + gi * M_STATE:M_WIDTH + (gi + 1) * M_STATE], conv)
        cg = each(lambda c: c[:, M_WIDTH + NB + gi * M_STATE:M_WIDTH + NB + (gi + 1) * M_STATE], conv)
        h = [h_scr[s, gi] for s in rng]
        cbm = each(_mm_nt, cg, bg)
        bgt = each(lambda b_: _mm_nt(eye_h, b_), bg)
        y_off = each(lambda c_, h_, e_: _mm(c_, h_) * e_[:, gs], cg, h, eacum_x)
        ms = []
        for e in range(M_HPG):
            he = gi * M_HPG + e
            ms.append(each(lambda c_, a_, t_: c_ * jnp.exp(jnp.where(causal, a_[:, he:he + 1] - t_[he:he + 1, :],
                                                                     -jnp.inf)), cbm, acum, acum_t))
        if L % 128 == 0:
            mcat = [jnp.concatenate([ms[e][s] for e in range(M_HPG)], axis=1) for s in rng]
            xst = each(lambda x: jnp.concatenate([jnp.where(lane_head == e, x[:, gs], 0.0).astype(_MXU)
                                                  for e in range(M_HPG)], axis=0), xd)
            y_dg = each(_mm, mcat, xst)
        else:
            y_dg = [sum(_mm(ms[e][s], jnp.where(lane_head == e, xd[s][:, gs], 0.0)) for e in range(M_HPG))
                    for s in rng]
        hn = each(lambda h_, d_, b_, x: h_ * d_[:, gs] + _mm(b_, x[:, gs]), h, cdec_x, bgt, xdd)
        for s in rng:
            h_scr[s, gi] = hn[s]
            ys[s].append(y_off[s] + y_dg[s])
    for s in rng:
        y = jnp.concatenate(ys[s], axis=1) + dx_ref[...] * xs[s]
        zz = z_ref[s * L:(s + 1) * L, :]
        y = y * (zz * _sigmoid(zz))
        outs = []
        for gi in range(M_GROUPS):
            yg = y[:, gi * M_GW:(gi + 1) * M_GW]
            outs.append(yg * lax.rsqrt(jnp.mean(yg * yg, axis=-1, keepdims=True) + EPS))
        y_ref[s * L:(s + 1) * L, :] = jnp.concatenate(outs, axis=1) * ng_ref[...]

    @pl.when(step == n_steps - 1)
    def _():
        for s in rng:
            for gi in range(M_GROUPS):
                hfin_ref[s, gi * M_HPG:(gi + 1) * M_HPG] = h_scr[s, gi].T.reshape(M_HPG, M_HEAD, M_STATE)


def ssd_block(xbc, z, dt, conv_prev, h0, l, conv_w, conv_b, dt_bias, a_log, d_x, norm_g, eh, stack, *, B, T, L, G,
              t_real):
    n_steps = T // L
    assert G == 1 or n_steps == 1
    heads = M_GROUPS * M_HPG
    seq = lambda w: pl.BlockSpec((G * L, w), lambda b, s: (b * n_steps + s, 0))
    st_spec = lambda: pl.BlockSpec((None, G, heads, M_HEAD, M_STATE), lambda b, s: (l, b, 0, 0, 0))
    in_specs = [seq(M_CONV_DIM), seq(M_WIDTH), seq(128),
                pl.BlockSpec((None, G, 3, M_CONV_DIM), lambda b, s: (l, b, 0, 0)),
                st_spec(),
                _const_spec((4, M_CONV_DIM)), _const_spec((1, M_CONV_DIM)),
                _const_spec((1, 128)), _const_spec((1, 128)),
                _const_spec((1, M_WIDTH)), _const_spec((1, M_WIDTH)),
                _const_spec((128, M_WIDTH))]
    kern, in_specs, args, aliases = _stacked_out(
        functools.partial(_ssd_kernel, L=L, G=G, t_real=t_real, n_steps=n_steps), in_specs,
        [xbc, z, dt, conv_prev, h0, conv_w, conv_b, dt_bias, a_log, d_x, norm_g, eh], 1, stack)
    return pl.pallas_call(
        kern,
        out_shape=[jax.ShapeDtypeStruct((B * T, M_WIDTH), F32),
                   jax.ShapeDtypeStruct((h0.shape[0], B, heads, M_HEAD, M_STATE), F32)],
        grid=(B // G, n_steps),
        in_specs=in_specs,
        out_specs=[seq(M_WIDTH), st_spec()],
        scratch_shapes=[pltpu.VMEM((G, 8, M_CONV_DIM), F32),
                        pltpu.VMEM((G, M_GROUPS, M_STATE, M_GW), F32)],
        input_output_aliases=aliases,
        compiler_params=_cparams(("parallel", "arbitrary")),
        name="ssd",
    )(*args)


def _attn_kernel(q_ref, k_ref, v_ref, o_ref, *, nb, tq, heads_split):
    scale = XA_HD ** -0.5
    for j in range(nb):
        if heads_split:
            k_all = pltpu.einshape("mhd->hmd", k_ref[j])
            v_all = pltpu.einshape("mhd->hmd", v_ref[j])
        outs = []
        for hd in range(XA_HEADS):
            cs = slice(hd * XA_HD, (hd + 1) * XA_HD)
            q = q_ref[j * tq:(j + 1) * tq, cs]
            kh = k_all[hd] if heads_split else k_ref[j, :, cs]
            vh = v_all[hd] if heads_split else v_ref[j, :, cs]
            s = _mm_nt(q, kh) * scale
            s = s - jnp.max(s, axis=-1, keepdims=True)
            p = jnp.exp(s)
            p = p / jnp.sum(p, axis=-1, keepdims=True)
            outs.append(_mm(p, vh))
        o_ref[j * tq:(j + 1) * tq, :] = jnp.concatenate(outs, axis=1)


def cross_attention(q, mk, mv, kv_index, *, B, T, tq, nb):
    nlead = len(kv_index)
    heads_split = mk.ndim - nlead == 4
    kv_blk = mk.shape[nlead + 1:]
    n_t = T // tq
    kv_spec = pl.BlockSpec((None,) * nlead + (nb,) + kv_blk,
                           lambda b, s: tuple(kv_index) + (b,) + (0,) * len(kv_blk))
    q_spec = pl.BlockSpec((nb * tq, D), lambda b, s: (b * n_t + s, 0))
    return pl.pallas_call(
        functools.partial(_attn_kernel, nb=nb, tq=tq, heads_split=heads_split),
        out_shape=jax.ShapeDtypeStruct((B * T, D), F32),
        grid=(B // nb, n_t),
        in_specs=[q_spec, kv_spec, kv_spec],
        out_specs=q_spec,
        compiler_params=_cparams(("parallel", "arbitrary")),
        name="xattn",
    )(q, mk, mv)


def _pad_time(x2, B, T, Tp):
    if Tp == T:
        return x2
    w = x2.shape[-1]
    return jnp.pad(x2.reshape(B, T, w), ((0, 0), (0, Tp - T), (0, 0))).reshape(B * Tp, w)


def _layer(l, h, grp, W, st, acc):
    T, B = grp['T'], grp['B']
    n = T * B
    gates = norm_matmul(h, W['g_mix_pre'], W['w_gates'], name="in_gates")
    p_r = norm_matmul(h, W['g_mix_pre'], W['w_rwkv'], name="in_rwkv")
    u_s5 = norm_matmul(h, W['g_mix_pre'], W['w_s5'], name="in_s5")
    z_m = norm_matmul(h, W['g_mix_pre'], W['w_z'], name="in_z")
    xbc = norm_matmul(h, W['g_mix_pre'], W['w_xbc'], name="in_xbc")
    dt_m = norm_matmul(h, W['g_mix_pre'], W['w_dt'], tn=128, name="in_dt")

    seqs = rwkv_prep(p_r, st['shift'][l], W['mu'], W['wa'], W['w0'], W['a0'], W['g2'],
                     W['k_k'], W['k_a'], T=T)
    Tp = grp['rwkv_Tpad']
    seqs = [_pad_time(x, B, T, Tp).reshape(B, Tp, D) for x in seqs]
    yr, s_fin = rwkv_recurrence(*seqs, st['rwkv'], l, W['r_k'], W['ln_w'], W['ln_b'], acc[0],
                                C=grp['rwkv_C'], G=grp['rwkv_G'])
    yr = yr[:, :T].reshape(n, D)
    shift_new = p_r.reshape(B, T, -1)[:, T - 1]

    u3 = u_s5.reshape(grp['s5_view'])
    ys, s5r, s5i = s5_scan(u3, st['s5r'], st['s5i'], l, W['s5_ar'], W['s5_ai'], W['s5_bre'], W['s5_bim'],
                           W['s5_cre'], W['s5_cim'], W['s5_d'], nb=B, tc=grp['s5_tc'], blk=grp['s5_blk'])
    ys = ys.reshape(n, D)

    Lc = grp['ssd_L']
    Tm = grp['ssd_Tpad']
    ym, ssm_fin = ssd_block(_pad_time(xbc, B, T, Tm), _pad_time(z_m, B, T, Tm), _pad_time(dt_m, B, T, Tm),
                            st['conv'], st['ssm'], l, W['conv_w'], W['conv_b'], W['dt_bias'],
                            W['a_log'], W['d_x'], W['m_norm'], W['eh'], acc[1], B=B, T=Tm, L=Lc, G=grp['ssd_G'],
                            t_real=min(T, Lc))
    ym = ym.reshape(B, Tm, M_WIDTH)[:, :T].reshape(n, M_WIDTH)
    conv_new = jnp.concatenate([st['conv'][l], xbc.reshape(B, T, M_CONV_DIM)[:, max(T - 3, 0):]], axis=1)[:, -3:]

    h = mixer_merge(gates, yr, ys, ym, h, W['w_out_rwkv'], W['s5_w_glu'], W['w_out_s5'], W['w_out_mamba'],
                    W['w_out'], W['g_mix_post'])

    q = norm_matmul(h, W['g_xa_pre'], W['xa_wq'], name="xa_q")
    o = cross_attention(q, st['mk'], st['mv'], st['kv_index'](l), B=B, T=T, tq=grp['xa_tq'], nb=grp['xa_nb'])
    h = matmul_norm_residual(o, W['xa_wo'], W['g_xa_post'], h, name="xa_out")

    h = mlp_block(h, W['g_mlp_pre'], W['mlp_w1'], W['mlp_w2'], W['g_mlp_post'])
    return h, (s_fin, shift_new, s5r.reshape(B, 64, 64), s5i.reshape(B, 64, 64), conv_new, ssm_fin)


def _group_cfg(T, B):
    cfg = dict(T=T, B=B)
    g = 8 if B % 8 == 0 else 1
    if T % 64 == 0:
        cfg.update(rwkv_C=64, rwkv_Tpad=T, rwkv_G=g)
    else:
        tp = -(-T // 8) * 8
        cfg.update(rwkv_C=tp, rwkv_Tpad=tp, rwkv_G=16 if B % 16 == 0 else g)
    if T % 8 == 0 and (512 // B) >= 8 and T % (512 // B) == 0:
        tc = 512 // B
        cfg.update(s5_view=(B, T, D), s5_tc=tc, s5_blk=(B, tc))
    else:
        cfg.update(s5_view=(1, B * T, D), s5_tc=T, s5_blk=(1, B * T))
    if T % 128 == 0:
        cfg.update(ssd_L=128, ssd_Tpad=T, ssd_G=1)
    else:
        tp = -(-T // 8) * 8
        cfg.update(ssd_L=tp, ssd_Tpad=tp, ssd_G=4 if B % 4 == 0 else 1)
    if T >= 64:
        cfg.update(xa_tq=_row_tile(T, 512), xa_nb=1)
    else:
        cfg.update(xa_tq=T, xa_nb=4 if B % 4 == 0 else 1)
    return cfg


def kernel(x_prompt, x_sample, cache_mem_k, cache_mem_v, state_rwkv, state_rwkv_shift, state_s5_re, state_s5_im, state_conv, state_ssm, mem_prompt, norm_mix_pre, norm_mix_post, norm_xa_pre, norm_xa_post, norm_mlp_pre, norm_mlp_post, norm_mem, w_in, w_out, rwkv_mu, rwkv_w0, rwkv_w2, rwkv_a0, rwkv_a2, rwkv_g2, rwkv_k_k, rwkv_k_a, rwkv_r_k, rwkv_ln_w, rwkv_ln_b, w_out_rwkv, s5_a_re, s5_a_im, s5_log_dt, s5_b_re, s5_b_im, s5_c_re, s5_c_im, s5_d, s5_w_glu, w_out_s5, m_conv_w, m_conv_b, m_dt_bias, m_a_log, m_d, m_norm, w_out_mamba, xa_wq, xa_wk, xa_wv, xa_wo, mlp_w1, mlp_w2):
    depth = w_in.shape[0]
    bp, tp, _ = x_prompt.shape
    bs, ts, _ = x_sample.shape
    mlen = mem_prompt.shape[1]
    bf = lambda x: x.astype(_MXU)
    row = lambda x: x.reshape(1, -1).astype(F32)

    eh = (jnp.arange(128)[:, None] == (jnp.arange(M_WIDTH)[None, :] // M_HEAD)).astype(_MXU)

    col = [0]
    for sz in (3 * D, 3 * D + 256, D, M_WIDTH, M_CONV_DIM, 32):
        col.append(col[-1] + sz)
    layers = []
    for l in range(depth):
        wl = w_in[l]
        ar, ai, bre, bim, cre, cim = _s5_discretize(s5_a_re[l], s5_a_im[l], s5_log_dt[l], s5_b_re[l], s5_b_im[l],
                                                    s5_c_re[l], s5_c_im[l])
        zero = jnp.zeros((64, D), F32)
        wa = jnp.concatenate([jnp.concatenate([rwkv_w2[l], zero], axis=1),
                              jnp.concatenate([zero, rwkv_a2[l]], axis=1)], axis=0)
        pad32 = lambda x: jnp.pad(x.reshape(1, -1).astype(F32), ((0, 0), (0, 128 - x.shape[-1])))
        layers.append(dict(
            g_mix_pre=row(norm_mix_pre[l]), g_mix_post=row(norm_mix_post[l]),
            g_xa_pre=row(norm_xa_pre[l]), g_xa_post=row(norm_xa_post[l]),
            g_mlp_pre=row(norm_mlp_pre[l]), g_mlp_post=row(norm_mlp_post[l]),
            w_gates=bf(wl[:, col[0]:col[1]]), w_rwkv=bf(wl[:, col[1]:col[2]]), w_s5=bf(wl[:, col[2]:col[3]]),
            w_z=bf(wl[:, col[3]:col[4]]), w_xbc=bf(wl[:, col[4]:col[5]]),
            w_dt=bf(jnp.pad(wl[:, col[5]:col[6]], ((0, 0), (0, 96)))),
            mu=row(rwkv_mu[l]), wa=bf(wa), w0=row(rwkv_w0[l]), a0=row(rwkv_a0[l]), g2=bf(rwkv_g2[l]),
            k_k=row(rwkv_k_k[l]), k_a=row(rwkv_k_a[l]), r_k=row(rwkv_r_k[l]),
            ln_w=row(rwkv_ln_w[l]), ln_b=row(rwkv_ln_b[l]),
            s5_ar=ar, s5_ai=ai, s5_bre=bre, s5_bim=bim, s5_cre=cre, s5_cim=cim, s5_d=row(s5_d[l]),
            conv_w=m_conv_w[l].astype(F32), conv_b=row(m_conv_b[l]), dt_bias=pad32(m_dt_bias[l]),
            a_log=pad32(m_a_log[l]), d_x=row(jnp.repeat(m_d[l], M_HEAD)), m_norm=row(m_norm[l]), eh=eh,
            w_out_rwkv=bf(w_out_rwkv[l]), s5_w_glu=bf(s5_w_glu[l]), w_out_s5=bf(w_out_s5[l]),
            w_out_mamba=bf(w_out_mamba[l]), w_out=bf(w_out[l]),
            xa_wq=bf(xa_wq[l]), xa_wk=bf(xa_wk[l]), xa_wv=bf(xa_wv[l]), xa_wo=bf(xa_wo[l]),
            mlp_w1=bf(mlp_w1[l]), mlp_w2=bf(mlp_w2[l]), g_mem=row(norm_mem[l]),
        ))

    hp = x_prompt.reshape(bp * tp, D).astype(F32)
    hs = x_sample.reshape(bs * ts, D).astype(F32)
    mem2 = mem_prompt.reshape(bp * mlen, D).astype(F32)

    cfg_p = _group_cfg(tp, bp)
    cfg_s = _group_cfg(ts, bs)
    zeros_p = dict(
        rwkv=jnp.zeros((depth, bp) + state_rwkv.shape[2:], F32),
        shift=jnp.zeros((depth, bp, state_rwkv_shift.shape[-1]), F32),
        s5r=jnp.zeros((depth, bp, state_s5_re.shape[2] * state_s5_re.shape[3]), F32),
        s5i=jnp.zeros((depth, bp, state_s5_re.shape[2] * state_s5_re.shape[3]), F32),
        conv=jnp.zeros((depth, bp) + state_conv.shape[2:], F32),
        ssm=jnp.zeros((depth, bp) + state_ssm.shape[2:], F32),
    )
    st_s = dict(
        rwkv=state_rwkv, shift=state_rwkv_shift,
        s5r=state_s5_re.reshape(depth, bs, -1), s5i=state_s5_im.reshape(depth, bs, -1),
        conv=state_conv, ssm=state_ssm,
        mk=cache_mem_k, mv=cache_mem_v,
        kv_index=lambda l: (l,),
    )

    mk_out, mv_out = [], []
    st_p_out = [[] for _ in range(6)]
    st_s_out = [[] for _ in range(6)]
    acc_p = acc_s = (None, None)
    for l in range(depth):
        W = layers[l]
        mk = norm_matmul(mem2, W['g_mem'], W['xa_wk'], name="mem_k")
        mv = norm_matmul(mem2, W['g_mem'], W['xa_wv'], name="mem_v")
        st_p = dict(zeros_p, mk=mk.reshape(bp, mlen, D), mv=mv.reshape(bp, mlen, D), kv_index=lambda l: ())
        hp, new_p = _layer(l, hp, cfg_p, W, st_p, acc_p)
        hs, new_s = _layer(l, hs, cfg_s, W, st_s, acc_s)
        acc_p = (new_p[0], new_p[5])
        acc_s = (new_s[0], new_s[5])
        mk_out.append(mk.reshape(bp, mlen, XA_HEADS, XA_HD))
        mv_out.append(mv.reshape(bp, mlen, XA_HEADS, XA_HD))
        for i in range(1, 5):
            st_p_out[i].append(new_p[i])
            st_s_out[i].append(new_s[i])

    stk = lambda xs: jnp.stack(xs).astype(F32)
    y_p = hp.reshape(bp, tp, D).astype(x_prompt.dtype)
    y_s = hs.reshape(bs, ts, D).astype(x_sample.dtype)
    return (y_p, y_s, stk(mk_out), stk(mv_out),
            acc_p[0], *[stk(st_p_out[i]) for i in range(1, 5)], acc_p[1],
            acc_s[0], *[stk(st_s_out[i]) for i in range(1, 5)], acc_s[1])
```

```python
import functools
import math

import numpy as np
import jax
import jax.numpy as jnp
from jax import lax
from jax.experimental import pallas as pl
from jax.experimental.pallas import tpu as pltpu

F32 = jnp.float32
_MXU = jnp.bfloat16
EPS = 1e-6
R_LN_EPS = 64e-5
D = 1024
R_HEAD = 64
R_PAIR = 2 * R_HEAD
S5_BLK_CH = 128
S5_BLK_ST = 512
M_HEAD = 64
M_STATE = 128
M_GROUPS = 4
M_HPG = 8
M_GW = M_HPG * M_HEAD
M_WIDTH = 2048
M_CONV_DIM = 3072
XA_HEADS = 4
XA_HD = 256
VMEM_LIMIT = 56 * 1024 * 1024


def _mm(a, b):
    return jnp.dot(a.astype(_MXU), b.astype(_MXU), preferred_element_type=F32)


def _mm_nt(a, b):
    return lax.dot_general(a.astype(_MXU), b.astype(_MXU), (((1,), (1,)), ((), ())),
                           preferred_element_type=F32)


def _mm_tn(a, b):
    return lax.dot_general(a.astype(_MXU), b.astype(_MXU), (((0,), (0,)), ((), ())),
                           preferred_element_type=F32)


def _split3(x):
    hi = x.astype(_MXU)
    r1 = x - hi.astype(F32)
    mid = r1.astype(_MXU)
    lo = (r1 - mid.astype(F32)).astype(_MXU)
    return hi, mid, lo


def _mm_sel_l(sel, x):
    return sum(jnp.dot(sel, p, preferred_element_type=F32) for p in _split3(x))


def _mm_sel_r(x, sel):
    return sum(jnp.dot(p, sel, preferred_element_type=F32) for p in _split3(x))


def _mm_nt_sel_l(sel, x):
    return sum(lax.dot_general(sel, p, (((1,), (1,)), ((), ())), preferred_element_type=F32)
               for p in _split3(x))


def _sigmoid(x):
    return 1.0 / (1.0 + jnp.exp(-x))


def _softplus(x):
    return jnp.maximum(x, 0.0) + jnp.log(1.0 + jnp.exp(-jnp.abs(x)))


def _rms(x, g):
    return x * lax.rsqrt(jnp.mean(x * x, axis=-1, keepdims=True) + EPS) * g


def _iota(shape, dim):
    return lax.broadcasted_iota(jnp.int32, shape, dim)


def _eye(n, dtype):
    return jnp.where(_iota((n, n), 0) == _iota((n, n), 1), 1.0, 0.0).astype(dtype)


def _cparams(sem):
    return pltpu.CompilerParams(dimension_semantics=sem, vmem_limit_bytes=VMEM_LIMIT)


def _row_tile(n, want):
    t = min(n, want)
    while n % t:
        t //= 2
    return t


def _col_tile(c, cap):
    best = 128
    for t in range(128, min(c, cap) + 1, 128):
        if c % t == 0:
            best = t
    return best


def _const_spec(shape):
    nd = len(shape)
    return pl.BlockSpec(shape, lambda *_: (0,) * nd, pipeline_mode=pl.Buffered(1))


def _norm_mm_kernel(x_ref, g_ref, w_ref, o_ref, xn_ref):
    @pl.when(pl.program_id(1) == 0)
    def _():
        xn_ref[...] = _rms(x_ref[...], g_ref[...]).astype(_MXU)

    o_ref[...] = jnp.dot(xn_ref[...], w_ref[...], preferred_element_type=F32)


def norm_matmul(x, g, w, *, tm=1024, tn=1664, name="norm_mm"):
    n, d = x.shape
    c = w.shape[1]
    tm = _row_tile(n, tm)
    tn = _col_tile(c, tn)
    return pl.pallas_call(
        _norm_mm_kernel,
        out_shape=jax.ShapeDtypeStruct((n, c), F32),
        grid=(n // tm, c // tn),
        in_specs=[pl.BlockSpec((tm, d), lambda i, j: (i, 0)),
                  pl.BlockSpec((1, d), lambda i, j: (0, 0)),
                  pl.BlockSpec((d, tn), lambda i, j: (0, j))],
        out_specs=pl.BlockSpec((tm, tn), lambda i, j: (i, j)),
        scratch_shapes=[pltpu.VMEM((tm, d), _MXU)],
        compiler_params=_cparams(("parallel", "arbitrary")),
        name=name,
    )(x, g, w)


def _mm_norm_res_kernel(x_ref, w_ref, g_ref, h_ref, o_ref):
    y = _mm(x_ref[...], w_ref[...])
    o_ref[...] = h_ref[...] + _rms(y, g_ref[...])


def matmul_norm_residual(x, w, g, h, *, tm=512, name="mm_norm_res"):
    n, k = x.shape
    d = w.shape[1]
    tm = _row_tile(n, tm)
    return pl.pallas_call(
        _mm_norm_res_kernel,
        out_shape=jax.ShapeDtypeStruct((n, d), F32),
        grid=(n // tm,),
        in_specs=[pl.BlockSpec((tm, k), lambda i: (i, 0)),
                  _const_spec((k, d)),
                  _const_spec((1, d)),
                  pl.BlockSpec((tm, d), lambda i: (i, 0))],
        out_specs=pl.BlockSpec((tm, d), lambda i: (i, 0)),
        compiler_params=_cparams(("parallel",)),
        name=name,
    )(x, w, g, h)


def _mlp_kernel(h_ref, g1_ref, w1_ref, w2_ref, g2_ref, o_ref, *, n_chunks, ck):
    h = h_ref[...]
    xn = _rms(h, g1_ref[...]).astype(_MXU)
    acc = jnp.zeros(h.shape, F32)
    for j in range(n_chunks):
        a = jnp.dot(xn, w1_ref[:, j * ck:(j + 1) * ck], preferred_element_type=F32)
        a = jnp.square(jnp.maximum(a, 0.0))
        acc = acc + jnp.dot(a.astype(_MXU), w2_ref[j * ck:(j + 1) * ck, :], preferred_element_type=F32)
    o_ref[...] = h + _rms(acc, g2_ref[...])


def mlp_block(h, g1, w1, w2, g2, *, tm=512, ck=1024):
    n, d = h.shape
    f = w1.shape[1]
    tm = _row_tile(n, tm)
    return pl.pallas_call(
        functools.partial(_mlp_kernel, n_chunks=f // ck, ck=ck),
        out_shape=jax.ShapeDtypeStruct((n, d), F32),
        grid=(n // tm,),
        in_specs=[pl.BlockSpec((tm, d), lambda i: (i, 0)),
                  _const_spec((1, d)),
                  _const_spec((d, f)),
                  _const_spec((f, d)),
                  _const_spec((1, d))],
        out_specs=pl.BlockSpec((tm, d), lambda i: (i, 0)),
        compiler_params=_cparams(("parallel",)),
        name="mlp",
    )(h, g1, w1, w2, g2)


def _norm_mm_tmajor_kernel(x_ref, g_ref, w_ref, perm_ref, o_ref):
    nb, tt, d = x_ref.shape
    xn = _rms(x_ref[...].reshape(nb * tt, d), g_ref[...]).astype(_MXU)
    xt = jnp.dot(perm_ref[...], xn, preferred_element_type=F32).astype(_MXU)
    o_ref[...] = jnp.dot(xt, w_ref[...], preferred_element_type=F32).reshape(o_ref.shape)


def norm_matmul_tmajor(x3, g, w, *, rows=512, name="norm_mm_t"):
    nb, t, d = x3.shape
    c = w.shape[1]
    tt = rows // nb
    assert t % tt == 0 and tt % 8 == 0
    perm = jnp.asarray(_time_major_perm(nb, tt), _MXU)
    return pl.pallas_call(
        _norm_mm_tmajor_kernel,
        out_shape=jax.ShapeDtypeStruct((t, nb, c), F32),
        grid=(t // tt,),
        in_specs=[pl.BlockSpec((nb, tt, d), lambda i: (0, i, 0)),
                  _const_spec((1, d)), _const_spec((d, c)), _const_spec((nb * tt, nb * tt))],
        out_specs=pl.BlockSpec((tt, nb, c), lambda i: (i, 0, 0)),
        compiler_params=_cparams(("parallel",)),
        name=name,
    )(x3, g, w, perm)


def _merge_kernel(gates_ref, yr_ref, ys_ref, ym_ref, h_ref, wr_ref, wglu_ref, ws_ref, wm_ref, wo_ref, g_ref,
                  *rest, ys_tmajor):
    o_ref = rest[-1]
    rows2d = lambda ref: ref[...].reshape(-1, ref.shape[-1])
    o_r = _mm(rows2d(yr_ref), wr_ref[...])
    ys = rows2d(ys_ref)
    y3 = ys * _sigmoid(_mm(ys, wglu_ref[...]))
    if ys_tmajor:
        y3 = jnp.dot(rest[0][...], y3.astype(_MXU), preferred_element_type=F32)
    o_s = _mm(y3, ws_ref[...])
    o_m = _mm(rows2d(ym_ref), wm_ref[...])
    gates = rows2d(gates_ref)
    merged = (_sigmoid(gates[:, 0:D]) * o_r + _sigmoid(gates[:, D:2 * D]) * o_s
              + _sigmoid(gates[:, 2 * D:3 * D]) * o_m)
    mix = _mm(merged, wo_ref[...])
    o_ref[...] = (rows2d(h_ref) + _rms(mix, g_ref[...])).reshape(o_ref.shape)


def mixer_merge(gates, yr, ys, ym, h, wr, wglu, ws, wm, wo, g, *, tm=256, seq=None):
    n = h.shape[0]
    weights = [_const_spec((D, D)), _const_spec((D, D)), _const_spec((D, D)),
               _const_spec((M_WIDTH, D)), _const_spec((D, D)), _const_spec((1, D))]
    if seq is None:
        tm = _row_tile(n, tm)
        row = lambda w: pl.BlockSpec((tm, w), lambda i: (i, 0))
        in_specs = [row(3 * D), row(D), row(D), row(M_WIDTH), row(D)] + weights
        args = [gates, yr, ys, ym, h, wr, wglu, ws, wm, wo, g]
        out_spec, out_shape, grid = row(D), (n, D), (n // tm,)
    else:
        nb, t = seq
        tt = tm // nb
        assert t % tt == 0 and tt % 8 == 0
        row = lambda w: pl.BlockSpec((nb, tt, w), lambda i: (0, i, 0))
        v3 = lambda x: x.reshape(nb, t, x.shape[-1])
        in_specs = ([row(3 * D), row(D), pl.BlockSpec((tt, nb, D), lambda i: (i, 0, 0)), row(M_WIDTH), row(D)]
                    + weights + [_const_spec((nb * tt, nb * tt))])
        args = [v3(gates), v3(yr), ys, v3(ym), v3(h), wr, wglu, ws, wm, wo, g,
                jnp.asarray(_time_major_perm(nb, tt).T, _MXU)]
        out_spec, out_shape, grid = row(D), (nb, t, D), (t // tt,)
    out = pl.pallas_call(
        functools.partial(_merge_kernel, ys_tmajor=seq is not None),
        out_shape=jax.ShapeDtypeStruct(out_shape, F32),
        grid=grid,
        in_specs=in_specs,
        out_specs=out_spec,
        compiler_params=_cparams(("parallel",)),
        name="mixer_merge",
    )(*args)
    return out.reshape(n, D)


def _rwkv_prep_kernel(p_ref, aux_ref, shift_ref, mu_ref, wa_ref, w0_ref, a0_ref, g2_ref, kk_ref, ka_ref,
                      r_ref, e_ref, k_ref, v_ref, kk_out_ref, a_out_ref, g_ref, *, T, tiles_per_seq):
    i = pl.program_id(0)
    p = p_ref[...]
    tm = p.shape[0]
    row = _iota((tm, 1), 0)
    rolled = pltpu.roll(p, 1, axis=0)
    if tiles_per_seq >= 1:
        b = i // tiles_per_seq
        tt = i - b * tiles_per_seq
        first = jnp.where(tt == 0, shift_ref[pl.ds(b % 8, 1), :], aux_ref[7:8, :])
        prev = jnp.where(row == 0, first, rolled)
    else:
        prev = jnp.where((row & (T - 1)) == 0, shift_ref[...], rolled)
    pm = p + mu_ref[...] * (prev - p)
    r = pm[:, 0:D]
    k = pm[:, D:2 * D]
    v = pm[:, 2 * D:3 * D]
    x_wa = pm[:, 3 * D:3 * D + 128]
    xg = pm[:, 3 * D + 128:3 * D + 256]
    lora_in = jnp.where(_iota(x_wa.shape, 1) < 64, jnp.tanh(x_wa), x_wa)
    lwa = _mm(lora_in, wa_ref[...])
    zw = w0_ref[...] + lwa[:, 0:D]
    w_log = -_softplus(-zw) - 0.5
    a = _sigmoid(a0_ref[...] + lwa[:, D:2 * D])
    g = _mm(_sigmoid(xg), g2_ref[...])
    r_ref[...] = r
    e_ref[...] = jnp.exp(w_log)
    k_ref[...] = k * (1.0 + (a - 1.0) * ka_ref[...])
    v_ref[...] = v
    kk_out_ref[...] = k * kk_ref[...]
    a_out_ref[...] = a
    g_ref[...] = g


def rwkv_prep(p, shift_prev, mu, wa, w0, a0, g2, k_k, k_a, *, T, tm=256):
    n, rc = p.shape
    tm = _row_tile(n, tm)
    if tm <= T:
        assert T % tm == 0 and tm % 8 == 0 and shift_prev.shape[0] % 8 == 0
        tps = T // tm
        aux_spec = pl.BlockSpec((8, rc), lambda i: (jnp.maximum(i * (tm // 8) - 1, 0), 0))
        shift_spec = pl.BlockSpec((8, rc), lambda i: ((i // tps) // 8, 0))
        shift_arg = shift_prev
    else:
        assert tm % T == 0 and (T & (T - 1)) == 0
        tps = 0
        aux_spec = pl.BlockSpec((8, rc), lambda i: (0, 0))
        shift_spec = pl.BlockSpec((tm, rc), lambda i: (i, 0))
        shift_arg = jnp.repeat(shift_prev, T, axis=0)
    row = lambda: pl.BlockSpec((tm, D), lambda i: (i, 0))
    outs = [jax.ShapeDtypeStruct((n, D), F32)] * 7
    return pl.pallas_call(
        functools.partial(_rwkv_prep_kernel, T=T, tiles_per_seq=tps),
        out_shape=outs,
        grid=(n // tm,),
        in_specs=[pl.BlockSpec((tm, rc), lambda i: (i, 0)),
                  aux_spec, shift_spec,
                  _const_spec((1, rc)),
                  _const_spec((128, 2 * D)),
                  _const_spec((1, D)), _const_spec((1, D)),
                  _const_spec((128, D)),
                  _const_spec((1, D)), _const_spec((1, D))],
        out_specs=[row() for _ in range(7)],
        compiler_params=_cparams(("parallel",)),
        name="rwkv_prep",
    )(p, p, shift_arg, mu, wa, w0, a0, g2, k_k, k_a)


def _rwkv_rec_kernel(r_ref, e_ref, k_ref, v_ref, kk_ref, a_ref, g_ref, s0_ref, rk_ref, lnw_ref, lnb_ref,
                     y_ref, sfin_ref, s_scr, *, C, G, NP, n_steps):
    step = pl.program_id(2)
    C2 = 2 * C
    probs = [(gi, pp) for gi in range(G) for pp in range(NP)]
    rng = range(len(probs))

    def ld(ref, q):
        gi, pp = probs[q]
        return ref[gi, :, pp * R_PAIR:(pp + 1) * R_PAIR]

    def par(ref, q):
        pp = probs[q][1]
        return ref[:, pp * R_PAIR:(pp + 1) * R_PAIR]

    lane = _iota((1, R_PAIR), 1)
    m0 = lane < R_HEAD
    ri = _iota((R_PAIR, R_PAIR), 0)
    ci = _iota((R_PAIR, R_PAIR), 1)
    bd = (ri < R_HEAD) == (ci < R_HEAD)
    rt_i = _iota((C2, C2), 0)
    ct_i = _iota((C2, C2), 1)
    same = (rt_i < C) == (ct_i < C)
    tr = jnp.where(rt_i < C, rt_i, rt_i - C)
    ts = jnp.where(ct_i < C, ct_i, ct_i - C)
    m_sl = same & (ts < tr)
    m_li = same & (ts <= tr)
    eye2 = jnp.where(rt_i == ct_i, 1.0, 0.0)
    lvl_masks = []
    m = 1
    while m < C:
        lvl_masks.append(((tr & ~(2 * m - 1)) == (ts & ~(2 * m - 1))) & ((tr & m) != 0) & ((ts & m) == 0))
        m *= 2
    tril = jnp.where(_iota((C, C), 1) <= _iota((C, C), 0), 1.0, 0.0).astype(_MXU)

    @pl.when(step == 0)
    def _():
        z = jnp.zeros((R_HEAD, R_HEAD), F32)
        for q, (gi, pp) in enumerate(probs):
            s_scr[q] = jnp.concatenate([jnp.concatenate([s0_ref[gi, 2 * pp], z], axis=1),
                                        jnp.concatenate([z, s0_ref[gi, 2 * pp + 1]], axis=1)], axis=0)

    def stack(x):
        return jnp.concatenate([jnp.where(m0, x, 0.0), jnp.where(m0, 0.0, x)], axis=0)

    def fold(xs):
        return xs[:C] + xs[C:]

    def each(f, *lists):
        return [f(*xs) for xs in zip(*lists)]

    def head_sum(x):
        s0 = jnp.sum(jnp.where(m0, x, 0.0), axis=-1, keepdims=True)
        s1 = jnp.sum(jnp.where(m0, 0.0, x), axis=-1, keepdims=True)
        return jnp.where(m0, s0, s1)

    r = [ld(r_ref, q) for q in rng]
    e = [ld(e_ref, q) for q in rng]
    k = [ld(k_ref, q) for q in rng]
    v = [ld(v_ref, q) for q in rng]
    kk = [ld(kk_ref, q) for q in rng]
    an = each(lambda x: -x * lax.rsqrt(head_sum(x * x) + 1e-12), kk)
    bn = each(lambda n_, q: -n_ * ld(a_ref, q), an, rng)
    cs = each(lambda x: _mm_sel_l(tril, x), e)
    cl = each(lambda x: x[C - 1:C, :], cs)
    at = each(lambda a_, e_, c_: a_ * jnp.exp(e_ - c_), an, e, cs)
    rt = each(lambda r_, c_: r_ * jnp.exp(-c_), r, cs)
    ecs = each(jnp.exp, cs)
    bt_ = each(lambda b_, x: b_ * x, bn, ecs)
    kt = each(lambda k_, x: k_ * x, k, ecs)
    ecl = each(lambda c_, l_: jnp.exp(c_ - l_), cs, cl)
    bh = each(lambda b_, x: b_ * x, bn, ecl)
    kh = each(lambda k_, x: k_ * x, k, ecl)
    As, Rs, Vs = each(stack, at), each(stack, rt), each(stack, v)
    b2 = each(lambda x: jnp.concatenate([x, x], axis=0), bt_)
    k2 = each(lambda x: jnp.concatenate([x, x], axis=0), kt)
    N = each(lambda a_, b_: jnp.where(m_sl, _mm_nt(a_, b_), 0.0), As, b2)
    Ak = each(lambda a_, b_: jnp.where(m_sl, _mm_nt(a_, b_), 0.0), As, k2)
    Arb = each(lambda a_, b_: jnp.where(m_li, _mm_nt(a_, b_), 0.0), Rs, b2)
    Ark = each(lambda a_, b_: jnp.where(m_li, _mm_nt(a_, b_), 0.0), Rs, k2)
    T = each(lambda n_: eye2 + jnp.where(lvl_masks[0], n_, 0.0), N)
    for lm in lvl_masks[1:]:
        LT = each(lambda n_, t_: _mm(jnp.where(lm, n_, 0.0), t_), N, T)
        T = each(lambda t_, x: t_ + _mm(t_, x), T, LT)
    AkV = each(_mm, Ak, Vs)
    X = each(lambda t_, a_, u_: _mm(t_, jnp.concatenate([a_, u_], axis=1)), T, As, AkV)
    Z = each(_mm, Arb, X)
    ArkV = each(_mm, Ark, Vs)
    Rp = each(lambda r_, z_: r_ + fold(z_[:, 0:R_PAIR]), rt, Z)
    Y0 = each(lambda z_, a_: fold(z_[:, R_PAIR:] + a_), Z, ArkV)
    Ap = each(lambda x: fold(x[:, 0:R_PAIR]), X)
    U0 = each(lambda x: fold(x[:, R_PAIR:]), X)
    P = each(lambda a_, b_: jnp.where(bd, _mm_tn(a_, b_), 0.0), Ap, bh)
    Q = each(lambda u_, v_, b_, k_: jnp.where(bd, _mm_tn(jnp.concatenate([u_, v_], axis=0),
                                                         jnp.concatenate([b_, k_], axis=0)), 0.0), U0, v, bh, kh)
    S = [s_scr[q] for q in rng]
    y = each(lambda y_, r_, s_: y_ + _mm_nt(r_, s_), Y0, Rp, S)
    S = each(lambda s_, l_, p_, q_: s_ * jnp.exp(-l_) + _mm(s_, p_) + q_, S, cl, P, Q)
    for q in rng:
        s_scr[q] = S[q]
    dlt = each(lambda y_: y_ - head_sum(y_) * (1.0 / R_HEAD), y)
    var = each(lambda d_: head_sum(d_ * d_) * (1.0 / R_HEAD), dlt)
    bonus = each(lambda r_, k_, v_, q: head_sum(r_ * k_ * par(rk_ref, q)) * v_, r, k, v, rng)
    for q, (gi, pp) in enumerate(probs):
        yn = dlt[q] * lax.rsqrt(var[q] + R_LN_EPS) * par(lnw_ref, q) + par(lnb_ref, q)
        y_ref[gi, :, pp * R_PAIR:(pp + 1) * R_PAIR] = (yn + bonus[q]) * ld(g_ref, q)

    @pl.when(step == n_steps - 1)
    def _():
        for q, (gi, pp) in enumerate(probs):
            sfin_ref[gi, 2 * pp] = S[q][0:R_HEAD, 0:R_HEAD]
            sfin_ref[gi, 2 * pp + 1] = S[q][R_HEAD:, R_HEAD:]


def _stacked_out(kernel_fn, in_specs, args, out_index, stack):
    if stack is None:
        return kernel_fn, in_specs, args, {}
    pos = len(args)

    def with_alias(*refs):
        return kernel_fn(*refs[:pos], *refs[pos + 1:])

    return with_alias, in_specs + [pl.BlockSpec(memory_space=pl.ANY)], args + [stack], {pos: out_index}


def rwkv_recurrence(r, e, k, v, kk, a, g, s0, l, r_k, ln_w, ln_b, stack, *, C, G, NP):
    B, T, _ = r.shape
    n_steps = T // C
    n_pairs = D // R_PAIR
    blk = lambda: pl.BlockSpec((G, C, NP * R_PAIR), lambda b, p, s: (b, s, p))
    par = lambda: pl.BlockSpec((1, NP * R_PAIR), lambda b, p, s: (0, p))
    st_spec = lambda: pl.BlockSpec((None, G, 2 * NP, R_HEAD, R_HEAD), lambda b, p, s: (l, b, p, 0, 0))
    in_specs = [blk() for _ in range(7)] + [st_spec(), par(), par(), par()]
    kern, in_specs, args, aliases = _stacked_out(
        functools.partial(_rwkv_rec_kernel, C=C, G=G, NP=NP, n_steps=n_steps), in_specs,
        [r, e, k, v, kk, a, g, s0, r_k, ln_w, ln_b], 1, stack)
    y, sfin = pl.pallas_call(
        kern,
        out_shape=[jax.ShapeDtypeStruct((B, T, D), F32),
                   jax.ShapeDtypeStruct((s0.shape[0], B, 2 * n_pairs, R_HEAD, R_HEAD), F32)],
        grid=(B // G, n_pairs // NP, n_steps),
        in_specs=in_specs,
        out_specs=[blk(), st_spec()],
        scratch_shapes=[pltpu.VMEM((G * NP, R_PAIR, R_PAIR), F32)],
        input_output_aliases=aliases,
        compiler_params=_cparams(("parallel", "parallel", "arbitrary")),
        name="rwkv_rec",
    )(*args)
    return y, sfin


def _s5_kernel(u_ref, h0r_ref, h0i_ref, ar_ref, ai_ref, bre_ref, bim_ref, cre_ref, cim_ref, d_ref, *rest,
               tc, nb, n_steps, permute):
    if permute:
        perm_ref, permt_ref = rest[:2]
        rest = rest[2:]
    y_ref, hr_out, hi_out, inr_scr, ini_scr, hr_scr, hi_scr = rest
    step = pl.program_id(1)
    rows = nb * tc

    @pl.when(step == 0)
    def _():
        hr_scr[...] = h0r_ref[...]
        hi_scr[...] = h0i_ref[...]

    u = u_ref[...].reshape(rows, S5_BLK_CH)
    if permute:
        ut = jnp.dot(perm_ref[...], u.astype(_MXU), preferred_element_type=F32).astype(_MXU)
    else:
        ut = u.astype(_MXU)
    inr_scr[...] = jnp.dot(ut, bre_ref[0], preferred_element_type=F32)
    ini_scr[...] = jnp.dot(ut, bim_ref[0], preferred_element_type=F32)
    ar = ar_ref[...]
    ai = ai_ref[...]

    def body(t, carry):
        hr, hi = carry
        sl = pl.ds(pl.multiple_of(t * nb, nb), nb)
        nr = ar * hr - ai * hi + inr_scr[sl, :]
        ni = ar * hi + ai * hr + ini_scr[sl, :]
        inr_scr[sl, :] = nr
        ini_scr[sl, :] = ni
        return nr, ni

    hr, hi = lax.fori_loop(0, tc, body, (hr_scr[...], hi_scr[...]))
    hr_scr[...] = hr
    hi_scr[...] = hi
    yt = (jnp.dot(inr_scr[...].astype(_MXU), cre_ref[0], preferred_element_type=F32)
          - jnp.dot(ini_scr[...].astype(_MXU), cim_ref[0], preferred_element_type=F32))
    if permute:
        yt_hi = yt.astype(_MXU)
        yt_lo = (yt - yt_hi.astype(F32)).astype(_MXU)
        yt = (jnp.dot(permt_ref[...], yt_hi, preferred_element_type=F32)
              + jnp.dot(permt_ref[...], yt_lo, preferred_element_type=F32))
    y = yt + d_ref[...] * u
    y = 0.5 * y * (1.0 + jnp.tanh(math.sqrt(2.0 / math.pi) * (y + 0.044715 * (y * y * y))))
    y_ref[...] = y.reshape(y_ref.shape)

    @pl.when(step == n_steps - 1)
    def _():
        hr_out[...] = hr
        hi_out[...] = hi


def _time_major_perm(nb, tc):
    rows = nb * tc
    j = np.arange(rows)
    perm_np = np.zeros((rows, rows), np.float32)
    perm_np[(j % tc) * nb + j // tc, j] = 1.0
    return perm_np


def s5_scan(u3, h0r, h0i, l, ar, ai, bre, bim, cre, cim, d, *, nb, tc, blk):
    A, R, _ = u3.shape
    ga, tr = blk
    assert ga * tr == nb * tc
    tmajor = (ga, tr) == (tc, nb) and R == nb
    nblk = D // S5_BLK_CH
    n_steps = A // ga if tmajor else R // tr
    rows = nb * tc
    u_spec = (pl.BlockSpec((ga, tr, S5_BLK_CH), lambda c, s: (s, 0, c)) if tmajor
              else pl.BlockSpec((ga, tr, S5_BLK_CH), lambda c, s: (0, s, c)))
    if tmajor:
        perm_specs, perm_args = [], []
    else:
        assert A == ga
        perm_np = _time_major_perm(nb, tc)
        perm_specs = [_const_spec((rows, rows)), _const_spec((rows, rows))]
        perm_args = [jnp.asarray(perm_np, _MXU), jnp.asarray(perm_np.T, _MXU)]
    st = lambda: pl.BlockSpec((None, nb, S5_BLK_ST), lambda c, s: (l, 0, c))
    vec = lambda w: pl.BlockSpec((1, w), lambda c, s: (0, c))
    mat = lambda a, b: pl.BlockSpec((1, a, b), lambda c, s: (c, 0, 0))
    return pl.pallas_call(
        functools.partial(_s5_kernel, tc=tc, nb=nb, n_steps=n_steps, permute=not tmajor),
        out_shape=[jax.ShapeDtypeStruct(u3.shape, F32),
                   jax.ShapeDtypeStruct((nb, nblk * S5_BLK_ST), F32),
                   jax.ShapeDtypeStruct((nb, nblk * S5_BLK_ST), F32)],
        grid=(nblk, n_steps),
        in_specs=[u_spec,
                  st(), st(), vec(S5_BLK_ST), vec(S5_BLK_ST),
                  mat(S5_BLK_CH, S5_BLK_ST), mat(S5_BLK_CH, S5_BLK_ST),
                  mat(S5_BLK_ST, S5_BLK_CH), mat(S5_BLK_ST, S5_BLK_CH),
                  vec(S5_BLK_CH)] + perm_specs,
        out_specs=[u_spec,
                   pl.BlockSpec((nb, S5_BLK_ST), lambda c, s: (0, c)),
                   pl.BlockSpec((nb, S5_BLK_ST), lambda c, s: (0, c))],
        scratch_shapes=[pltpu.VMEM((rows, S5_BLK_ST), F32), pltpu.VMEM((rows, S5_BLK_ST), F32),
                        pltpu.VMEM((nb, S5_BLK_ST), F32), pltpu.VMEM((nb, S5_BLK_ST), F32)],
        compiler_params=_cparams(("parallel", "arbitrary")),
        name="s5_scan",
    )(u3, h0r, h0i, ar, ai, bre, bim, cre, cim, d, *perm_args)


def _s5_discretize(a_re, a_im, log_dt, b_re, b_im, c_re, c_im):
    g, p, hch = b_re.shape
    dt = jnp.exp(log_dt.astype(F32))[:, None]
    mag = jnp.exp(dt * a_re)
    abar_re = mag * jnp.cos(dt * a_im)
    abar_im = mag * jnp.sin(dt * a_im)
    den = a_re * a_re + a_im * a_im
    nr = abar_re - 1.0
    q_re = (nr * a_re + abar_im * a_im) / den
    q_im = (abar_im * a_re - nr * a_im) / den
    bb_re = q_re[..., None] * b_re - q_im[..., None] * b_im
    bb_im = q_re[..., None] * b_im + q_im[..., None] * b_re
    nblk = D // S5_BLK_CH
    gl = g // nblk
    eye = jnp.eye(gl, dtype=F32)

    def in_blocks(bb):
        t = jnp.transpose(bb, (0, 2, 1)).reshape(nblk, gl, hch, p)
        return jnp.einsum('cghp,gk->cghkp', t, eye).reshape(nblk, gl * hch, gl * p).astype(_MXU)

    def out_blocks(cc):
        t = jnp.transpose(cc, (0, 2, 1)).reshape(nblk, gl, p, hch)
        return jnp.einsum('cgph,gk->cgpkh', t, eye).reshape(nblk, gl * p, gl * hch).astype(_MXU)

    return (abar_re.reshape(1, g * p), abar_im.reshape(1, g * p), in_blocks(bb_re), in_blocks(bb_im),
            out_blocks(c_re), out_blocks(c_im))


def _ssd_kernel(xbc_ref, z_ref, dt_ref, cprev_ref, h0_ref, cw_ref, cb_ref, dtb_ref, alog_ref, dx_ref, ng_ref,
                eh_ref, y_ref, hfin_ref, ext_scr, h_scr, *, L, G, t_real, n_steps):
    step = pl.program_id(1)
    rng = range(G)
    NB = M_GROUPS * M_STATE

    def each(f, *lists):
        return [f(*xs) for xs in zip(*lists)]

    @pl.when(step == 0)
    def _():
        for s in rng:
            ext_scr[s] = jnp.zeros((8, M_CONV_DIM), F32)
            ext_scr[s, 5:8, :] = cprev_ref[s]
            for gi in range(M_GROUPS):
                h_scr[s, gi] = h0_ref[s, gi * M_HPG:(gi + 1) * M_HPG].reshape(M_GW, M_STATE).T

    tril = jnp.where(_iota((L, L), 1) <= _iota((L, L), 0), 1.0, 0.0).astype(_MXU)
    causal = _iota((L, L), 1) <= _iota((L, L), 0)
    eye_h = _eye(128, _MXU)
    eh = eh_ref[...]
    lane_head = _iota((1, M_GW), 1) // M_HEAD
    row8 = _iota((8, 1), 0)
    a = -jnp.exp(alog_ref[...])
    conv, dts = [], []
    for s in rng:
        x = xbc_ref[s * L:(s + 1) * L, :]
        prev8 = ext_scr[s]
        c = cb_ref[...] + cw_ref[3:4, :] * x
        for j in range(1, 4):
            xr = pltpu.roll(x, j, axis=0)
            head = jnp.where(row8 < j, pltpu.roll(prev8, j, axis=0), xr[0:8])
            xr = head if L == 8 else jnp.concatenate([head, xr[8:]], axis=0)
            c = c + cw_ref[3 - j:4 - j, :] * xr
        ext_scr[s] = x[L - 8:L]
        conv.append(c * _sigmoid(c))
        dt = _softplus(dt_ref[s * L:(s + 1) * L, :] + dtb_ref[...])
        if t_real < L:
            dt = jnp.where(_iota(dt.shape, 0) < t_real, dt, 0.0)
        dts.append(dt)
    xs = each(lambda c: c[:, 0:M_WIDTH], conv)
    acum = each(lambda d_: _mm_sel_l(tril, d_ * a), dts)
    acum_t = each(lambda x: _mm_nt_sel_l(eye_h, x), acum)
    dt_x = each(lambda d_: _mm_sel_r(d_, eh), dts)
    acum_x = each(lambda x: _mm_sel_r(x, eh), acum)
    acl_x = each(lambda x: x[L - 1:L, :], acum_x)
    xd = each(lambda x, d_: x * d_, xs, dt_x)
    xdd = each(lambda x, l_, c_: x * jnp.exp(l_ - c_), xd, acl_x, acum_x)
    eacum_x = each(jnp.exp, acum_x)
    cdec_x = each(jnp.exp, acl_x)
    ys = [[] for _ in rng]
    for gi in range(M_GROUPS):
        gs = slice(gi * M_GW, (gi + 1) * M_GW)
        bg = each(lambda c: c[:, M_WIDTH + gi * M_STATE:M_WIDTH + (gi + 1) * M_STATE], conv)
        cg = each(lambda c: c[:, M_WIDTH + NB + gi * M_STATE:M_WIDTH + NB + (gi + 1) * M_STATE], conv)
        h = [h_scr[s, gi] for s in rng]
        cbm = each(_mm_nt, cg, bg)
        bgt = each(lambda b_: _mm_nt(eye_h, b_), bg)
        y_off = each(lambda c_, h_, e_: _mm(c_, h_) * e_[:, gs], cg, h, eacum_x)
        ms = []
        for e in range(M_HPG):
            he = gi * M_HPG + e
            ms.append(each(lambda c_, a_, t_: c_ * jnp.exp(jnp.where(causal, a_[:, he:he + 1] - t_[he:he + 1, :],
                                                                     -jnp.inf)), cbm, acum, acum_t))
        if L % 128 == 0:
            mcat = [jnp.concatenate([ms[e][s] for e in range(M_HPG)], axis=1) for s in rng]
            xst = each(lambda x: jnp.concatenate([jnp.where(lane_head == e, x[:, gs], 0.0).astype(_MXU)
                                                  for e in range(M_HPG)], axis=0), xd)
            y_dg = each(_mm, mcat, xst)
        else:
            y_dg = [sum(_mm(ms[e][s], jnp.where(lane_head == e, xd[s][:, gs], 0.0)) for e in range(M_HPG))
                    for s in rng]
        hn = each(lambda h_, d_, b_, x: h_ * d_[:, gs] + _mm(b_, x[:, gs]), h, cdec_x, bgt, xdd)
        for s in rng:
            h_scr[s, gi] = hn[s]
            ys[s].append(y_off[s] + y_dg[s])
    for s in rng:
        y = jnp.concatenate(ys[s], axis=1) + dx_ref[...] * xs[s]
        zz = z_ref[s * L:(s + 1) * L, :]
        y = y * (zz * _sigmoid(zz))
        outs = []
        for gi in range(M_GROUPS):
            yg = y[:, gi * M_GW:(gi + 1) * M_GW]
            outs.append(yg * lax.rsqrt(jnp.mean(yg * yg, axis=-1, keepdims=True) + EPS))
        y_ref[s * L:(s + 1) * L, :] = jnp.concatenate(outs, axis=1) * ng_ref[...]

    @pl.when(step == n_steps - 1)
    def _():
        for s in rng:
            for gi in range(M_GROUPS):
                hfin_ref[s, gi * M_HPG:(gi + 1) * M_HPG] = h_scr[s, gi].T.reshape(M_HPG, M_HEAD, M_STATE)


def ssd_block(xbc, z, dt, conv_prev, h0, l, conv_w, conv_b, dt_bias, a_log, d_x, norm_g, eh, stack, *, B, T, L, G,
              t_real):
    n_steps = T // L
    assert G == 1 or n_steps == 1
    heads = M_GROUPS * M_HPG
    seq = lambda w: pl.BlockSpec((G * L, w), lambda b, s: (b * n_steps + s, 0))
    st_spec = lambda: pl.BlockSpec((None, G, heads, M_HEAD, M_STATE), lambda b, s: (l, b, 0, 0, 0))
    in_specs = [seq(M_CONV_DIM), seq(M_WIDTH), seq(128),
                pl.BlockSpec((None, G, 3, M_CONV_DIM), lambda b, s: (l, b, 0, 0)),
                st_spec(),
                _const_spec((4, M_CONV_DIM)), _const_spec((1, M_CONV_DIM)),
                _const_spec((1, 128)), _const_spec((1, 128)),
                _const_spec((1, M_WIDTH)), _const_spec((1, M_WIDTH)),
                _const_spec((128, M_WIDTH))]
    kern, in_specs, args, aliases = _stacked_out(
        functools.partial(_ssd_kernel, L=L, G=G, t_real=t_real, n_steps=n_steps), in_specs,
        [xbc, z, dt, conv_prev, h0, conv_w, conv_b, dt_bias, a_log, d_x, norm_g, eh], 1, stack)
    return pl.pallas_call(
        kern,
        out_shape=[jax.ShapeDtypeStruct((B * T, M_WIDTH), F32),
                   jax.ShapeDtypeStruct((h0.shape[0], B, heads, M_HEAD, M_STATE), F32)],
        grid=(B // G, n_steps),
        in_specs=in_specs,
        out_specs=[seq(M_WIDTH), st_spec()],
        scratch_shapes=[pltpu.VMEM((G, 8, M_CONV_DIM), F32),
                        pltpu.VMEM((G, M_GROUPS, M_STATE, M_GW), F32)],
        input_output_aliases=aliases,
        compiler_params=_cparams(("parallel", "arbitrary")),
        name="ssd",
    )(*args)


def _attn_kernel(q_ref, k_ref, v_ref, o_ref, *, nb, tq, heads_split):
    scale = XA_HD ** -0.5
    for j in range(nb):
        if heads_split:
            k_all = pltpu.einshape("mhd->hmd", k_ref[j])
            v_all = pltpu.einshape("mhd->hmd", v_ref[j])
        outs = []
        for hd in range(XA_HEADS):
            cs = slice(hd * XA_HD, (hd + 1) * XA_HD)
            q = q_ref[j * tq:(j + 1) * tq, cs]
            kh = k_all[hd] if heads_split else k_ref[j, :, cs]
            vh = v_all[hd] if heads_split else v_ref[j, :, cs]
            s = _mm_nt(q, kh) * scale
            s = s - jnp.max(s, axis=-1, keepdims=True)
            p = jnp.exp(s)
            p = p / jnp.sum(p, axis=-1, keepdims=True)
            outs.append(_mm(p, vh))
        o_ref[j * tq:(j + 1) * tq, :] = jnp.concatenate(outs, axis=1)


def cross_attention(q, mk, mv, kv_index, *, B, T, tq, nb):
    nlead = len(kv_index)
    heads_split = mk.ndim - nlead == 4
    kv_blk = mk.shape[nlead + 1:]
    n_t = T // tq
    kv_spec = pl.BlockSpec((None,) * nlead + (nb,) + kv_blk,
                           lambda b, s: tuple(kv_index) + (b,) + (0,) * len(kv_blk))
    q_spec = pl.BlockSpec((nb * tq, D), lambda b, s: (b * n_t + s, 0))
    return pl.pallas_call(
        functools.partial(_attn_kernel, nb=nb, tq=tq, heads_split=heads_split),
        out_shape=jax.ShapeDtypeStruct((B * T, D), F32),
        grid=(B // nb, n_t),
        in_specs=[q_spec, kv_spec, kv_spec],
        out_specs=q_spec,
        compiler_params=_cparams(("parallel", "arbitrary")),
        name="xattn",
    )(q, mk, mv)


def _pad_time(x2, B, T, Tp):
    if Tp == T:
        return x2
    w = x2.shape[-1]
    return jnp.pad(x2.reshape(B, T, w), ((0, 0), (0, Tp - T), (0, 0))).reshape(B * Tp, w)


def _layer(l, h, grp, W, st, acc):
    T, B = grp['T'], grp['B']
    n = T * B
    gates = norm_matmul(h, W['g_mix_pre'], W['w_gates'], name="in_gates")
    p_r = norm_matmul(h, W['g_mix_pre'], W['w_rwkv'], name="in_rwkv")
    if grp['s5_tmajor']:
        u3 = norm_matmul_tmajor(h.reshape(B, T, D), W['g_mix_pre'], W['w_s5'], name="in_s5")
    else:
        u3 = norm_matmul(h, W['g_mix_pre'], W['w_s5'], name="in_s5").reshape(1, n, D)
    z_m =norm_matmul(h, W['g_mix_pre'], W['w_z'], name="in_z")
    xbc = norm_matmul(h, W['g_mix_pre'], W['w_xbc'], name="in_xbc")
    dt_m = norm_matmul(h, W['g_mix_pre'], W['w_dt'], tn=128, name="in_dt")

    seqs = rwkv_prep(p_r, st['shift'][l], W['mu'], W['wa'], W['w0'], W['a0'], W['g2'],
                     W['k_k'], W['k_a'], T=T)
    Tp = grp['rwkv_Tpad']
    seqs = [_pad_time(x, B, T, Tp).reshape(B, Tp, D) for x in seqs]
    yr, s_fin = rwkv_recurrence(*seqs, st['rwkv'], l, W['r_k'], W['ln_w'], W['ln_b'], acc[0],
                                C=grp['rwkv_C'], G=grp['rwkv_G'], NP=grp['rwkv_NP'])
    yr = yr[:, :T].reshape(n, D)
    shift_new = p_r.reshape(B, T, -1)[:, T - 1]

    ys, s5r, s5i = s5_scan(u3, st['s5r'], st['s5i'], l, W['s5_ar'], W['s5_ai'], W['s5_bre'], W['s5_bim'],
                           W['s5_cre'], W['s5_cim'], W['s5_d'], nb=B, tc=grp['s5_tc'], blk=grp['s5_blk'])

    Lc = grp['ssd_L']
    Tm = grp['ssd_Tpad']
    ym, ssm_fin = ssd_block(_pad_time(xbc, B, T, Tm), _pad_time(z_m, B, T, Tm), _pad_time(dt_m, B, T, Tm),
                            st['conv'], st['ssm'], l, W['conv_w'], W['conv_b'], W['dt_bias'],
                            W['a_log'], W['d_x'], W['m_norm'], W['eh'], acc[1], B=B, T=Tm, L=Lc, G=grp['ssd_G'],
                            t_real=min(T, Lc))
    ym = ym.reshape(B, Tm, M_WIDTH)[:, :T].reshape(n, M_WIDTH)
    conv_new = jnp.concatenate([st['conv'][l], xbc.reshape(B, T, M_CONV_DIM)[:, max(T - 3, 0):]], axis=1)[:, -3:]

    h = mixer_merge(gates, yr, ys if grp['s5_tmajor'] else ys.reshape(n, D), ym, h, W['w_out_rwkv'],
                    W['s5_w_glu'], W['w_out_s5'], W['w_out_mamba'], W['w_out'], W['g_mix_post'],
                    seq=(B, T) if grp['s5_tmajor'] else None)

    q = norm_matmul(h, W['g_xa_pre'], W['xa_wq'], name="xa_q")
    o = cross_attention(q, st['mk'], st['mv'], st['kv_index'](l), B=B, T=T, tq=grp['xa_tq'], nb=grp['xa_nb'])
    h = matmul_norm_residual(o, W['xa_wo'], W['g_xa_post'], h, name="xa_out")

    h = mlp_block(h, W['g_mlp_pre'], W['mlp_w1'], W['mlp_w2'], W['g_mlp_post'])
    return h, (s_fin, shift_new, s5r.reshape(B, 64, 64), s5i.reshape(B, 64, 64), conv_new, ssm_fin)


def _group_cfg(T, B):
    cfg = dict(T=T, B=B)
    g = 8 if B % 8 == 0 else 1
    if T % 64 == 0:
        cfg.update(rwkv_C=64, rwkv_Tpad=T, rwkv_G=g, rwkv_NP=2)
    else:
        tp = -(-T // 8) * 8
        cfg.update(rwkv_C=tp, rwkv_Tpad=tp, rwkv_G=16 if B % 16 == 0 else g, rwkv_NP=2)
    if T % 8 == 0 and (512 // B) >= 8 and T % (512 // B) == 0:
        tc = 512 // B
        cfg.update(s5_tmajor=True, s5_tc=tc, s5_blk=(tc, B))
    else:
        cfg.update(s5_tmajor=False, s5_tc=T, s5_blk=(1, B * T))
    if T % 128 == 0:
        cfg.update(ssd_L=128, ssd_Tpad=T, ssd_G=1)
    else:
        tp = -(-T // 8) * 8
        cfg.update(ssd_L=tp, ssd_Tpad=tp, ssd_G=4 if B % 4 == 0 else 1)
    if T >= 64:
        cfg.update(xa_tq=_row_tile(T, 512), xa_nb=1)
    else:
        cfg.update(xa_tq=T, xa_nb=4 if B % 4 == 0 else 1)
    return cfg


def kernel(x_prompt, x_sample, cache_mem_k, cache_mem_v, state_rwkv, state_rwkv_shift, state_s5_re, state_s5_im, state_conv, state_ssm, mem_prompt, norm_mix_pre, norm_mix_post, norm_xa_pre, norm_xa_post, norm_mlp_pre, norm_mlp_post, norm_mem, w_in, w_out, rwkv_mu, rwkv_w0, rwkv_w2, rwkv_a0, rwkv_a2, rwkv_g2, rwkv_k_k, rwkv_k_a, rwkv_r_k, rwkv_ln_w, rwkv_ln_b, w_out_rwkv, s5_a_re, s5_a_im, s5_log_dt, s5_b_re, s5_b_im, s5_c_re, s5_c_im, s5_d, s5_w_glu, w_out_s5, m_conv_w, m_conv_b, m_dt_bias, m_a_log, m_d, m_norm, w_out_mamba, xa_wq, xa_wk, xa_wv, xa_wo, mlp_w1, mlp_w2):
    depth = w_in.shape[0]
    bp, tp, _ = x_prompt.shape
    bs, ts, _ = x_sample.shape
    mlen = mem_prompt.shape[1]
    bf = lambda x: x.astype(_MXU)
    row = lambda x: x.reshape(1, -1).astype(F32)

    eh = (jnp.arange(128)[:, None] == (jnp.arange(M_WIDTH)[None, :] // M_HEAD)).astype(_MXU)

    col = [0]
    for sz in (3 * D, 3 * D + 256, D, M_WIDTH, M_CONV_DIM, 32):
        col.append(col[-1] + sz)
    layers = []
    for l in range(depth):
        wl = w_in[l]
        ar, ai, bre, bim, cre, cim = _s5_discretize(s5_a_re[l], s5_a_im[l], s5_log_dt[l], s5_b_re[l], s5_b_im[l],
                                                    s5_c_re[l], s5_c_im[l])
        zero = jnp.zeros((64, D), F32)
        wa = jnp.concatenate([jnp.concatenate([rwkv_w2[l], zero], axis=1),
                              jnp.concatenate([zero, rwkv_a2[l]], axis=1)], axis=0)
        pad32 = lambda x: jnp.pad(x.reshape(1, -1).astype(F32), ((0, 0), (0, 128 - x.shape[-1])))
        layers.append(dict(
            g_mix_pre=row(norm_mix_pre[l]), g_mix_post=row(norm_mix_post[l]),
            g_xa_pre=row(norm_xa_pre[l]), g_xa_post=row(norm_xa_post[l]),
            g_mlp_pre=row(norm_mlp_pre[l]), g_mlp_post=row(norm_mlp_post[l]),
            w_gates=bf(wl[:, col[0]:col[1]]), w_rwkv=bf(wl[:, col[1]:col[2]]), w_s5=bf(wl[:, col[2]:col[3]]),
            w_z=bf(wl[:, col[3]:col[4]]), w_xbc=bf(wl[:, col[4]:col[5]]),
            w_dt=bf(jnp.pad(wl[:, col[5]:col[6]], ((0, 0), (0, 96)))),
            mu=row(rwkv_mu[l]), wa=bf(wa), w0=row(rwkv_w0[l]), a0=row(rwkv_a0[l]), g2=bf(rwkv_g2[l]),
            k_k=row(rwkv_k_k[l]), k_a=row(rwkv_k_a[l]), r_k=row(rwkv_r_k[l]),
            ln_w=row(rwkv_ln_w[l]), ln_b=row(rwkv_ln_b[l]),
            s5_ar=ar, s5_ai=ai, s5_bre=bre, s5_bim=bim, s5_cre=cre, s5_cim=cim, s5_d=row(s5_d[l]),
            conv_w=m_conv_w[l].astype(F32), conv_b=row(m_conv_b[l]), dt_bias=pad32(m_dt_bias[l]),
            a_log=pad32(m_a_log[l]), d_x=row(jnp.repeat(m_d[l], M_HEAD)), m_norm=row(m_norm[l]), eh=eh,
            w_out_rwkv=bf(w_out_rwkv[l]), s5_w_glu=bf(s5_w_glu[l]), w_out_s5=bf(w_out_s5[l]),
            w_out_mamba=bf(w_out_mamba[l]), w_out=bf(w_out[l]),
            xa_wq=bf(xa_wq[l]), xa_wk=bf(xa_wk[l]), xa_wv=bf(xa_wv[l]), xa_wo=bf(xa_wo[l]),
            mlp_w1=bf(mlp_w1[l]), mlp_w2=bf(mlp_w2[l]), g_mem=row(norm_mem[l]),
        ))

    hp = x_prompt.reshape(bp * tp, D).astype(F32)
    hs = x_sample.reshape(bs * ts, D).astype(F32)
    mem2 = mem_prompt.reshape(bp * mlen, D).astype(F32)

    cfg_p = _group_cfg(tp, bp)
    cfg_s = _group_cfg(ts, bs)
    zeros_p = dict(
        rwkv=jnp.zeros((depth, bp) + state_rwkv.shape[2:], F32),
        shift=jnp.zeros((depth, bp, state_rwkv_shift.shape[-1]), F32),
        s5r=jnp.zeros((depth, bp, state_s5_re.shape[2] * state_s5_re.shape[3]), F32),
        s5i=jnp.zeros((depth, bp, state_s5_re.shape[2] * state_s5_re.shape[3]), F32),
        conv=jnp.zeros((depth, bp) + state_conv.shape[2:], F32),
        ssm=jnp.zeros((depth, bp) + state_ssm.shape[2:], F32),
    )
    st_s = dict(
        rwkv=state_rwkv, shift=state_rwkv_shift,
        s5r=state_s5_re.reshape(depth, bs, -1), s5i=state_s5_im.reshape(depth, bs, -1),
        conv=state_conv, ssm=state_ssm,
        mk=cache_mem_k, mv=cache_mem_v,
        kv_index=lambda l: (l,),
    )

    mk_out, mv_out = [], []
    st_p_out = [[] for _ in range(6)]
    st_s_out = [[] for _ in range(6)]
    acc_p = acc_s = (None, None)
    for l in range(depth):
        W = layers[l]
        mk = norm_matmul(mem2, W['g_mem'], W['xa_wk'], name="mem_k")
        mv = norm_matmul(mem2, W['g_mem'], W['xa_wv'], name="mem_v")
        st_p = dict(zeros_p, mk=mk.reshape(bp, mlen, D), mv=mv.reshape(bp, mlen, D), kv_index=lambda l: ())
        hp, new_p = _layer(l, hp, cfg_p, W, st_p, acc_p)
        hs, new_s = _layer(l, hs, cfg_s, W, st_s, acc_s)
        acc_p = (new_p[0], new_p[5])
        acc_s = (new_s[0], new_s[5])
        mk_out.append(mk.reshape(bp, mlen, XA_HEADS, XA_HD))
        mv_out.append(mv.reshape(bp, mlen, XA_HEADS, XA_HD))
        for i in range(1, 5):
            st_p_out[i].append(new_p[i])
            st_s_out[i].append(new_s[i])

    stk = lambda xs: jnp.stack(xs).astype(F32)
    y_p = hp.reshape(bp, tp, D).astype(x_prompt.dtype)
    y_s = hs.reshape(bs, ts, D).astype(x_sample.dtype)
    return (y_p, y_s, stk(mk_out), stk(mv_out),
            acc_p[0], *[stk(st_p_out[i]) for i in range(1, 5)], acc_p[1],
            acc_s[0], *[stk(st_s_out[i]) for i in range(1, 5)], acc_s[1])
```

```python
import functools
import math

import numpy as np
import jax
import jax.numpy as jnp
from jax import lax
from jax.experimental import pallas as pl
from jax.experimental.pallas import tpu as pltpu

F32 = jnp.float32
_MXU = jnp.bfloat16
EPS = 1e-6
R_LN_EPS = 64e-5
D = 1024
R_HEAD = 64
R_PAIR = 2 * R_HEAD
S5_BLK_CH = 128
S5_BLK_ST = 512
M_HEAD = 64
M_STATE = 128
M_GROUPS = 4
M_HPG = 8
M_GW = M_HPG * M_HEAD
M_WIDTH = 2048
M_CONV_DIM = 3072
XA_HEADS = 4
XA_HD = 256
VMEM_LIMIT = 56 * 1024 * 1024


def _mm(a, b):
    return jnp.dot(a.astype(_MXU), b.astype(_MXU), preferred_element_type=F32)


def _mm_nt(a, b):
    return lax.dot_general(a.astype(_MXU), b.astype(_MXU), (((1,), (1,)), ((), ())),
                           preferred_element_type=F32)


def _mm_tn(a, b):
    return lax.dot_general(a.astype(_MXU), b.astype(_MXU), (((0,), (0,)), ((), ())),
                           preferred_element_type=F32)


def _split3(x):
    hi = x.astype(_MXU)
    r1 = x - hi.astype(F32)
    mid = r1.astype(_MXU)
    lo = (r1 - mid.astype(F32)).astype(_MXU)
    return hi, mid, lo


def _mm_sel_l(sel, x):
    return sum(jnp.dot(sel, p, preferred_element_type=F32) for p in _split3(x))


def _mm_sel_r(x, sel):
    return sum(jnp.dot(p, sel, preferred_element_type=F32) for p in _split3(x))


def _mm_nt_sel_l(sel, x):
    return sum(lax.dot_general(sel, p, (((1,), (1,)), ((), ())), preferred_element_type=F32)
               for p in _split3(x))


def _sigmoid(x):
    return 1.0 / (1.0 + jnp.exp(-x))


def _softplus(x):
    return jnp.maximum(x, 0.0) + jnp.log(1.0 + jnp.exp(-jnp.abs(x)))


def _rms(x, g):
    return x * lax.rsqrt(jnp.mean(x * x, axis=-1, keepdims=True) + EPS) * g


def _iota(shape, dim):
    return lax.broadcasted_iota(jnp.int32, shape, dim)


def _eye(n, dtype):
    return jnp.where(_iota((n, n), 0) == _iota((n, n), 1), 1.0, 0.0).astype(dtype)


def _cparams(sem):
    return pltpu.CompilerParams(dimension_semantics=sem, vmem_limit_bytes=VMEM_LIMIT)


def _row_tile(n, want):
    t = min(n, want)
    while n % t:
        t //= 2
    return t


def _col_tile(c, cap):
    best = 128
    for t in range(128, min(c, cap) + 1, 128):
        if c % t == 0:
            best = t
    return best


def _const_spec(shape):
    nd = len(shape)
    return pl.BlockSpec(shape, lambda *_: (0,) * nd, pipeline_mode=pl.Buffered(1))


def _norm_mm_kernel(x_ref, g_ref, w_ref, o_ref, xn_ref):
    @pl.when(pl.program_id(1) == 0)
    def _():
        xn_ref[...] = _rms(x_ref[...], g_ref[...]).astype(_MXU)

    o_ref[...] = jnp.dot(xn_ref[...], w_ref[...], preferred_element_type=F32)


def norm_matmul(x, g, w, *, tm=1024, tn=1664, name="norm_mm"):
    n, d = x.shape
    c = w.shape[1]
    tm = _row_tile(n, tm)
    tn = _col_tile(c, tn)
    return pl.pallas_call(
        _norm_mm_kernel,
        out_shape=jax.ShapeDtypeStruct((n, c), F32),
        grid=(n // tm, c // tn),
        in_specs=[pl.BlockSpec((tm, d), lambda i, j: (i, 0)),
                  pl.BlockSpec((1, d), lambda i, j: (0, 0)),
                  pl.BlockSpec((d, tn), lambda i, j: (0, j))],
        out_specs=pl.BlockSpec((tm, tn), lambda i, j: (i, j)),
        scratch_shapes=[pltpu.VMEM((tm, d), _MXU)],
        compiler_params=_cparams(("parallel", "arbitrary")),
        name=name,
    )(x, g, w)


def _mm_norm_res_kernel(x_ref, w_ref, g_ref, h_ref, o_ref):
    y = _mm(x_ref[...], w_ref[...])
    o_ref[...] = h_ref[...] + _rms(y, g_ref[...])


def matmul_norm_residual(x, w, g, h, *, tm=512, name="mm_norm_res"):
    n, k = x.shape
    d = w.shape[1]
    tm = _row_tile(n, tm)
    return pl.pallas_call(
        _mm_norm_res_kernel,
        out_shape=jax.ShapeDtypeStruct((n, d), F32),
        grid=(n // tm,),
        in_specs=[pl.BlockSpec((tm, k), lambda i: (i, 0)),
                  _const_spec((k, d)),
                  _const_spec((1, d)),
                  pl.BlockSpec((tm, d), lambda i: (i, 0))],
        out_specs=pl.BlockSpec((tm, d), lambda i: (i, 0)),
        compiler_params=_cparams(("parallel",)),
        name=name,
    )(x, w, g, h)


def _mlp_kernel(h_ref, g1_ref, w1_ref, w2_ref, g2_ref, o_ref, *, n_chunks, ck):
    h = h_ref[...]
    xn = _rms(h, g1_ref[...]).astype(_MXU)
    acc = jnp.zeros(h.shape, F32)
    for j in range(n_chunks):
        a = jnp.dot(xn, w1_ref[:, j * ck:(j + 1) * ck], preferred_element_type=F32)
        a = jnp.square(jnp.maximum(a, 0.0))
        acc = acc + jnp.dot(a.astype(_MXU), w2_ref[j * ck:(j + 1) * ck, :], preferred_element_type=F32)
    o_ref[...] = h + _rms(acc, g2_ref[...])


def mlp_block(h, g1, w1, w2, g2, *, tm=512, ck=1024):
    n, d = h.shape
    f = w1.shape[1]
    tm = _row_tile(n, tm)
    return pl.pallas_call(
        functools.partial(_mlp_kernel, n_chunks=f // ck, ck=ck),
        out_shape=jax.ShapeDtypeStruct((n, d), F32),
        grid=(n // tm,),
        in_specs=[pl.BlockSpec((tm, d), lambda i: (i, 0)),
                  _const_spec((1, d)),
                  _const_spec((d, f)),
                  _const_spec((f, d)),
                  _const_spec((1, d))],
        out_specs=pl.BlockSpec((tm, d), lambda i: (i, 0)),
        compiler_params=_cparams(("parallel",)),
        name="mlp",
    )(h, g1, w1, w2, g2)


def _norm_mm_tmajor_kernel(x_ref, g_ref, w_ref, perm_ref, o_ref):
    nb, tt, d = x_ref.shape
    xn = _rms(x_ref[...].reshape(nb * tt, d), g_ref[...]).astype(_MXU)
    xt = jnp.dot(perm_ref[...], xn, preferred_element_type=F32).astype(_MXU)
    o_ref[...] = jnp.dot(xt, w_ref[...], preferred_element_type=F32).reshape(o_ref.shape)


def norm_matmul_tmajor(x3, g, w, *, rows=512, name="norm_mm_t"):
    nb, t, d = x3.shape
    c = w.shape[1]
    tt = rows // nb
    assert t % tt == 0 and tt % 8 == 0
    perm = jnp.asarray(_time_major_perm(nb, tt), _MXU)
    return pl.pallas_call(
        _norm_mm_tmajor_kernel,
        out_shape=jax.ShapeDtypeStruct((t, nb, c), F32),
        grid=(t // tt,),
        in_specs=[pl.BlockSpec((nb, tt, d), lambda i: (0, i, 0)),
                  _const_spec((1, d)), _const_spec((d, c)), _const_spec((nb * tt, nb * tt))],
        out_specs=pl.BlockSpec((tt, nb, c), lambda i: (i, 0, 0)),
        compiler_params=_cparams(("parallel",)),
        name=name,
    )(x3, g, w, perm)


def _merge_kernel(gates_ref, yr_ref, ys_ref, ym_ref, h_ref, wr_ref, wglu_ref, ws_ref, wm_ref, wo_ref, g_ref,
                  *rest, ys_tmajor):
    o_ref = rest[-1]
    rows2d = lambda ref: ref[...].reshape(-1, ref.shape[-1])
    o_r = _mm(rows2d(yr_ref), wr_ref[...])
    ys = rows2d(ys_ref)
    y3 = ys * _sigmoid(_mm(ys, wglu_ref[...]))
    if ys_tmajor:
        y3 = jnp.dot(rest[0][...], y3.astype(_MXU), preferred_element_type=F32)
    o_s = _mm(y3, ws_ref[...])
    o_m = _mm(rows2d(ym_ref), wm_ref[...])
    gates = rows2d(gates_ref)
    merged = (_sigmoid(gates[:, 0:D]) * o_r + _sigmoid(gates[:, D:2 * D]) * o_s
              + _sigmoid(gates[:, 2 * D:3 * D]) * o_m)
    mix = _mm(merged, wo_ref[...])
    o_ref[...] = (rows2d(h_ref) + _rms(mix, g_ref[...])).reshape(o_ref.shape)


def mixer_merge(gates, yr, ys, ym, h, wr, wglu, ws, wm, wo, g, *, tm=256, seq=None):
    n = h.shape[0]
    weights = [_const_spec((D, D)), _const_spec((D, D)), _const_spec((D, D)),
               _const_spec((M_WIDTH, D)), _const_spec((D, D)), _const_spec((1, D))]
    if seq is None:
        tm = _row_tile(n, tm)
        row = lambda w: pl.BlockSpec((tm, w), lambda i: (i, 0))
        in_specs = [row(3 * D), row(D), row(D), row(M_WIDTH), row(D)] + weights
        args = [gates, yr, ys, ym, h, wr, wglu, ws, wm, wo, g]
        out_spec, out_shape, grid = row(D), (n, D), (n // tm,)
    else:
        nb, t = seq
        tt = tm // nb
        assert t % tt == 0 and tt % 8 == 0
        row = lambda w: pl.BlockSpec((nb, tt, w), lambda i: (0, i, 0))
        v3 = lambda x: x.reshape(nb, t, x.shape[-1])
        in_specs = ([row(3 * D), row(D), pl.BlockSpec((tt, nb, D), lambda i: (i, 0, 0)), row(M_WIDTH), row(D)]
                    + weights + [_const_spec((nb * tt, nb * tt))])
        args = [v3(gates), v3(yr), ys, v3(ym), v3(h), wr, wglu, ws, wm, wo, g,
                jnp.asarray(_time_major_perm(nb, tt).T, _MXU)]
        out_spec, out_shape, grid = row(D), (nb, t, D), (t // tt,)
    out = pl.pallas_call(
        functools.partial(_merge_kernel, ys_tmajor=seq is not None),
        out_shape=jax.ShapeDtypeStruct(out_shape, F32),
        grid=grid,
        in_specs=in_specs,
        out_specs=out_spec,
        compiler_params=_cparams(("parallel",)),
        name="mixer_merge",
    )(*args)
    return out.reshape(n, D)


def _rwkv_prep_kernel(p_ref, aux_ref, shift_ref, mu_ref, wa_ref, w0_ref, a0_ref, g2_ref, kk_ref, ka_ref,
                      r_ref, e_ref, k_ref, v_ref, kk_out_ref, a_out_ref, g_ref, *, T, tiles_per_seq):
    i = pl.program_id(0)
    p = p_ref[...]
    tm = p.shape[0]
    row = _iota((tm, 1), 0)
    rolled = pltpu.roll(p, 1, axis=0)
    if tiles_per_seq >= 1:
        b = i // tiles_per_seq
        tt = i - b * tiles_per_seq
        first = jnp.where(tt == 0, shift_ref[pl.ds(b % 8, 1), :], aux_ref[7:8, :])
        prev = jnp.where(row == 0, first, rolled)
    else:
        prev = jnp.where((row & (T - 1)) == 0, shift_ref[...], rolled)
    pm = p + mu_ref[...] * (prev - p)
    r = pm[:, 0:D]
    k = pm[:, D:2 * D]
    v = pm[:, 2 * D:3 * D]
    x_wa = pm[:, 3 * D:3 * D + 128]
    xg = pm[:, 3 * D + 128:3 * D + 256]
    lora_in = jnp.where(_iota(x_wa.shape, 1) < 64, jnp.tanh(x_wa), x_wa)
    lwa = _mm(lora_in, wa_ref[...])
    zw = w0_ref[...] + lwa[:, 0:D]
    w_log = -_softplus(-zw) - 0.5
    a = _sigmoid(a0_ref[...] + lwa[:, D:2 * D])
    g = _mm(_sigmoid(xg), g2_ref[...])
    r_ref[...] = r
    e_ref[...] = jnp.exp(w_log)
    k_ref[...] = k * (1.0 + (a - 1.0) * ka_ref[...])
    v_ref[...] = v
    kk_out_ref[...] = k * kk_ref[...]
    a_out_ref[...] = a
    g_ref[...] = g


def rwkv_prep(p, shift_prev, mu, wa, w0, a0, g2, k_k, k_a, *, T, tm=256):
    n, rc = p.shape
    tm = _row_tile(n, tm)
    if tm <= T:
        assert T % tm == 0 and tm % 8 == 0 and shift_prev.shape[0] % 8 == 0
        tps = T // tm
        aux_spec = pl.BlockSpec((8, rc), lambda i: (jnp.maximum(i * (tm // 8) - 1, 0), 0))
        shift_spec = pl.BlockSpec((8, rc), lambda i: ((i // tps) // 8, 0))
        shift_arg = shift_prev
    else:
        assert tm % T == 0 and (T & (T - 1)) == 0
        tps = 0
        aux_spec = pl.BlockSpec((8, rc), lambda i: (0, 0))
        shift_spec = pl.BlockSpec((tm, rc), lambda i: (i, 0))
        shift_arg = jnp.repeat(shift_prev, T, axis=0)
    row = lambda: pl.BlockSpec((tm, D), lambda i: (i, 0))
    outs = [jax.ShapeDtypeStruct((n, D), F32)] * 7
    return pl.pallas_call(
        functools.partial(_rwkv_prep_kernel, T=T, tiles_per_seq=tps),
        out_shape=outs,
        grid=(n // tm,),
        in_specs=[pl.BlockSpec((tm, rc), lambda i: (i, 0)),
                  aux_spec, shift_spec,
                  _const_spec((1, rc)),
                  _const_spec((128, 2 * D)),
                  _const_spec((1, D)), _const_spec((1, D)),
                  _const_spec((128, D)),
                  _const_spec((1, D)), _const_spec((1, D))],
        out_specs=[row() for _ in range(7)],
        compiler_params=_cparams(("parallel",)),
        name="rwkv_prep",
    )(p, p, shift_arg, mu, wa, w0, a0, g2, k_k, k_a)


def _rwkv_rec_kernel(r_ref, e_ref, k_ref, v_ref, kk_ref, a_ref, g_ref, s0_ref, rk_ref, lnw_ref, lnb_ref,
                     y_ref, sfin_ref, s_scr, *, C, G, NP, n_steps):
    step = pl.program_id(2)
    C2 = 2 * C
    probs = [(gi, pp) for gi in range(G) for pp in range(NP)]
    rng = range(len(probs))

    def ld(ref, q):
        gi, pp = probs[q]
        return ref[gi, :, pp * R_PAIR:(pp + 1) * R_PAIR]

    def par(ref, q):
        pp = probs[q][1]
        return ref[:, pp * R_PAIR:(pp + 1) * R_PAIR]

    lane = _iota((1, R_PAIR), 1)
    m0 = lane < R_HEAD
    ri = _iota((R_PAIR, R_PAIR), 0)
    ci = _iota((R_PAIR, R_PAIR), 1)
    bd = (ri < R_HEAD) == (ci < R_HEAD)
    rt_i = _iota((C2, C2), 0)
    ct_i = _iota((C2, C2), 1)
    same = (rt_i < C) == (ct_i < C)
    tr = jnp.where(rt_i < C, rt_i, rt_i - C)
    ts = jnp.where(ct_i < C, ct_i, ct_i - C)
    m_sl = same & (ts < tr)
    m_li = same & (ts <= tr)
    eye2 = jnp.where(rt_i == ct_i, 1.0, 0.0)
    lvl_masks = []
    m = 1
    while m < C:
        lvl_masks.append(((tr & ~(2 * m - 1)) == (ts & ~(2 * m - 1))) & ((tr & m) != 0) & ((ts & m) == 0))
        m *= 2
    tril = jnp.where(_iota((C, C), 1) <= _iota((C, C), 0), 1.0, 0.0).astype(_MXU)

    @pl.when(step == 0)
    def _():
        z = jnp.zeros((R_HEAD, R_HEAD), F32)
        for q, (gi, pp) in enumerate(probs):
            s_scr[q] = jnp.concatenate([jnp.concatenate([s0_ref[gi, 2 * pp], z], axis=1),
                                        jnp.concatenate([z, s0_ref[gi, 2 * pp + 1]], axis=1)], axis=0)

    def stack(x):
        return jnp.concatenate([jnp.where(m0, x, 0.0), jnp.where(m0, 0.0, x)], axis=0)

    def fold(xs):
        return xs[:C] + xs[C:]

    def each(f, *lists):
        return [f(*xs) for xs in zip(*lists)]

    def head_sum(x):
        s0 = jnp.sum(jnp.where(m0, x, 0.0), axis=-1, keepdims=True)
        s1 = jnp.sum(jnp.where(m0, 0.0, x), axis=-1, keepdims=True)
        return jnp.where(m0, s0, s1)

    r = [ld(r_ref, q) for q in rng]
    e = [ld(e_ref, q) for q in rng]
    k = [ld(k_ref, q) for q in rng]
    v = [ld(v_ref, q) for q in rng]
    kk = [ld(kk_ref, q) for q in rng]
    an = each(lambda x: -x * lax.rsqrt(head_sum(x * x) + 1e-12), kk)
    bn = each(lambda n_, q: -n_ * ld(a_ref, q), an, rng)
    cs = each(lambda x: _mm_sel_l(tril, x), e)
    cl = each(lambda x: x[C - 1:C, :], cs)
    at = each(lambda a_, e_, c_: a_ * jnp.exp(e_ - c_), an, e, cs)
    rt = each(lambda r_, c_: r_ * jnp.exp(-c_), r, cs)
    ecs = each(jnp.exp, cs)
    bt_ = each(lambda b_, x: b_ * x, bn, ecs)
    kt = each(lambda k_, x: k_ * x, k, ecs)
    ecl = each(lambda c_, l_: jnp.exp(c_ - l_), cs, cl)
    bh = each(lambda b_, x: b_ * x, bn, ecl)
    kh = each(lambda k_, x: k_ * x, k, ecl)
    As, Rs, Vs = each(stack, at), each(stack, rt), each(stack, v)
    b2 = each(lambda x: jnp.concatenate([x, x], axis=0), bt_)
    k2 = each(lambda x: jnp.concatenate([x, x], axis=0), kt)
    N = each(lambda a_, b_: jnp.where(m_sl, _mm_nt(a_, b_), 0.0), As, b2)
    Ak = each(lambda a_, b_: jnp.where(m_sl, _mm_nt(a_, b_), 0.0), As, k2)
    Arb = each(lambda a_, b_: jnp.where(m_li, _mm_nt(a_, b_), 0.0), Rs, b2)
    Ark = each(lambda a_, b_: jnp.where(m_li, _mm_nt(a_, b_), 0.0), Rs, k2)
    T = each(lambda n_: eye2 + jnp.where(lvl_masks[0], n_, 0.0), N)
    for lm in lvl_masks[1:]:
        LT = each(lambda n_, t_: _mm(jnp.where(lm, n_, 0.0), t_), N, T)
        T = each(lambda t_, x: t_ + _mm(t_, x), T, LT)
    AkV = each(_mm, Ak, Vs)
    X = each(lambda t_, a_, u_: _mm(t_, jnp.concatenate([a_, u_], axis=1)), T, As, AkV)
    Z = each(_mm, Arb, X)
    ArkV = each(_mm, Ark, Vs)
    Rp = each(lambda r_, z_: r_ + fold(z_[:, 0:R_PAIR]), rt, Z)
    Y0 = each(lambda z_, a_: fold(z_[:, R_PAIR:] + a_), Z, ArkV)
    Ap = each(lambda x: fold(x[:, 0:R_PAIR]), X)
    U0 = each(lambda x: fold(x[:, R_PAIR:]), X)
    P = each(lambda a_, b_: jnp.where(bd, _mm_tn(a_, b_), 0.0), Ap, bh)
    Q = each(lambda u_, v_, b_, k_: jnp.where(bd, _mm_tn(jnp.concatenate([u_, v_], axis=0),
                                                         jnp.concatenate([b_, k_], axis=0)), 0.0), U0, v, bh, kh)
    S = [s_scr[q] for q in rng]
    y = each(lambda y_, r_, s_: y_ + _mm_nt(r_, s_), Y0, Rp, S)
    S = each(lambda s_, l_, p_, q_: s_ * jnp.exp(-l_) + _mm(s_, p_) + q_, S, cl, P, Q)
    for q in rng:
        s_scr[q] = S[q]
    dlt = each(lambda y_: y_ - head_sum(y_) * (1.0 / R_HEAD), y)
    var = each(lambda d_: head_sum(d_ * d_) * (1.0 / R_HEAD), dlt)
    bonus = each(lambda r_, k_, v_, q: head_sum(r_ * k_ * par(rk_ref, q)) * v_, r, k, v, rng)
    for q, (gi, pp) in enumerate(probs):
        yn = dlt[q] * lax.rsqrt(var[q] + R_LN_EPS) * par(lnw_ref, q) + par(lnb_ref, q)
        y_ref[gi, :, pp * R_PAIR:(pp + 1) * R_PAIR] = (yn + bonus[q]) * ld(g_ref, q)

    @pl.when(step == n_steps - 1)
    def _():
        for q, (gi, pp) in enumerate(probs):
            sfin_ref[gi, 2 * pp] = S[q][0:R_HEAD, 0:R_HEAD]
            sfin_ref[gi, 2 * pp + 1] = S[q][R_HEAD:, R_HEAD:]


def _stacked_out(kernel_fn, in_specs, args, out_index, stack):
    if stack is None:
        return kernel_fn, in_specs, args, {}
    pos = len(args)

    def with_alias(*refs):
        return kernel_fn(*refs[:pos], *refs[pos + 1:])

    return with_alias, in_specs + [pl.BlockSpec(memory_space=pl.ANY)], args + [stack], {pos: out_index}


def rwkv_recurrence(r, e, k, v, kk, a, g, s0, l, r_k, ln_w, ln_b, stack, *, C, G, NP):
    B, T, _ = r.shape
    n_steps = T // C
    n_pairs = D // R_PAIR
    blk = lambda: pl.BlockSpec((G, C, NP * R_PAIR), lambda b, p, s: (b, s, p))
    par = lambda: pl.BlockSpec((1, NP * R_PAIR), lambda b, p, s: (0, p))
    st_spec = lambda: pl.BlockSpec((None, G, 2 * NP, R_HEAD, R_HEAD), lambda b, p, s: (l, b, p, 0, 0))
    in_specs = [blk() for _ in range(7)] + [st_spec(), par(), par(), par()]
    kern, in_specs, args, aliases = _stacked_out(
        functools.partial(_rwkv_rec_kernel, C=C, G=G, NP=NP, n_steps=n_steps), in_specs,
        [r, e, k, v, kk, a, g, s0, r_k, ln_w, ln_b], 1, stack)
    y, sfin = pl.pallas_call(
        kern,
        out_shape=[jax.ShapeDtypeStruct((B, T, D), F32),
                   jax.ShapeDtypeStruct((s0.shape[0], B, 2 * n_pairs, R_HEAD, R_HEAD), F32)],
        grid=(B // G, n_pairs // NP, n_steps),
        in_specs=in_specs,
        out_specs=[blk(), st_spec()],
        scratch_shapes=[pltpu.VMEM((G * NP, R_PAIR, R_PAIR), F32)],
        input_output_aliases=aliases,
        compiler_params=_cparams(("parallel", "parallel", "arbitrary")),
        name="rwkv_rec",
    )(*args)
    return y, sfin


def _s5_kernel(u_ref, h0r_ref, h0i_ref, ar_ref, ai_ref, bre_ref, bim_ref, cre_ref, cim_ref, d_ref, *rest,
               tc, nb, kb, n_steps, permute):
    if permute:
        perm_ref, permt_ref = rest[:2]
        rest = rest[2:]
    y_ref, hr_out, hi_out, inr_scr, ini_scr, hr_scr, hi_scr = rest
    step = pl.program_id(1)
    rows = nb * tc

    @pl.when(step == 0)
    def _():
        hr_scr[...] = h0r_ref[...]
        hi_scr[...] = h0i_ref[...]

    u = u_ref[...].reshape(rows, kb * S5_BLK_CH)
    ch = lambda j: slice(j * S5_BLK_CH, (j + 1) * S5_BLK_CH)
    stt = lambda j: slice(j * S5_BLK_ST, (j + 1) * S5_BLK_ST)
    if permute:
        ut = jnp.dot(perm_ref[...], u.astype(_MXU), preferred_element_type=F32).astype(_MXU)
    else:
        ut = u.astype(_MXU)
    for j in range(kb):
        inr_scr[:, stt(j)] = jnp.dot(ut[:, ch(j)], bre_ref[j], preferred_element_type=F32)
        ini_scr[:, stt(j)] = jnp.dot(ut[:, ch(j)], bim_ref[j], preferred_element_type=F32)
    ar = ar_ref[...]
    ai = ai_ref[...]

    def body(t, carry):
        hr, hi = carry
        sl = pl.ds(pl.multiple_of(t * nb, nb), nb)
        nr = ar * hr - ai * hi + inr_scr[sl, :]
        ni = ar * hi + ai * hr + ini_scr[sl, :]
        inr_scr[sl, :] = nr
        ini_scr[sl, :] = ni
        return nr, ni

    hr, hi = lax.fori_loop(0, tc, body, (hr_scr[...], hi_scr[...]))
    hr_scr[...] = hr
    hi_scr[...] = hi
    yts = [jnp.dot(inr_scr[:, stt(j)].astype(_MXU), cre_ref[j], preferred_element_type=F32)
           - jnp.dot(ini_scr[:, stt(j)].astype(_MXU), cim_ref[j], preferred_element_type=F32) for j in range(kb)]
    yt = yts[0] if kb == 1 else jnp.concatenate(yts, axis=1)
    if permute:
        yt_hi = yt.astype(_MXU)
        yt_lo = (yt - yt_hi.astype(F32)).astype(_MXU)
        yt = (jnp.dot(permt_ref[...], yt_hi, preferred_element_type=F32)
              + jnp.dot(permt_ref[...], yt_lo, preferred_element_type=F32))
    y = yt + d_ref[...] * u
    y = 0.5 * y * (1.0 + jnp.tanh(math.sqrt(2.0 / math.pi) * (y + 0.044715 * (y * y * y))))
    y_ref[...] = y.reshape(y_ref.shape)

    @pl.when(step == n_steps - 1)
    def _():
        hr_out[...] = hr
        hi_out[...] = hi


def _time_major_perm(nb, tc):
    rows = nb * tc
    j = np.arange(rows)
    perm_np = np.zeros((rows, rows), np.float32)
    perm_np[(j % tc) * nb + j // tc, j] = 1.0
    return perm_np


def s5_scan(u3, h0r, h0i, l, ar, ai, bre, bim, cre, cim, d, *, nb, tc, blk, kb):
    A, R, _ = u3.shape
    ga, tr = blk
    assert ga * tr == nb * tc
    tmajor = (ga, tr) == (tc, nb) and R == nb
    nblk = D // S5_BLK_CH
    n_steps = A // ga if tmajor else R // tr
    rows = nb * tc
    u_spec = (pl.BlockSpec((ga, tr, kb * S5_BLK_CH), lambda c, s: (s, 0, c)) if tmajor
              else pl.BlockSpec((ga, tr, kb * S5_BLK_CH), lambda c, s: (0, s, c)))
    if tmajor:
        perm_specs, perm_args = [], []
    else:
        assert A == ga
        perm_np = _time_major_perm(nb, tc)
        perm_specs = [_const_spec((rows, rows)), _const_spec((rows, rows))]
        perm_args = [jnp.asarray(perm_np, _MXU), jnp.asarray(perm_np.T, _MXU)]
    st = lambda: pl.BlockSpec((None, nb, kb * S5_BLK_ST), lambda c, s: (l, 0, c))
    vec = lambda w: pl.BlockSpec((1, kb * w), lambda c, s: (0, c))
    mat = lambda a, b: pl.BlockSpec((kb, a, b), lambda c, s: (c, 0, 0))
    return pl.pallas_call(
        functools.partial(_s5_kernel, tc=tc, nb=nb, kb=kb, n_steps=n_steps, permute=not tmajor),
        out_shape=[jax.ShapeDtypeStruct(u3.shape, F32),
                   jax.ShapeDtypeStruct((nb, nblk * S5_BLK_ST), F32),
                   jax.ShapeDtypeStruct((nb, nblk * S5_BLK_ST), F32)],
        grid=(nblk // kb, n_steps),
        in_specs=[u_spec,
                  st(), st(), vec(S5_BLK_ST), vec(S5_BLK_ST),
                  mat(S5_BLK_CH, S5_BLK_ST), mat(S5_BLK_CH, S5_BLK_ST),
                  mat(S5_BLK_ST, S5_BLK_CH), mat(S5_BLK_ST, S5_BLK_CH),
                  vec(S5_BLK_CH)] + perm_specs,
        out_specs=[u_spec,
                   pl.BlockSpec((nb, kb * S5_BLK_ST), lambda c, s: (0, c)),
                   pl.BlockSpec((nb, kb * S5_BLK_ST), lambda c, s: (0, c))],
        scratch_shapes=[pltpu.VMEM((rows, kb * S5_BLK_ST), F32), pltpu.VMEM((rows, kb * S5_BLK_ST), F32),
                        pltpu.VMEM((nb, kb * S5_BLK_ST), F32), pltpu.VMEM((nb, kb * S5_BLK_ST), F32)],
        compiler_params=_cparams(("parallel", "arbitrary")),
        name="s5_scan",
    )(u3, h0r, h0i, ar, ai, bre, bim, cre, cim, d, *perm_args)


def _s5_discretize(a_re, a_im, log_dt, b_re, b_im, c_re, c_im):
    g, p, hch = b_re.shape
    dt = jnp.exp(log_dt.astype(F32))[:, None]
    mag = jnp.exp(dt * a_re)
    abar_re = mag * jnp.cos(dt * a_im)
    abar_im = mag * jnp.sin(dt * a_im)
    den = a_re * a_re + a_im * a_im
    nr = abar_re - 1.0
    q_re = (nr * a_re + abar_im * a_im) / den
    q_im = (abar_im * a_re - nr * a_im) / den
    bb_re = q_re[..., None] * b_re - q_im[..., None] * b_im
    bb_im = q_re[..., None] * b_im + q_im[..., None] * b_re
    nblk = D // S5_BLK_CH
    gl = g // nblk
    eye = jnp.eye(gl, dtype=F32)

    def in_blocks(bb):
        t = jnp.transpose(bb, (0, 2, 1)).reshape(nblk, gl, hch, p)
        return jnp.einsum('cghp,gk->cghkp', t, eye).reshape(nblk, gl * hch, gl * p).astype(_MXU)

    def out_blocks(cc):
        t = jnp.transpose(cc, (0, 2, 1)).reshape(nblk, gl, p, hch)
        return jnp.einsum('cgph,gk->cgpkh', t, eye).reshape(nblk, gl * p, gl * hch).astype(_MXU)

    return (abar_re.reshape(1, g * p), abar_im.reshape(1, g * p), in_blocks(bb_re), in_blocks(bb_im),
            out_blocks(c_re), out_blocks(c_im))


def _ssd_kernel(xbc_ref, z_ref, dt_ref, cprev_ref, h0_ref, cw_ref, cb_ref, dtb_ref, alog_ref, dx_ref, ng_ref,
                eh_ref, y_ref, hfin_ref, ext_scr, h_scr, *, L, G, t_real, n_steps):
    step = pl.program_id(1)
    rng = range(G)
    NB = M_GROUPS * M_STATE

    def each(f, *lists):
        return [f(*xs) for xs in zip(*lists)]

    @pl.when(step == 0)
    def _():
        for s in rng:
            ext_scr[s] = jnp.zeros((8, M_CONV_DIM), F32)
            ext_scr[s, 5:8, :] = cprev_ref[s]
            for gi in range(M_GROUPS):
                h_scr[s, gi] = h0_ref[s, gi * M_HPG:(gi + 1) * M_HPG].reshape(M_GW, M_STATE).T

    tril = jnp.where(_iota((L, L), 1) <= _iota((L, L), 0), 1.0, 0.0).astype(_MXU)
    causal = _iota((L, L), 1) <= _iota((L, L), 0)
    eye_h = _eye(128, _MXU)
    eh = eh_ref[...]
    lane_head = _iota((1, M_GW), 1) // M_HEAD
    row8 = _iota((8, 1), 0)
    a = -jnp.exp(alog_ref[...])
    conv, dts = [], []
    for s in rng:
        x = xbc_ref[s * L:(s + 1) * L, :]
        prev8 = ext_scr[s]
        c = cb_ref[...] + cw_ref[3:4, :] * x
        for j in range(1, 4):
            xr = pltpu.roll(x, j, axis=0)
            head = jnp.where(row8 < j, pltpu.roll(prev8, j, axis=0), xr[0:8])
            xr = head if L == 8 else jnp.concatenate([head, xr[8:]], axis=0)
            c = c + cw_ref[3 - j:4 - j, :] * xr
        ext_scr[s] = x[L - 8:L]
        conv.append(c * _sigmoid(c))
        dt = _softplus(dt_ref[s * L:(s + 1) * L, :] + dtb_ref[...])
        if t_real < L:
            dt = jnp.where(_iota(dt.shape, 0) < t_real, dt, 0.0)
        dts.append(dt)
    xs = each(lambda c: c[:, 0:M_WIDTH], conv)
    acum = each(lambda d_: _mm_sel_l(tril, d_ * a), dts)
    acum_t = each(lambda x: _mm_nt_sel_l(eye_h, x), acum)
    dt_x = each(lambda d_: _mm_sel_r(d_, eh), dts)
    acum_x = each(lambda x: _mm_sel_r(x, eh), acum)
    acl_x = each(lambda x: x[L - 1:L, :], acum_x)
    xd = each(lambda x, d_: x * d_, xs, dt_x)
    xdd = each(lambda x, l_, c_: x * jnp.exp(l_ - c_), xd, acl_x, acum_x)
    eacum_x = each(jnp.exp, acum_x)
    cdec_x = each(jnp.exp, acl_x)
    ys = [[] for _ in rng]
    for gi in range(M_GROUPS):
        gs = slice(gi * M_GW, (gi + 1) * M_GW)
        bg = each(lambda c: c[:, M_WIDTH + gi * M_STATE:M_WIDTH + (gi + 1) * M_STATE], conv)
        cg = each(lambda c: c[:, M_WIDTH + NB + gi * M_STATE:M_WIDTH + NB + (gi + 1) * M_STATE], conv)
        h = [h_scr[s, gi] for s in rng]
        cbm = each(_mm_nt, cg, bg)
        bgt = each(lambda b_: _mm_nt(eye_h, b_), bg)
        y_off = each(lambda c_, h_, e_: _mm(c_, h_) * e_[:, gs], cg, h, eacum_x)
        ms = []
        for e in range(M_HPG):
            he = gi * M_HPG + e
            ms.append(each(lambda c_, a_, t_: c_ * jnp.exp(jnp.where(causal, a_[:, he:he + 1] - t_[he:he + 1, :],
                                                                     -jnp.inf)), cbm, acum, acum_t))
        if L % 128 == 0:
            mcat = [jnp.concatenate([ms[e][s] for e in range(M_HPG)], axis=1) for s in rng]
            xst = each(lambda x: jnp.concatenate([jnp.where(lane_head == e, x[:, gs], 0.0).astype(_MXU)
                                                  for e in range(M_HPG)], axis=0), xd)
            y_dg = each(_mm, mcat, xst)
        else:
            y_dg = [sum(_mm(ms[e][s], jnp.where(lane_head == e, xd[s][:, gs], 0.0)) for e in range(M_HPG))
                    for s in rng]
        hn = each(lambda h_, d_, b_, x: h_ * d_[:, gs] + _mm(b_, x[:, gs]), h, cdec_x, bgt, xdd)
        for s in rng:
            h_scr[s, gi] = hn[s]
            ys[s].append(y_off[s] + y_dg[s])
    for s in rng:
        y = jnp.concatenate(ys[s], axis=1) + dx_ref[...] * xs[s]
        zz = z_ref[s * L:(s + 1) * L, :]
        y = y * (zz * _sigmoid(zz))
        outs = []
        for gi in range(M_GROUPS):
            yg = y[:, gi * M_GW:(gi + 1) * M_GW]
            outs.append(yg * lax.rsqrt(jnp.mean(yg * yg, axis=-1, keepdims=True) + EPS))
        y_ref[s * L:(s + 1) * L, :] = jnp.concatenate(outs, axis=1) * ng_ref[...]

    @pl.when(step == n_steps - 1)
    def _():
        for s in rng:
            for gi in range(M_GROUPS):
                hfin_ref[s, gi * M_HPG:(gi + 1) * M_HPG] = h_scr[s, gi].T.reshape(M_HPG, M_HEAD, M_STATE)


def ssd_block(xbc, z, dt, conv_prev, h0, l, conv_w, conv_b, dt_bias, a_log, d_x, norm_g, eh, stack, *, B, T, L, G,
              t_real):
    n_steps = T // L
    assert G == 1 or n_steps == 1
    heads = M_GROUPS * M_HPG
    seq = lambda w: pl.BlockSpec((G * L, w), lambda b, s: (b * n_steps + s, 0))
    st_spec = lambda: pl.BlockSpec((None, G, heads, M_HEAD, M_STATE), lambda b, s: (l, b, 0, 0, 0))
    in_specs = [seq(M_CONV_DIM), seq(M_WIDTH), seq(128),
                pl.BlockSpec((None, G, 3, M_CONV_DIM), lambda b, s: (l, b, 0, 0)),
                st_spec(),
                _const_spec((4, M_CONV_DIM)), _const_spec((1, M_CONV_DIM)),
                _const_spec((1, 128)), _const_spec((1, 128)),
                _const_spec((1, M_WIDTH)), _const_spec((1, M_WIDTH)),
                _const_spec((128, M_WIDTH))]
    kern, in_specs, args, aliases = _stacked_out(
        functools.partial(_ssd_kernel, L=L, G=G, t_real=t_real, n_steps=n_steps), in_specs,
        [xbc, z, dt, conv_prev, h0, conv_w, conv_b, dt_bias, a_log, d_x, norm_g, eh], 1, stack)
    return pl.pallas_call(
        kern,
        out_shape=[jax.ShapeDtypeStruct((B * T, M_WIDTH), F32),
                   jax.ShapeDtypeStruct((h0.shape[0], B, heads, M_HEAD, M_STATE), F32)],
        grid=(B // G, n_steps),
        in_specs=in_specs,
        out_specs=[seq(M_WIDTH), st_spec()],
        scratch_shapes=[pltpu.VMEM((G, 8, M_CONV_DIM), F32),
                        pltpu.VMEM((G, M_GROUPS, M_STATE, M_GW), F32)],
        input_output_aliases=aliases,
        compiler_params=_cparams(("parallel", "arbitrary")),
        name="ssd",
    )(*args)


def _attn_kernel(q_ref, k_ref, v_ref, o_ref, *, nb, tq, heads_split):
    scale = XA_HD ** -0.5
    for j in range(nb):
        if heads_split:
            k_all = pltpu.einshape("mhd->hmd", k_ref[j])
            v_all = pltpu.einshape("mhd->hmd", v_ref[j])
        outs = []
        for hd in range(XA_HEADS):
            cs = slice(hd * XA_HD, (hd + 1) * XA_HD)
            q = q_ref[j * tq:(j + 1) * tq, cs]
            kh = k_all[hd] if heads_split else k_ref[j, :, cs]
            vh = v_all[hd] if heads_split else v_ref[j, :, cs]
            s = _mm_nt(q, kh) * scale
            s = s - jnp.max(s, axis=-1, keepdims=True)
            p = jnp.exp(s)
            p = p / jnp.sum(p, axis=-1, keepdims=True)
            outs.append(_mm(p, vh))
        o_ref[j * tq:(j + 1) * tq, :] = jnp.concatenate(outs, axis=1)


def cross_attention(q, mk, mv, kv_index, *, B, T, tq, nb):
    nlead = len(kv_index)
    heads_split = mk.ndim - nlead == 4
    kv_blk = mk.shape[nlead + 1:]
    n_t = T // tq
    kv_spec = pl.BlockSpec((None,) * nlead + (nb,) + kv_blk,
                           lambda b, s: tuple(kv_index) + (b,) + (0,) * len(kv_blk))
    q_spec = pl.BlockSpec((nb * tq, D), lambda b, s: (b * n_t + s, 0))
    return pl.pallas_call(
        functools.partial(_attn_kernel, nb=nb, tq=tq, heads_split=heads_split),
        out_shape=jax.ShapeDtypeStruct((B * T, D), F32),
        grid=(B // nb, n_t),
        in_specs=[q_spec, kv_spec, kv_spec],
        out_specs=q_spec,
        compiler_params=_cparams(("parallel", "arbitrary")),
        name="xattn",
    )(q, mk, mv)


def _pad_time(x2, B, T, Tp):
    if Tp == T:
        return x2
    w = x2.shape[-1]
    return jnp.pad(x2.reshape(B, T, w), ((0, 0), (0, Tp - T), (0, 0))).reshape(B * Tp, w)


def _layer(l, h, grp, W, st, acc):
    T, B = grp['T'], grp['B']
    n = T * B
    gates = norm_matmul(h, W['g_mix_pre'], W['w_gates'], name="in_gates")
    p_r = norm_matmul(h, W['g_mix_pre'], W['w_rwkv'], name="in_rwkv")
    if grp['s5_tmajor']:
        u3 = norm_matmul_tmajor(h.reshape(B, T, D), W['g_mix_pre'], W['w_s5'], name="in_s5")
    else:
        u3 = norm_matmul(h, W['g_mix_pre'], W['w_s5'], name="in_s5").reshape(1, n, D)
    z_m =norm_matmul(h, W['g_mix_pre'], W['w_z'], name="in_z")
    xbc = norm_matmul(h, W['g_mix_pre'], W['w_xbc'], name="in_xbc")
    dt_m = norm_matmul(h, W['g_mix_pre'], W['w_dt'], tn=128, name="in_dt")

    seqs = rwkv_prep(p_r, st['shift'][l], W['mu'], W['wa'], W['w0'], W['a0'], W['g2'],
                     W['k_k'], W['k_a'], T=T)
    Tp = grp['rwkv_Tpad']
    seqs = [_pad_time(x, B, T, Tp).reshape(B, Tp, D) for x in seqs]
    yr, s_fin = rwkv_recurrence(*seqs, st['rwkv'], l, W['r_k'], W['ln_w'], W['ln_b'], acc[0],
                                C=grp['rwkv_C'], G=grp['rwkv_G'], NP=grp['rwkv_NP'])
    yr = yr[:, :T].reshape(n, D)
    shift_new = p_r.reshape(B, T, -1)[:, T - 1]

    ys, s5r, s5i = s5_scan(u3, st['s5r'], st['s5i'], l, W['s5_ar'], W['s5_ai'], W['s5_bre'], W['s5_bim'],
                           W['s5_cre'], W['s5_cim'], W['s5_d'], nb=B, tc=grp['s5_tc'], blk=grp['s5_blk'],
                           kb=2 if B <= 16 else 1)

    Lc = grp['ssd_L']
    Tm = grp['ssd_Tpad']
    ym, ssm_fin = ssd_block(_pad_time(xbc, B, T, Tm), _pad_time(z_m, B, T, Tm), _pad_time(dt_m, B, T, Tm),
                            st['conv'], st['ssm'], l, W['conv_w'], W['conv_b'], W['dt_bias'],
                            W['a_log'], W['d_x'], W['m_norm'], W['eh'], acc[1], B=B, T=Tm, L=Lc, G=grp['ssd_G'],
                            t_real=min(T, Lc))
    ym = ym.reshape(B, Tm, M_WIDTH)[:, :T].reshape(n, M_WIDTH)
    conv_new = jnp.concatenate([st['conv'][l], xbc.reshape(B, T, M_CONV_DIM)[:, max(T - 3, 0):]], axis=1)[:, -3:]

    h = mixer_merge(gates, yr, ys if grp['s5_tmajor'] else ys.reshape(n, D), ym, h, W['w_out_rwkv'],
                    W['s5_w_glu'], W['w_out_s5'], W['w_out_mamba'], W['w_out'], W['g_mix_post'],
                    seq=(B, T) if grp['s5_tmajor'] else None)

    q = norm_matmul(h, W['g_xa_pre'], W['xa_wq'], name="xa_q")
    o = cross_attention(q, st['mk'], st['mv'], st['kv_index'](l), B=B, T=T, tq=grp['xa_tq'], nb=grp['xa_nb'])
    h = matmul_norm_residual(o, W['xa_wo'], W['g_xa_post'], h, name="xa_out")

    h = mlp_block(h, W['g_mlp_pre'], W['mlp_w1'], W['mlp_w2'], W['g_mlp_post'])
    return h, (s_fin, shift_new, s5r.reshape(B, 64, 64), s5i.reshape(B, 64, 64), conv_new, ssm_fin)


def _group_cfg(T, B):
    cfg = dict(T=T, B=B)
    g = 8 if B % 8 == 0 else 1
    if T % 64 == 0:
        cfg.update(rwkv_C=64, rwkv_Tpad=T, rwkv_G=g, rwkv_NP=2)
    else:
        tp = -(-T // 8) * 8
        cfg.update(rwkv_C=tp, rwkv_Tpad=tp, rwkv_G=16 if B % 16 == 0 else g, rwkv_NP=2)
    if T % 8 == 0 and (512 // B) >= 8 and T % (512 // B) == 0:
        tc = 512 // B
        cfg.update(s5_tmajor=True, s5_tc=tc, s5_blk=(tc, B))
    else:
        cfg.update(s5_tmajor=False, s5_tc=T, s5_blk=(1, B * T))
    if T % 128 == 0:
        cfg.update(ssd_L=128, ssd_Tpad=T, ssd_G=1)
    else:
        tp = -(-T // 8) * 8
        cfg.update(ssd_L=tp, ssd_Tpad=tp, ssd_G=4 if B % 4 == 0 else 1)
    if T >= 64:
        cfg.update(xa_tq=_row_tile(T, 512), xa_nb=1)
    else:
        cfg.update(xa_tq=T, xa_nb=4 if B % 4 == 0 else 1)
    return cfg


def kernel(x_prompt, x_sample, cache_mem_k, cache_mem_v, state_rwkv, state_rwkv_shift, state_s5_re, state_s5_im, state_conv, state_ssm, mem_prompt, norm_mix_pre, norm_mix_post, norm_xa_pre, norm_xa_post, norm_mlp_pre, norm_mlp_post, norm_mem, w_in, w_out, rwkv_mu, rwkv_w0, rwkv_w2, rwkv_a0, rwkv_a2, rwkv_g2, rwkv_k_k, rwkv_k_a, rwkv_r_k, rwkv_ln_w, rwkv_ln_b, w_out_rwkv, s5_a_re, s5_a_im, s5_log_dt, s5_b_re, s5_b_im, s5_c_re, s5_c_im, s5_d, s5_w_glu, w_out_s5, m_conv_w, m_conv_b, m_dt_bias, m_a_log, m_d, m_norm, w_out_mamba, xa_wq, xa_wk, xa_wv, xa_wo, mlp_w1, mlp_w2):
    depth = w_in.shape[0]
    bp, tp, _ = x_prompt.shape
    bs, ts, _ = x_sample.shape
    mlen = mem_prompt.shape[1]
    bf = lambda x: x.astype(_MXU)
    row = lambda x: x.reshape(1, -1).astype(F32)

    eh = (jnp.arange(128)[:, None] == (jnp.arange(M_WIDTH)[None, :] // M_HEAD)).astype(_MXU)

    col = [0]
    for sz in (3 * D, 3 * D + 256, D, M_WIDTH, M_CONV_DIM, 32):
        col.append(col[-1] + sz)
    layers = []
    for l in range(depth):
        wl = w_in[l]
        ar, ai, bre, bim, cre, cim = _s5_discretize(s5_a_re[l], s5_a_im[l], s5_log_dt[l], s5_b_re[l], s5_b_im[l],
                                                    s5_c_re[l], s5_c_im[l])
        zero = jnp.zeros((64, D), F32)
        wa = jnp.concatenate([jnp.concatenate([rwkv_w2[l], zero], axis=1),
                              jnp.concatenate([zero, rwkv_a2[l]], axis=1)], axis=0)
        pad32 = lambda x: jnp.pad(x.reshape(1, -1).astype(F32), ((0, 0), (0, 128 - x.shape[-1])))
        layers.append(dict(
            g_mix_pre=row(norm_mix_pre[l]), g_mix_post=row(norm_mix_post[l]),
            g_xa_pre=row(norm_xa_pre[l]), g_xa_post=row(norm_xa_post[l]),
            g_mlp_pre=row(norm_mlp_pre[l]), g_mlp_post=row(norm_mlp_post[l]),
            w_gates=bf(wl[:, col[0]:col[1]]), w_rwkv=bf(wl[:, col[1]:col[2]]), w_s5=bf(wl[:, col[2]:col[3]]),
            w_z=bf(wl[:, col[3]:col[4]]), w_xbc=bf(wl[:, col[4]:col[5]]),
            w_dt=bf(jnp.pad(wl[:, col[5]:col[6]], ((0, 0), (0, 96)))),
            mu=row(rwkv_mu[l]), wa=bf(wa), w0=row(rwkv_w0[l]), a0=row(rwkv_a0[l]), g2=bf(rwkv_g2[l]),
            k_k=row(rwkv_k_k[l]), k_a=row(rwkv_k_a[l]), r_k=row(rwkv_r_k[l]),
            ln_w=row(rwkv_ln_w[l]), ln_b=row(rwkv_ln_b[l]),
            s5_ar=ar, s5_ai=ai, s5_bre=bre, s5_bim=bim, s5_cre=cre, s5_cim=cim, s5_d=row(s5_d[l]),
            conv_w=m_conv_w[l].astype(F32), conv_b=row(m_conv_b[l]), dt_bias=pad32(m_dt_bias[l]),
            a_log=pad32(m_a_log[l]), d_x=row(jnp.repeat(m_d[l], M_HEAD)), m_norm=row(m_norm[l]), eh=eh,
            w_out_rwkv=bf(w_out_rwkv[l]), s5_w_glu=bf(s5_w_glu[l]), w_out_s5=bf(w_out_s5[l]),
            w_out_mamba=bf(w_out_mamba[l]), w_out=bf(w_out[l]),
            xa_wq=bf(xa_wq[l]), xa_wk=bf(xa_wk[l]), xa_wv=bf(xa_wv[l]), xa_wo=bf(xa_wo[l]),
            mlp_w1=bf(mlp_w1[l]), mlp_w2=bf(mlp_w2[l]), g_mem=row(norm_mem[l]),
        ))

    hp = x_prompt.reshape(bp * tp, D).astype(F32)
    hs = x_sample.reshape(bs * ts, D).astype(F32)
    mem2 = mem_prompt.reshape(bp * mlen, D).astype(F32)

    cfg_p = _group_cfg(tp, bp)
    cfg_s = _group_cfg(ts, bs)
    zeros_p = dict(
        rwkv=jnp.zeros((depth, bp) + state_rwkv.shape[2:], F32),
        shift=jnp.zeros((depth, bp, state_rwkv_shift.shape[-1]), F32),
        s5r=jnp.zeros((depth, bp, state_s5_re.shape[2] * state_s5_re.shape[3]), F32),
        s5i=jnp.zeros((depth, bp, state_s5_re.shape[2] * state_s5_re.shape[3]), F32),
        conv=jnp.zeros((depth, bp) + state_conv.shape[2:], F32),
        ssm=jnp.zeros((depth, bp) + state_ssm.shape[2:], F32),
    )
    st_s = dict(
        rwkv=state_rwkv, shift=state_rwkv_shift,
        s5r=state_s5_re.reshape(depth, bs, -1), s5i=state_s5_im.reshape(depth, bs, -1),
        conv=state_conv, ssm=state_ssm,
        mk=cache_mem_k, mv=cache_mem_v,
        kv_index=lambda l: (l,),
    )

    mk_out, mv_out = [], []
    st_p_out = [[] for _ in range(6)]
    st_s_out = [[] for _ in range(6)]
    acc_p = acc_s = (None, None)
    for l in range(depth):
        W = layers[l]
        mk = norm_matmul(mem2, W['g_mem'], W['xa_wk'], name="mem_k")
        mv = norm_matmul(mem2, W['g_mem'], W['xa_wv'], name="mem_v")
        st_p = dict(zeros_p, mk=mk.reshape(bp, mlen, D), mv=mv.reshape(bp, mlen, D), kv_index=lambda l: ())
        hp, new_p = _layer(l, hp, cfg_p, W, st_p, acc_p)
        hs, new_s = _layer(l, hs, cfg_s, W, st_s, acc_s)
        acc_p = (new_p[0], new_p[5])
        acc_s = (new_s[0], new_s[5])
        mk_out.append(mk.reshape(bp, mlen, XA_HEADS, XA_HD))
        mv_out.append(mv.reshape(bp, mlen, XA_HEADS, XA_HD))
        for i in range(1, 5):
            st_p_out[i].append(new_p[i])
            st_s_out[i].append(new_s[i])

    stk = lambda xs: jnp.stack(xs).astype(F32)
    y_p = hp.reshape(bp, tp, D).astype(x_prompt.dtype)
    y_s = hs.reshape(bs, ts, D).astype(x_sample.dtype)
    return (y_p, y_s, stk(mk_out), stk(mv_out),
            acc_p[0], *[stk(st_p_out[i]) for i in range(1, 5)], acc_p[1],
            acc_s[0], *[stk(st_s_out[i]) for i in range(1, 5)], acc_s[1])
```

```python
import functools
import math

import numpy as np
import jax
import jax.numpy as jnp
from jax import lax
from jax.experimental import pallas as pl
from jax.experimental.pallas import tpu as pltpu

F32 = jnp.float32
_MXU = jnp.bfloat16
EPS = 1e-6
R_LN_EPS = 64e-5
D = 1024
R_HEAD = 64
R_PAIR = 2 * R_HEAD
S5_BLK_CH = 128
S5_BLK_ST = 512
M_HEAD = 64
M_STATE = 128
M_GROUPS = 4
M_HPG = 8
M_GW = M_HPG * M_HEAD
M_WIDTH = 2048
M_CONV_DIM = 3072
XA_HEADS = 4
XA_HD = 256
VMEM_LIMIT = 56 * 1024 * 1024


def _mm(a, b):
    return jnp.dot(a.astype(_MXU), b.astype(_MXU), preferred_element_type=F32)


def _mm_nt(a, b):
    return lax.dot_general(a.astype(_MXU), b.astype(_MXU), (((1,), (1,)), ((), ())),
                           preferred_element_type=F32)


def _mm_tn(a, b):
    return lax.dot_general(a.astype(_MXU), b.astype(_MXU), (((0,), (0,)), ((), ())),
                           preferred_element_type=F32)


def _split3(x):
    hi = x.astype(_MXU)
    r1 = x - hi.astype(F32)
    mid = r1.astype(_MXU)
    lo = (r1 - mid.astype(F32)).astype(_MXU)
    return hi, mid, lo


def _mm_sel_l(sel, x):
    return sum(jnp.dot(sel, p, preferred_element_type=F32) for p in _split3(x))


def _mm_sel_r(x, sel):
    return sum(jnp.dot(p, sel, preferred_element_type=F32) for p in _split3(x))


def _mm_nt_sel_l(sel, x):
    return sum(lax.dot_general(sel, p, (((1,), (1,)), ((), ())), preferred_element_type=F32)
               for p in _split3(x))


def _sigmoid(x):
    return 1.0 / (1.0 + jnp.exp(-x))


def _softplus(x):
    return jnp.maximum(x, 0.0) + jnp.log(1.0 + jnp.exp(-jnp.abs(x)))


def _rms(x, g):
    return x * lax.rsqrt(jnp.mean(x * x, axis=-1, keepdims=True) + EPS) * g


def _iota(shape, dim):
    return lax.broadcasted_iota(jnp.int32, shape, dim)


def _eye(n, dtype):
    return jnp.where(_iota((n, n), 0) == _iota((n, n), 1), 1.0, 0.0).astype(dtype)


def _cparams(sem):
    return pltpu.CompilerParams(dimension_semantics=sem, vmem_limit_bytes=VMEM_LIMIT)


def _row_tile(n, want):
    t = min(n, want)
    while n % t:
        t //= 2
    return t


def _col_tile(c, cap):
    best = 128
    for t in range(128, min(c, cap) + 1, 128):
        if c % t == 0:
            best = t
    return best


def _const_spec(shape):
    nd = len(shape)
    return pl.BlockSpec(shape, lambda *_: (0,) * nd, pipeline_mode=pl.Buffered(1))


def _norm_mm_kernel(x_ref, g_ref, w_ref, o_ref, xn_ref):
    @pl.when(pl.program_id(1) == 0)
    def _():
        xn_ref[...] = _rms(x_ref[...], g_ref[...]).astype(_MXU)

    o_ref[...] = jnp.dot(xn_ref[...], w_ref[...], preferred_element_type=F32)


def norm_matmul(x, g, w, *, tm=1024, tn=1664, name="norm_mm"):
    n, d = x.shape
    c = w.shape[1]
    tm = _row_tile(n, tm)
    tn = _col_tile(c, tn)
    return pl.pallas_call(
        _norm_mm_kernel,
        out_shape=jax.ShapeDtypeStruct((n, c), F32),
        grid=(n // tm, c // tn),
        in_specs=[pl.BlockSpec((tm, d), lambda i, j: (i, 0)),
                  pl.BlockSpec((1, d), lambda i, j: (0, 0)),
                  pl.BlockSpec((d, tn), lambda i, j: (0, j))],
        out_specs=pl.BlockSpec((tm, tn), lambda i, j: (i, j)),
        scratch_shapes=[pltpu.VMEM((tm, d), _MXU)],
        compiler_params=_cparams(("parallel", "arbitrary")),
        name=name,
    )(x, g, w)


def _mm_norm_res_kernel(x_ref, w_ref, g_ref, h_ref, o_ref):
    y = _mm(x_ref[...], w_ref[...])
    o_ref[...] = h_ref[...] + _rms(y, g_ref[...])


def matmul_norm_residual(x, w, g, h, *, tm=512, name="mm_norm_res"):
    n, k = x.shape
    d = w.shape[1]
    tm = _row_tile(n, tm)
    return pl.pallas_call(
        _mm_norm_res_kernel,
        out_shape=jax.ShapeDtypeStruct((n, d), F32),
        grid=(n // tm,),
        in_specs=[pl.BlockSpec((tm, k), lambda i: (i, 0)),
                  _const_spec((k, d)),
                  _const_spec((1, d)),
                  pl.BlockSpec((tm, d), lambda i: (i, 0))],
        out_specs=pl.BlockSpec((tm, d), lambda i: (i, 0)),
        compiler_params=_cparams(("parallel",)),
        name=name,
    )(x, w, g, h)


def _mlp_kernel(h_ref, g1_ref, w1_ref, w2_ref, g2_ref, o_ref, *, n_chunks, ck):
    h = h_ref[...]
    xn = _rms(h, g1_ref[...]).astype(_MXU)
    acc = jnp.zeros(h.shape, F32)
    for j in range(n_chunks):
        a = jnp.dot(xn, w1_ref[:, j * ck:(j + 1) * ck], preferred_element_type=F32)
        a = jnp.square(jnp.maximum(a, 0.0))
        acc = acc + jnp.dot(a.astype(_MXU), w2_ref[j * ck:(j + 1) * ck, :], preferred_element_type=F32)
    o_ref[...] = h + _rms(acc, g2_ref[...])


def mlp_block(h, g1, w1, w2, g2, *, tm=512, ck=1024):
    n, d = h.shape
    f = w1.shape[1]
    tm = _row_tile(n, tm)
    return pl.pallas_call(
        functools.partial(_mlp_kernel, n_chunks=f // ck, ck=ck),
        out_shape=jax.ShapeDtypeStruct((n, d), F32),
        grid=(n // tm,),
        in_specs=[pl.BlockSpec((tm, d), lambda i: (i, 0)),
                  _const_spec((1, d)),
                  _const_spec((d, f)),
                  _const_spec((f, d)),
                  _const_spec((1, d))],
        out_specs=pl.BlockSpec((tm, d), lambda i: (i, 0)),
        compiler_params=_cparams(("parallel",)),
        name="mlp",
    )(h, g1, w1, w2, g2)


def _norm_mm_tmajor_kernel(x_ref, g_ref, w_ref, perm_ref, o_ref):
    nb, tt, d = x_ref.shape
    xn = _rms(x_ref[...].reshape(nb * tt, d), g_ref[...]).astype(_MXU)
    xt = jnp.dot(perm_ref[...], xn, preferred_element_type=F32).astype(_MXU)
    o_ref[...] = jnp.dot(xt, w_ref[...], preferred_element_type=F32).reshape(o_ref.shape)


def norm_matmul_tmajor(x3, g, w, *, rows=512, name="norm_mm_t"):
    nb, t, d = x3.shape
    c = w.shape[1]
    tt = rows // nb
    assert t % tt == 0 and tt % 8 == 0
    perm = jnp.asarray(_time_major_perm(nb, tt), _MXU)
    return pl.pallas_call(
        _norm_mm_tmajor_kernel,
        out_shape=jax.ShapeDtypeStruct((t, nb, c), F32),
        grid=(t // tt,),
        in_specs=[pl.BlockSpec((nb, tt, d), lambda i: (0, i, 0)),
                  _const_spec((1, d)), _const_spec((d, c)), _const_spec((nb * tt, nb * tt))],
        out_specs=pl.BlockSpec((tt, nb, c), lambda i: (i, 0, 0)),
        compiler_params=_cparams(("parallel",)),
        name=name,
    )(x3, g, w, perm)


def _merge_kernel(gates_ref, yr_ref, ys_ref, ym_ref, h_ref, wr_ref, wglu_ref, ws_ref, wm_ref, wo_ref, g_ref,
                  *rest, ys_tmajor):
    o_ref = rest[-1]
    rows2d = lambda ref: ref[...].reshape(-1, ref.shape[-1])
    o_r = _mm(rows2d(yr_ref), wr_ref[...])
    ys = rows2d(ys_ref)
    y3 = ys * _sigmoid(_mm(ys, wglu_ref[...]))
    if ys_tmajor:
        y3 = jnp.dot(rest[0][...], y3.astype(_MXU), preferred_element_type=F32)
    o_s = _mm(y3, ws_ref[...])
    o_m = _mm(rows2d(ym_ref), wm_ref[...])
    gates = rows2d(gates_ref)
    merged = (_sigmoid(gates[:, 0:D]) * o_r + _sigmoid(gates[:, D:2 * D]) * o_s
              + _sigmoid(gates[:, 2 * D:3 * D]) * o_m)
    mix = _mm(merged, wo_ref[...])
    o_ref[...] = (rows2d(h_ref) + _rms(mix, g_ref[...])).reshape(o_ref.shape)


def mixer_merge(gates, yr, ys, ym, h, wr, wglu, ws, wm, wo, g, *, tm=256, seq=None):
    n = h.shape[0]
    weights = [_const_spec((D, D)), _const_spec((D, D)), _const_spec((D, D)),
               _const_spec((M_WIDTH, D)), _const_spec((D, D)), _const_spec((1, D))]
    if seq is None:
        tm = _row_tile(n, tm)
        row = lambda w: pl.BlockSpec((tm, w), lambda i: (i, 0))
        in_specs = [row(3 * D), row(D), row(D), row(M_WIDTH), row(D)] + weights
        args = [gates, yr, ys, ym, h, wr, wglu, ws, wm, wo, g]
        out_spec, out_shape, grid = row(D), (n, D), (n // tm,)
    else:
        nb, t = seq
        tt = tm // nb
        assert t % tt == 0 and tt % 8 == 0
        row = lambda w: pl.BlockSpec((nb, tt, w), lambda i: (0, i, 0))
        v3 = lambda x: x.reshape(nb, t, x.shape[-1])
        in_specs = ([row(3 * D), row(D), pl.BlockSpec((tt, nb, D), lambda i: (i, 0, 0)), row(M_WIDTH), row(D)]
                    + weights + [_const_spec((nb * tt, nb * tt))])
        args = [v3(gates), v3(yr), ys, v3(ym), v3(h), wr, wglu, ws, wm, wo, g,
                jnp.asarray(_time_major_perm(nb, tt).T, _MXU)]
        out_spec, out_shape, grid = row(D), (nb, t, D), (t // tt,)
    out = pl.pallas_call(
        functools.partial(_merge_kernel, ys_tmajor=seq is not None),
        out_shape=jax.ShapeDtypeStruct(out_shape, F32),
        grid=grid,
        in_specs=in_specs,
        out_specs=out_spec,
        compiler_params=_cparams(("parallel",)),
        name="mixer_merge",
    )(*args)
    return out.reshape(n, D)


def _rwkv_rec_kernel(pr_ref, pk_ref, pv_ref, pl_ref, shr_ref, shk_ref, shv_ref, shl_ref,
                     mur_ref, muk_ref, muv_ref, mul_ref, w2_ref, a2_ref, g2_ref, w0_ref, a0_ref, kk_ref, ka_ref,
                     s0_ref, rk_ref, lnw_ref, lnb_ref,
                     y_ref, sfin_ref, s_scr, carry_scr, *, C, G, NP, n_steps, t_real):
    step = pl.program_id(2)
    C2 = 2 * C
    W = NP * R_PAIR
    probs = [(gi, pp) for gi in range(G) for pp in range(NP)]
    rng = range(len(probs))
    row_c = _iota((C, 1), 0)

    @pl.when(step == 0)
    def _():
        for j, sh_ref in enumerate((shr_ref, shk_ref, shv_ref, shl_ref)):
            carry_scr[j] = sh_ref[...]

    def shift_mix(j, p_ref, mu_ref):
        out = []
        for gi in range(G):
            x = p_ref[gi]
            prev = jnp.where(row_c == 0, carry_scr[j, gi:gi + 1, :], pltpu.roll(x, 1, axis=0))
            carry_scr[j, gi:gi + 1, :] = x[C - 1:C, :]
            out.append(x + mu_ref[...] * (prev - x))
        return out

    pm_r = shift_mix(0, pr_ref, mur_ref)
    pm_k = shift_mix(1, pk_ref, muk_ref)
    pm_v = shift_mix(2, pv_ref, muv_ref)
    xl = jnp.concatenate(shift_mix(3, pl_ref, mul_ref), axis=0)
    x_wa = xl[:, 0:128]
    lora_in = jnp.where(_iota(x_wa.shape, 1) < 64, jnp.tanh(x_wa), x_wa)
    w_log = -_softplus(-(w0_ref[...] + _mm(lora_in, w2_ref[...]))) - 0.5
    e_all = jnp.exp(w_log)
    a_all = _sigmoid(a0_ref[...] + _mm(lora_in, a2_ref[...]))
    g_all = _mm(_sigmoid(xl[:, 128:256]), g2_ref[...])
    valid = row_c < t_real

    def seq_rows(x, gi, keep_pad=False):
        x = x[gi * C:(gi + 1) * C]
        return x if (keep_pad or t_real >= C) else jnp.where(valid, x, 0.0)

    def pad0(x):
        return x if t_real >= C else jnp.where(valid, x, 0.0)

    lanes = lambda x, pp: x[:, pp * R_PAIR:(pp + 1) * R_PAIR]
    r, e, k, v, kk, a_l, g_l = [], [], [], [], [], [], []
    for gi, pp in probs:
        a_q = lanes(seq_rows(a_all, gi, True), pp)
        k_raw = lanes(pad0(pm_k[gi]), pp)
        r.append(lanes(pad0(pm_r[gi]), pp))
        e.append(lanes(seq_rows(e_all, gi), pp))
        k.append(k_raw * (1.0 + (a_q - 1.0) * lanes(ka_ref[...], pp)))
        v.append(lanes(pad0(pm_v[gi]), pp))
        kk.append(k_raw * lanes(kk_ref[...], pp))
        a_l.append(a_q)
        g_l.append(lanes(seq_rows(g_all, gi, True), pp))

    def par(ref, q):
        pp = probs[q][1]
        return ref[:, pp * R_PAIR:(pp + 1) * R_PAIR]

    lane = _iota((1, R_PAIR), 1)
    m0 = lane < R_HEAD
    ri = _iota((R_PAIR, R_PAIR), 0)
    ci = _iota((R_PAIR, R_PAIR), 1)
    bd = (ri < R_HEAD) == (ci < R_HEAD)
    rt_i = _iota((C2, C2), 0)
    ct_i = _iota((C2, C2), 1)
    same = (rt_i < C) == (ct_i < C)
    tr = jnp.where(rt_i < C, rt_i, rt_i - C)
    ts = jnp.where(ct_i < C, ct_i, ct_i - C)
    m_sl = same & (ts < tr)
    m_li = same & (ts <= tr)
    eye2 = jnp.where(rt_i == ct_i, 1.0, 0.0)
    lvl_masks = []
    m = 1
    while m < C:
        lvl_masks.append(((tr & ~(2 * m - 1)) == (ts & ~(2 * m - 1))) & ((tr & m) != 0) & ((ts & m) == 0))
        m *= 2
    tril = jnp.where(_iota((C, C), 1) <= _iota((C, C), 0), 1.0, 0.0).astype(_MXU)

    @pl.when(step == 0)
    def _():
        z = jnp.zeros((R_HEAD, R_HEAD), F32)
        for q, (gi, pp) in enumerate(probs):
            s_scr[q] = jnp.concatenate([jnp.concatenate([s0_ref[gi, 2 * pp], z], axis=1),
                                        jnp.concatenate([z, s0_ref[gi, 2 * pp + 1]], axis=1)], axis=0)

    def stack(x):
        return jnp.concatenate([jnp.where(m0, x, 0.0), jnp.where(m0, 0.0, x)], axis=0)

    def fold(xs):
        return xs[:C] + xs[C:]

    def each(f, *lists):
        return [f(*xs) for xs in zip(*lists)]

    def head_sum(x):
        s0 = jnp.sum(jnp.where(m0, x, 0.0), axis=-1, keepdims=True)
        s1 = jnp.sum(jnp.where(m0, 0.0, x), axis=-1, keepdims=True)
        return jnp.where(m0, s0, s1)

    an = each(lambda x: -x * lax.rsqrt(head_sum(x * x) + 1e-12), kk)
    bn = each(lambda n_, a_: -n_ * a_, an, a_l)
    cs = each(lambda x: _mm_sel_l(tril, x), e)
    cl = each(lambda x: x[C - 1:C, :], cs)
    at = each(lambda a_, e_, c_: a_ * jnp.exp(e_ - c_), an, e, cs)
    rt = each(lambda r_, c_: r_ * jnp.exp(-c_), r, cs)
    ecs = each(jnp.exp, cs)
    bt_ = each(lambda b_, x: b_ * x, bn, ecs)
    kt = each(lambda k_, x: k_ * x, k, ecs)
    ecl = each(lambda c_, l_: jnp.exp(c_ - l_), cs, cl)
    bh = each(lambda b_, x: b_ * x, bn, ecl)
    kh = each(lambda k_, x: k_ * x, k, ecl)
    As, Rs, Vs = each(stack, at), each(stack, rt), each(stack, v)
    b2 = each(lambda x: jnp.concatenate([x, x], axis=0), bt_)
    k2 = each(lambda x: jnp.concatenate([x, x], axis=0), kt)
    N = each(lambda a_, b_: jnp.where(m_sl, _mm_nt(a_, b_), 0.0), As, b2)
    Ak = each(lambda a_, b_: jnp.where(m_sl, _mm_nt(a_, b_), 0.0), As, k2)
    Arb = each(lambda a_, b_: jnp.where(m_li, _mm_nt(a_, b_), 0.0), Rs, b2)
    Ark = each(lambda a_, b_: jnp.where(m_li, _mm_nt(a_, b_), 0.0), Rs, k2)
    T = each(lambda n_: eye2 + jnp.where(lvl_masks[0], n_, 0.0), N)
    for lm in lvl_masks[1:]:
        LT = each(lambda n_, t_: _mm(jnp.where(lm, n_, 0.0), t_), N, T)
        T = each(lambda t_, x: t_ + _mm(t_, x), T, LT)
    AkV = each(_mm, Ak, Vs)
    X = each(lambda t_, a_, u_: _mm(t_, jnp.concatenate([a_, u_], axis=1)), T, As, AkV)
    Z = each(_mm, Arb, X)
    ArkV = each(_mm, Ark, Vs)
    Rp = each(lambda r_, z_: r_ + fold(z_[:, 0:R_PAIR]), rt, Z)
    Y0 = each(lambda z_, a_: fold(z_[:, R_PAIR:] + a_), Z, ArkV)
    Ap = each(lambda x: fold(x[:, 0:R_PAIR]), X)
    U0 = each(lambda x: fold(x[:, R_PAIR:]), X)
    P = each(lambda a_, b_: jnp.where(bd, _mm_tn(a_, b_), 0.0), Ap, bh)
    Q = each(lambda u_, v_, b_, k_: jnp.where(bd, _mm_tn(jnp.concatenate([u_, v_], axis=0),
                                                         jnp.concatenate([b_, k_], axis=0)), 0.0), U0, v, bh, kh)
    S = [s_scr[q] for q in rng]
    y = each(lambda y_, r_, s_: y_ + _mm_nt(r_, s_), Y0, Rp, S)
    S = each(lambda s_, l_, p_, q_: s_ * jnp.exp(-l_) + _mm(s_, p_) + q_, S, cl, P, Q)
    for q in rng:
        s_scr[q] = S[q]
    dlt = each(lambda y_: y_ - head_sum(y_) * (1.0 / R_HEAD), y)
    var = each(lambda d_: head_sum(d_ * d_) * (1.0 / R_HEAD), dlt)
    bonus = each(lambda r_, k_, v_, q: head_sum(r_ * k_ * par(rk_ref, q)) * v_, r, k, v, rng)
    for q, (gi, pp) in enumerate(probs):
        yn = dlt[q] * lax.rsqrt(var[q] + R_LN_EPS) * par(lnw_ref, q) + par(lnb_ref, q)
        y_ref[gi, :, pp * R_PAIR:(pp + 1) * R_PAIR] = (yn + bonus[q]) * g_l[q]

    @pl.when(step == n_steps - 1)
    def _():
        for q, (gi, pp) in enumerate(probs):
            sfin_ref[gi, 2 * pp] = S[q][0:R_HEAD, 0:R_HEAD]
            sfin_ref[gi, 2 * pp + 1] = S[q][R_HEAD:, R_HEAD:]


def _stacked_out(kernel_fn, in_specs, args, out_index, stack):
    pos = len(args)

    def with_alias(*refs):
        return kernel_fn(*refs[:pos], *refs[pos + 1:])

    return with_alias, in_specs + [pl.BlockSpec(memory_space=pl.ANY)], args + [stack], {pos: out_index}


def rwkv_recurrence(p3, shift_prev, mu, wa, g2, w0, a0, k_k, k_a, s0, l, r_k, ln_w, ln_b, stack, *, C, G, NP,
                    t_real):
    B, T, _ = p3.shape
    n_steps = T // C
    n_pairs = D // R_PAIR
    W = NP * R_PAIR
    assert W == 256 and t_real <= C and (t_real == C or n_steps == 1)
    seg = D // W
    cblk = lambda off: pl.BlockSpec((G, C, W), lambda b, p, s: (b, s, off + p))
    sblk = lambda off: pl.BlockSpec((G, W), lambda b, p, s: (b, off + p))
    mblk = lambda off: pl.BlockSpec((1, W), lambda b, p, s: (0, off + p))
    lora_c = pl.BlockSpec((G, C, W), lambda b, p, s: (b, s, 3 * seg))
    lora_s = pl.BlockSpec((G, W), lambda b, p, s: (b, 3 * seg))
    lora_m = pl.BlockSpec((1, W), lambda b, p, s: (0, 3 * seg))
    wblk = lambda off: pl.BlockSpec((128, W), lambda b, p, s: (0, off + p))
    par = lambda: pl.BlockSpec((1, W), lambda b, p, s: (0, p))
    out_blk = pl.BlockSpec((G, C, W), lambda b, p, s: (b, s, p))
    st_spec = lambda: pl.BlockSpec((None, G, 2 * NP, R_HEAD, R_HEAD), lambda b, p, s: (l, b, p, 0, 0))
    in_specs = ([cblk(0), cblk(seg), cblk(2 * seg), lora_c,
                 sblk(0), sblk(seg), sblk(2 * seg), lora_s,
                 mblk(0), mblk(seg), mblk(2 * seg), lora_m,
                 wblk(0), wblk(seg), wblk(0)]
                + [par() for _ in range(4)] + [st_spec(), par(), par(), par()])
    kern, in_specs, args, aliases = _stacked_out(
        functools.partial(_rwkv_rec_kernel, C=C, G=G, NP=NP, n_steps=n_steps, t_real=t_real), in_specs,
        [p3, p3, p3, p3, shift_prev, shift_prev, shift_prev, shift_prev, mu, mu, mu, mu, wa, wa, g2,
         w0, a0, k_k, k_a, s0, r_k, ln_w, ln_b], 1, stack)
    y, sfin = pl.pallas_call(
        kern,
        out_shape=[jax.ShapeDtypeStruct((B, T, D), F32),
                   jax.ShapeDtypeStruct((s0.shape[0], B, 2 * n_pairs, R_HEAD, R_HEAD), F32)],
        grid=(B // G, n_pairs // NP, n_steps),
        in_specs=in_specs,
        out_specs=[out_blk, st_spec()],
        scratch_shapes=[pltpu.VMEM((G * NP, R_PAIR, R_PAIR), F32), pltpu.VMEM((4, G, W), F32)],
        input_output_aliases=aliases,
        compiler_params=_cparams(("parallel", "parallel", "arbitrary")),
        name="rwkv_rec",
    )(*args)
    return y, sfin


def _s5_kernel(u_ref, h0r_ref, h0i_ref, ar_ref, ai_ref, bre_ref, bim_ref, cre_ref, cim_ref, d_ref, *rest,
               tc, nb, kb, n_steps, permute):
    if permute:
        perm_ref, permt_ref = rest[:2]
        rest = rest[2:]
    y_ref, hr_out, hi_out, inr_scr, ini_scr, hr_scr, hi_scr = rest
    step = pl.program_id(1)
    rows = nb * tc

    @pl.when(step == 0)
    def _():
        hr_scr[...] = h0r_ref[...]
        hi_scr[...] = h0i_ref[...]

    u = u_ref[...].reshape(rows, kb * S5_BLK_CH)
    ch = lambda j: slice(j * S5_BLK_CH, (j + 1) * S5_BLK_CH)
    stt = lambda j: slice(j * S5_BLK_ST, (j + 1) * S5_BLK_ST)
    if permute:
        ut = jnp.dot(perm_ref[...], u.astype(_MXU), preferred_element_type=F32).astype(_MXU)
    else:
        ut = u.astype(_MXU)
    for j in range(kb):
        inr_scr[:, stt(j)] = jnp.dot(ut[:, ch(j)], bre_ref[j], preferred_element_type=F32)
        ini_scr[:, stt(j)] = jnp.dot(ut[:, ch(j)], bim_ref[j], preferred_element_type=F32)
    ar = ar_ref[...]
    ai = ai_ref[...]

    def body(t, carry):
        hr, hi = carry
        sl = pl.ds(pl.multiple_of(t * nb, nb), nb)
        nr = ar * hr - ai * hi + inr_scr[sl, :]
        ni = ar * hi + ai * hr + ini_scr[sl, :]
        inr_scr[sl, :] = nr
        ini_scr[sl, :] = ni
        return nr, ni

    hr, hi = lax.fori_loop(0, tc, body, (hr_scr[...], hi_scr[...]))
    hr_scr[...] = hr
    hi_scr[...] = hi
    yts = [jnp.dot(inr_scr[:, stt(j)].astype(_MXU), cre_ref[j], preferred_element_type=F32)
           - jnp.dot(ini_scr[:, stt(j)].astype(_MXU), cim_ref[j], preferred_element_type=F32) for j in range(kb)]
    yt = yts[0] if kb == 1 else jnp.concatenate(yts, axis=1)
    if permute:
        yt_hi = yt.astype(_MXU)
        yt_lo = (yt - yt_hi.astype(F32)).astype(_MXU)
        yt = (jnp.dot(permt_ref[...], yt_hi, preferred_element_type=F32)
              + jnp.dot(permt_ref[...], yt_lo, preferred_element_type=F32))
    y = yt + d_ref[...] * u
    y = 0.5 * y * (1.0 + jnp.tanh(math.sqrt(2.0 / math.pi) * (y + 0.044715 * (y * y * y))))
    y_ref[...] = y.reshape(y_ref.shape)

    @pl.when(step == n_steps - 1)
    def _():
        hr_out[...] = hr
        hi_out[...] = hi


def _time_major_perm(nb, tc):
    rows = nb * tc
    j = np.arange(rows)
    perm_np = np.zeros((rows, rows), np.float32)
    perm_np[(j % tc) * nb + j // tc, j] = 1.0
    return perm_np


def s5_scan(u3, h0r, h0i, l, ar, ai, bre, bim, cre, cim, d, *, nb, tc, blk, kb):
    A, R, _ = u3.shape
    ga, tr = blk
    assert ga * tr == nb * tc
    tmajor = (ga, tr) == (tc, nb) and R == nb
    nblk = D // S5_BLK_CH
    n_steps = A // ga if tmajor else R // tr
    rows = nb * tc
    u_spec = (pl.BlockSpec((ga, tr, kb * S5_BLK_CH), lambda c, s: (s, 0, c)) if tmajor
              else pl.BlockSpec((ga, tr, kb * S5_BLK_CH), lambda c, s: (0, s, c)))
    if tmajor:
        perm_specs, perm_args = [], []
    else:
        assert A == ga
        perm_np = _time_major_perm(nb, tc)
        perm_specs = [_const_spec((rows, rows)), _const_spec((rows, rows))]
        perm_args = [jnp.asarray(perm_np, _MXU), jnp.asarray(perm_np.T, _MXU)]
    st = lambda: pl.BlockSpec((None, nb, kb * S5_BLK_ST), lambda c, s: (l, 0, c))
    vec = lambda w: pl.BlockSpec((1, kb * w), lambda c, s: (0, c))
    mat = lambda a, b: pl.BlockSpec((kb, a, b), lambda c, s: (c, 0, 0))
    return pl.pallas_call(
        functools.partial(_s5_kernel, tc=tc, nb=nb, kb=kb, n_steps=n_steps, permute=not tmajor),
        out_shape=[jax.ShapeDtypeStruct(u3.shape, F32),
                   jax.ShapeDtypeStruct((nb, nblk * S5_BLK_ST), F32),
                   jax.ShapeDtypeStruct((nb, nblk * S5_BLK_ST), F32)],
        grid=(nblk // kb, n_steps),
        in_specs=[u_spec,
                  st(), st(), vec(S5_BLK_ST), vec(S5_BLK_ST),
                  mat(S5_BLK_CH, S5_BLK_ST), mat(S5_BLK_CH, S5_BLK_ST),
                  mat(S5_BLK_ST, S5_BLK_CH), mat(S5_BLK_ST, S5_BLK_CH),
                  vec(S5_BLK_CH)] + perm_specs,
        out_specs=[u_spec,
                   pl.BlockSpec((nb, kb * S5_BLK_ST), lambda c, s: (0, c)),
                   pl.BlockSpec((nb, kb * S5_BLK_ST), lambda c, s: (0, c))],
        scratch_shapes=[pltpu.VMEM((rows, kb * S5_BLK_ST), F32), pltpu.VMEM((rows, kb * S5_BLK_ST), F32),
                        pltpu.VMEM((nb, kb * S5_BLK_ST), F32), pltpu.VMEM((nb, kb * S5_BLK_ST), F32)],
        compiler_params=_cparams(("parallel", "arbitrary")),
        name="s5_scan",
    )(u3, h0r, h0i, ar, ai, bre, bim, cre, cim, d, *perm_args)


def _s5_discretize(a_re, a_im, log_dt, b_re, b_im, c_re, c_im):
    g, p, hch = b_re.shape
    dt = jnp.exp(log_dt.astype(F32))[:, None]
    mag = jnp.exp(dt * a_re)
    abar_re = mag * jnp.cos(dt * a_im)
    abar_im = mag * jnp.sin(dt * a_im)
    den = a_re * a_re + a_im * a_im
    nr = abar_re - 1.0
    q_re = (nr * a_re + abar_im * a_im) / den
    q_im = (abar_im * a_re - nr * a_im) / den
    bb_re = q_re[..., None] * b_re - q_im[..., None] * b_im
    bb_im = q_re[..., None] * b_im + q_im[..., None] * b_re
    nblk = D // S5_BLK_CH
    gl = g // nblk
    eye = jnp.eye(gl, dtype=F32)

    def in_blocks(bb):
        t = jnp.transpose(bb, (0, 2, 1)).reshape(nblk, gl, hch, p)
        return jnp.einsum('cghp,gk->cghkp', t, eye).reshape(nblk, gl * hch, gl * p).astype(_MXU)

    def out_blocks(cc):
        t = jnp.transpose(cc, (0, 2, 1)).reshape(nblk, gl, p, hch)
        return jnp.einsum('cgph,gk->cgpkh', t, eye).reshape(nblk, gl * p, gl * hch).astype(_MXU)

    return (abar_re.reshape(1, g * p), abar_im.reshape(1, g * p), in_blocks(bb_re), in_blocks(bb_im),
            out_blocks(c_re), out_blocks(c_im))


def _ssd_kernel(xbc_ref, z_ref, dt_ref, cprev_ref, h0_ref, cw_ref, cb_ref, dtb_ref, alog_ref, dx_ref, ng_ref,
                eh_ref, y_ref, hfin_ref, ext_scr, h_scr, *, L, G, t_real, n_steps):
    step = pl.program_id(1)
    rng = range(G)
    NB = M_GROUPS * M_STATE

    def each(f, *lists):
        return [f(*xs) for xs in zip(*lists)]

    @pl.when(step == 0)
    def _():
        for s in rng:
            ext_scr[s] = jnp.zeros((8, M_CONV_DIM), F32)
            ext_scr[s, 5:8, :] = cprev_ref[s]
            for gi in range(M_GROUPS):
                h_scr[s, gi] = h0_ref[s, gi * M_HPG:(gi + 1) * M_HPG].reshape(M_GW, M_STATE).T

    tril = jnp.where(_iota((L, L), 1) <= _iota((L, L), 0), 1.0, 0.0).astype(_MXU)
    causal = _iota((L, L), 1) <= _iota((L, L), 0)
    eye_h = _eye(128, _MXU)
    eh = eh_ref[...]
    lane_head = _iota((1, M_GW), 1) // M_HEAD
    row8 = _iota((8, 1), 0)
    a = -jnp.exp(alog_ref[...])
    conv, dts = [], []
    for s in rng:
        x = xbc_ref[s * L:(s + 1) * L, :]
        prev8 = ext_scr[s]
        c = cb_ref[...] + cw_ref[3:4, :] * x
        for j in range(1, 4):
            xr = pltpu.roll(x, j, axis=0)
            head = jnp.where(row8 < j, pltpu.roll(prev8, j, axis=0), xr[0:8])
            xr = head if L == 8 else jnp.concatenate([head, xr[8:]], axis=0)
            c = c + cw_ref[3 - j:4 - j, :] * xr
        ext_scr[s] = x[L - 8:L]
        conv.append(c * _sigmoid(c))
        dt = _softplus(dt_ref[s * L:(s + 1) * L, :] + dtb_ref[...])
        if t_real < L:
            dt = jnp.where(_iota(dt.shape, 0) < t_real, dt, 0.0)
        dts.append(dt)
    xs = each(lambda c: c[:, 0:M_WIDTH], conv)
    acum = each(lambda d_: _mm_sel_l(tril, d_ * a), dts)
    acum_t = each(lambda x: _mm_nt_sel_l(eye_h, x), acum)
    dt_x = each(lambda d_: _mm_sel_r(d_, eh), dts)
    acum_x = each(lambda x: _mm_sel_r(x, eh), acum)
    acl_x = each(lambda x: x[L - 1:L, :], acum_x)
    xd = each(lambda x, d_: x * d_, xs, dt_x)
    xdd = each(lambda x, l_, c_: x * jnp.exp(l_ - c_), xd, acl_x, acum_x)
    eacum_x = each(jnp.exp, acum_x)
    cdec_x = each(jnp.exp, acl_x)
    ys = [[] for _ in rng]
    for gi in range(M_GROUPS):
        gs = slice(gi * M_GW, (gi + 1) * M_GW)
        bg = each(lambda c: c[:, M_WIDTH + gi * M_STATE:M_WIDTH + (gi + 1) * M_STATE], conv)
        cg = each(lambda c: c[:, M_WIDTH + NB + gi * M_STATE:M_WIDTH + NB + (gi + 1) * M_STATE], conv)
        h = [h_scr[s, gi] for s in rng]
        cbm = each(_mm_nt, cg, bg)
        bgt = each(lambda b_: _mm_nt(eye_h, b_), bg)
        y_off = each(lambda c_, h_, e_: _mm(c_, h_) * e_[:, gs], cg, h, eacum_x)
        ms = []
        for e in range(M_HPG):
            he = gi * M_HPG + e
            ms.append(each(lambda c_, a_, t_: c_ * jnp.exp(jnp.where(causal, a_[:, he:he + 1] - t_[he:he + 1, :],
                                                                     -jnp.inf)), cbm, acum, acum_t))
        if L % 128 == 0:
            mcat = [jnp.concatenate([ms[e][s] for e in range(M_HPG)], axis=1) for s in rng]
            xst = each(lambda x: jnp.concatenate([jnp.where(lane_head == e, x[:, gs], 0.0).astype(_MXU)
                                                  for e in range(M_HPG)], axis=0), xd)
            y_dg = each(_mm, mcat, xst)
        else:
            y_dg = [sum(_mm(ms[e][s], jnp.where(lane_head == e, xd[s][:, gs], 0.0)) for e in range(M_HPG))
                    for s in rng]
        hn = each(lambda h_, d_, b_, x: h_ * d_[:, gs] + _mm(b_, x[:, gs]), h, cdec_x, bgt, xdd)
        for s in rng:
            h_scr[s, gi] = hn[s]
            ys[s].append(y_off[s] + y_dg[s])
    for s in rng:
        y = jnp.concatenate(ys[s], axis=1) + dx_ref[...] * xs[s]
        zz = z_ref[s * L:(s + 1) * L, :]
        y = y * (zz * _sigmoid(zz))
        outs = []
        for gi in range(M_GROUPS):
            yg = y[:, gi * M_GW:(gi + 1) * M_GW]
            outs.append(yg * lax.rsqrt(jnp.mean(yg * yg, axis=-1, keepdims=True) + EPS))
        y_ref[s * L:(s + 1) * L, :] = jnp.concatenate(outs, axis=1) * ng_ref[...]

    @pl.when(step == n_steps - 1)
    def _():
        for s in rng:
            for gi in range(M_GROUPS):
                hfin_ref[s, gi * M_HPG:(gi + 1) * M_HPG] = h_scr[s, gi].T.reshape(M_HPG, M_HEAD, M_STATE)


def ssd_block(xbc, z, dt, conv_prev, h0, l, conv_w, conv_b, dt_bias, a_log, d_x, norm_g, eh, stack, *, B, T, L, G,
              t_real):
    n_steps = T // L
    assert G == 1 or n_steps == 1
    heads = M_GROUPS * M_HPG
    seq = lambda w: pl.BlockSpec((G * L, w), lambda b, s: (b * n_steps + s, 0))
    st_spec = lambda: pl.BlockSpec((None, G, heads, M_HEAD, M_STATE), lambda b, s: (l, b, 0, 0, 0))
    in_specs = [seq(M_CONV_DIM), seq(M_WIDTH), seq(128),
                pl.BlockSpec((None, G, 3, M_CONV_DIM), lambda b, s: (l, b, 0, 0)),
                st_spec(),
                _const_spec((4, M_CONV_DIM)), _const_spec((1, M_CONV_DIM)),
                _const_spec((1, 128)), _const_spec((1, 128)),
                _const_spec((1, M_WIDTH)), _const_spec((1, M_WIDTH)),
                _const_spec((128, M_WIDTH))]
    kern, in_specs, args, aliases = _stacked_out(
        functools.partial(_ssd_kernel, L=L, G=G, t_real=t_real, n_steps=n_steps), in_specs,
        [xbc, z, dt, conv_prev, h0, conv_w, conv_b, dt_bias, a_log, d_x, norm_g, eh], 1, stack)
    return pl.pallas_call(
        kern,
        out_shape=[jax.ShapeDtypeStruct((B * T, M_WIDTH), F32),
                   jax.ShapeDtypeStruct((h0.shape[0], B, heads, M_HEAD, M_STATE), F32)],
        grid=(B // G, n_steps),
        in_specs=in_specs,
        out_specs=[seq(M_WIDTH), st_spec()],
        scratch_shapes=[pltpu.VMEM((G, 8, M_CONV_DIM), F32),
                        pltpu.VMEM((G, M_GROUPS, M_STATE, M_GW), F32)],
        input_output_aliases=aliases,
        compiler_params=_cparams(("parallel", "arbitrary")),
        name="ssd",
    )(*args)


def _attn_kernel(q_ref, k_ref, v_ref, o_ref, *, nb, tq, heads_split):
    scale = XA_HD ** -0.5
    for j in range(nb):
        if heads_split:
            k_all = pltpu.einshape("mhd->hmd", k_ref[j])
            v_all = pltpu.einshape("mhd->hmd", v_ref[j])
        outs = []
        for hd in range(XA_HEADS):
            cs = slice(hd * XA_HD, (hd + 1) * XA_HD)
            q = q_ref[j * tq:(j + 1) * tq, cs]
            kh = k_all[hd] if heads_split else k_ref[j, :, cs]
            vh = v_all[hd] if heads_split else v_ref[j, :, cs]
            s = _mm_nt(q, kh) * scale
            s = s - jnp.max(s, axis=-1, keepdims=True)
            p = jnp.exp(s)
            p = p / jnp.sum(p, axis=-1, keepdims=True)
            outs.append(_mm(p, vh))
        o_ref[j * tq:(j + 1) * tq, :] = jnp.concatenate(outs, axis=1)


def cross_attention(q, mk, mv, kv_index, *, B, T, tq, nb):
    nlead = len(kv_index)
    heads_split = mk.ndim - nlead == 4
    kv_blk = mk.shape[nlead + 1:]
    n_t = T // tq
    kv_spec = pl.BlockSpec((None,) * nlead + (nb,) + kv_blk,
                           lambda b, s: tuple(kv_index) + (b,) + (0,) * len(kv_blk))
    q_spec = pl.BlockSpec((nb * tq, D), lambda b, s: (b * n_t + s, 0))
    return pl.pallas_call(
        functools.partial(_attn_kernel, nb=nb, tq=tq, heads_split=heads_split),
        out_shape=jax.ShapeDtypeStruct((B * T, D), F32),
        grid=(B // nb, n_t),
        in_specs=[q_spec, kv_spec, kv_spec],
        out_specs=q_spec,
        compiler_params=_cparams(("parallel", "arbitrary")),
        name="xattn",
    )(q, mk, mv)


def _pad_time(x2, B, T, Tp):
    if Tp == T:
        return x2
    w = x2.shape[-1]
    return jnp.pad(x2.reshape(B, T, w), ((0, 0), (0, Tp - T), (0, 0))).reshape(B * Tp, w)


def _layer(l, h, grp, W, st, acc):
    T, B = grp['T'], grp['B']
    n = T * B
    gates = norm_matmul(h, W['g_mix_pre'], W['w_gates'], name="in_gates")
    p_r = norm_matmul(h, W['g_mix_pre'], W['w_rwkv'], name="in_rwkv")
    if grp['s5_tmajor']:
        u3 = norm_matmul_tmajor(h.reshape(B, T, D), W['g_mix_pre'], W['w_s5'], name="in_s5")
    else:
        u3 = norm_matmul(h, W['g_mix_pre'], W['w_s5'], name="in_s5").reshape(1, n, D)
    z_m =norm_matmul(h, W['g_mix_pre'], W['w_z'], name="in_z")
    xbc = norm_matmul(h, W['g_mix_pre'], W['w_xbc'], name="in_xbc")
    dt_m = norm_matmul(h, W['g_mix_pre'], W['w_dt'], tn=128, name="in_dt")

    Tp = grp['rwkv_Tpad']
    p3 = _pad_time(p_r, B, T, Tp).reshape(B, Tp, p_r.shape[-1])
    yr, s_fin = rwkv_recurrence(p3, st['shift'][l], W['mu'], W['wa'], W['g2'], W['w0'], W['a0'], W['k_k'],
                                W['k_a'], st['rwkv'], l, W['r_k'], W['ln_w'], W['ln_b'], acc[0],
                                C=grp['rwkv_C'], G=grp['rwkv_G'], NP=grp['rwkv_NP'], t_real=min(T, grp['rwkv_C']))
    yr = yr[:, :T].reshape(n, D)
    shift_new = p_r.reshape(B, T, -1)[:, T - 1]

    ys, s5r, s5i = s5_scan(u3, st['s5r'], st['s5i'], l, W['s5_ar'], W['s5_ai'], W['s5_bre'], W['s5_bim'],
                           W['s5_cre'], W['s5_cim'], W['s5_d'], nb=B, tc=grp['s5_tc'], blk=grp['s5_blk'],
                           kb=2 if B <= 16 else 1)

    Lc = grp['ssd_L']
    Tm = grp['ssd_Tpad']
    ym, ssm_fin = ssd_block(_pad_time(xbc, B, T, Tm), _pad_time(z_m, B, T, Tm), _pad_time(dt_m, B, T, Tm),
                            st['conv'], st['ssm'], l, W['conv_w'], W['conv_b'], W['dt_bias'],
                            W['a_log'], W['d_x'], W['m_norm'], W['eh'], acc[1], B=B, T=Tm, L=Lc, G=grp['ssd_G'],
                            t_real=min(T, Lc))
    ym = ym.reshape(B, Tm, M_WIDTH)[:, :T].reshape(n, M_WIDTH)
    conv_new = jnp.concatenate([st['conv'][l], xbc.reshape(B, T, M_CONV_DIM)[:, max(T - 3, 0):]], axis=1)[:, -3:]

    h = mixer_merge(gates, yr, ys if grp['s5_tmajor'] else ys.reshape(n, D), ym, h, W['w_out_rwkv'],
                    W['s5_w_glu'], W['w_out_s5'], W['w_out_mamba'], W['w_out'], W['g_mix_post'],
                    seq=(B, T) if grp['s5_tmajor'] else None)

    q = norm_matmul(h, W['g_xa_pre'], W['xa_wq'], name="xa_q")
    o = cross_attention(q, st['mk'], st['mv'], st['kv_index'](l), B=B, T=T, tq=grp['xa_tq'], nb=grp['xa_nb'])
    h = matmul_norm_residual(o, W['xa_wo'], W['g_xa_post'], h, name="xa_out")

    h = mlp_block(h, W['g_mlp_pre'], W['mlp_w1'], W['mlp_w2'], W['g_mlp_post'])
    return h, (s_fin, shift_new, s5r.reshape(B, 64, 64), s5i.reshape(B, 64, 64), conv_new, ssm_fin)


def _group_cfg(T, B):
    cfg = dict(T=T, B=B)
    g = 8 if B % 8 == 0 else 1
    if T % 64 == 0:
        cfg.update(rwkv_C=64, rwkv_Tpad=T, rwkv_G=g, rwkv_NP=2)
    else:
        tp = -(-T // 8) * 8
        cfg.update(rwkv_C=tp, rwkv_Tpad=tp, rwkv_G=16 if B % 16 == 0 else g, rwkv_NP=2)
    if T % 8 == 0 and (512 // B) >= 8 and T % (512 // B) == 0:
        tc = 512 // B
        cfg.update(s5_tmajor=True, s5_tc=tc, s5_blk=(tc, B))
    else:
        cfg.update(s5_tmajor=False, s5_tc=T, s5_blk=(1, B * T))
    if T % 128 == 0:
        cfg.update(ssd_L=128, ssd_Tpad=T, ssd_G=1)
    else:
        tp = -(-T // 8) * 8
        cfg.update(ssd_L=tp, ssd_Tpad=tp, ssd_G=4 if B % 4 == 0 else 1)
    if T >= 64:
        cfg.update(xa_tq=_row_tile(T, 512), xa_nb=1)
    else:
        cfg.update(xa_tq=T, xa_nb=4 if B % 4 == 0 else 1)
    return cfg


def kernel(x_prompt, x_sample, cache_mem_k, cache_mem_v, state_rwkv, state_rwkv_shift, state_s5_re, state_s5_im, state_conv, state_ssm, mem_prompt, norm_mix_pre, norm_mix_post, norm_xa_pre, norm_xa_post, norm_mlp_pre, norm_mlp_post, norm_mem, w_in, w_out, rwkv_mu, rwkv_w0, rwkv_w2, rwkv_a0, rwkv_a2, rwkv_g2, rwkv_k_k, rwkv_k_a, rwkv_r_k, rwkv_ln_w, rwkv_ln_b, w_out_rwkv, s5_a_re, s5_a_im, s5_log_dt, s5_b_re, s5_b_im, s5_c_re, s5_c_im, s5_d, s5_w_glu, w_out_s5, m_conv_w, m_conv_b, m_dt_bias, m_a_log, m_d, m_norm, w_out_mamba, xa_wq, xa_wk, xa_wv, xa_wo, mlp_w1, mlp_w2):
    depth = w_in.shape[0]
    bp, tp, _ = x_prompt.shape
    bs, ts, _ = x_sample.shape
    mlen = mem_prompt.shape[1]
    bf = lambda x: x.astype(_MXU)
    row = lambda x: x.reshape(1, -1).astype(F32)

    eh = (jnp.arange(128)[:, None] == (jnp.arange(M_WIDTH)[None, :] // M_HEAD)).astype(_MXU)

    col = [0]
    for sz in (3 * D, 3 * D + 256, D, M_WIDTH, M_CONV_DIM, 32):
        col.append(col[-1] + sz)
    layers = []
    for l in range(depth):
        wl = w_in[l]
        ar, ai, bre, bim, cre, cim = _s5_discretize(s5_a_re[l], s5_a_im[l], s5_log_dt[l], s5_b_re[l], s5_b_im[l],
                                                    s5_c_re[l], s5_c_im[l])
        zero = jnp.zeros((64, D), F32)
        wa = jnp.concatenate([jnp.concatenate([rwkv_w2[l], zero], axis=1),
                              jnp.concatenate([zero, rwkv_a2[l]], axis=1)], axis=0)
        pad32 = lambda x: jnp.pad(x.reshape(1, -1).astype(F32), ((0, 0), (0, 128 - x.shape[-1])))
        layers.append(dict(
            g_mix_pre=row(norm_mix_pre[l]), g_mix_post=row(norm_mix_post[l]),
            g_xa_pre=row(norm_xa_pre[l]), g_xa_post=row(norm_xa_post[l]),
            g_mlp_pre=row(norm_mlp_pre[l]), g_mlp_post=row(norm_mlp_post[l]),
            w_gates=bf(wl[:, col[0]:col[1]]), w_rwkv=bf(wl[:, col[1]:col[2]]), w_s5=bf(wl[:, col[2]:col[3]]),
            w_z=bf(wl[:, col[3]:col[4]]), w_xbc=bf(wl[:, col[4]:col[5]]),
            w_dt=bf(jnp.pad(wl[:, col[5]:col[6]], ((0, 0), (0, 96)))),
            mu=row(rwkv_mu[l]), wa=bf(wa), w0=row(rwkv_w0[l]), a0=row(rwkv_a0[l]), g2=bf(rwkv_g2[l]),
            k_k=row(rwkv_k_k[l]), k_a=row(rwkv_k_a[l]), r_k=row(rwkv_r_k[l]),
            ln_w=row(rwkv_ln_w[l]), ln_b=row(rwkv_ln_b[l]),
            s5_ar=ar, s5_ai=ai, s5_bre=bre, s5_bim=bim, s5_cre=cre, s5_cim=cim, s5_d=row(s5_d[l]),
            conv_w=m_conv_w[l].astype(F32), conv_b=row(m_conv_b[l]), dt_bias=pad32(m_dt_bias[l]),
            a_log=pad32(m_a_log[l]), d_x=row(jnp.repeat(m_d[l], M_HEAD)), m_norm=row(m_norm[l]), eh=eh,
            w_out_rwkv=bf(w_out_rwkv[l]), s5_w_glu=bf(s5_w_glu[l]), w_out_s5=bf(w_out_s5[l]),
            w_out_mamba=bf(w_out_mamba[l]), w_out=bf(w_out[l]),
            xa_wq=bf(xa_wq[l]), xa_wk=bf(xa_wk[l]), xa_wv=bf(xa_wv[l]), xa_wo=bf(xa_wo[l]),
            mlp_w1=bf(mlp_w1[l]), mlp_w2=bf(mlp_w2[l]), g_mem=row(norm_mem[l]),
        ))

    hp = x_prompt.reshape(bp * tp, D).astype(F32)
    hs = x_sample.reshape(bs * ts, D).astype(F32)
    mem2 = mem_prompt.reshape(bp * mlen, D).astype(F32)

    cfg_p = _group_cfg(tp, bp)
    cfg_s = _group_cfg(ts, bs)
    zeros_p = dict(
        rwkv=jnp.zeros((depth, bp) + state_rwkv.shape[2:], F32),
        shift=jnp.zeros((depth, bp, state_rwkv_shift.shape[-1]), F32),
        s5r=jnp.zeros((depth, bp, state_s5_re.shape[2] * state_s5_re.shape[3]), F32),
        s5i=jnp.zeros((depth, bp, state_s5_re.shape[2] * state_s5_re.shape[3]), F32),
        conv=jnp.zeros((depth, bp) + state_conv.shape[2:], F32),
        ssm=jnp.zeros((depth, bp) + state_ssm.shape[2:], F32),
    )
    st_s = dict(
        rwkv=state_rwkv, shift=state_rwkv_shift,
        s5r=state_s5_re.reshape(depth, bs, -1), s5i=state_s5_im.reshape(depth, bs, -1),
        conv=state_conv, ssm=state_ssm,
        mk=cache_mem_k, mv=cache_mem_v,
        kv_index=lambda l: (l,),
    )

    mk_out, mv_out = [], []
    st_p_out = [[] for _ in range(6)]
    st_s_out = [[] for _ in range(6)]
    acc_p = (jnp.zeros_like(zeros_p['rwkv']), jnp.zeros_like(zeros_p['ssm']))
    acc_s = (jnp.zeros_like(state_rwkv, dtype=F32), jnp.zeros_like(state_ssm, dtype=F32))
    for l in range(depth):
        W = layers[l]
        mk = norm_matmul(mem2, W['g_mem'], W['xa_wk'], name="mem_k")
        mv = norm_matmul(mem2, W['g_mem'], W['xa_wv'], name="mem_v")
        st_p = dict(zeros_p, mk=mk.reshape(bp, mlen, D), mv=mv.reshape(bp, mlen, D), kv_index=lambda l: ())
        hp, new_p = _layer(l, hp, cfg_p, W, st_p, acc_p)
        hs, new_s = _layer(l, hs, cfg_s, W, st_s, acc_s)
        acc_p = (new_p[0], new_p[5])
        acc_s = (new_s[0], new_s[5])
        mk_out.append(mk.reshape(bp, mlen, XA_HEADS, XA_HD))
        mv_out.append(mv.reshape(bp, mlen, XA_HEADS, XA_HD))
        for i in range(1, 5):
            st_p_out[i].append(new_p[i])
            st_s_out[i].append(new_s[i])

    stk = lambda xs: jnp.stack(xs).astype(F32)
    y_p = hp.reshape(bp, tp, D).astype(x_prompt.dtype)
    y_s = hs.reshape(bs, ts, D).astype(x_sample.dtype)
    return (y_p, y_s, stk(mk_out), stk(mv_out),
            acc_p[0], *[stk(st_p_out[i]) for i in range(1, 5)], acc_p[1],
            acc_s[0], *[stk(st_s_out[i]) for i in range(1, 5)], acc_s[1])
```

```python
import functools
import math

import numpy as np
import jax
import jax.numpy as jnp
from jax import lax
from jax.experimental import pallas as pl
from jax.experimental.pallas import tpu as pltpu

F32 = jnp.float32
_MXU = jnp.bfloat16
EPS = 1e-6
R_LN_EPS = 64e-5
D = 1024
R_HEAD = 64
R_PAIR = 2 * R_HEAD
S5_BLK_CH = 128
S5_BLK_ST = 512
M_HEAD = 64
M_STATE = 128
M_GROUPS = 4
M_HPG = 8
M_GW = M_HPG * M_HEAD
M_WIDTH = 2048
M_CONV_DIM = 3072
XA_HEADS = 4
XA_HD = 256
VMEM_LIMIT = 56 * 1024 * 1024


def _mm(a, b):
    return jnp.dot(a.astype(_MXU), b.astype(_MXU), preferred_element_type=F32)


def _mm_nt(a, b):
    return lax.dot_general(a.astype(_MXU), b.astype(_MXU), (((1,), (1,)), ((), ())),
                           preferred_element_type=F32)


def _mm_tn(a, b):
    return lax.dot_general(a.astype(_MXU), b.astype(_MXU), (((0,), (0,)), ((), ())),
                           preferred_element_type=F32)


def _split3(x):
    hi = x.astype(_MXU)
    r1 = x - hi.astype(F32)
    mid = r1.astype(_MXU)
    lo = (r1 - mid.astype(F32)).astype(_MXU)
    return hi, mid, lo


def _mm_sel_l(sel, x):
    return sum(jnp.dot(sel, p, preferred_element_type=F32) for p in _split3(x))


def _mm_sel_r(x, sel):
    return sum(jnp.dot(p, sel, preferred_element_type=F32) for p in _split3(x))


def _mm_nt_sel_l(sel, x):
    return sum(lax.dot_general(sel, p, (((1,), (1,)), ((), ())), preferred_element_type=F32)
               for p in _split3(x))


def _sigmoid(x):
    return 1.0 / (1.0 + jnp.exp(-x))


def _softplus(x):
    return jnp.maximum(x, 0.0) + jnp.log(1.0 + jnp.exp(-jnp.abs(x)))


def _rms(x, g):
    return x * lax.rsqrt(jnp.mean(x * x, axis=-1, keepdims=True) + EPS) * g


def _iota(shape, dim):
    return lax.broadcasted_iota(jnp.int32, shape, dim)


def _eye(n, dtype):
    return jnp.where(_iota((n, n), 0) == _iota((n, n), 1), 1.0, 0.0).astype(dtype)


def _cparams(sem):
    return pltpu.CompilerParams(dimension_semantics=sem, vmem_limit_bytes=VMEM_LIMIT)


def _row_tile(n, want):
    t = min(n, want)
    while n % t:
        t //= 2
    return t


def _col_tile(c, cap):
    best = 128
    for t in range(128, min(c, cap) + 1, 128):
        if c % t == 0:
            best = t
    return best


def _const_spec(shape):
    nd = len(shape)
    return pl.BlockSpec(shape, lambda *_: (0,) * nd, pipeline_mode=pl.Buffered(1))


def _norm_mm_kernel(x_ref, g_ref, w_ref, o_ref, xn_ref):
    @pl.when(pl.program_id(1) == 0)
    def _():
        xn_ref[...] = _rms(x_ref[...], g_ref[...]).astype(_MXU)

    o_ref[...] = jnp.dot(xn_ref[...], w_ref[...], preferred_element_type=F32)


def norm_matmul(x, g, w, *, tm=1024, tn=1664, name="norm_mm"):
    n, d = x.shape
    c = w.shape[1]
    tm = _row_tile(n, tm)
    tn = _col_tile(c, tn)
    return pl.pallas_call(
        _norm_mm_kernel,
        out_shape=jax.ShapeDtypeStruct((n, c), F32),
        grid=(n // tm, c // tn),
        in_specs=[pl.BlockSpec((tm, d), lambda i, j: (i, 0)),
                  pl.BlockSpec((1, d), lambda i, j: (0, 0)),
                  pl.BlockSpec((d, tn), lambda i, j: (0, j))],
        out_specs=pl.BlockSpec((tm, tn), lambda i, j: (i, j)),
        scratch_shapes=[pltpu.VMEM((tm, d), _MXU)],
        compiler_params=_cparams(("parallel", "arbitrary")),
        name=name,
    )(x, g, w)


def _mm_norm_res_kernel(x_ref, w_ref, g_ref, h_ref, o_ref):
    y = _mm(x_ref[...], w_ref[...])
    o_ref[...] = h_ref[...] + _rms(y, g_ref[...])


def matmul_norm_residual(x, w, g, h, *, tm=512, name="mm_norm_res"):
    n, k = x.shape
    d = w.shape[1]
    tm = _row_tile(n, tm)
    return pl.pallas_call(
        _mm_norm_res_kernel,
        out_shape=jax.ShapeDtypeStruct((n, d), F32),
        grid=(n // tm,),
        in_specs=[pl.BlockSpec((tm, k), lambda i: (i, 0)),
                  _const_spec((k, d)),
                  _const_spec((1, d)),
                  pl.BlockSpec((tm, d), lambda i: (i, 0))],
        out_specs=pl.BlockSpec((tm, d), lambda i: (i, 0)),
        compiler_params=_cparams(("parallel",)),
        name=name,
    )(x, w, g, h)


def _mlp_kernel(h_ref, g1_ref, w1_ref, w2_ref, g2_ref, o_ref, *, n_chunks, ck):
    h = h_ref[...]
    xn = _rms(h, g1_ref[...]).astype(_MXU)
    acc = jnp.zeros(h.shape, F32)
    for j in range(n_chunks):
        a = jnp.dot(xn, w1_ref[:, j * ck:(j + 1) * ck], preferred_element_type=F32)
        a = jnp.square(jnp.maximum(a, 0.0))
        acc = acc + jnp.dot(a.astype(_MXU), w2_ref[j * ck:(j + 1) * ck, :], preferred_element_type=F32)
    o_ref[...] = h + _rms(acc, g2_ref[...])


def mlp_block(h, g1, w1, w2, g2, *, tm=512, ck=1024):
    n, d = h.shape
    f = w1.shape[1]
    tm = _row_tile(n, tm)
    return pl.pallas_call(
        functools.partial(_mlp_kernel, n_chunks=f // ck, ck=ck),
        out_shape=jax.ShapeDtypeStruct((n, d), F32),
        grid=(n // tm,),
        in_specs=[pl.BlockSpec((tm, d), lambda i: (i, 0)),
                  _const_spec((1, d)),
                  _const_spec((d, f)),
                  _const_spec((f, d)),
                  _const_spec((1, d))],
        out_specs=pl.BlockSpec((tm, d), lambda i: (i, 0)),
        compiler_params=_cparams(("parallel",)),
        name="mlp",
    )(h, g1, w1, w2, g2)


def _norm_mm_tmajor_kernel(x_ref, g_ref, w_ref, perm_ref, o_ref):
    nb, tt, d = x_ref.shape
    xn = _rms(x_ref[...].reshape(nb * tt, d), g_ref[...]).astype(_MXU)
    xt = jnp.dot(perm_ref[...], xn, preferred_element_type=F32).astype(_MXU)
    o_ref[...] = jnp.dot(xt, w_ref[...], preferred_element_type=F32).reshape(o_ref.shape)


def norm_matmul_tmajor(x3, g, w, *, rows=512, name="norm_mm_t"):
    nb, t, d = x3.shape
    c = w.shape[1]
    tt = rows // nb
    assert t % tt == 0 and tt % 8 == 0
    perm = jnp.asarray(_time_major_perm(nb, tt), _MXU)
    return pl.pallas_call(
        _norm_mm_tmajor_kernel,
        out_shape=jax.ShapeDtypeStruct((t, nb, c), F32),
        grid=(t // tt,),
        in_specs=[pl.BlockSpec((nb, tt, d), lambda i: (0, i, 0)),
                  _const_spec((1, d)), _const_spec((d, c)), _const_spec((nb * tt, nb * tt))],
        out_specs=pl.BlockSpec((tt, nb, c), lambda i: (i, 0, 0)),
        compiler_params=_cparams(("parallel",)),
        name=name,
    )(x3, g, w, perm)


def _merge_kernel(gates_ref, yr_ref, ys_ref, ym_ref, h_ref, wr_ref, wglu_ref, ws_ref, wm_ref, wo_ref, g_ref,
                  *rest, ys_tmajor):
    o_ref = rest[-1]
    rows2d = lambda ref: ref[...].reshape(-1, ref.shape[-1])
    o_r = _mm(rows2d(yr_ref), wr_ref[...])
    ys = rows2d(ys_ref)
    y3 = ys * _sigmoid(_mm(ys, wglu_ref[...]))
    if ys_tmajor:
        y3 = jnp.dot(rest[0][...], y3.astype(_MXU), preferred_element_type=F32)
    o_s = _mm(y3, ws_ref[...])
    o_m = _mm(rows2d(ym_ref), wm_ref[...])
    gates = rows2d(gates_ref)
    merged = (_sigmoid(gates[:, 0:D]) * o_r + _sigmoid(gates[:, D:2 * D]) * o_s
              + _sigmoid(gates[:, 2 * D:3 * D]) * o_m)
    mix = _mm(merged, wo_ref[...])
    o_ref[...] = (rows2d(h_ref) + _rms(mix, g_ref[...])).reshape(o_ref.shape)


def mixer_merge(gates, yr, ys, ym, h, wr, wglu, ws, wm, wo, g, *, tm=256, seq=None):
    n = h.shape[0]
    weights = [_const_spec((D, D)), _const_spec((D, D)), _const_spec((D, D)),
               _const_spec((M_WIDTH, D)), _const_spec((D, D)), _const_spec((1, D))]
    if seq is None:
        tm = _row_tile(n, tm)
        row = lambda w: pl.BlockSpec((tm, w), lambda i: (i, 0))
        in_specs = [row(3 * D), row(D), row(D), row(M_WIDTH), row(D)] + weights
        args = [gates, yr, ys, ym, h, wr, wglu, ws, wm, wo, g]
        out_spec, out_shape, grid = row(D), (n, D), (n // tm,)
    else:
        nb, t = seq
        tt = tm // nb
        assert t % tt == 0 and tt % 8 == 0
        row = lambda w: pl.BlockSpec((nb, tt, w), lambda i: (0, i, 0))
        v3 = lambda x: x.reshape(nb, t, x.shape[-1])
        in_specs = ([row(3 * D), row(D), pl.BlockSpec((tt, nb, D), lambda i: (i, 0, 0)), row(M_WIDTH), row(D)]
                    + weights + [_const_spec((nb * tt, nb * tt))])
        args = [v3(gates), v3(yr), ys, v3(ym), v3(h), wr, wglu, ws, wm, wo, g,
                jnp.asarray(_time_major_perm(nb, tt).T, _MXU)]
        out_spec, out_shape, grid = row(D), (nb, t, D), (t // tt,)
    out = pl.pallas_call(
        functools.partial(_merge_kernel, ys_tmajor=seq is not None),
        out_shape=jax.ShapeDtypeStruct(out_shape, F32),
        grid=grid,
        in_specs=in_specs,
        out_specs=out_spec,
        compiler_params=_cparams(("parallel",)),
        name="mixer_merge",
    )(*args)
    return out.reshape(n, D)


def _rwkv_rec_kernel(pr_ref, pk_ref, pv_ref, pl_ref, shr_ref, shk_ref, shv_ref, shl_ref,
                     mur_ref, muk_ref, muv_ref, mul_ref, w2_ref, a2_ref, g2_ref, w0_ref, a0_ref, kk_ref, ka_ref,
                     s0_ref, rk_ref, lnw_ref, lnb_ref,
                     y_ref, sfin_ref, s_scr, carry_scr, *, C, G, NP, n_steps, t_real):
    step = pl.program_id(2)
    C2 = 2 * C
    W = NP * R_PAIR
    probs = [(gi, pp) for gi in range(G) for pp in range(NP)]
    rng = range(len(probs))
    row_c = _iota((C, 1), 0)

    @pl.when(step == 0)
    def _():
        for j, sh_ref in enumerate((shr_ref, shk_ref, shv_ref, shl_ref)):
            carry_scr[j] = sh_ref[...]

    def shift_mix(j, p_ref, mu_ref):
        out = []
        for gi in range(G):
            x = p_ref[gi]
            prev = jnp.where(row_c == 0, carry_scr[j, gi:gi + 1, :], pltpu.roll(x, 1, axis=0))
            carry_scr[j, gi:gi + 1, :] = x[C - 1:C, :]
            out.append(x + mu_ref[...] * (prev - x))
        return out

    pm_r = shift_mix(0, pr_ref, mur_ref)
    pm_k = shift_mix(1, pk_ref, muk_ref)
    pm_v = shift_mix(2, pv_ref, muv_ref)
    xl = jnp.concatenate(shift_mix(3, pl_ref, mul_ref), axis=0)
    x_wa = xl[:, 0:128]
    lora_in = jnp.where(_iota(x_wa.shape, 1) < 64, jnp.tanh(x_wa), x_wa)
    w_log = -_softplus(-(w0_ref[...] + _mm(lora_in, w2_ref[...]))) - 0.5
    e_all = jnp.exp(w_log)
    a_all = _sigmoid(a0_ref[...] + _mm(lora_in, a2_ref[...]))
    g_all = _mm(_sigmoid(xl[:, 128:256]), g2_ref[...])
    valid = row_c < t_real

    def seq_rows(x, gi, keep_pad=False):
        x = x[gi * C:(gi + 1) * C]
        return x if (keep_pad or t_real >= C) else jnp.where(valid, x, 0.0)

    def pad0(x):
        return x if t_real >= C else jnp.where(valid, x, 0.0)

    lanes = lambda x, pp: x[:, pp * R_PAIR:(pp + 1) * R_PAIR]
    r, e, k, v, kk, a_l, g_l = [], [], [], [], [], [], []
    for gi, pp in probs:
        a_q = lanes(seq_rows(a_all, gi, True), pp)
        k_raw = lanes(pad0(pm_k[gi]), pp)
        r.append(lanes(pad0(pm_r[gi]), pp))
        e.append(lanes(seq_rows(e_all, gi), pp))
        k.append(k_raw * (1.0 + (a_q - 1.0) * lanes(ka_ref[...], pp)))
        v.append(lanes(pad0(pm_v[gi]), pp))
        kk.append(k_raw * lanes(kk_ref[...], pp))
        a_l.append(a_q)
        g_l.append(lanes(seq_rows(g_all, gi, True), pp))

    def par(ref, q):
        pp = probs[q][1]
        return ref[:, pp * R_PAIR:(pp + 1) * R_PAIR]

    lane = _iota((1, R_PAIR), 1)
    m0 = lane < R_HEAD
    ri = _iota((R_PAIR, R_PAIR), 0)
    ci = _iota((R_PAIR, R_PAIR), 1)
    bd = (ri < R_HEAD) == (ci < R_HEAD)
    rt_i = _iota((C2, C2), 0)
    ct_i = _iota((C2, C2), 1)
    same = (rt_i < C) == (ct_i < C)
    tr = jnp.where(rt_i < C, rt_i, rt_i - C)
    ts = jnp.where(ct_i < C, ct_i, ct_i - C)
    m_sl = same & (ts < tr)
    m_li = same & (ts <= tr)
    eye2 = jnp.where(rt_i == ct_i, 1.0, 0.0)
    lvl_masks = []
    m = 1
    while m < C:
        lvl_masks.append(((tr & ~(2 * m - 1)) == (ts & ~(2 * m - 1))) & ((tr & m) != 0) & ((ts & m) == 0))
        m *= 2
    tril = jnp.where(_iota((C, C), 1) <= _iota((C, C), 0), 1.0, 0.0).astype(_MXU)

    @pl.when(step == 0)
    def _():
        z = jnp.zeros((R_HEAD, R_HEAD), F32)
        for q, (gi, pp) in enumerate(probs):
            s_scr[q] = jnp.concatenate([jnp.concatenate([s0_ref[gi, 2 * pp], z], axis=1),
                                        jnp.concatenate([z, s0_ref[gi, 2 * pp + 1]], axis=1)], axis=0)

    def stack(x):
        return jnp.concatenate([jnp.where(m0, x, 0.0), jnp.where(m0, 0.0, x)], axis=0)

    def fold(xs):
        return xs[:C] + xs[C:]

    def each(f, *lists):
        return [f(*xs) for xs in zip(*lists)]

    def head_sum(x):
        s0 = jnp.sum(jnp.where(m0, x, 0.0), axis=-1, keepdims=True)
        s1 = jnp.sum(jnp.where(m0, 0.0, x), axis=-1, keepdims=True)
        return jnp.where(m0, s0, s1)

    an = each(lambda x: -x * lax.rsqrt(head_sum(x * x) + 1e-12), kk)
    bn = each(lambda n_, a_: -n_ * a_, an, a_l)
    cs = each(lambda x: _mm_sel_l(tril, x), e)
    cl = each(lambda x: x[C - 1:C, :], cs)
    at = each(lambda a_, e_, c_: a_ * jnp.exp(e_ - c_), an, e, cs)
    rt = each(lambda r_, c_: r_ * jnp.exp(-c_), r, cs)
    ecs = each(jnp.exp, cs)
    bt_ = each(lambda b_, x: b_ * x, bn, ecs)
    kt = each(lambda k_, x: k_ * x, k, ecs)
    ecl = each(lambda c_, l_: jnp.exp(c_ - l_), cs, cl)
    bh = each(lambda b_, x: b_ * x, bn, ecl)
    kh = each(lambda k_, x: k_ * x, k, ecl)
    As, Rs, Vs = each(stack, at), each(stack, rt), each(stack, v)
    b2 = each(lambda x: jnp.concatenate([x, x], axis=0), bt_)
    k2 = each(lambda x: jnp.concatenate([x, x], axis=0), kt)
    N = each(lambda a_, b_: jnp.where(m_sl, _mm_nt(a_, b_), 0.0), As, b2)
    Ak = each(lambda a_, b_: jnp.where(m_sl, _mm_nt(a_, b_), 0.0), As, k2)
    Arb = each(lambda a_, b_: jnp.where(m_li, _mm_nt(a_, b_), 0.0), Rs, b2)
    Ark = each(lambda a_, b_: jnp.where(m_li, _mm_nt(a_, b_), 0.0), Rs, k2)
    T = each(lambda n_: eye2 + jnp.where(lvl_masks[0], n_, 0.0), N)
    for lm in lvl_masks[1:]:
        LT = each(lambda n_, t_: _mm(jnp.where(lm, n_, 0.0), t_), N, T)
        T = each(lambda t_, x: t_ + _mm(t_, x), T, LT)
    AkV = each(_mm, Ak, Vs)
    X = each(lambda t_, a_, u_: _mm(t_, jnp.concatenate([a_, u_], axis=1)), T, As, AkV)
    Z = each(_mm, Arb, X)
    ArkV = each(_mm, Ark, Vs)
    Rp = each(lambda r_, z_: r_ + fold(z_[:, 0:R_PAIR]), rt, Z)
    Y0 = each(lambda z_, a_: fold(z_[:, R_PAIR:] + a_), Z, ArkV)
    Ap = each(lambda x: fold(x[:, 0:R_PAIR]), X)
    U0 = each(lambda x: fold(x[:, R_PAIR:]), X)
    P = each(lambda a_, b_: jnp.where(bd, _mm_tn(a_, b_), 0.0), Ap, bh)
    Q = each(lambda u_, v_, b_, k_: jnp.where(bd, _mm_tn(jnp.concatenate([u_, v_], axis=0),
                                                         jnp.concatenate([b_, k_], axis=0)), 0.0), U0, v, bh, kh)
    S = [s_scr[q] for q in rng]
    y = each(lambda y_, r_, s_: y_ + _mm_nt(r_, s_), Y0, Rp, S)
    S = each(lambda s_, l_, p_, q_: s_ * jnp.exp(-l_) + _mm(s_, p_) + q_, S, cl, P, Q)
    for q in rng:
        s_scr[q] = S[q]
    dlt = each(lambda y_: y_ - head_sum(y_) * (1.0 / R_HEAD), y)
    var = each(lambda d_: head_sum(d_ * d_) * (1.0 / R_HEAD), dlt)
    bonus = each(lambda r_, k_, v_, q: head_sum(r_ * k_ * par(rk_ref, q)) * v_, r, k, v, rng)
    for q, (gi, pp) in enumerate(probs):
        yn = dlt[q] * lax.rsqrt(var[q] + R_LN_EPS) * par(lnw_ref, q) + par(lnb_ref, q)
        y_ref[gi, :, pp * R_PAIR:(pp + 1) * R_PAIR] = (yn + bonus[q]) * g_l[q]

    @pl.when(step == n_steps - 1)
    def _():
        for q, (gi, pp) in enumerate(probs):
            sfin_ref[gi, 2 * pp] = S[q][0:R_HEAD, 0:R_HEAD]
            sfin_ref[gi, 2 * pp + 1] = S[q][R_HEAD:, R_HEAD:]


def _stacked_out(kernel_fn, in_specs, args, out_index, stack):
    pos = len(args)

    def with_alias(*refs):
        return kernel_fn(*refs[:pos], *refs[pos + 1:])

    return with_alias, in_specs + [pl.BlockSpec(memory_space=pl.ANY)], args + [stack], {pos: out_index}


def rwkv_recurrence(p3, shift_prev, mu, wa, g2, w0, a0, k_k, k_a, s0, l, r_k, ln_w, ln_b, stack, *, C, G, NP,
                    t_real):
    B, T, _ = p3.shape
    n_steps = T // C
    n_pairs = D // R_PAIR
    W = NP * R_PAIR
    assert W == 256 and t_real <= C and (t_real == C or n_steps == 1)
    seg = D // W
    cblk = lambda off: pl.BlockSpec((G, C, W), lambda b, p, s: (b, s, off + p))
    sblk = lambda off: pl.BlockSpec((G, W), lambda b, p, s: (b, off + p))
    mblk = lambda off: pl.BlockSpec((1, W), lambda b, p, s: (0, off + p))
    lora_c = pl.BlockSpec((G, C, W), lambda b, p, s: (b, s, 3 * seg))
    lora_s = pl.BlockSpec((G, W), lambda b, p, s: (b, 3 * seg))
    lora_m = pl.BlockSpec((1, W), lambda b, p, s: (0, 3 * seg))
    wblk = lambda off: pl.BlockSpec((128, W), lambda b, p, s: (0, off + p))
    par = lambda: pl.BlockSpec((1, W), lambda b, p, s: (0, p))
    out_blk = pl.BlockSpec((G, C, W), lambda b, p, s: (b, s, p))
    st_spec = lambda: pl.BlockSpec((None, G, 2 * NP, R_HEAD, R_HEAD), lambda b, p, s: (l, b, p, 0, 0))
    in_specs = ([cblk(0), cblk(seg), cblk(2 * seg), lora_c,
                 sblk(0), sblk(seg), sblk(2 * seg), lora_s,
                 mblk(0), mblk(seg), mblk(2 * seg), lora_m,
                 wblk(0), wblk(seg), wblk(0)]
                + [par() for _ in range(4)] + [st_spec(), par(), par(), par()])
    kern, in_specs, args, aliases = _stacked_out(
        functools.partial(_rwkv_rec_kernel, C=C, G=G, NP=NP, n_steps=n_steps, t_real=t_real), in_specs,
        [p3, p3, p3, p3, shift_prev, shift_prev, shift_prev, shift_prev, mu, mu, mu, mu, wa, wa, g2,
         w0, a0, k_k, k_a, s0, r_k, ln_w, ln_b], 1, stack)
    y, sfin = pl.pallas_call(
        kern,
        out_shape=[jax.ShapeDtypeStruct((B, T, D), F32),
                   jax.ShapeDtypeStruct((s0.shape[0], B, 2 * n_pairs, R_HEAD, R_HEAD), F32)],
        grid=(B // G, n_pairs // NP, n_steps),
        in_specs=in_specs,
        out_specs=[out_blk, st_spec()],
        scratch_shapes=[pltpu.VMEM((G * NP, R_PAIR, R_PAIR), F32), pltpu.VMEM((4, G, W), F32)],
        input_output_aliases=aliases,
        compiler_params=_cparams(("parallel", "parallel", "arbitrary")),
        name="rwkv_rec",
    )(*args)
    return y, sfin


def _s5_kernel(u_ref, h0r_ref, h0i_ref, ar_ref, ai_ref, bre_ref, bim_ref, cre_ref, cim_ref, d_ref, *rest,
               tc, nb, kb, n_steps, permute):
    if permute:
        perm_ref, permt_ref = rest[:2]
        rest = rest[2:]
    y_ref, hr_out, hi_out, inr_scr, ini_scr, hr_scr, hi_scr = rest
    step = pl.program_id(1)
    rows = nb * tc

    @pl.when(step == 0)
    def _():
        hr_scr[...] = h0r_ref[...]
        hi_scr[...] = h0i_ref[...]

    u = u_ref[...].reshape(rows, kb * S5_BLK_CH)
    ch = lambda j: slice(j * S5_BLK_CH, (j + 1) * S5_BLK_CH)
    stt = lambda j: slice(j * S5_BLK_ST, (j + 1) * S5_BLK_ST)
    if permute:
        ut = jnp.dot(perm_ref[...], u.astype(_MXU), preferred_element_type=F32).astype(_MXU)
    else:
        ut = u.astype(_MXU)
    for j in range(kb):
        inr_scr[:, stt(j)] = jnp.dot(ut[:, ch(j)], bre_ref[j], preferred_element_type=F32)
        ini_scr[:, stt(j)] = jnp.dot(ut[:, ch(j)], bim_ref[j], preferred_element_type=F32)
    ar = ar_ref[...]
    ai = ai_ref[...]

    def body(t, carry):
        hr, hi = carry
        sl = pl.ds(pl.multiple_of(t * nb, nb), nb)
        nr = ar * hr - ai * hi + inr_scr[sl, :]
        ni = ar * hi + ai * hr + ini_scr[sl, :]
        inr_scr[sl, :] = nr
        ini_scr[sl, :] = ni
        return nr, ni

    hr, hi = lax.fori_loop(0, tc, body, (hr_scr[...], hi_scr[...]))
    hr_scr[...] = hr
    hi_scr[...] = hi
    yts = [jnp.dot(inr_scr[:, stt(j)].astype(_MXU), cre_ref[j], preferred_element_type=F32)
           - jnp.dot(ini_scr[:, stt(j)].astype(_MXU), cim_ref[j], preferred_element_type=F32) for j in range(kb)]
    yt = yts[0] if kb == 1 else jnp.concatenate(yts, axis=1)
    if permute:
        yt_hi = yt.astype(_MXU)
        yt_lo = (yt - yt_hi.astype(F32)).astype(_MXU)
        yt = (jnp.dot(permt_ref[...], yt_hi, preferred_element_type=F32)
              + jnp.dot(permt_ref[...], yt_lo, preferred_element_type=F32))
    y = yt + d_ref[...] * u
    y = 0.5 * y * (1.0 + jnp.tanh(math.sqrt(2.0 / math.pi) * (y + 0.044715 * (y * y * y))))
    y_ref[...] = y.reshape(y_ref.shape)

    @pl.when(step == n_steps - 1)
    def _():
        hr_out[...] = hr
        hi_out[...] = hi


def _time_major_perm(nb, tc):
    rows = nb * tc
    j = np.arange(rows)
    perm_np = np.zeros((rows, rows), np.float32)
    perm_np[(j % tc) * nb + j // tc, j] = 1.0
    return perm_np


def s5_scan(u3, h0r, h0i, l, ar, ai, bre, bim, cre, cim, d, *, nb, tc, blk, kb):
    A, R, _ = u3.shape
    ga, tr = blk
    assert ga * tr == nb * tc
    tmajor = (ga, tr) == (tc, nb) and R == nb
    nblk = D // S5_BLK_CH
    n_steps = A // ga if tmajor else R // tr
    rows = nb * tc
    u_spec = (pl.BlockSpec((ga, tr, kb * S5_BLK_CH), lambda c, s: (s, 0, c)) if tmajor
              else pl.BlockSpec((ga, tr, kb * S5_BLK_CH), lambda c, s: (0, s, c)))
    if tmajor:
        perm_specs, perm_args = [], []
    else:
        assert A == ga
        perm_np = _time_major_perm(nb, tc)
        perm_specs = [_const_spec((rows, rows)), _const_spec((rows, rows))]
        perm_args = [jnp.asarray(perm_np, _MXU), jnp.asarray(perm_np.T, _MXU)]
    st = lambda: pl.BlockSpec((None, nb, kb * S5_BLK_ST), lambda c, s: (l, 0, c))
    vec = lambda w: pl.BlockSpec((1, kb * w), lambda c, s: (0, c))
    mat = lambda a, b: pl.BlockSpec((kb, a, b), lambda c, s: (c, 0, 0))
    return pl.pallas_call(
        functools.partial(_s5_kernel, tc=tc, nb=nb, kb=kb, n_steps=n_steps, permute=not tmajor),
        out_shape=[jax.ShapeDtypeStruct(u3.shape, F32),
                   jax.ShapeDtypeStruct((nb, nblk * S5_BLK_ST), F32),
                   jax.ShapeDtypeStruct((nb, nblk * S5_BLK_ST), F32)],
        grid=(nblk // kb, n_steps),
        in_specs=[u_spec,
                  st(), st(), vec(S5_BLK_ST), vec(S5_BLK_ST),
                  mat(S5_BLK_CH, S5_BLK_ST), mat(S5_BLK_CH, S5_BLK_ST),
                  mat(S5_BLK_ST, S5_BLK_CH), mat(S5_BLK_ST, S5_BLK_CH),
                  vec(S5_BLK_CH)] + perm_specs,
        out_specs=[u_spec,
                   pl.BlockSpec((nb, kb * S5_BLK_ST), lambda c, s: (0, c)),
                   pl.BlockSpec((nb, kb * S5_BLK_ST), lambda c, s: (0, c))],
        scratch_shapes=[pltpu.VMEM((rows, kb * S5_BLK_ST), F32), pltpu.VMEM((rows, kb * S5_BLK_ST), F32),
                        pltpu.VMEM((nb, kb * S5_BLK_ST), F32), pltpu.VMEM((nb, kb * S5_BLK_ST), F32)],
        compiler_params=_cparams(("parallel", "arbitrary")),
        name="s5_scan",
    )(u3, h0r, h0i, ar, ai, bre, bim, cre, cim, d, *perm_args)


def _s5_discretize(a_re, a_im, log_dt, b_re, b_im, c_re, c_im):
    g, p, hch = b_re.shape
    dt = jnp.exp(log_dt.astype(F32))[:, None]
    mag = jnp.exp(dt * a_re)
    abar_re = mag * jnp.cos(dt * a_im)
    abar_im = mag * jnp.sin(dt * a_im)
    den = a_re * a_re + a_im * a_im
    nr = abar_re - 1.0
    q_re = (nr * a_re + abar_im * a_im) / den
    q_im = (abar_im * a_re - nr * a_im) / den
    bb_re = q_re[..., None] * b_re - q_im[..., None] * b_im
    bb_im = q_re[..., None] * b_im + q_im[..., None] * b_re
    nblk = D // S5_BLK_CH
    gl = g // nblk
    eye = jnp.eye(gl, dtype=F32)

    def in_blocks(bb):
        t = jnp.transpose(bb, (0, 2, 1)).reshape(nblk, gl, hch, p)
        return jnp.einsum('cghp,gk->cghkp', t, eye).reshape(nblk, gl * hch, gl * p).astype(_MXU)

    def out_blocks(cc):
        t = jnp.transpose(cc, (0, 2, 1)).reshape(nblk, gl, p, hch)
        return jnp.einsum('cgph,gk->cgpkh', t, eye).reshape(nblk, gl * p, gl * hch).astype(_MXU)

    return (abar_re.reshape(1, g * p), abar_im.reshape(1, g * p), in_blocks(bb_re), in_blocks(bb_im),
            out_blocks(c_re), out_blocks(c_im))


def _ssd_kernel(xbc_ref, z_ref, dt_ref, cprev_ref, h0_ref, cw_ref, cb_ref, dtb_ref, alog_ref, dx_ref, ng_ref,
                eh_ref, y_ref, hfin_ref, ext_scr, h_scr, *, L, G, t_real, n_steps):
    step = pl.program_id(1)
    rng = range(G)
    NB = M_GROUPS * M_STATE

    def each(f, *lists):
        return [f(*xs) for xs in zip(*lists)]

    @pl.when(step == 0)
    def _():
        for s in rng:
            ext_scr[s] = jnp.zeros((8, M_CONV_DIM), F32)
            ext_scr[s, 5:8, :] = cprev_ref[s]
            for gi in range(M_GROUPS):
                h_scr[s, gi] = h0_ref[s, gi * M_HPG:(gi + 1) * M_HPG].reshape(M_GW, M_STATE).T

    tril = jnp.where(_iota((L, L), 1) <= _iota((L, L), 0), 1.0, 0.0).astype(_MXU)
    causal = _iota((L, L), 1) <= _iota((L, L), 0)
    eye_h = _eye(128, _MXU)
    eh = eh_ref[...]
    lane_head = _iota((1, M_GW), 1) // M_HEAD
    row8 = _iota((8, 1), 0)
    a = -jnp.exp(alog_ref[...])
    conv, dts = [], []
    for s in rng:
        x = xbc_ref[s * L:(s + 1) * L, :]
        prev8 = ext_scr[s]
        c = cb_ref[...] + cw_ref[3:4, :] * x
        for j in range(1, 4):
            xr = pltpu.roll(x, j, axis=0)
            head = jnp.where(row8 < j, pltpu.roll(prev8, j, axis=0), xr[0:8])
            xr = head if L == 8 else jnp.concatenate([head, xr[8:]], axis=0)
            c = c + cw_ref[3 - j:4 - j, :] * xr
        ext_scr[s] = x[L - 8:L]
        conv.append(c * _sigmoid(c))
        dt = _softplus(dt_ref[s * L:(s + 1) * L, :] + dtb_ref[...])
        if t_real < L:
            dt = jnp.where(_iota(dt.shape, 0) < t_real, dt, 0.0)
        dts.append(dt)
    xs = each(lambda c: c[:, 0:M_WIDTH], conv)
    acum = each(lambda d_: _mm_sel_l(tril, d_ * a), dts)
    acum_t = each(lambda x: _mm_nt_sel_l(eye_h, x), acum)
    dt_x = each(lambda d_: _mm_sel_r(d_, eh), dts)
    acum_x = each(lambda x: _mm_sel_r(x, eh), acum)
    acl_x = each(lambda x: x[L - 1:L, :], acum_x)
    xd = each(lambda x, d_: x * d_, xs, dt_x)
    xdd = each(lambda x, l_, c_: x * jnp.exp(l_ - c_), xd, acl_x, acum_x)
    eacum_x = each(jnp.exp, acum_x)
    cdec_x = each(jnp.exp, acl_x)
    ys = [[] for _ in rng]
    for gi in range(M_GROUPS):
        gs = slice(gi * M_GW, (gi + 1) * M_GW)
        bg = each(lambda c: c[:, M_WIDTH + gi * M_STATE:M_WIDTH + (gi + 1) * M_STATE], conv)
        cg = each(lambda c: c[:, M_WIDTH + NB + gi * M_STATE:M_WIDTH + NB + (gi + 1) * M_STATE], conv)
        h = [h_scr[s, gi] for s in rng]
        cbm = each(_mm_nt, cg, bg)
        bgt = each(lambda b_: _mm_nt(eye_h, b_), bg)
        y_off = each(lambda c_, h_, e_: _mm(c_, h_) * e_[:, gs], cg, h, eacum_x)
        ms = []
        for e in range(M_HPG):
            he = gi * M_HPG + e
            ms.append(each(lambda c_, a_, t_: c_ * jnp.exp(jnp.where(causal, a_[:, he:he + 1] - t_[he:he + 1, :],
                                                                     -jnp.inf)), cbm, acum, acum_t))
        if L % 128 == 0:
            mcat = [jnp.concatenate([ms[e][s] for e in range(M_HPG)], axis=1) for s in rng]
            xst = each(lambda x: jnp.concatenate([jnp.where(lane_head == e, x[:, gs], 0.0).astype(_MXU)
                                                  for e in range(M_HPG)], axis=0), xd)
            y_dg = each(_mm, mcat, xst)
        else:
            y_dg = [sum(_mm(ms[e][s], jnp.where(lane_head == e, xd[s][:, gs], 0.0)) for e in range(M_HPG))
                    for s in rng]
        hn = each(lambda h_, d_, b_, x: h_ * d_[:, gs] + _mm(b_, x[:, gs]), h, cdec_x, bgt, xdd)
        for s in rng:
            h_scr[s, gi] = hn[s]
            ys[s].append(y_off[s] + y_dg[s])
    for s in rng:
        y = jnp.concatenate(ys[s], axis=1) + dx_ref[...] * xs[s]
        zz = z_ref[s * L:(s + 1) * L, :]
        y = y * (zz * _sigmoid(zz))
        outs = []
        for gi in range(M_GROUPS):
            yg = y[:, gi * M_GW:(gi + 1) * M_GW]
            outs.append(yg * lax.rsqrt(jnp.mean(yg * yg, axis=-1, keepdims=True) + EPS))
        y_ref[s * L:(s + 1) * L, :] = jnp.concatenate(outs, axis=1) * ng_ref[...]

    @pl.when(step == n_steps - 1)
    def _():
        for s in rng:
            for gi in range(M_GROUPS):
                hfin_ref[s, gi * M_HPG:(gi + 1) * M_HPG] = h_scr[s, gi].T.reshape(M_HPG, M_HEAD, M_STATE)


def ssd_block(xbc, z, dt, conv_prev, h0, l, conv_w, conv_b, dt_bias, a_log, d_x, norm_g, eh, stack, *, B, T, L, G,
              t_real):
    n_steps = T // L
    assert G == 1 or n_steps == 1
    heads = M_GROUPS * M_HPG
    seq = lambda w: pl.BlockSpec((G * L, w), lambda b, s: (b * n_steps + s, 0))
    st_spec = lambda: pl.BlockSpec((None, G, heads, M_HEAD, M_STATE), lambda b, s: (l, b, 0, 0, 0))
    in_specs = [seq(M_CONV_DIM), seq(M_WIDTH), seq(128),
                pl.BlockSpec((None, G, 3, M_CONV_DIM), lambda b, s: (l, b, 0, 0)),
                st_spec(),
                _const_spec((4, M_CONV_DIM)), _const_spec((1, M_CONV_DIM)),
                _const_spec((1, 128)), _const_spec((1, 128)),
                _const_spec((1, M_WIDTH)), _const_spec((1, M_WIDTH)),
                _const_spec((128, M_WIDTH))]
    kern, in_specs, args, aliases = _stacked_out(
        functools.partial(_ssd_kernel, L=L, G=G, t_real=t_real, n_steps=n_steps), in_specs,
        [xbc, z, dt, conv_prev, h0, conv_w, conv_b, dt_bias, a_log, d_x, norm_g, eh], 1, stack)
    return pl.pallas_call(
        kern,
        out_shape=[jax.ShapeDtypeStruct((B * T, M_WIDTH), F32),
                   jax.ShapeDtypeStruct((h0.shape[0], B, heads, M_HEAD, M_STATE), F32)],
        grid=(B // G, n_steps),
        in_specs=in_specs,
        out_specs=[seq(M_WIDTH), st_spec()],
        scratch_shapes=[pltpu.VMEM((G, 8, M_CONV_DIM), F32),
                        pltpu.VMEM((G, M_GROUPS, M_STATE, M_GW), F32)],
        input_output_aliases=aliases,
        compiler_params=_cparams(("parallel", "arbitrary")),
        name="ssd",
    )(*args)


def _attn_kernel(q_ref, k_ref, v_ref, o_ref, *, nb, tq, heads_split):
    scale = XA_HD ** -0.5
    for j in range(nb):
        if heads_split:
            k_all = pltpu.einshape("mhd->hmd", k_ref[j])
            v_all = pltpu.einshape("mhd->hmd", v_ref[j])
        outs = []
        for hd in range(XA_HEADS):
            cs = slice(hd * XA_HD, (hd + 1) * XA_HD)
            q = q_ref[j * tq:(j + 1) * tq, cs]
            kh = k_all[hd] if heads_split else k_ref[j, :, cs]
            vh = v_all[hd] if heads_split else v_ref[j, :, cs]
            s = _mm_nt(q, kh) * scale
            s = s - jnp.max(s, axis=-1, keepdims=True)
            p = jnp.exp(s)
            p = p / jnp.sum(p, axis=-1, keepdims=True)
            outs.append(_mm(p, vh))
        o_ref[j * tq:(j + 1) * tq, :] = jnp.concatenate(outs, axis=1)


def cross_attention(q, mk, mv, kv_index, *, B, T, tq, nb):
    nlead = len(kv_index)
    heads_split = mk.ndim - nlead == 4
    kv_blk = mk.shape[nlead + 1:]
    n_t = T // tq
    kv_spec = pl.BlockSpec((None,) * nlead + (nb,) + kv_blk,
                           lambda b, s: tuple(kv_index) + (b,) + (0,) * len(kv_blk))
    q_spec = pl.BlockSpec((nb * tq, D), lambda b, s: (b * n_t + s, 0))
    return pl.pallas_call(
        functools.partial(_attn_kernel, nb=nb, tq=tq, heads_split=heads_split),
        out_shape=jax.ShapeDtypeStruct((B * T, D), F32),
        grid=(B // nb, n_t),
        in_specs=[q_spec, kv_spec, kv_spec],
        out_specs=q_spec,
        compiler_params=_cparams(("parallel", "arbitrary")),
        name="xattn",
    )(q, mk, mv)


def _xattn_block_kernel(h_ref, gq_ref, wq_ref, k_ref, v_ref, wo_ref, go_ref, o_ref):
    h = h_ref[...]
    q = _mm(_rms(h, gq_ref[...]), wq_ref[...])
    scale = XA_HD ** -0.5
    outs = []
    for hd in range(XA_HEADS):
        cs = slice(hd * XA_HD, (hd + 1) * XA_HD)
        s = _mm_nt(q[:, cs], k_ref[0, :, cs]) * scale
        s = s - jnp.max(s, axis=-1, keepdims=True)
        p = jnp.exp(s)
        p = p / jnp.sum(p, axis=-1, keepdims=True)
        outs.append(_mm(p, v_ref[0, :, cs]))
    xa = _mm(jnp.concatenate(outs, axis=1), wo_ref[...])
    o_ref[...] = h + _rms(xa, go_ref[...])


def cross_attention_block(h, gq, wq, mk, mv, wo, go, *, B, T, tq):
    m = mk.shape[1]
    n_t = T // tq
    row = pl.BlockSpec((tq, D), lambda b, s: (b * n_t + s, 0))
    kv = pl.BlockSpec((1, m, D), lambda b, s: (b, 0, 0))
    return pl.pallas_call(
        _xattn_block_kernel,
        out_shape=jax.ShapeDtypeStruct((B * T, D), F32),
        grid=(B, n_t),
        in_specs=[row, _const_spec((1, D)), _const_spec((D, D)), kv, kv, _const_spec((D, D)), _const_spec((1, D))],
        out_specs=row,
        compiler_params=_cparams(("parallel", "arbitrary")),
        name="xattn_block",
    )(h, gq, wq, mk, mv, wo, go)


def _pad_time(x2, B, T, Tp):
    if Tp == T:
        return x2
    w = x2.shape[-1]
    return jnp.pad(x2.reshape(B, T, w), ((0, 0), (0, Tp - T), (0, 0))).reshape(B * Tp, w)


def _layer(l, h, grp, W, st, acc):
    T, B = grp['T'], grp['B']
    n = T * B
    gates = norm_matmul(h, W['g_mix_pre'], W['w_gates'], name="in_gates")
    p_r = norm_matmul(h, W['g_mix_pre'], W['w_rwkv'], name="in_rwkv")
    if grp['s5_tmajor']:
        u3 = norm_matmul_tmajor(h.reshape(B, T, D), W['g_mix_pre'], W['w_s5'], name="in_s5")
    else:
        u3 = norm_matmul(h, W['g_mix_pre'], W['w_s5'], name="in_s5").reshape(1, n, D)
    z_m =norm_matmul(h, W['g_mix_pre'], W['w_z'], name="in_z")
    xbc = norm_matmul(h, W['g_mix_pre'], W['w_xbc'], name="in_xbc")
    dt_m = norm_matmul(h, W['g_mix_pre'], W['w_dt'], tn=128, name="in_dt")

    Tp = grp['rwkv_Tpad']
    p3 = _pad_time(p_r, B, T, Tp).reshape(B, Tp, p_r.shape[-1])
    yr, s_fin = rwkv_recurrence(p3, st['shift'][l], W['mu'], W['wa'], W['g2'], W['w0'], W['a0'], W['k_k'],
                                W['k_a'], st['rwkv'], l, W['r_k'], W['ln_w'], W['ln_b'], acc[0],
                                C=grp['rwkv_C'], G=grp['rwkv_G'], NP=grp['rwkv_NP'], t_real=min(T, grp['rwkv_C']))
    yr = yr[:, :T].reshape(n, D)
    shift_new = p_r.reshape(B, T, -1)[:, T - 1]

    ys, s5r, s5i = s5_scan(u3, st['s5r'], st['s5i'], l, W['s5_ar'], W['s5_ai'], W['s5_bre'], W['s5_bim'],
                           W['s5_cre'], W['s5_cim'], W['s5_d'], nb=B, tc=grp['s5_tc'], blk=grp['s5_blk'],
                           kb=2 if B <= 16 else 1)

    Lc = grp['ssd_L']
    Tm = grp['ssd_Tpad']
    ym, ssm_fin = ssd_block(_pad_time(xbc, B, T, Tm), _pad_time(z_m, B, T, Tm), _pad_time(dt_m, B, T, Tm),
                            st['conv'], st['ssm'], l, W['conv_w'], W['conv_b'], W['dt_bias'],
                            W['a_log'], W['d_x'], W['m_norm'], W['eh'], acc[1], B=B, T=Tm, L=Lc, G=grp['ssd_G'],
                            t_real=min(T, Lc))
    ym = ym.reshape(B, Tm, M_WIDTH)[:, :T].reshape(n, M_WIDTH)
    conv_new = jnp.concatenate([st['conv'][l], xbc.reshape(B, T, M_CONV_DIM)[:, max(T - 3, 0):]], axis=1)[:, -3:]

    h = mixer_merge(gates, yr, ys if grp['s5_tmajor'] else ys.reshape(n, D), ym, h, W['w_out_rwkv'],
                    W['s5_w_glu'], W['w_out_s5'], W['w_out_mamba'], W['w_out'], W['g_mix_post'],
                    seq=(B, T) if grp['s5_tmajor'] else None)

    if grp['xa_nb'] == 1 and st['mk'].ndim == 3:
        h = cross_attention_block(h, W['g_xa_pre'], W['xa_wq'], st['mk'], st['mv'], W['xa_wo'], W['g_xa_post'],
                                  B=B, T=T, tq=grp['xa_tq'])
    else:
        q = norm_matmul(h, W['g_xa_pre'], W['xa_wq'], name="xa_q")
        o = cross_attention(q, st['mk'], st['mv'], st['kv_index'](l), B=B, T=T, tq=grp['xa_tq'],
                            nb=grp['xa_nb'])
        h = matmul_norm_residual(o, W['xa_wo'], W['g_xa_post'], h, name="xa_out")

    h = mlp_block(h, W['g_mlp_pre'], W['mlp_w1'], W['mlp_w2'], W['g_mlp_post'])
    return h, (s_fin, shift_new, s5r.reshape(B, 64, 64), s5i.reshape(B, 64, 64), conv_new, ssm_fin)


def _group_cfg(T, B):
    cfg = dict(T=T, B=B)
    g = 8 if B % 8 == 0 else 1
    if T % 64 == 0:
        cfg.update(rwkv_C=64, rwkv_Tpad=T, rwkv_G=g, rwkv_NP=2)
    else:
        tp = -(-T // 8) * 8
        cfg.update(rwkv_C=tp, rwkv_Tpad=tp, rwkv_G=16 if B % 16 == 0 else g, rwkv_NP=2)
    if T % 8 == 0 and (512 // B) >= 8 and T % (512 // B) == 0:
        tc = 512 // B
        cfg.update(s5_tmajor=True, s5_tc=tc, s5_blk=(tc, B))
    else:
        cfg.update(s5_tmajor=False, s5_tc=T, s5_blk=(1, B * T))
    if T % 128 == 0:
        cfg.update(ssd_L=128, ssd_Tpad=T, ssd_G=1)
    else:
        tp = -(-T // 8) * 8
        cfg.update(ssd_L=tp, ssd_Tpad=tp, ssd_G=4 if B % 4 == 0 else 1)
    if T >= 64:
        cfg.update(xa_tq=_row_tile(T, 512), xa_nb=1)
    else:
        cfg.update(xa_tq=T, xa_nb=4 if B % 4 == 0 else 1)
    return cfg


def kernel(x_prompt, x_sample, cache_mem_k, cache_mem_v, state_rwkv, state_rwkv_shift, state_s5_re, state_s5_im, state_conv, state_ssm, mem_prompt, norm_mix_pre, norm_mix_post, norm_xa_pre, norm_xa_post, norm_mlp_pre, norm_mlp_post, norm_mem, w_in, w_out, rwkv_mu, rwkv_w0, rwkv_w2, rwkv_a0, rwkv_a2, rwkv_g2, rwkv_k_k, rwkv_k_a, rwkv_r_k, rwkv_ln_w, rwkv_ln_b, w_out_rwkv, s5_a_re, s5_a_im, s5_log_dt, s5_b_re, s5_b_im, s5_c_re, s5_c_im, s5_d, s5_w_glu, w_out_s5, m_conv_w, m_conv_b, m_dt_bias, m_a_log, m_d, m_norm, w_out_mamba, xa_wq, xa_wk, xa_wv, xa_wo, mlp_w1, mlp_w2):
    depth = w_in.shape[0]
    bp, tp, _ = x_prompt.shape
    bs, ts, _ = x_sample.shape
    mlen = mem_prompt.shape[1]
    bf = lambda x: x.astype(_MXU)
    row = lambda x: x.reshape(1, -1).astype(F32)

    eh = (jnp.arange(128)[:, None] == (jnp.arange(M_WIDTH)[None, :] // M_HEAD)).astype(_MXU)

    col = [0]
    for sz in (3 * D, 3 * D + 256, D, M_WIDTH, M_CONV_DIM, 32):
        col.append(col[-1] + sz)
    layers = []
    w_in_b = bf(w_in)
    big = {name: bf(w) for name, w in dict(
        w_out_rwkv=w_out_rwkv, s5_w_glu=s5_w_glu, w_out_s5=w_out_s5, w_out_mamba=w_out_mamba, w_out=w_out,
        xa_wq=xa_wq, xa_wk=xa_wk, xa_wv=xa_wv, xa_wo=xa_wo, mlp_w1=mlp_w1, mlp_w2=mlp_w2).items()}
    for l in range(depth):
        wl = w_in_b[l]
        ar, ai, bre, bim, cre, cim = _s5_discretize(s5_a_re[l], s5_a_im[l], s5_log_dt[l], s5_b_re[l], s5_b_im[l],
                                                    s5_c_re[l], s5_c_im[l])
        zero = jnp.zeros((64, D), F32)
        wa = jnp.concatenate([jnp.concatenate([rwkv_w2[l], zero], axis=1),
                              jnp.concatenate([zero, rwkv_a2[l]], axis=1)], axis=0)
        pad32 = lambda x: jnp.pad(x.reshape(1, -1).astype(F32), ((0, 0), (0, 128 - x.shape[-1])))
        layers.append(dict(
            g_mix_pre=row(norm_mix_pre[l]), g_mix_post=row(norm_mix_post[l]),
            g_xa_pre=row(norm_xa_pre[l]), g_xa_post=row(norm_xa_post[l]),
            g_mlp_pre=row(norm_mlp_pre[l]), g_mlp_post=row(norm_mlp_post[l]),
            w_gates=bf(wl[:, col[0]:col[1]]), w_rwkv=bf(wl[:, col[1]:col[2]]), w_s5=bf(wl[:, col[2]:col[3]]),
            w_z=bf(wl[:, col[3]:col[4]]), w_xbc=bf(wl[:, col[4]:col[5]]),
            w_dt=bf(jnp.pad(wl[:, col[5]:col[6]], ((0, 0), (0, 96)))),
            mu=row(rwkv_mu[l]), wa=bf(wa), w0=row(rwkv_w0[l]), a0=row(rwkv_a0[l]), g2=bf(rwkv_g2[l]),
            k_k=row(rwkv_k_k[l]), k_a=row(rwkv_k_a[l]), r_k=row(rwkv_r_k[l]),
            ln_w=row(rwkv_ln_w[l]), ln_b=row(rwkv_ln_b[l]),
            s5_ar=ar, s5_ai=ai, s5_bre=bre, s5_bim=bim, s5_cre=cre, s5_cim=cim, s5_d=row(s5_d[l]),
            conv_w=m_conv_w[l].astype(F32), conv_b=row(m_conv_b[l]), dt_bias=pad32(m_dt_bias[l]),
            a_log=pad32(m_a_log[l]), d_x=row(jnp.repeat(m_d[l], M_HEAD)), m_norm=row(m_norm[l]), eh=eh,
            g_mem=row(norm_mem[l]), **{name: w[l] for name, w in big.items()},
        ))

    hp = x_prompt.reshape(bp * tp, D).astype(F32)
    hs = x_sample.reshape(bs * ts, D).astype(F32)
    mem2 = mem_prompt.reshape(bp * mlen, D).astype(F32)

    cfg_p = _group_cfg(tp, bp)
    cfg_s = _group_cfg(ts, bs)
    zeros_p = dict(
        rwkv=jnp.zeros((depth, bp) + state_rwkv.shape[2:], F32),
        shift=jnp.zeros((depth, bp, state_rwkv_shift.shape[-1]), F32),
        s5r=jnp.zeros((depth, bp, state_s5_re.shape[2] * state_s5_re.shape[3]), F32),
        s5i=jnp.zeros((depth, bp, state_s5_re.shape[2] * state_s5_re.shape[3]), F32),
        conv=jnp.zeros((depth, bp) + state_conv.shape[2:], F32),
        ssm=jnp.zeros((depth, bp) + state_ssm.shape[2:], F32),
    )
    st_s = dict(
        rwkv=state_rwkv, shift=state_rwkv_shift,
        s5r=state_s5_re.reshape(depth, bs, -1), s5i=state_s5_im.reshape(depth, bs, -1),
        conv=state_conv, ssm=state_ssm,
        mk=cache_mem_k, mv=cache_mem_v,
        kv_index=lambda l: (l,),
    )

    mk_out, mv_out = [], []
    st_p_out = [[] for _ in range(6)]
    st_s_out = [[] for _ in range(6)]
    acc_p = (jnp.zeros_like(zeros_p['rwkv']), jnp.zeros_like(zeros_p['ssm']))
    acc_s = (jnp.zeros_like(state_rwkv, dtype=F32), jnp.zeros_like(state_ssm, dtype=F32))
    for l in range(depth):
        W = layers[l]
        mk = norm_matmul(mem2, W['g_mem'], W['xa_wk'], name="mem_k")
        mv = norm_matmul(mem2, W['g_mem'], W['xa_wv'], name="mem_v")
        st_p = dict(zeros_p, mk=mk.reshape(bp, mlen, D), mv=mv.reshape(bp, mlen, D), kv_index=lambda l: ())
        hp, new_p = _layer(l, hp, cfg_p, W, st_p, acc_p)
        hs, new_s = _layer(l, hs, cfg_s, W, st_s, acc_s)
        acc_p = (new_p[0], new_p[5])
        acc_s = (new_s[0], new_s[5])
        mk_out.append(mk.reshape(bp, mlen, XA_HEADS, XA_HD))
        mv_out.append(mv.reshape(bp, mlen, XA_HEADS, XA_HD))
        for i in range(1, 5):
            st_p_out[i].append(new_p[i])
            st_s_out[i].append(new_s[i])

    stk = lambda xs: jnp.stack(xs).astype(F32)
    y_p = hp.reshape(bp, tp, D).astype(x_prompt.dtype)
    y_s = hs.reshape(bs, ts, D).astype(x_sample.dtype)
    return (y_p, y_s, stk(mk_out), stk(mv_out),
            acc_p[0], *[stk(st_p_out[i]) for i in range(1, 5)], acc_p[1],
            acc_s[0], *[stk(st_s_out[i]) for i in range(1, 5)], acc_s[1])
```

```python
import functools
import math

import numpy as np
import jax
import jax.numpy as jnp
from jax import lax
from jax.experimental import pallas as pl
from jax.experimental.pallas import tpu as pltpu

F32 = jnp.float32
_MXU = jnp.bfloat16
EPS = 1e-6
R_LN_EPS = 64e-5
D = 1024
R_HEAD = 64
R_PAIR = 2 * R_HEAD
S5_BLK_CH = 128
S5_BLK_ST = 512
M_HEAD = 64
M_STATE = 128
M_GROUPS = 4
M_HPG = 8
M_GW = M_HPG * M_HEAD
M_WIDTH = 2048
M_CONV_DIM = 3072
XA_HEADS = 4
XA_HD = 256
VMEM_LIMIT = 56 * 1024 * 1024


def _mm(a, b):
    return jnp.dot(a.astype(_MXU), b.astype(_MXU), preferred_element_type=F32)


def _mm_nt(a, b):
    return lax.dot_general(a.astype(_MXU), b.astype(_MXU), (((1,), (1,)), ((), ())),
                           preferred_element_type=F32)


def _mm_tn(a, b):
    return lax.dot_general(a.astype(_MXU), b.astype(_MXU), (((0,), (0,)), ((), ())),
                           preferred_element_type=F32)


def _split3(x):
    hi = x.astype(_MXU)
    r1 = x - hi.astype(F32)
    mid = r1.astype(_MXU)
    lo = (r1 - mid.astype(F32)).astype(_MXU)
    return hi, mid, lo


def _mm_sel_l(sel, x):
    return sum(jnp.dot(sel, p, preferred_element_type=F32) for p in _split3(x))


def _mm_sel_r(x, sel):
    return sum(jnp.dot(p, sel, preferred_element_type=F32) for p in _split3(x))


def _mm_nt_sel_l(sel, x):
    return sum(lax.dot_general(sel, p, (((1,), (1,)), ((), ())), preferred_element_type=F32)
               for p in _split3(x))


def _sigmoid(x):
    return 1.0 / (1.0 + jnp.exp(-x))


def _softplus(x):
    return jnp.maximum(x, 0.0) + jnp.log(1.0 + jnp.exp(-jnp.abs(x)))


def _rms(x, g):
    return x * lax.rsqrt(jnp.mean(x * x, axis=-1, keepdims=True) + EPS) * g


def _iota(shape, dim):
    return lax.broadcasted_iota(jnp.int32, shape, dim)


def _eye(n, dtype):
    return jnp.where(_iota((n, n), 0) == _iota((n, n), 1), 1.0, 0.0).astype(dtype)


def _cparams(sem):
    return pltpu.CompilerParams(dimension_semantics=sem, vmem_limit_bytes=VMEM_LIMIT)


def _row_tile(n, want):
    t = min(n, want)
    while n % t:
        t //= 2
    return t


def _col_tile(c, cap):
    best = 128
    for t in range(128, min(c, cap) + 1, 128):
        if c % t == 0:
            best = t
    return best


def _const_spec(shape):
    nd = len(shape)
    return pl.BlockSpec(shape, lambda *_: (0,) * nd, pipeline_mode=pl.Buffered(1))


def _norm_mm_kernel(x_ref, g_ref, w_ref, o_ref, xn_ref):
    @pl.when(pl.program_id(1) == 0)
    def _():
        xn_ref[...] = _rms(x_ref[...], g_ref[...]).astype(_MXU)

    o_ref[...] = jnp.dot(xn_ref[...], w_ref[...], preferred_element_type=F32)


def norm_matmul(x, g, w, *, tm=2048, tn=896, name="norm_mm"):
    n, d = x.shape
    c = w.shape[1]
    tm = _row_tile(n, tm)
    tn = _col_tile(c, tn)
    return pl.pallas_call(
        _norm_mm_kernel,
        out_shape=jax.ShapeDtypeStruct((n, c), F32),
        grid=(n // tm, c // tn),
        in_specs=[pl.BlockSpec((tm, d), lambda i, j: (i, 0)),
                  pl.BlockSpec((1, d), lambda i, j: (0, 0)),
                  pl.BlockSpec((d, tn), lambda i, j: (0, j))],
        out_specs=pl.BlockSpec((tm, tn), lambda i, j: (i, j)),
        scratch_shapes=[pltpu.VMEM((tm, d), _MXU)],
        compiler_params=_cparams(("parallel", "arbitrary")),
        name=name,
    )(x, g, w)


def _mm_norm_res_kernel(x_ref, w_ref, g_ref, h_ref, o_ref):
    y = _mm(x_ref[...], w_ref[...])
    o_ref[...] = h_ref[...] + _rms(y, g_ref[...])


def matmul_norm_residual(x, w, g, h, *, tm=512, name="mm_norm_res"):
    n, k = x.shape
    d = w.shape[1]
    tm = _row_tile(n, tm)
    return pl.pallas_call(
        _mm_norm_res_kernel,
        out_shape=jax.ShapeDtypeStruct((n, d), F32),
        grid=(n // tm,),
        in_specs=[pl.BlockSpec((tm, k), lambda i: (i, 0)),
                  _const_spec((k, d)),
                  _const_spec((1, d)),
                  pl.BlockSpec((tm, d), lambda i: (i, 0))],
        out_specs=pl.BlockSpec((tm, d), lambda i: (i, 0)),
        compiler_params=_cparams(("parallel",)),
        name=name,
    )(x, w, g, h)


def _mlp_kernel(h_ref, g1_ref, w1_ref, w2_ref, g2_ref, o_ref, *, n_chunks, ck):
    h = h_ref[...]
    xn = _rms(h, g1_ref[...]).astype(_MXU)
    acc = jnp.zeros(h.shape, F32)
    for j in range(n_chunks):
        a = jnp.dot(xn, w1_ref[:, j * ck:(j + 1) * ck], preferred_element_type=F32)
        a = jnp.square(jnp.maximum(a, 0.0))
        acc = acc + jnp.dot(a.astype(_MXU), w2_ref[j * ck:(j + 1) * ck, :], preferred_element_type=F32)
    o_ref[...] = h + _rms(acc, g2_ref[...])


def mlp_block(h, g1, w1, w2, g2, *, tm=1024, ck=1024):
    n, d = h.shape
    f = w1.shape[1]
    tm = _row_tile(n, tm)
    return pl.pallas_call(
        functools.partial(_mlp_kernel, n_chunks=f // ck, ck=ck),
        out_shape=jax.ShapeDtypeStruct((n, d), F32),
        grid=(n // tm,),
        in_specs=[pl.BlockSpec((tm, d), lambda i: (i, 0)),
                  _const_spec((1, d)),
                  _const_spec((d, f)),
                  _const_spec((f, d)),
                  _const_spec((1, d))],
        out_specs=pl.BlockSpec((tm, d), lambda i: (i, 0)),
        compiler_params=_cparams(("parallel",)),
        name="mlp",
    )(h, g1, w1, w2, g2)


def _norm_mm_tmajor_kernel(x_ref, g_ref, w_ref, perm_ref, o_ref):
    nb, tt, d = x_ref.shape
    xn = _rms(x_ref[...].reshape(nb * tt, d), g_ref[...]).astype(_MXU)
    xt = jnp.dot(perm_ref[...], xn, preferred_element_type=F32).astype(_MXU)
    o_ref[...] = jnp.dot(xt, w_ref[...], preferred_element_type=F32).reshape(o_ref.shape)


def norm_matmul_tmajor(x3, g, w, *, rows=512, name="norm_mm_t"):
    nb, t, d = x3.shape
    c = w.shape[1]
    tt = rows // nb
    assert t % tt == 0 and tt % 8 == 0
    perm = jnp.asarray(_time_major_perm(nb, tt), _MXU)
    return pl.pallas_call(
        _norm_mm_tmajor_kernel,
        out_shape=jax.ShapeDtypeStruct((t, nb, c), F32),
        grid=(t // tt,),
        in_specs=[pl.BlockSpec((nb, tt, d), lambda i: (0, i, 0)),
                  _const_spec((1, d)), _const_spec((d, c)), _const_spec((nb * tt, nb * tt))],
        out_specs=pl.BlockSpec((tt, nb, c), lambda i: (i, 0, 0)),
        compiler_params=_cparams(("parallel",)),
        name=name,
    )(x3, g, w, perm)


def _merge_kernel(gates_ref, yr_ref, ys_ref, ym_ref, h_ref, wr_ref, wglu_ref, ws_ref, wm_ref, wo_ref, g_ref,
                  *rest, ys_tmajor):
    o_ref = rest[-1]
    rows2d = lambda ref: ref[...].reshape(-1, ref.shape[-1])
    o_r = _mm(rows2d(yr_ref), wr_ref[...])
    ys = rows2d(ys_ref)
    y3 = ys * _sigmoid(_mm(ys, wglu_ref[...]))
    if ys_tmajor:
        y3 = jnp.dot(rest[0][...], y3.astype(_MXU), preferred_element_type=F32)
    o_s = _mm(y3, ws_ref[...])
    o_m = _mm(rows2d(ym_ref), wm_ref[...])
    gates = rows2d(gates_ref)
    merged = (_sigmoid(gates[:, 0:D]) * o_r + _sigmoid(gates[:, D:2 * D]) * o_s
              + _sigmoid(gates[:, 2 * D:3 * D]) * o_m)
    mix = _mm(merged, wo_ref[...])
    o_ref[...] = (rows2d(h_ref) + _rms(mix, g_ref[...])).reshape(o_ref.shape)


def mixer_merge(gates, yr, ys, ym, h, wr, wglu, ws, wm, wo, g, *, tm=256, seq=None):
    n = h.shape[0]
    weights = [_const_spec((D, D)), _const_spec((D, D)), _const_spec((D, D)),
               _const_spec((M_WIDTH, D)), _const_spec((D, D)), _const_spec((1, D))]
    if seq is None:
        tm = _row_tile(n, tm)
        row = lambda w: pl.BlockSpec((tm, w), lambda i: (i, 0))
        in_specs = [row(3 * D), row(D), row(D), row(M_WIDTH), row(D)] + weights
        args = [gates, yr, ys, ym, h, wr, wglu, ws, wm, wo, g]
        out_spec, out_shape, grid = row(D), (n, D), (n // tm,)
    else:
        nb, t = seq
        tt = tm // nb
        assert t % tt == 0 and tt % 8 == 0
        row = lambda w: pl.BlockSpec((nb, tt, w), lambda i: (0, i, 0))
        v3 = lambda x: x.reshape(nb, t, x.shape[-1])
        in_specs = ([row(3 * D), row(D), pl.BlockSpec((tt, nb, D), lambda i: (i, 0, 0)), row(M_WIDTH), row(D)]
                    + weights + [_const_spec((nb * tt, nb * tt))])
        args = [v3(gates), v3(yr), ys, v3(ym), v3(h), wr, wglu, ws, wm, wo, g,
                jnp.asarray(_time_major_perm(nb, tt).T, _MXU)]
        out_spec, out_shape, grid = row(D), (nb, t, D), (t // tt,)
    out = pl.pallas_call(
        functools.partial(_merge_kernel, ys_tmajor=seq is not None),
        out_shape=jax.ShapeDtypeStruct(out_shape, F32),
        grid=grid,
        in_specs=in_specs,
        out_specs=out_spec,
        compiler_params=_cparams(("parallel",)),
        name="mixer_merge",
    )(*args)
    return out.reshape(n, D)


def _rwkv_rec_kernel(pr_ref, pk_ref, pv_ref, pl_ref, shr_ref, shk_ref, shv_ref, shl_ref,
                     mur_ref, muk_ref, muv_ref, mul_ref, w2_ref, a2_ref, g2_ref, w0_ref, a0_ref, kk_ref, ka_ref,
                     s0_ref, rk_ref, lnw_ref, lnb_ref,
                     y_ref, sfin_ref, s_scr, carry_scr, *, C, G, NP, n_steps, t_real):
    step = pl.program_id(2)
    C2 = 2 * C
    W = NP * R_PAIR
    probs = [(gi, pp) for gi in range(G) for pp in range(NP)]
    rng = range(len(probs))
    row_c = _iota((C, 1), 0)

    @pl.when(step == 0)
    def _():
        for j, sh_ref in enumerate((shr_ref, shk_ref, shv_ref, shl_ref)):
            carry_scr[j] = sh_ref[...]

    def shift_mix(j, p_ref, mu_ref):
        out = []
        for gi in range(G):
            x = p_ref[gi]
            prev = jnp.where(row_c == 0, carry_scr[j, gi:gi + 1, :], pltpu.roll(x, 1, axis=0))
            carry_scr[j, gi:gi + 1, :] = x[C - 1:C, :]
            out.append(x + mu_ref[...] * (prev - x))
        return out

    pm_r = shift_mix(0, pr_ref, mur_ref)
    pm_k = shift_mix(1, pk_ref, muk_ref)
    pm_v = shift_mix(2, pv_ref, muv_ref)
    xl = jnp.concatenate(shift_mix(3, pl_ref, mul_ref), axis=0)
    x_wa = xl[:, 0:128]
    lora_in = jnp.where(_iota(x_wa.shape, 1) < 64, jnp.tanh(x_wa), x_wa)
    w_log = -_softplus(-(w0_ref[...] + _mm(lora_in, w2_ref[...]))) - 0.5
    e_all = jnp.exp(w_log)
    a_all = _sigmoid(a0_ref[...] + _mm(lora_in, a2_ref[...]))
    g_all = _mm(_sigmoid(xl[:, 128:256]), g2_ref[...])
    valid = row_c < t_real

    def seq_rows(x, gi, keep_pad=False):
        x = x[gi * C:(gi + 1) * C]
        return x if (keep_pad or t_real >= C) else jnp.where(valid, x, 0.0)

    def pad0(x):
        return x if t_real >= C else jnp.where(valid, x, 0.0)

    lanes = lambda x, pp: x[:, pp * R_PAIR:(pp + 1) * R_PAIR]
    r, e, k, v, kk, a_l, g_l = [], [], [], [], [], [], []
    for gi, pp in probs:
        a_q = lanes(seq_rows(a_all, gi, True), pp)
        k_raw = lanes(pad0(pm_k[gi]), pp)
        r.append(lanes(pad0(pm_r[gi]), pp))
        e.append(lanes(seq_rows(e_all, gi), pp))
        k.append(k_raw * (1.0 + (a_q - 1.0) * lanes(ka_ref[...], pp)))
        v.append(lanes(pad0(pm_v[gi]), pp))
        kk.append(k_raw * lanes(kk_ref[...], pp))
        a_l.append(a_q)
        g_l.append(lanes(seq_rows(g_all, gi, True), pp))

    def par(ref, q):
        pp = probs[q][1]
        return ref[:, pp * R_PAIR:(pp + 1) * R_PAIR]

    lane = _iota((1, R_PAIR), 1)
    m0 = lane < R_HEAD
    ri = _iota((R_PAIR, R_PAIR), 0)
    ci = _iota((R_PAIR, R_PAIR), 1)
    bd = (ri < R_HEAD) == (ci < R_HEAD)
    rt_i = _iota((C2, C2), 0)
    ct_i = _iota((C2, C2), 1)
    same = (rt_i < C) == (ct_i < C)
    tr = jnp.where(rt_i < C, rt_i, rt_i - C)
    ts = jnp.where(ct_i < C, ct_i, ct_i - C)
    m_sl = same & (ts < tr)
    m_li = same & (ts <= tr)
    eye2 = jnp.where(rt_i == ct_i, 1.0, 0.0)
    lvl_masks = []
    m = 1
    while m < C:
        lvl_masks.append(((tr & ~(2 * m - 1)) == (ts & ~(2 * m - 1))) & ((tr & m) != 0) & ((ts & m) == 0))
        m *= 2
    tril = jnp.where(_iota((C, C), 1) <= _iota((C, C), 0), 1.0, 0.0).astype(_MXU)

    @pl.when(step == 0)
    def _():
        z = jnp.zeros((R_HEAD, R_HEAD), F32)
        for q, (gi, pp) in enumerate(probs):
            s_scr[q] = jnp.concatenate([jnp.concatenate([s0_ref[gi, 2 * pp], z], axis=1),
                                        jnp.concatenate([z, s0_ref[gi, 2 * pp + 1]], axis=1)], axis=0)

    def stack(x):
        return jnp.concatenate([jnp.where(m0, x, 0.0), jnp.where(m0, 0.0, x)], axis=0)

    def fold(xs):
        return xs[:C] + xs[C:]

    def each(f, *lists):
        return [f(*xs) for xs in zip(*lists)]

    def head_sum(x):
        s0 = jnp.sum(jnp.where(m0, x, 0.0), axis=-1, keepdims=True)
        s1 = jnp.sum(jnp.where(m0, 0.0, x), axis=-1, keepdims=True)
        return jnp.where(m0, s0, s1)

    an = each(lambda x: -x * lax.rsqrt(head_sum(x * x) + 1e-12), kk)
    bn = each(lambda n_, a_: -n_ * a_, an, a_l)
    cs = each(lambda x: _mm_sel_l(tril, x), e)
    cl = each(lambda x: x[C - 1:C, :], cs)
    at = each(lambda a_, e_, c_: a_ * jnp.exp(e_ - c_), an, e, cs)
    rt = each(lambda r_, c_: r_ * jnp.exp(-c_), r, cs)
    ecs = each(jnp.exp, cs)
    bt_ = each(lambda b_, x: b_ * x, bn, ecs)
    kt = each(lambda k_, x: k_ * x, k, ecs)
    ecl = each(lambda c_, l_: jnp.exp(c_ - l_), cs, cl)
    bh = each(lambda b_, x: b_ * x, bn, ecl)
    kh = each(lambda k_, x: k_ * x, k, ecl)
    As, Rs, Vs = each(stack, at), each(stack, rt), each(stack, v)
    b2 = each(lambda x: jnp.concatenate([x, x], axis=0), bt_)
    k2 = each(lambda x: jnp.concatenate([x, x], axis=0), kt)
    N = each(lambda a_, b_: jnp.where(m_sl, _mm_nt(a_, b_), 0.0), As, b2)
    Ak = each(lambda a_, b_: jnp.where(m_sl, _mm_nt(a_, b_), 0.0), As, k2)
    Arb = each(lambda a_, b_: jnp.where(m_li, _mm_nt(a_, b_), 0.0), Rs, b2)
    Ark = each(lambda a_, b_: jnp.where(m_li, _mm_nt(a_, b_), 0.0), Rs, k2)
    T = each(lambda n_: eye2 + jnp.where(lvl_masks[0], n_, 0.0), N)
    for lm in lvl_masks[1:]:
        LT = each(lambda n_, t_: _mm(jnp.where(lm, n_, 0.0), t_), N, T)
        T = each(lambda t_, x: t_ + _mm(t_, x), T, LT)
    AkV = each(_mm, Ak, Vs)
    X = each(lambda t_, a_, u_: _mm(t_, jnp.concatenate([a_, u_], axis=1)), T, As, AkV)
    Z = each(_mm, Arb, X)
    ArkV = each(_mm, Ark, Vs)
    Rp = each(lambda r_, z_: r_ + fold(z_[:, 0:R_PAIR]), rt, Z)
    Y0 = each(lambda z_, a_: fold(z_[:, R_PAIR:] + a_), Z, ArkV)
    Ap = each(lambda x: fold(x[:, 0:R_PAIR]), X)
    U0 = each(lambda x: fold(x[:, R_PAIR:]), X)
    P = each(lambda a_, b_: jnp.where(bd, _mm_tn(a_, b_), 0.0), Ap, bh)
    Q = each(lambda u_, v_, b_, k_: jnp.where(bd, _mm_tn(jnp.concatenate([u_, v_], axis=0),
                                                         jnp.concatenate([b_, k_], axis=0)), 0.0), U0, v, bh, kh)
    S = [s_scr[q] for q in rng]
    y = each(lambda y_, r_, s_: y_ + _mm_nt(r_, s_), Y0, Rp, S)
    S = each(lambda s_, l_, p_, q_: s_ * jnp.exp(-l_) + _mm(s_, p_) + q_, S, cl, P, Q)
    for q in rng:
        s_scr[q] = S[q]
    dlt = each(lambda y_: y_ - head_sum(y_) * (1.0 / R_HEAD), y)
    var = each(lambda d_: head_sum(d_ * d_) * (1.0 / R_HEAD), dlt)
    bonus = each(lambda r_, k_, v_, q: head_sum(r_ * k_ * par(rk_ref, q)) * v_, r, k, v, rng)
    for q, (gi, pp) in enumerate(probs):
        yn = dlt[q] * lax.rsqrt(var[q] + R_LN_EPS) * par(lnw_ref, q) + par(lnb_ref, q)
        y_ref[gi, :, pp * R_PAIR:(pp + 1) * R_PAIR] = (yn + bonus[q]) * g_l[q]

    @pl.when(step == n_steps - 1)
    def _():
        for q, (gi, pp) in enumerate(probs):
            sfin_ref[gi, 2 * pp] = S[q][0:R_HEAD, 0:R_HEAD]
            sfin_ref[gi, 2 * pp + 1] = S[q][R_HEAD:, R_HEAD:]


def _stacked_out(kernel_fn, in_specs, args, out_index, stack):
    pos = len(args)

    def with_alias(*refs):
        return kernel_fn(*refs[:pos], *refs[pos + 1:])

    return with_alias, in_specs + [pl.BlockSpec(memory_space=pl.ANY)], args + [stack], {pos: out_index}


def rwkv_recurrence(p3, shift_prev, mu, wa, g2, w0, a0, k_k, k_a, s0, l, r_k, ln_w, ln_b, stack, *, C, G, NP,
                    t_real):
    B, T, _ = p3.shape
    n_steps = T // C
    n_pairs = D // R_PAIR
    W = NP * R_PAIR
    assert W == 256 and t_real <= C and (t_real == C or n_steps == 1)
    seg = D // W
    cblk = lambda off: pl.BlockSpec((G, C, W), lambda b, p, s: (b, s, off + p))
    sblk = lambda off: pl.BlockSpec((G, W), lambda b, p, s: (b, off + p))
    mblk = lambda off: pl.BlockSpec((1, W), lambda b, p, s: (0, off + p))
    lora_c = pl.BlockSpec((G, C, W), lambda b, p, s: (b, s, 3 * seg))
    lora_s = pl.BlockSpec((G, W), lambda b, p, s: (b, 3 * seg))
    lora_m = pl.BlockSpec((1, W), lambda b, p, s: (0, 3 * seg))
    wblk = lambda off: pl.BlockSpec((128, W), lambda b, p, s: (0, off + p))
    par = lambda: pl.BlockSpec((1, W), lambda b, p, s: (0, p))
    out_blk = pl.BlockSpec((G, C, W), lambda b, p, s: (b, s, p))
    st_spec = lambda: pl.BlockSpec((None, G, 2 * NP, R_HEAD, R_HEAD), lambda b, p, s: (l, b, p, 0, 0))
    in_specs = ([cblk(0), cblk(seg), cblk(2 * seg), lora_c,
                 sblk(0), sblk(seg), sblk(2 * seg), lora_s,
                 mblk(0), mblk(seg), mblk(2 * seg), lora_m,
                 wblk(0), wblk(seg), wblk(0)]
                + [par() for _ in range(4)] + [st_spec(), par(), par(), par()])
    kern, in_specs, args, aliases = _stacked_out(
        functools.partial(_rwkv_rec_kernel, C=C, G=G, NP=NP, n_steps=n_steps, t_real=t_real), in_specs,
        [p3, p3, p3, p3, shift_prev, shift_prev, shift_prev, shift_prev, mu, mu, mu, mu, wa, wa, g2,
         w0, a0, k_k, k_a, s0, r_k, ln_w, ln_b], 1, stack)
    y, sfin = pl.pallas_call(
        kern,
        out_shape=[jax.ShapeDtypeStruct((B, T, D), F32),
                   jax.ShapeDtypeStruct((s0.shape[0], B, 2 * n_pairs, R_HEAD, R_HEAD), F32)],
        grid=(B // G, n_pairs // NP, n_steps),
        in_specs=in_specs,
        out_specs=[out_blk, st_spec()],
        scratch_shapes=[pltpu.VMEM((G * NP, R_PAIR, R_PAIR), F32), pltpu.VMEM((4, G, W), F32)],
        input_output_aliases=aliases,
        compiler_params=_cparams(("parallel", "parallel", "arbitrary")),
        name="rwkv_rec",
    )(*args)
    return y, sfin


def _s5_kernel(u_ref, h0r_ref, h0i_ref, ar_ref, ai_ref, bre_ref, bim_ref, cre_ref, cim_ref, d_ref, *rest,
               tc, nb, kb, n_steps, permute):
    if permute:
        perm_ref, permt_ref = rest[:2]
        rest = rest[2:]
    y_ref, hr_out, hi_out, inr_scr, ini_scr, hr_scr, hi_scr = rest
    step = pl.program_id(1)
    rows = nb * tc

    @pl.when(step == 0)
    def _():
        hr_scr[...] = h0r_ref[...]
        hi_scr[...] = h0i_ref[...]

    u = u_ref[...].reshape(rows, kb * S5_BLK_CH)
    ch = lambda j: slice(j * S5_BLK_CH, (j + 1) * S5_BLK_CH)
    stt = lambda j: slice(j * S5_BLK_ST, (j + 1) * S5_BLK_ST)
    if permute:
        ut = jnp.dot(perm_ref[...], u.astype(_MXU), preferred_element_type=F32).astype(_MXU)
    else:
        ut = u.astype(_MXU)
    for j in range(kb):
        inr_scr[:, stt(j)] = jnp.dot(ut[:, ch(j)], bre_ref[j], preferred_element_type=F32)
        ini_scr[:, stt(j)] = jnp.dot(ut[:, ch(j)], bim_ref[j], preferred_element_type=F32)
    ar = ar_ref[...]
    ai = ai_ref[...]

    def body(t, carry):
        hr, hi = carry
        sl = pl.ds(pl.multiple_of(t * nb, nb), nb)
        nr = ar * hr - ai * hi + inr_scr[sl, :]
        ni = ar * hi + ai * hr + ini_scr[sl, :]
        inr_scr[sl, :] = nr
        ini_scr[sl, :] = ni
        return nr, ni

    hr, hi = lax.fori_loop(0, tc, body, (hr_scr[...], hi_scr[...]))
    hr_scr[...] = hr
    hi_scr[...] = hi
    yts = [jnp.dot(inr_scr[:, stt(j)].astype(_MXU), cre_ref[j], preferred_element_type=F32)
           - jnp.dot(ini_scr[:, stt(j)].astype(_MXU), cim_ref[j], preferred_element_type=F32) for j in range(kb)]
    yt = yts[0] if kb == 1 else jnp.concatenate(yts, axis=1)
    if permute:
        yt_hi = yt.astype(_MXU)
        yt_lo = (yt - yt_hi.astype(F32)).astype(_MXU)
        yt = (jnp.dot(permt_ref[...], yt_hi, preferred_element_type=F32)
              + jnp.dot(permt_ref[...], yt_lo, preferred_element_type=F32))
    y = yt + d_ref[...] * u
    y = 0.5 * y * (1.0 + jnp.tanh(math.sqrt(2.0 / math.pi) * (y + 0.044715 * (y * y * y))))
    y_ref[...] = y.reshape(y_ref.shape)

    @pl.when(step == n_steps - 1)
    def _():
        hr_out[...] = hr
        hi_out[...] = hi


def _time_major_perm(nb, tc):
    rows = nb * tc
    j = np.arange(rows)
    perm_np = np.zeros((rows, rows), np.float32)
    perm_np[(j % tc) * nb + j // tc, j] = 1.0
    return perm_np


def s5_scan(u3, h0r, h0i, l, ar, ai, bre, bim, cre, cim, d, *, nb, tc, blk, kb):
    A, R, _ = u3.shape
    ga, tr = blk
    assert ga * tr == nb * tc
    tmajor = (ga, tr) == (tc, nb) and R == nb
    nblk = D // S5_BLK_CH
    n_steps = A // ga if tmajor else R // tr
    rows = nb * tc
    u_spec = (pl.BlockSpec((ga, tr, kb * S5_BLK_CH), lambda c, s: (s, 0, c)) if tmajor
              else pl.BlockSpec((ga, tr, kb * S5_BLK_CH), lambda c, s: (0, s, c)))
    if tmajor:
        perm_specs, perm_args = [], []
    else:
        assert A == ga
        perm_np = _time_major_perm(nb, tc)
        perm_specs = [_const_spec((rows, rows)), _const_spec((rows, rows))]
        perm_args = [jnp.asarray(perm_np, _MXU), jnp.asarray(perm_np.T, _MXU)]
    st = lambda: pl.BlockSpec((None, nb, kb * S5_BLK_ST), lambda c, s: (l, 0, c))
    vec = lambda w: pl.BlockSpec((1, kb * w), lambda c, s: (0, c))
    mat = lambda a, b: pl.BlockSpec((kb, a, b), lambda c, s: (c, 0, 0))
    return pl.pallas_call(
        functools.partial(_s5_kernel, tc=tc, nb=nb, kb=kb, n_steps=n_steps, permute=not tmajor),
        out_shape=[jax.ShapeDtypeStruct(u3.shape, F32),
                   jax.ShapeDtypeStruct((nb, nblk * S5_BLK_ST), F32),
                   jax.ShapeDtypeStruct((nb, nblk * S5_BLK_ST), F32)],
        grid=(nblk // kb, n_steps),
        in_specs=[u_spec,
                  st(), st(), vec(S5_BLK_ST), vec(S5_BLK_ST),
                  mat(S5_BLK_CH, S5_BLK_ST), mat(S5_BLK_CH, S5_BLK_ST),
                  mat(S5_BLK_ST, S5_BLK_CH), mat(S5_BLK_ST, S5_BLK_CH),
                  vec(S5_BLK_CH)] + perm_specs,
        out_specs=[u_spec,
                   pl.BlockSpec((nb, kb * S5_BLK_ST), lambda c, s: (0, c)),
                   pl.BlockSpec((nb, kb * S5_BLK_ST), lambda c, s: (0, c))],
        scratch_shapes=[pltpu.VMEM((rows, kb * S5_BLK_ST), F32), pltpu.VMEM((rows, kb * S5_BLK_ST), F32),
                        pltpu.VMEM((nb, kb * S5_BLK_ST), F32), pltpu.VMEM((nb, kb * S5_BLK_ST), F32)],
        compiler_params=_cparams(("parallel", "arbitrary")),
        name="s5_scan",
    )(u3, h0r, h0i, ar, ai, bre, bim, cre, cim, d, *perm_args)


def _s5_discretize(a_re, a_im, log_dt, b_re, b_im, c_re, c_im):
    g, p, hch = b_re.shape
    dt = jnp.exp(log_dt.astype(F32))[:, None]
    mag = jnp.exp(dt * a_re)
    abar_re = mag * jnp.cos(dt * a_im)
    abar_im = mag * jnp.sin(dt * a_im)
    den = a_re * a_re + a_im * a_im
    nr = abar_re - 1.0
    q_re = (nr * a_re + abar_im * a_im) / den
    q_im = (abar_im * a_re - nr * a_im) / den
    bb_re = q_re[..., None] * b_re - q_im[..., None] * b_im
    bb_im = q_re[..., None] * b_im + q_im[..., None] * b_re
    nblk = D // S5_BLK_CH
    gl = g // nblk
    eye = jnp.eye(gl, dtype=F32)

    def in_blocks(bb):
        t = jnp.transpose(bb, (0, 2, 1)).reshape(nblk, gl, hch, p)
        return jnp.einsum('cghp,gk->cghkp', t, eye).reshape(nblk, gl * hch, gl * p).astype(_MXU)

    def out_blocks(cc):
        t = jnp.transpose(cc, (0, 2, 1)).reshape(nblk, gl, p, hch)
        return jnp.einsum('cgph,gk->cgpkh', t, eye).reshape(nblk, gl * p, gl * hch).astype(_MXU)

    return (abar_re.reshape(1, g * p), abar_im.reshape(1, g * p), in_blocks(bb_re), in_blocks(bb_im),
            out_blocks(c_re), out_blocks(c_im))


def _ssd_kernel(xbc_ref, z_ref, dt_ref, cprev_ref, h0_ref, cw_ref, cb_ref, dtb_ref, alog_ref, dx_ref, ng_ref,
                eh_ref, y_ref, hfin_ref, ext_scr, h_scr, *, L, G, t_real, n_steps):
    step = pl.program_id(1)
    rng = range(G)
    NB = M_GROUPS * M_STATE

    def each(f, *lists):
        return [f(*xs) for xs in zip(*lists)]

    @pl.when(step == 0)
    def _():
        for s in rng:
            ext_scr[s] = jnp.zeros((8, M_CONV_DIM), F32)
            ext_scr[s, 5:8, :] = cprev_ref[s]
            for gi in range(M_GROUPS):
                h_scr[s, gi] = h0_ref[s, gi * M_HPG:(gi + 1) * M_HPG].reshape(M_GW, M_STATE).T

    tril = jnp.where(_iota((L, L), 1) <= _iota((L, L), 0), 1.0, 0.0).astype(_MXU)
    causal = _iota((L, L), 1) <= _iota((L, L), 0)
    eye_h = _eye(128, _MXU)
    eh = eh_ref[...]
    lane_head = _iota((1, M_GW), 1) // M_HEAD
    row8 = _iota((8, 1), 0)
    a = -jnp.exp(alog_ref[...])
    conv, dts = [], []
    for s in rng:
        x = xbc_ref[s * L:(s + 1) * L, :]
        prev8 = ext_scr[s]
        c = cb_ref[...] + cw_ref[3:4, :] * x
        for j in range(1, 4):
            xr = pltpu.roll(x, j, axis=0)
            head = jnp.where(row8 < j, pltpu.roll(prev8, j, axis=0), xr[0:8])
            xr = head if L == 8 else jnp.concatenate([head, xr[8:]], axis=0)
            c = c + cw_ref[3 - j:4 - j, :] * xr
        ext_scr[s] = x[L - 8:L]
        conv.append(c * _sigmoid(c))
        dt = _softplus(dt_ref[s * L:(s + 1) * L, :] + dtb_ref[...])
        if t_real < L:
            dt = jnp.where(_iota(dt.shape, 0) < t_real, dt, 0.0)
        dts.append(dt)
    xs = each(lambda c: c[:, 0:M_WIDTH], conv)
    acum = each(lambda d_: _mm_sel_l(tril, d_ * a), dts)
    acum_t = each(lambda x: _mm_nt_sel_l(eye_h, x), acum)
    dt_x = each(lambda d_: _mm_sel_r(d_, eh), dts)
    acum_x = each(lambda x: _mm_sel_r(x, eh), acum)
    acl_x = each(lambda x: x[L - 1:L, :], acum_x)
    xd = each(lambda x, d_: x * d_, xs, dt_x)
    xdd = each(lambda x, l_, c_: x * jnp.exp(l_ - c_), xd, acl_x, acum_x)
    eacum_x = each(jnp.exp, acum_x)
    cdec_x = each(jnp.exp, acl_x)
    ys = [[] for _ in rng]
    for gi in range(M_GROUPS):
        gs = slice(gi * M_GW, (gi + 1) * M_GW)
        bg = each(lambda c: c[:, M_WIDTH + gi * M_STATE:M_WIDTH + (gi + 1) * M_STATE], conv)
        cg = each(lambda c: c[:, M_WIDTH + NB + gi * M_STATE:M_WIDTH + NB + (gi + 1) * M_STATE], conv)
        h = [h_scr[s, gi] for s in rng]
        cbm = each(_mm_nt, cg, bg)
        bgt = each(lambda b_: _mm_nt(eye_h, b_), bg)
        y_off = each(lambda c_, h_, e_: _mm(c_, h_) * e_[:, gs], cg, h, eacum_x)
        ms = []
        for e in range(M_HPG):
            he = gi * M_HPG + e
            ms.append(each(lambda c_, a_, t_: c_ * jnp.exp(jnp.where(causal, a_[:, he:he + 1] - t_[he:he + 1, :],
                                                                     -jnp.inf)), cbm, acum, acum_t))
        if L % 128 == 0:
            mcat = [jnp.concatenate([ms[e][s] for e in range(M_HPG)], axis=1) for s in rng]
            xst = each(lambda x: jnp.concatenate([jnp.where(lane_head == e, x[:, gs], 0.0).astype(_MXU)
                                                  for e in range(M_HPG)], axis=0), xd)
            y_dg = each(_mm, mcat, xst)
        else:
            y_dg = [sum(_mm(ms[e][s], jnp.where(lane_head == e, xd[s][:, gs], 0.0)) for e in range(M_HPG))
                    for s in rng]
        hn = each(lambda h_, d_, b_, x: h_ * d_[:, gs] + _mm(b_, x[:, gs]), h, cdec_x, bgt, xdd)
        for s in rng:
            h_scr[s, gi] = hn[s]
            ys[s].append(y_off[s] + y_dg[s])
    for s in rng:
        y = jnp.concatenate(ys[s], axis=1) + dx_ref[...] * xs[s]
        zz = z_ref[s * L:(s + 1) * L, :]
        y = y * (zz * _sigmoid(zz))
        outs = []
        for gi in range(M_GROUPS):
            yg = y[:, gi * M_GW:(gi + 1) * M_GW]
            outs.append(yg * lax.rsqrt(jnp.mean(yg * yg, axis=-1, keepdims=True) + EPS))
        y_ref[s * L:(s + 1) * L, :] = jnp.concatenate(outs, axis=1) * ng_ref[...]

    @pl.when(step == n_steps - 1)
    def _():
        for s in rng:
            for gi in range(M_GROUPS):
                hfin_ref[s, gi * M_HPG:(gi + 1) * M_HPG] = h_scr[s, gi].T.reshape(M_HPG, M_HEAD, M_STATE)


def ssd_block(xbc, z, dt, conv_prev, h0, l, conv_w, conv_b, dt_bias, a_log, d_x, norm_g, eh, stack, *, B, T, L, G,
              t_real):
    n_steps = T // L
    assert G == 1 or n_steps == 1
    heads = M_GROUPS * M_HPG
    seq = lambda w: pl.BlockSpec((G * L, w), lambda b, s: (b * n_steps + s, 0))
    st_spec = lambda: pl.BlockSpec((None, G, heads, M_HEAD, M_STATE), lambda b, s: (l, b, 0, 0, 0))
    in_specs = [seq(M_CONV_DIM), seq(M_WIDTH), seq(128),
                pl.BlockSpec((None, G, 3, M_CONV_DIM), lambda b, s: (l, b, 0, 0)),
                st_spec(),
                _const_spec((4, M_CONV_DIM)), _const_spec((1, M_CONV_DIM)),
                _const_spec((1, 128)), _const_spec((1, 128)),
                _const_spec((1, M_WIDTH)), _const_spec((1, M_WIDTH)),
                _const_spec((128, M_WIDTH))]
    kern, in_specs, args, aliases = _stacked_out(
        functools.partial(_ssd_kernel, L=L, G=G, t_real=t_real, n_steps=n_steps), in_specs,
        [xbc, z, dt, conv_prev, h0, conv_w, conv_b, dt_bias, a_log, d_x, norm_g, eh], 1, stack)
    return pl.pallas_call(
        kern,
        out_shape=[jax.ShapeDtypeStruct((B * T, M_WIDTH), F32),
                   jax.ShapeDtypeStruct((h0.shape[0], B, heads, M_HEAD, M_STATE), F32)],
        grid=(B // G, n_steps),
        in_specs=in_specs,
        out_specs=[seq(M_WIDTH), st_spec()],
        scratch_shapes=[pltpu.VMEM((G, 8, M_CONV_DIM), F32),
                        pltpu.VMEM((G, M_GROUPS, M_STATE, M_GW), F32)],
        input_output_aliases=aliases,
        compiler_params=_cparams(("parallel", "arbitrary")),
        name="ssd",
    )(*args)


def _attn_kernel(q_ref, k_ref, v_ref, o_ref, *, nb, tq, heads_split):
    scale = XA_HD ** -0.5
    for j in range(nb):
        if heads_split:
            k_all = pltpu.einshape("mhd->hmd", k_ref[j])
            v_all = pltpu.einshape("mhd->hmd", v_ref[j])
        outs = []
        for hd in range(XA_HEADS):
            cs = slice(hd * XA_HD, (hd + 1) * XA_HD)
            q = q_ref[j * tq:(j + 1) * tq, cs]
            kh = k_all[hd] if heads_split else k_ref[j, :, cs]
            vh = v_all[hd] if heads_split else v_ref[j, :, cs]
            s = _mm_nt(q, kh) * scale
            s = s - jnp.max(s, axis=-1, keepdims=True)
            p = jnp.exp(s)
            p = p / jnp.sum(p, axis=-1, keepdims=True)
            outs.append(_mm(p, vh))
        o_ref[j * tq:(j + 1) * tq, :] = jnp.concatenate(outs, axis=1)


def cross_attention(q, mk, mv, kv_index, *, B, T, tq, nb):
    nlead = len(kv_index)
    heads_split = mk.ndim - nlead == 4
    kv_blk = mk.shape[nlead + 1:]
    n_t = T // tq
    kv_spec = pl.BlockSpec((None,) * nlead + (nb,) + kv_blk,
                           lambda b, s: tuple(kv_index) + (b,) + (0,) * len(kv_blk))
    q_spec = pl.BlockSpec((nb * tq, D), lambda b, s: (b * n_t + s, 0))
    return pl.pallas_call(
        functools.partial(_attn_kernel, nb=nb, tq=tq, heads_split=heads_split),
        out_shape=jax.ShapeDtypeStruct((B * T, D), F32),
        grid=(B // nb, n_t),
        in_specs=[q_spec, kv_spec, kv_spec],
        out_specs=q_spec,
        compiler_params=_cparams(("parallel", "arbitrary")),
        name="xattn",
    )(q, mk, mv)


def _xattn_block_kernel(h_ref, gq_ref, wq_ref, k_ref, v_ref, wo_ref, go_ref, o_ref):
    h = h_ref[...]
    q = _mm(_rms(h, gq_ref[...]), wq_ref[...])
    scale = XA_HD ** -0.5
    outs = []
    for hd in range(XA_HEADS):
        cs = slice(hd * XA_HD, (hd + 1) * XA_HD)
        s = _mm_nt(q[:, cs], k_ref[0, :, cs]) * scale
        s = s - jnp.max(s, axis=-1, keepdims=True)
        p = jnp.exp(s)
        p = p / jnp.sum(p, axis=-1, keepdims=True)
        outs.append(_mm(p, v_ref[0, :, cs]))
    xa = _mm(jnp.concatenate(outs, axis=1), wo_ref[...])
    o_ref[...] = h + _rms(xa, go_ref[...])


def cross_attention_block(h, gq, wq, mk, mv, wo, go, *, B, T, tq):
    m = mk.shape[1]
    n_t = T // tq
    row = pl.BlockSpec((tq, D), lambda b, s: (b * n_t + s, 0))
    kv = pl.BlockSpec((1, m, D), lambda b, s: (b, 0, 0))
    return pl.pallas_call(
        _xattn_block_kernel,
        out_shape=jax.ShapeDtypeStruct((B * T, D), F32),
        grid=(B, n_t),
        in_specs=[row, _const_spec((1, D)), _const_spec((D, D)), kv, kv, _const_spec((D, D)), _const_spec((1, D))],
        out_specs=row,
        compiler_params=_cparams(("parallel", "arbitrary")),
        name="xattn_block",
    )(h, gq, wq, mk, mv, wo, go)


def _pad_time(x2, B, T, Tp):
    if Tp == T:
        return x2
    w = x2.shape[-1]
    return jnp.pad(x2.reshape(B, T, w), ((0, 0), (0, Tp - T), (0, 0))).reshape(B * Tp, w)


def _layer(l, h, grp, W, st, acc):
    T, B = grp['T'], grp['B']
    n = T * B
    gates = norm_matmul(h, W['g_mix_pre'], W['w_gates'], name="in_gates")
    p_r = norm_matmul(h, W['g_mix_pre'], W['w_rwkv'], name="in_rwkv")
    if grp['s5_tmajor']:
        u3 = norm_matmul_tmajor(h.reshape(B, T, D), W['g_mix_pre'], W['w_s5'], name="in_s5")
    else:
        u3 = norm_matmul(h, W['g_mix_pre'], W['w_s5'], name="in_s5").reshape(1, n, D)
    z_m =norm_matmul(h, W['g_mix_pre'], W['w_z'], name="in_z")
    xbc = norm_matmul(h, W['g_mix_pre'], W['w_xbc'], name="in_xbc")
    dt_m = norm_matmul(h, W['g_mix_pre'], W['w_dt'], tn=128, name="in_dt")

    Tp = grp['rwkv_Tpad']
    p3 = _pad_time(p_r, B, T, Tp).reshape(B, Tp, p_r.shape[-1])
    yr, s_fin = rwkv_recurrence(p3, st['shift'][l], W['mu'], W['wa'], W['g2'], W['w0'], W['a0'], W['k_k'],
                                W['k_a'], st['rwkv'], l, W['r_k'], W['ln_w'], W['ln_b'], acc[0],
                                C=grp['rwkv_C'], G=grp['rwkv_G'], NP=grp['rwkv_NP'], t_real=min(T, grp['rwkv_C']))
    yr = yr[:, :T].reshape(n, D)
    shift_new = p_r.reshape(B, T, -1)[:, T - 1]

    ys, s5r, s5i = s5_scan(u3, st['s5r'], st['s5i'], l, W['s5_ar'], W['s5_ai'], W['s5_bre'], W['s5_bim'],
                           W['s5_cre'], W['s5_cim'], W['s5_d'], nb=B, tc=grp['s5_tc'], blk=grp['s5_blk'],
                           kb=2 if B <= 16 else 1)

    Lc = grp['ssd_L']
    Tm = grp['ssd_Tpad']
    ym, ssm_fin = ssd_block(_pad_time(xbc, B, T, Tm), _pad_time(z_m, B, T, Tm), _pad_time(dt_m, B, T, Tm),
                            st['conv'], st['ssm'], l, W['conv_w'], W['conv_b'], W['dt_bias'],
                            W['a_log'], W['d_x'], W['m_norm'], W['eh'], acc[1], B=B, T=Tm, L=Lc, G=grp['ssd_G'],
                            t_real=min(T, Lc))
    ym = ym.reshape(B, Tm, M_WIDTH)[:, :T].reshape(n, M_WIDTH)
    conv_new = jnp.concatenate([st['conv'][l], xbc.reshape(B, T, M_CONV_DIM)[:, max(T - 3, 0):]], axis=1)[:, -3:]

    h = mixer_merge(gates, yr, ys if grp['s5_tmajor'] else ys.reshape(n, D), ym, h, W['w_out_rwkv'],
                    W['s5_w_glu'], W['w_out_s5'], W['w_out_mamba'], W['w_out'], W['g_mix_post'],
                    seq=(B, T) if grp['s5_tmajor'] else None)

    if grp['xa_nb'] == 1 and st['mk'].ndim == 3:
        h = cross_attention_block(h, W['g_xa_pre'], W['xa_wq'], st['mk'], st['mv'], W['xa_wo'], W['g_xa_post'],
                                  B=B, T=T, tq=grp['xa_tq'])
    else:
        q = norm_matmul(h, W['g_xa_pre'], W['xa_wq'], name="xa_q")
        o = cross_attention(q, st['mk'], st['mv'], st['kv_index'](l), B=B, T=T, tq=grp['xa_tq'],
                            nb=grp['xa_nb'])
        h = matmul_norm_residual(o, W['xa_wo'], W['g_xa_post'], h, name="xa_out")

    h = mlp_block(h, W['g_mlp_pre'], W['mlp_w1'], W['mlp_w2'], W['g_mlp_post'])
    return h, (s_fin, shift_new, s5r.reshape(B, 64, 64), s5i.reshape(B, 64, 64), conv_new, ssm_fin)


def _group_cfg(T, B):
    cfg = dict(T=T, B=B)
    g = 8 if B % 8 == 0 else 1
    if T % 64 == 0:
        cfg.update(rwkv_C=64, rwkv_Tpad=T, rwkv_G=g, rwkv_NP=2)
    else:
        tp = -(-T // 8) * 8
        cfg.update(rwkv_C=tp, rwkv_Tpad=tp, rwkv_G=16 if B % 16 == 0 else g, rwkv_NP=2)
    if T % 8 == 0 and (1024 // B) >= 8 and T % (1024 // B) == 0:
        tc = 1024 // B
        cfg.update(s5_tmajor=True, s5_tc=tc, s5_blk=(tc, B))
    else:
        cfg.update(s5_tmajor=False, s5_tc=T, s5_blk=(1, B * T))
    if T % 128 == 0:
        cfg.update(ssd_L=128, ssd_Tpad=T, ssd_G=1)
    else:
        tp = -(-T // 8) * 8
        cfg.update(ssd_L=tp, ssd_Tpad=tp, ssd_G=4 if B % 4 == 0 else 1)
    if T >= 64:
        cfg.update(xa_tq=_row_tile(T, 512), xa_nb=1)
    else:
        cfg.update(xa_tq=T, xa_nb=4 if B % 4 == 0 else 1)
    return cfg


def kernel(x_prompt, x_sample, cache_mem_k, cache_mem_v, state_rwkv, state_rwkv_shift, state_s5_re, state_s5_im, state_conv, state_ssm, mem_prompt, norm_mix_pre, norm_mix_post, norm_xa_pre, norm_xa_post, norm_mlp_pre, norm_mlp_post, norm_mem, w_in, w_out, rwkv_mu, rwkv_w0, rwkv_w2, rwkv_a0, rwkv_a2, rwkv_g2, rwkv_k_k, rwkv_k_a, rwkv_r_k, rwkv_ln_w, rwkv_ln_b, w_out_rwkv, s5_a_re, s5_a_im, s5_log_dt, s5_b_re, s5_b_im, s5_c_re, s5_c_im, s5_d, s5_w_glu, w_out_s5, m_conv_w, m_conv_b, m_dt_bias, m_a_log, m_d, m_norm, w_out_mamba, xa_wq, xa_wk, xa_wv, xa_wo, mlp_w1, mlp_w2):
    depth = w_in.shape[0]
    bp, tp, _ = x_prompt.shape
    bs, ts, _ = x_sample.shape
    mlen = mem_prompt.shape[1]
    bf = lambda x: x.astype(_MXU)
    row = lambda x: x.reshape(1, -1).astype(F32)

    eh = (jnp.arange(128)[:, None] == (jnp.arange(M_WIDTH)[None, :] // M_HEAD)).astype(_MXU)

    col = [0]
    for sz in (3 * D, 3 * D + 256, D, M_WIDTH, M_CONV_DIM, 32):
        col.append(col[-1] + sz)
    layers = []
    w_in_b = bf(w_in)
    big = {name: bf(w) for name, w in dict(
        w_out_rwkv=w_out_rwkv, s5_w_glu=s5_w_glu, w_out_s5=w_out_s5, w_out_mamba=w_out_mamba, w_out=w_out,
        xa_wq=xa_wq, xa_wk=xa_wk, xa_wv=xa_wv, xa_wo=xa_wo, mlp_w1=mlp_w1, mlp_w2=mlp_w2).items()}
    for l in range(depth):
        wl = w_in_b[l]
        ar, ai, bre, bim, cre, cim = _s5_discretize(s5_a_re[l], s5_a_im[l], s5_log_dt[l], s5_b_re[l], s5_b_im[l],
                                                    s5_c_re[l], s5_c_im[l])
        zero = jnp.zeros((64, D), F32)
        wa = jnp.concatenate([jnp.concatenate([rwkv_w2[l], zero], axis=1),
                              jnp.concatenate([zero, rwkv_a2[l]], axis=1)], axis=0)
        pad32 = lambda x: jnp.pad(x.reshape(1, -1).astype(F32), ((0, 0), (0, 128 - x.shape[-1])))
        layers.append(dict(
            g_mix_pre=row(norm_mix_pre[l]), g_mix_post=row(norm_mix_post[l]),
            g_xa_pre=row(norm_xa_pre[l]), g_xa_post=row(norm_xa_post[l]),
            g_mlp_pre=row(norm_mlp_pre[l]), g_mlp_post=row(norm_mlp_post[l]),
            w_gates=bf(wl[:, col[0]:col[1]]), w_rwkv=bf(wl[:, col[1]:col[2]]), w_s5=bf(wl[:, col[2]:col[3]]),
            w_z=bf(wl[:, col[3]:col[4]]), w_xbc=bf(wl[:, col[4]:col[5]]),
            w_dt=bf(jnp.pad(wl[:, col[5]:col[6]], ((0, 0), (0, 96)))),
            mu=row(rwkv_mu[l]), wa=bf(wa), w0=row(rwkv_w0[l]), a0=row(rwkv_a0[l]), g2=bf(rwkv_g2[l]),
            k_k=row(rwkv_k_k[l]), k_a=row(rwkv_k_a[l]), r_k=row(rwkv_r_k[l]),
            ln_w=row(rwkv_ln_w[l]), ln_b=row(rwkv_ln_b[l]),
            s5_ar=ar, s5_ai=ai, s5_bre=bre, s5_bim=bim, s5_cre=cre, s5_cim=cim, s5_d=row(s5_d[l]),
            conv_w=m_conv_w[l].astype(F32), conv_b=row(m_conv_b[l]), dt_bias=pad32(m_dt_bias[l]),
            a_log=pad32(m_a_log[l]), d_x=row(jnp.repeat(m_d[l], M_HEAD)), m_norm=row(m_norm[l]), eh=eh,
            g_mem=row(norm_mem[l]), **{name: w[l] for name, w in big.items()},
        ))

    hp = x_prompt.reshape(bp * tp, D).astype(F32)
    hs = x_sample.reshape(bs * ts, D).astype(F32)
    mem2 = mem_prompt.reshape(bp * mlen, D).astype(F32)

    cfg_p = _group_cfg(tp, bp)
    cfg_s = _group_cfg(ts, bs)
    zeros_p = dict(
        rwkv=jnp.zeros((depth, bp) + state_rwkv.shape[2:], F32),
        shift=jnp.zeros((depth, bp, state_rwkv_shift.shape[-1]), F32),
        s5r=jnp.zeros((depth, bp, state_s5_re.shape[2] * state_s5_re.shape[3]), F32),
        s5i=jnp.zeros((depth, bp, state_s5_re.shape[2] * state_s5_re.shape[3]), F32),
        conv=jnp.zeros((depth, bp) + state_conv.shape[2:], F32),
        ssm=jnp.zeros((depth, bp) + state_ssm.shape[2:], F32),
    )
    st_s = dict(
        rwkv=state_rwkv, shift=state_rwkv_shift,
        s5r=state_s5_re.reshape(depth, bs, -1), s5i=state_s5_im.reshape(depth, bs, -1),
        conv=state_conv, ssm=state_ssm,
        mk=cache_mem_k, mv=cache_mem_v,
        kv_index=lambda l: (l,),
    )

    mk_out, mv_out = [], []
    st_p_out = [[] for _ in range(6)]
    st_s_out = [[] for _ in range(6)]
    acc_p = (jnp.zeros_like(zeros_p['rwkv']), jnp.zeros_like(zeros_p['ssm']))
    acc_s = (jnp.zeros_like(state_rwkv, dtype=F32), jnp.zeros_like(state_ssm, dtype=F32))
    for l in range(depth):
        W = layers[l]
        mk = norm_matmul(mem2, W['g_mem'], W['xa_wk'], name="mem_k")
        mv = norm_matmul(mem2, W['g_mem'], W['xa_wv'], name="mem_v")
        st_p = dict(zeros_p, mk=mk.reshape(bp, mlen, D), mv=mv.reshape(bp, mlen, D), kv_index=lambda l: ())
        hp, new_p = _layer(l, hp, cfg_p, W, st_p, acc_p)
        hs, new_s = _layer(l, hs, cfg_s, W, st_s, acc_s)
        acc_p = (new_p[0], new_p[5])
        acc_s = (new_s[0], new_s[5])
        mk_out.append(mk.reshape(bp, mlen, XA_HEADS, XA_HD))
        mv_out.append(mv.reshape(bp, mlen, XA_HEADS, XA_HD))
        for i in range(1, 5):
            st_p_out[i].append(new_p[i])
            st_s_out[i].append(new_s[i])

    stk = lambda xs: jnp.stack(xs).astype(F32)
    y_p = hp.reshape(bp, tp, D).astype(x_prompt.dtype)
    y_s = hs.reshape(bs, ts, D).astype(x_sample.dtype)
    return (y_p, y_s, stk(mk_out), stk(mv_out),
            acc_p[0], *[stk(st_p_out[i]) for i in range(1, 5)], acc_p[1],
            acc_s[0], *[stk(st_s_out[i]) for i in range(1, 5)], acc_s[1])
```

```python
import functools
import math

import numpy as np
import jax
import jax.numpy as jnp
from jax import lax
from jax.experimental import pallas as pl
from jax.experimental.pallas import tpu as pltpu

F32 = jnp.float32
_MXU = jnp.bfloat16
EPS = 1e-6
R_LN_EPS = 64e-5
D = 1024
R_HEAD = 64
R_PAIR = 2 * R_HEAD
S5_BLK_CH = 128
S5_BLK_ST = 512
M_HEAD = 64
M_STATE = 128
M_GROUPS = 4
M_HPG = 8
M_GW = M_HPG * M_HEAD
M_WIDTH = 2048
M_CONV_DIM = 3072
XA_HEADS = 4
XA_HD = 256
VMEM_LIMIT = 56 * 1024 * 1024


def _mm(a, b):
    return jnp.dot(a.astype(_MXU), b.astype(_MXU), preferred_element_type=F32)


def _mm_nt(a, b):
    return lax.dot_general(a.astype(_MXU), b.astype(_MXU), (((1,), (1,)), ((), ())),
                           preferred_element_type=F32)


def _mm_tn(a, b):
    return lax.dot_general(a.astype(_MXU), b.astype(_MXU), (((0,), (0,)), ((), ())),
                           preferred_element_type=F32)


def _split3(x):
    hi = x.astype(_MXU)
    r1 = x - hi.astype(F32)
    mid = r1.astype(_MXU)
    lo = (r1 - mid.astype(F32)).astype(_MXU)
    return hi, mid, lo


def _mm_sel_l(sel, x):
    return sum(jnp.dot(sel, p, preferred_element_type=F32) for p in _split3(x))


def _mm_sel_r(x, sel):
    return sum(jnp.dot(p, sel, preferred_element_type=F32) for p in _split3(x))


def _mm_nt_sel_l(sel, x):
    return sum(lax.dot_general(sel, p, (((1,), (1,)), ((), ())), preferred_element_type=F32)
               for p in _split3(x))


def _sigmoid(x):
    return 1.0 / (1.0 + jnp.exp(-x))


def _softplus(x):
    return jnp.maximum(x, 0.0) + jnp.log(1.0 + jnp.exp(-jnp.abs(x)))


def _rms(x, g):
    return x * lax.rsqrt(jnp.mean(x * x, axis=-1, keepdims=True) + EPS) * g


def _iota(shape, dim):
    return lax.broadcasted_iota(jnp.int32, shape, dim)


def _eye(n, dtype):
    return jnp.where(_iota((n, n), 0) == _iota((n, n), 1), 1.0, 0.0).astype(dtype)


def _cparams(sem):
    return pltpu.CompilerParams(dimension_semantics=sem, vmem_limit_bytes=VMEM_LIMIT)


def _row_tile(n, want):
    t = min(n, want)
    while n % t:
        t //= 2
    return t


def _col_tile(c, cap):
    best = 128
    for t in range(128, min(c, cap) + 1, 128):
        if c % t == 0:
            best = t
    return best


def _const_spec(shape):
    nd = len(shape)
    return pl.BlockSpec(shape, lambda *_: (0,) * nd, pipeline_mode=pl.Buffered(1))


def _norm_mm_kernel(x_ref, g_ref, w_ref, o_ref, xn_ref):
    @pl.when(pl.program_id(1) == 0)
    def _():
        xn_ref[...] = _rms(x_ref[...], g_ref[...]).astype(_MXU)

    o_ref[...] = jnp.dot(xn_ref[...], w_ref[...], preferred_element_type=F32)


def norm_matmul(x, g, w, *, tm=1024, tn=1664, name="norm_mm"):
    n, d = x.shape
    c = w.shape[1]
    tm = _row_tile(n, tm)
    tn = _col_tile(c, tn)
    return pl.pallas_call(
        _norm_mm_kernel,
        out_shape=jax.ShapeDtypeStruct((n, c), F32),
        grid=(n // tm, c // tn),
        in_specs=[pl.BlockSpec((tm, d), lambda i, j: (i, 0)),
                  pl.BlockSpec((1, d), lambda i, j: (0, 0)),
                  pl.BlockSpec((d, tn), lambda i, j: (0, j))],
        out_specs=pl.BlockSpec((tm, tn), lambda i, j: (i, j)),
        scratch_shapes=[pltpu.VMEM((tm, d), _MXU)],
        compiler_params=_cparams(("parallel", "arbitrary")),
        name=name,
    )(x, g, w)


def _mm_norm_res_kernel(x_ref, w_ref, g_ref, h_ref, o_ref):
    y = _mm(x_ref[...], w_ref[...])
    o_ref[...] = h_ref[...] + _rms(y, g_ref[...])


def matmul_norm_residual(x, w, g, h, *, tm=512, name="mm_norm_res"):
    n, k = x.shape
    d = w.shape[1]
    tm = _row_tile(n, tm)
    return pl.pallas_call(
        _mm_norm_res_kernel,
        out_shape=jax.ShapeDtypeStruct((n, d), F32),
        grid=(n // tm,),
        in_specs=[pl.BlockSpec((tm, k), lambda i: (i, 0)),
                  _const_spec((k, d)),
                  _const_spec((1, d)),
                  pl.BlockSpec((tm, d), lambda i: (i, 0))],
        out_specs=pl.BlockSpec((tm, d), lambda i: (i, 0)),
        compiler_params=_cparams(("parallel",)),
        name=name,
    )(x, w, g, h)


def _mlp_kernel(h_ref, g1_ref, w1_ref, w2_ref, g2_ref, o_ref, *, n_chunks, ck):
    h = h_ref[...]
    xn = _rms(h, g1_ref[...]).astype(_MXU)
    acc = jnp.zeros(h.shape, F32)
    for j in range(n_chunks):
        a = jnp.dot(xn, w1_ref[:, j * ck:(j + 1) * ck], preferred_element_type=F32)
        a = jnp.square(jnp.maximum(a, 0.0))
        acc = acc + jnp.dot(a.astype(_MXU), w2_ref[j * ck:(j + 1) * ck, :], preferred_element_type=F32)
    o_ref[...] = h + _rms(acc, g2_ref[...])


def mlp_block(h, g1, w1, w2, g2, *, tm=1024, ck=1024):
    n, d = h.shape
    f = w1.shape[1]
    tm = _row_tile(n, tm)
    return pl.pallas_call(
        functools.partial(_mlp_kernel, n_chunks=f // ck, ck=ck),
        out_shape=jax.ShapeDtypeStruct((n, d), F32),
        grid=(n // tm,),
        in_specs=[pl.BlockSpec((tm, d), lambda i: (i, 0)),
                  _const_spec((1, d)),
                  _const_spec((d, f)),
                  _const_spec((f, d)),
                  _const_spec((1, d))],
        out_specs=pl.BlockSpec((tm, d), lambda i: (i, 0)),
        compiler_params=_cparams(("parallel",)),
        name="mlp",
    )(h, g1, w1, w2, g2)


def _norm_mm_tmajor_kernel(x_ref, g_ref, w_ref, perm_ref, o_ref):
    nb, tt, d = x_ref.shape
    xn = _rms(x_ref[...].reshape(nb * tt, d), g_ref[...]).astype(_MXU)
    xt = jnp.dot(perm_ref[...], xn, preferred_element_type=F32).astype(_MXU)
    o_ref[...] = jnp.dot(xt, w_ref[...], preferred_element_type=F32).reshape(o_ref.shape)


def norm_matmul_tmajor(x3, g, w, *, rows=512, name="norm_mm_t"):
    nb, t, d = x3.shape
    c = w.shape[1]
    tt = rows // nb
    assert t % tt == 0 and tt % 8 == 0
    perm = jnp.asarray(_time_major_perm(nb, tt), _MXU)
    return pl.pallas_call(
        _norm_mm_tmajor_kernel,
        out_shape=jax.ShapeDtypeStruct((t, nb, c), F32),
        grid=(t // tt,),
        in_specs=[pl.BlockSpec((nb, tt, d), lambda i: (0, i, 0)),
                  _const_spec((1, d)), _const_spec((d, c)), _const_spec((nb * tt, nb * tt))],
        out_specs=pl.BlockSpec((tt, nb, c), lambda i: (i, 0, 0)),
        compiler_params=_cparams(("parallel",)),
        name=name,
    )(x3, g, w, perm)


def _merge_kernel(gates_ref, yr_ref, ys_ref, ym_ref, h_ref, wr_ref, wglu_ref, ws_ref, wm_ref, wo_ref, g_ref,
                  *rest, ys_tmajor):
    o_ref = rest[-1]
    rows2d = lambda ref: ref[...].reshape(-1, ref.shape[-1])
    o_r = _mm(rows2d(yr_ref), wr_ref[...])
    ys = rows2d(ys_ref)
    y3 = ys * _sigmoid(_mm(ys, wglu_ref[...]))
    if ys_tmajor:
        y3 = jnp.dot(rest[0][...], y3.astype(_MXU), preferred_element_type=F32)
    o_s = _mm(y3, ws_ref[...])
    o_m = _mm(rows2d(ym_ref), wm_ref[...])
    gates = rows2d(gates_ref)
    merged = (_sigmoid(gates[:, 0:D]) * o_r + _sigmoid(gates[:, D:2 * D]) * o_s
              + _sigmoid(gates[:, 2 * D:3 * D]) * o_m)
    mix = _mm(merged, wo_ref[...])
    o_ref[...] = (rows2d(h_ref) + _rms(mix, g_ref[...])).reshape(o_ref.shape)


def mixer_merge(gates, yr, ys, ym, h, wr, wglu, ws, wm, wo, g, *, tm=256, seq=None):
    n = h.shape[0]
    weights = [_const_spec((D, D)), _const_spec((D, D)), _const_spec((D, D)),
               _const_spec((M_WIDTH, D)), _const_spec((D, D)), _const_spec((1, D))]
    if seq is None:
        tm = _row_tile(n, tm)
        row = lambda w: pl.BlockSpec((tm, w), lambda i: (i, 0))
        in_specs = [row(3 * D), row(D), row(D), row(M_WIDTH), row(D)] + weights
        args = [gates, yr, ys, ym, h, wr, wglu, ws, wm, wo, g]
        out_spec, out_shape, grid = row(D), (n, D), (n // tm,)
    else:
        nb, t = seq
        tt = tm // nb
        assert t % tt == 0 and tt % 8 == 0
        row = lambda w: pl.BlockSpec((nb, tt, w), lambda i: (0, i, 0))
        v3 = lambda x: x.reshape(nb, t, x.shape[-1])
        in_specs = ([row(3 * D), row(D), pl.BlockSpec((tt, nb, D), lambda i: (i, 0, 0)), row(M_WIDTH), row(D)]
                    + weights + [_const_spec((nb * tt, nb * tt))])
        args = [v3(gates), v3(yr), ys, v3(ym), v3(h), wr, wglu, ws, wm, wo, g,
                jnp.asarray(_time_major_perm(nb, tt).T, _MXU)]
        out_spec, out_shape, grid = row(D), (nb, t, D), (t // tt,)
    out = pl.pallas_call(
        functools.partial(_merge_kernel, ys_tmajor=seq is not None),
        out_shape=jax.ShapeDtypeStruct(out_shape, F32),
        grid=grid,
        in_specs=in_specs,
        out_specs=out_spec,
        compiler_params=_cparams(("parallel",)),
        name="mixer_merge",
    )(*args)
    return out.reshape(n, D)


def _rwkv_rec_kernel(pr_ref, pk_ref, pv_ref, pl_ref, shr_ref, shk_ref, shv_ref, shl_ref,
                     mur_ref, muk_ref, muv_ref, mul_ref, w2_ref, a2_ref, g2_ref, w0_ref, a0_ref, kk_ref, ka_ref,
                     s0_ref, rk_ref, lnw_ref, lnb_ref,
                     y_ref, sfin_ref, s_scr, carry_scr, *, C, G, NP, n_steps, t_real):
    step = pl.program_id(2)
    C2 = 2 * C
    W = NP * R_PAIR
    probs = [(gi, pp) for gi in range(G) for pp in range(NP)]
    rng = range(len(probs))
    row_c = _iota((C, 1), 0)

    @pl.when(step == 0)
    def _():
        for j, sh_ref in enumerate((shr_ref, shk_ref, shv_ref, shl_ref)):
            carry_scr[j] = sh_ref[...]

    def shift_mix(j, p_ref, mu_ref):
        out = []
        for gi in range(G):
            x = p_ref[gi]
            prev = jnp.where(row_c == 0, carry_scr[j, gi:gi + 1, :], pltpu.roll(x, 1, axis=0))
            carry_scr[j, gi:gi + 1, :] = x[C - 1:C, :]
            out.append(x + mu_ref[...] * (prev - x))
        return out

    pm_r = shift_mix(0, pr_ref, mur_ref)
    pm_k = shift_mix(1, pk_ref, muk_ref)
    pm_v = shift_mix(2, pv_ref, muv_ref)
    xl = jnp.concatenate(shift_mix(3, pl_ref, mul_ref), axis=0)
    x_wa = xl[:, 0:128]
    lora_in = jnp.where(_iota(x_wa.shape, 1) < 64, jnp.tanh(x_wa), x_wa)
    w_log = -_softplus(-(w0_ref[...] + _mm(lora_in, w2_ref[...]))) - 0.5
    e_all = jnp.exp(w_log)
    a_all = _sigmoid(a0_ref[...] + _mm(lora_in, a2_ref[...]))
    g_all = _mm(_sigmoid(xl[:, 128:256]), g2_ref[...])
    valid = row_c < t_real

    def seq_rows(x, gi, keep_pad=False):
        x = x[gi * C:(gi + 1) * C]
        return x if (keep_pad or t_real >= C) else jnp.where(valid, x, 0.0)

    def pad0(x):
        return x if t_real >= C else jnp.where(valid, x, 0.0)

    lanes = lambda x, pp: x[:, pp * R_PAIR:(pp + 1) * R_PAIR]
    r, e, k, v, kk, a_l, g_l = [], [], [], [], [], [], []
    for gi, pp in probs:
        a_q = lanes(seq_rows(a_all, gi, True), pp)
        k_raw = lanes(pad0(pm_k[gi]), pp)
        r.append(lanes(pad0(pm_r[gi]), pp))
        e.append(lanes(seq_rows(e_all, gi), pp))
        k.append(k_raw * (1.0 + (a_q - 1.0) * lanes(ka_ref[...], pp)))
        v.append(lanes(pad0(pm_v[gi]), pp))
        kk.append(k_raw * lanes(kk_ref[...], pp))
        a_l.append(a_q)
        g_l.append(lanes(seq_rows(g_all, gi, True), pp))

    def par(ref, q):
        pp = probs[q][1]
        return ref[:, pp * R_PAIR:(pp + 1) * R_PAIR]

    lane = _iota((1, R_PAIR), 1)
    m0 = lane < R_HEAD
    ri = _iota((R_PAIR, R_PAIR), 0)
    ci = _iota((R_PAIR, R_PAIR), 1)
    bd = (ri < R_HEAD) == (ci < R_HEAD)
    rt_i = _iota((C2, C2), 0)
    ct_i = _iota((C2, C2), 1)
    same = (rt_i < C) == (ct_i < C)
    tr = jnp.where(rt_i < C, rt_i, rt_i - C)
    ts = jnp.where(ct_i < C, ct_i, ct_i - C)
    m_sl = same & (ts < tr)
    m_li = same & (ts <= tr)
    eye2 = jnp.where(rt_i == ct_i, 1.0, 0.0)
    lvl_masks = []
    m = 1
    while m < C:
        lvl_masks.append(((tr & ~(2 * m - 1)) == (ts & ~(2 * m - 1))) & ((tr & m) != 0) & ((ts & m) == 0))
        m *= 2
    tril = jnp.where(_iota((C, C), 1) <= _iota((C, C), 0), 1.0, 0.0).astype(_MXU)

    @pl.when(step == 0)
    def _():
        z = jnp.zeros((R_HEAD, R_HEAD), F32)
        for q, (gi, pp) in enumerate(probs):
            s_scr[q] = jnp.concatenate([jnp.concatenate([s0_ref[gi, 2 * pp], z], axis=1),
                                        jnp.concatenate([z, s0_ref[gi, 2 * pp + 1]], axis=1)], axis=0)

    def stack(x):
        return jnp.concatenate([jnp.where(m0, x, 0.0), jnp.where(m0, 0.0, x)], axis=0)

    def fold(xs):
        return xs[:C] + xs[C:]

    def each(f, *lists):
        return [f(*xs) for xs in zip(*lists)]

    def head_sum(x):
        s0 = jnp.sum(jnp.where(m0, x, 0.0), axis=-1, keepdims=True)
        s1 = jnp.sum(jnp.where(m0, 0.0, x), axis=-1, keepdims=True)
        return jnp.where(m0, s0, s1)

    an = each(lambda x: -x * lax.rsqrt(head_sum(x * x) + 1e-12), kk)
    bn = each(lambda n_, a_: -n_ * a_, an, a_l)
    cs = each(lambda x: _mm_sel_l(tril, x), e)
    cl = each(lambda x: x[C - 1:C, :], cs)
    at = each(lambda a_, e_, c_: a_ * jnp.exp(e_ - c_), an, e, cs)
    rt = each(lambda r_, c_: r_ * jnp.exp(-c_), r, cs)
    ecs = each(jnp.exp, cs)
    bt_ = each(lambda b_, x: b_ * x, bn, ecs)
    kt = each(lambda k_, x: k_ * x, k, ecs)
    ecl = each(lambda c_, l_: jnp.exp(c_ - l_), cs, cl)
    bh = each(lambda b_, x: b_ * x, bn, ecl)
    kh = each(lambda k_, x: k_ * x, k, ecl)
    As, Rs, Vs = each(stack, at), each(stack, rt), each(stack, v)
    b2 = each(lambda x: jnp.concatenate([x, x], axis=0), bt_)
    k2 = each(lambda x: jnp.concatenate([x, x], axis=0), kt)
    N = each(lambda a_, b_: jnp.where(m_sl, _mm_nt(a_, b_), 0.0), As, b2)
    Ak = each(lambda a_, b_: jnp.where(m_sl, _mm_nt(a_, b_), 0.0), As, k2)
    Arb = each(lambda a_, b_: jnp.where(m_li, _mm_nt(a_, b_), 0.0), Rs, b2)
    Ark = each(lambda a_, b_: jnp.where(m_li, _mm_nt(a_, b_), 0.0), Rs, k2)
    T = each(lambda n_: eye2 + jnp.where(lvl_masks[0], n_, 0.0), N)
    for lm in lvl_masks[1:]:
        LT = each(lambda n_, t_: _mm(jnp.where(lm, n_, 0.0), t_), N, T)
        T = each(lambda t_, x: t_ + _mm(t_, x), T, LT)
    AkV = each(_mm, Ak, Vs)
    X = each(lambda t_, a_, u_: _mm(t_, jnp.concatenate([a_, u_], axis=1)), T, As, AkV)
    Z = each(_mm, Arb, X)
    ArkV = each(_mm, Ark, Vs)
    Rp = each(lambda r_, z_: r_ + fold(z_[:, 0:R_PAIR]), rt, Z)
    Y0 = each(lambda z_, a_: fold(z_[:, R_PAIR:] + a_), Z, ArkV)
    Ap = each(lambda x: fold(x[:, 0:R_PAIR]), X)
    U0 = each(lambda x: fold(x[:, R_PAIR:]), X)
    P = each(lambda a_, b_: jnp.where(bd, _mm_tn(a_, b_), 0.0), Ap, bh)
    Q = each(lambda u_, v_, b_, k_: jnp.where(bd, _mm_tn(jnp.concatenate([u_, v_], axis=0),
                                                         jnp.concatenate([b_, k_], axis=0)), 0.0), U0, v, bh, kh)
    S = [s_scr[q] for q in rng]
    y = each(lambda y_, r_, s_: y_ + _mm_nt(r_, s_), Y0, Rp, S)
    S = each(lambda s_, l_, p_, q_: s_ * jnp.exp(-l_) + _mm(s_, p_) + q_, S, cl, P, Q)
    for q in rng:
        s_scr[q] = S[q]
    dlt = each(lambda y_: y_ - head_sum(y_) * (1.0 / R_HEAD), y)
    var = each(lambda d_: head_sum(d_ * d_) * (1.0 / R_HEAD), dlt)
    bonus = each(lambda r_, k_, v_, q: head_sum(r_ * k_ * par(rk_ref, q)) * v_, r, k, v, rng)
    for q, (gi, pp) in enumerate(probs):
        yn = dlt[q] * lax.rsqrt(var[q] + R_LN_EPS) * par(lnw_ref, q) + par(lnb_ref, q)
        y_ref[gi, :, pp * R_PAIR:(pp + 1) * R_PAIR] = (yn + bonus[q]) * g_l[q]

    @pl.when(step == n_steps - 1)
    def _():
        for q, (gi, pp) in enumerate(probs):
            sfin_ref[gi, 2 * pp] = S[q][0:R_HEAD, 0:R_HEAD]
            sfin_ref[gi, 2 * pp + 1] = S[q][R_HEAD:, R_HEAD:]


def _stacked_out(kernel_fn, in_specs, args, out_index, stack):
    pos = len(args)

    def with_alias(*refs):
        return kernel_fn(*refs[:pos], *refs[pos + 1:])

    return with_alias, in_specs + [pl.BlockSpec(memory_space=pl.ANY)], args + [stack], {pos: out_index}


def rwkv_recurrence(p3, shift_prev, mu, wa, g2, w0, a0, k_k, k_a, s0, l, r_k, ln_w, ln_b, stack, *, C, G, NP,
                    t_real):
    B, T, _ = p3.shape
    n_steps = T // C
    n_pairs = D // R_PAIR
    W = NP * R_PAIR
    assert W == 256 and t_real <= C and (t_real == C or n_steps == 1)
    seg = D // W
    cblk = lambda off: pl.BlockSpec((G, C, W), lambda b, p, s: (b, s, off + p))
    sblk = lambda off: pl.BlockSpec((G, W), lambda b, p, s: (b, off + p))
    mblk = lambda off: pl.BlockSpec((1, W), lambda b, p, s: (0, off + p))
    lora_c = pl.BlockSpec((G, C, W), lambda b, p, s: (b, s, 3 * seg))
    lora_s = pl.BlockSpec((G, W), lambda b, p, s: (b, 3 * seg))
    lora_m = pl.BlockSpec((1, W), lambda b, p, s: (0, 3 * seg))
    wblk = lambda off: pl.BlockSpec((128, W), lambda b, p, s: (0, off + p))
    par = lambda: pl.BlockSpec((1, W), lambda b, p, s: (0, p))
    out_blk = pl.BlockSpec((G, C, W), lambda b, p, s: (b, s, p))
    st_spec = lambda: pl.BlockSpec((None, G, 2 * NP, R_HEAD, R_HEAD), lambda b, p, s: (l, b, p, 0, 0))
    in_specs = ([cblk(0), cblk(seg), cblk(2 * seg), lora_c,
                 sblk(0), sblk(seg), sblk(2 * seg), lora_s,
                 mblk(0), mblk(seg), mblk(2 * seg), lora_m,
                 wblk(0), wblk(seg), wblk(0)]
                + [par() for _ in range(4)] + [st_spec(), par(), par(), par()])
    kern, in_specs, args, aliases = _stacked_out(
        functools.partial(_rwkv_rec_kernel, C=C, G=G, NP=NP, n_steps=n_steps, t_real=t_real), in_specs,
        [p3, p3, p3, p3, shift_prev, shift_prev, shift_prev, shift_prev, mu, mu, mu, mu, wa, wa, g2,
         w0, a0, k_k, k_a, s0, r_k, ln_w, ln_b], 1, stack)
    y, sfin = pl.pallas_call(
        kern,
        out_shape=[jax.ShapeDtypeStruct((B, T, D), F32),
                   jax.ShapeDtypeStruct((s0.shape[0], B, 2 * n_pairs, R_HEAD, R_HEAD), F32)],
        grid=(B // G, n_pairs // NP, n_steps),
        in_specs=in_specs,
        out_specs=[out_blk, st_spec()],
        scratch_shapes=[pltpu.VMEM((G * NP, R_PAIR, R_PAIR), F32), pltpu.VMEM((4, G, W), F32)],
        input_output_aliases=aliases,
        compiler_params=_cparams(("parallel", "parallel", "arbitrary")),
        name="rwkv_rec",
    )(*args)
    return y, sfin


def _s5_kernel(u_ref, h0r_ref, h0i_ref, ar_ref, ai_ref, bre_ref, bim_ref, cre_ref, cim_ref, d_ref, *rest,
               tc, nb, kb, n_steps, permute):
    if permute:
        perm_ref, permt_ref = rest[:2]
        rest = rest[2:]
    y_ref, hr_out, hi_out, inr_scr, ini_scr, hr_scr, hi_scr = rest
    step = pl.program_id(1)
    rows = nb * tc

    @pl.when(step == 0)
    def _():
        hr_scr[...] = h0r_ref[...]
        hi_scr[...] = h0i_ref[...]

    u = u_ref[...].reshape(rows, kb * S5_BLK_CH)
    ch = lambda j: slice(j * S5_BLK_CH, (j + 1) * S5_BLK_CH)
    stt = lambda j: slice(j * S5_BLK_ST, (j + 1) * S5_BLK_ST)
    if permute:
        ut = jnp.dot(perm_ref[...], u.astype(_MXU), preferred_element_type=F32).astype(_MXU)
    else:
        ut = u.astype(_MXU)
    for j in range(kb):
        inr_scr[:, stt(j)] = jnp.dot(ut[:, ch(j)], bre_ref[j], preferred_element_type=F32)
        ini_scr[:, stt(j)] = jnp.dot(ut[:, ch(j)], bim_ref[j], preferred_element_type=F32)
    ar = ar_ref[...]
    ai = ai_ref[...]

    def body(t, carry):
        hr, hi = carry
        sl = pl.ds(pl.multiple_of(t * nb, nb), nb)
        nr = ar * hr - ai * hi + inr_scr[sl, :]
        ni = ar * hi + ai * hr + ini_scr[sl, :]
        inr_scr[sl, :] = nr
        ini_scr[sl, :] = ni
        return nr, ni

    hr, hi = lax.fori_loop(0, tc, body, (hr_scr[...], hi_scr[...]))
    hr_scr[...] = hr
    hi_scr[...] = hi
    yts = [jnp.dot(inr_scr[:, stt(j)].astype(_MXU), cre_ref[j], preferred_element_type=F32)
           - jnp.dot(ini_scr[:, stt(j)].astype(_MXU), cim_ref[j], preferred_element_type=F32) for j in range(kb)]
    yt = yts[0] if kb == 1 else jnp.concatenate(yts, axis=1)
    if permute:
        yt_hi = yt.astype(_MXU)
        yt_lo = (yt - yt_hi.astype(F32)).astype(_MXU)
        yt = (jnp.dot(permt_ref[...], yt_hi, preferred_element_type=F32)
              + jnp.dot(permt_ref[...], yt_lo, preferred_element_type=F32))
    y = yt + d_ref[...] * u
    y = 0.5 * y * (1.0 + jnp.tanh(math.sqrt(2.0 / math.pi) * (y + 0.044715 * (y * y * y))))
    y_ref[...] = y.reshape(y_ref.shape)

    @pl.when(step == n_steps - 1)
    def _():
        hr_out[...] = hr
        hi_out[...] = hi


def _time_major_perm(nb, tc):
    rows = nb * tc
    j = np.arange(rows)
    perm_np = np.zeros((rows, rows), np.float32)
    perm_np[(j % tc) * nb + j // tc, j] = 1.0
    return perm_np


def s5_scan(u3, h0r, h0i, l, ar, ai, bre, bim, cre, cim, d, *, nb, tc, blk, kb):
    A, R, _ = u3.shape
    ga, tr = blk
    assert ga * tr == nb * tc
    tmajor = (ga, tr) == (tc, nb) and R == nb
    nblk = D // S5_BLK_CH
    n_steps = A // ga if tmajor else R // tr
    rows = nb * tc
    u_spec = (pl.BlockSpec((ga, tr, kb * S5_BLK_CH), lambda c, s: (s, 0, c)) if tmajor
              else pl.BlockSpec((ga, tr, kb * S5_BLK_CH), lambda c, s: (0, s, c)))
    if tmajor:
        perm_specs, perm_args = [], []
    else:
        assert A == ga
        perm_np = _time_major_perm(nb, tc)
        perm_specs = [_const_spec((rows, rows)), _const_spec((rows, rows))]
        perm_args = [jnp.asarray(perm_np, _MXU), jnp.asarray(perm_np.T, _MXU)]
    st = lambda: pl.BlockSpec((None, nb, kb * S5_BLK_ST), lambda c, s: (l, 0, c))
    vec = lambda w: pl.BlockSpec((1, kb * w), lambda c, s: (0, c))
    mat = lambda a, b: pl.BlockSpec((kb, a, b), lambda c, s: (c, 0, 0))
    return pl.pallas_call(
        functools.partial(_s5_kernel, tc=tc, nb=nb, kb=kb, n_steps=n_steps, permute=not tmajor),
        out_shape=[jax.ShapeDtypeStruct(u3.shape, F32),
                   jax.ShapeDtypeStruct((nb, nblk * S5_BLK_ST), F32),
                   jax.ShapeDtypeStruct((nb, nblk * S5_BLK_ST), F32)],
        grid=(nblk // kb, n_steps),
        in_specs=[u_spec,
                  st(), st(), vec(S5_BLK_ST), vec(S5_BLK_ST),
                  mat(S5_BLK_CH, S5_BLK_ST), mat(S5_BLK_CH, S5_BLK_ST),
                  mat(S5_BLK_ST, S5_BLK_CH), mat(S5_BLK_ST, S5_BLK_CH),
                  vec(S5_BLK_CH)] + perm_specs,
        out_specs=[u_spec,
                   pl.BlockSpec((nb, kb * S5_BLK_ST), lambda c, s: (0, c)),
                   pl.BlockSpec((nb, kb * S5_BLK_ST), lambda c, s: (0, c))],
        scratch_shapes=[pltpu.VMEM((rows, kb * S5_BLK_ST), F32), pltpu.VMEM((rows, kb * S5_BLK_ST), F32),
                        pltpu.VMEM((nb, kb * S5_BLK_ST), F32), pltpu.VMEM((nb, kb * S5_BLK_ST), F32)],
        compiler_params=_cparams(("parallel", "arbitrary")),
        name="s5_scan",
    )(u3, h0r, h0i, ar, ai, bre, bim, cre, cim, d, *perm_args)


def _s5_discretize(a_re, a_im, log_dt, b_re, b_im, c_re, c_im):
    g, p, hch = b_re.shape
    dt = jnp.exp(log_dt.astype(F32))[:, None]
    mag = jnp.exp(dt * a_re)
    abar_re = mag * jnp.cos(dt * a_im)
    abar_im = mag * jnp.sin(dt * a_im)
    den = a_re * a_re + a_im * a_im
    nr = abar_re - 1.0
    q_re = (nr * a_re + abar_im * a_im) / den
    q_im = (abar_im * a_re - nr * a_im) / den
    bb_re = q_re[..., None] * b_re - q_im[..., None] * b_im
    bb_im = q_re[..., None] * b_im + q_im[..., None] * b_re
    nblk = D // S5_BLK_CH
    gl = g // nblk
    eye = jnp.eye(gl, dtype=F32)

    def in_blocks(bb):
        t = jnp.transpose(bb, (0, 2, 1)).reshape(nblk, gl, hch, p)
        return jnp.einsum('cghp,gk->cghkp', t, eye).reshape(nblk, gl * hch, gl * p).astype(_MXU)

    def out_blocks(cc):
        t = jnp.transpose(cc, (0, 2, 1)).reshape(nblk, gl, p, hch)
        return jnp.einsum('cgph,gk->cgpkh', t, eye).reshape(nblk, gl * p, gl * hch).astype(_MXU)

    return (abar_re.reshape(1, g * p), abar_im.reshape(1, g * p), in_blocks(bb_re), in_blocks(bb_im),
            out_blocks(c_re), out_blocks(c_im))


def _ssd_kernel(xbc_ref, z_ref, dt_ref, cprev_ref, h0_ref, cw_ref, cb_ref, dtb_ref, alog_ref, dx_ref, ng_ref,
                eh_ref, y_ref, hfin_ref, ext_scr, h_scr, *, L, G, t_real, n_steps):
    step = pl.program_id(1)
    rng = range(G)
    NB = M_GROUPS * M_STATE

    def each(f, *lists):
        return [f(*xs) for xs in zip(*lists)]

    @pl.when(step == 0)
    def _():
        for s in rng:
            ext_scr[s] = jnp.zeros((8, M_CONV_DIM), F32)
            ext_scr[s, 5:8, :] = cprev_ref[s]
            for gi in range(M_GROUPS):
                h_scr[s, gi] = h0_ref[s, gi * M_HPG:(gi + 1) * M_HPG].reshape(M_GW, M_STATE).T

    tril = jnp.where(_iota((L, L), 1) <= _iota((L, L), 0), 1.0, 0.0).astype(_MXU)
    causal = _iota((L, L), 1) <= _iota((L, L), 0)
    eye_h = _eye(128, _MXU)
    eh = eh_ref[...]
    lane_head = _iota((1, M_GW), 1) // M_HEAD
    row8 = _iota((8, 1), 0)
    a = -jnp.exp(alog_ref[...])
    conv, dts = [], []
    for s in rng:
        x = xbc_ref[s * L:(s + 1) * L, :]
        prev8 = ext_scr[s]
        c = cb_ref[...] + cw_ref[3:4, :] * x
        for j in range(1, 4):
            xr = pltpu.roll(x, j, axis=0)
            head = jnp.where(row8 < j, pltpu.roll(prev8, j, axis=0), xr[0:8])
            xr = head if L == 8 else jnp.concatenate([head, xr[8:]], axis=0)
            c = c + cw_ref[3 - j:4 - j, :] * xr
        ext_scr[s] = x[L - 8:L]
        conv.append(c * _sigmoid(c))
        dt = _softplus(dt_ref[s * L:(s + 1) * L, :] + dtb_ref[...])
        if t_real < L:
            dt = jnp.where(_iota(dt.shape, 0) < t_real, dt, 0.0)
        dts.append(dt)
    xs = each(lambda c: c[:, 0:M_WIDTH], conv)
    acum = each(lambda d_: _mm_sel_l(tril, d_ * a), dts)
    acum_t = each(lambda x: _mm_nt_sel_l(eye_h, x), acum)
    dt_x = each(lambda d_: _mm_sel_r(d_, eh), dts)
    acum_x = each(lambda x: _mm_sel_r(x, eh), acum)
    acl_x = each(lambda x: x[L - 1:L, :], acum_x)
    xd = each(lambda x, d_: x * d_, xs, dt_x)
    xdd = each(lambda x, l_, c_: x * jnp.exp(l_ - c_), xd, acl_x, acum_x)
    eacum_x = each(jnp.exp, acum_x)
    cdec_x = each(jnp.exp, acl_x)
    ys = [[] for _ in rng]
    for gi in range(M_GROUPS):
        gs = slice(gi * M_GW, (gi + 1) * M_GW)
        bg = each(lambda c: c[:, M_WIDTH + gi * M_STATE:M_WIDTH + (gi + 1) * M_STATE], conv)
        cg = each(lambda c: c[:, M_WIDTH + NB + gi * M_STATE:M_WIDTH + NB + (gi + 1) * M_STATE], conv)
        h = [h_scr[s, gi] for s in rng]
        cbm = each(_mm_nt, cg, bg)
        bgt = each(lambda b_: _mm_nt(eye_h, b_), bg)
        y_off = each(lambda c_, h_, e_: _mm(c_, h_) * e_[:, gs], cg, h, eacum_x)
        ms = []
        for e in range(M_HPG):
            he = gi * M_HPG + e
            ms.append(each(lambda c_, a_, t_: c_ * jnp.exp(jnp.where(causal, a_[:, he:he + 1] - t_[he:he + 1, :],
                                                                     -jnp.inf)), cbm, acum, acum_t))
        if L % 128 == 0:
            mcat = [jnp.concatenate([ms[e][s] for e in range(M_HPG)], axis=1) for s in rng]
            xst = each(lambda x: jnp.concatenate([jnp.where(lane_head == e, x[:, gs], 0.0).astype(_MXU)
                                                  for e in range(M_HPG)], axis=0), xd)
            y_dg = each(_mm, mcat, xst)
        else:
            y_dg = [sum(_mm(ms[e][s], jnp.where(lane_head == e, xd[s][:, gs], 0.0)) for e in range(M_HPG))
                    for s in rng]
        hn = each(lambda h_, d_, b_, x: h_ * d_[:, gs] + _mm(b_, x[:, gs]), h, cdec_x, bgt, xdd)
        for s in rng:
            h_scr[s, gi] = hn[s]
            ys[s].append(y_off[s] + y_dg[s])
    for s in rng:
        y = jnp.concatenate(ys[s], axis=1) + dx_ref[...] * xs[s]
        zz = z_ref[s * L:(s + 1) * L, :]
        y = y * (zz * _sigmoid(zz))
        outs = []
        for gi in range(M_GROUPS):
            yg = y[:, gi * M_GW:(gi + 1) * M_GW]
            outs.append(yg * lax.rsqrt(jnp.mean(yg * yg, axis=-1, keepdims=True) + EPS))
        y_ref[s * L:(s + 1) * L, :] = jnp.concatenate(outs, axis=1) * ng_ref[...]

    @pl.when(step == n_steps - 1)
    def _():
        for s in rng:
            for gi in range(M_GROUPS):
                hfin_ref[s, gi * M_HPG:(gi + 1) * M_HPG] = h_scr[s, gi].T.reshape(M_HPG, M_HEAD, M_STATE)


def ssd_block(xbc, z, dt, conv_prev, h0, l, conv_w, conv_b, dt_bias, a_log, d_x, norm_g, eh, stack, *, B, T, L, G,
              t_real):
    n_steps = T // L
    assert G == 1 or n_steps == 1
    heads = M_GROUPS * M_HPG
    seq = lambda w: pl.BlockSpec((G * L, w), lambda b, s: (b * n_steps + s, 0))
    st_spec = lambda: pl.BlockSpec((None, G, heads, M_HEAD, M_STATE), lambda b, s: (l, b, 0, 0, 0))
    in_specs = [seq(M_CONV_DIM), seq(M_WIDTH), seq(128),
                pl.BlockSpec((None, G, 3, M_CONV_DIM), lambda b, s: (l, b, 0, 0)),
                st_spec(),
                _const_spec((4, M_CONV_DIM)), _const_spec((1, M_CONV_DIM)),
                _const_spec((1, 128)), _const_spec((1, 128)),
                _const_spec((1, M_WIDTH)), _const_spec((1, M_WIDTH)),
                _const_spec((128, M_WIDTH))]
    kern, in_specs, args, aliases = _stacked_out(
        functools.partial(_ssd_kernel, L=L, G=G, t_real=t_real, n_steps=n_steps), in_specs,
        [xbc, z, dt, conv_prev, h0, conv_w, conv_b, dt_bias, a_log, d_x, norm_g, eh], 1, stack)
    return pl.pallas_call(
        kern,
        out_shape=[jax.ShapeDtypeStruct((B * T, M_WIDTH), F32),
                   jax.ShapeDtypeStruct((h0.shape[0], B, heads, M_HEAD, M_STATE), F32)],
        grid=(B // G, n_steps),
        in_specs=in_specs,
        out_specs=[seq(M_WIDTH), st_spec()],
        scratch_shapes=[pltpu.VMEM((G, 8, M_CONV_DIM), F32),
                        pltpu.VMEM((G, M_GROUPS, M_STATE, M_GW), F32)],
        input_output_aliases=aliases,
        compiler_params=_cparams(("parallel", "arbitrary")),
        name="ssd",
    )(*args)


def _attn_kernel(q_ref, k_ref, v_ref, o_ref, *, nb, tq, heads_split):
    scale = XA_HD ** -0.5
    for j in range(nb):
        if heads_split:
            k_all = pltpu.einshape("mhd->hmd", k_ref[j])
            v_all = pltpu.einshape("mhd->hmd", v_ref[j])
        outs = []
        for hd in range(XA_HEADS):
            cs = slice(hd * XA_HD, (hd + 1) * XA_HD)
            q = q_ref[j * tq:(j + 1) * tq, cs]
            kh = k_all[hd] if heads_split else k_ref[j, :, cs]
            vh = v_all[hd] if heads_split else v_ref[j, :, cs]
            s = _mm_nt(q, kh) * scale
            s = s - jnp.max(s, axis=-1, keepdims=True)
            p = jnp.exp(s)
            p = p / jnp.sum(p, axis=-1, keepdims=True)
            outs.append(_mm(p, vh))
        o_ref[j * tq:(j + 1) * tq, :] = jnp.concatenate(outs, axis=1)


def cross_attention(q, mk, mv, kv_index, *, B, T, tq, nb):
    nlead = len(kv_index)
    heads_split = mk.ndim - nlead == 4
    kv_blk = mk.shape[nlead + 1:]
    n_t = T // tq
    kv_spec = pl.BlockSpec((None,) * nlead + (nb,) + kv_blk,
                           lambda b, s: tuple(kv_index) + (b,) + (0,) * len(kv_blk))
    q_spec = pl.BlockSpec((nb * tq, D), lambda b, s: (b * n_t + s, 0))
    return pl.pallas_call(
        functools.partial(_attn_kernel, nb=nb, tq=tq, heads_split=heads_split),
        out_shape=jax.ShapeDtypeStruct((B * T, D), F32),
        grid=(B // nb, n_t),
        in_specs=[q_spec, kv_spec, kv_spec],
        out_specs=q_spec,
        compiler_params=_cparams(("parallel", "arbitrary")),
        name="xattn",
    )(q, mk, mv)


def _xattn_block_kernel(h_ref, gq_ref, wq_ref, k_ref, v_ref, wo_ref, go_ref, o_ref):
    h = h_ref[...]
    q = _mm(_rms(h, gq_ref[...]), wq_ref[...])
    scale = XA_HD ** -0.5
    outs = []
    for hd in range(XA_HEADS):
        cs = slice(hd * XA_HD, (hd + 1) * XA_HD)
        s = _mm_nt(q[:, cs], k_ref[0, :, cs]) * scale
        s = s - jnp.max(s, axis=-1, keepdims=True)
        p = jnp.exp(s)
        p = p / jnp.sum(p, axis=-1, keepdims=True)
        outs.append(_mm(p, v_ref[0, :, cs]))
    xa = _mm(jnp.concatenate(outs, axis=1), wo_ref[...])
    o_ref[...] = h + _rms(xa, go_ref[...])


def cross_attention_block(h, gq, wq, mk, mv, wo, go, *, B, T, tq):
    m = mk.shape[1]
    n_t = T // tq
    row = pl.BlockSpec((tq, D), lambda b, s: (b * n_t + s, 0))
    kv = pl.BlockSpec((1, m, D), lambda b, s: (b, 0, 0))
    return pl.pallas_call(
        _xattn_block_kernel,
        out_shape=jax.ShapeDtypeStruct((B * T, D), F32),
        grid=(B, n_t),
        in_specs=[row, _const_spec((1, D)), _const_spec((D, D)), kv, kv, _const_spec((D, D)), _const_spec((1, D))],
        out_specs=row,
        compiler_params=_cparams(("parallel", "arbitrary")),
        name="xattn_block",
    )(h, gq, wq, mk, mv, wo, go)


def _pad_time(x2, B, T, Tp):
    if Tp == T:
        return x2
    w = x2.shape[-1]
    return jnp.pad(x2.reshape(B, T, w), ((0, 0), (0, Tp - T), (0, 0))).reshape(B * Tp, w)


def _layer(l, h, grp, W, st, acc):
    T, B = grp['T'], grp['B']
    n = T * B
    gates = norm_matmul(h, W['g_mix_pre'], W['w_gates'], name="in_gates")
    p_r = norm_matmul(h, W['g_mix_pre'], W['w_rwkv'], name="in_rwkv")
    if grp['s5_tmajor']:
        u3 = norm_matmul_tmajor(h.reshape(B, T, D), W['g_mix_pre'], W['w_s5'], name="in_s5")
    else:
        u3 = norm_matmul(h, W['g_mix_pre'], W['w_s5'], name="in_s5").reshape(1, n, D)
    z_m =norm_matmul(h, W['g_mix_pre'], W['w_z'], name="in_z")
    xbc = norm_matmul(h, W['g_mix_pre'], W['w_xbc'], name="in_xbc")
    dt_m = norm_matmul(h, W['g_mix_pre'], W['w_dt'], tn=128, name="in_dt")

    Tp = grp['rwkv_Tpad']
    p3 = _pad_time(p_r, B, T, Tp).reshape(B, Tp, p_r.shape[-1])
    yr, s_fin = rwkv_recurrence(p3, st['shift'][l], W['mu'], W['wa'], W['g2'], W['w0'], W['a0'], W['k_k'],
                                W['k_a'], st['rwkv'], l, W['r_k'], W['ln_w'], W['ln_b'], acc[0],
                                C=grp['rwkv_C'], G=grp['rwkv_G'], NP=grp['rwkv_NP'], t_real=min(T, grp['rwkv_C']))
    yr = yr[:, :T].reshape(n, D)
    shift_new = p_r.reshape(B, T, -1)[:, T - 1]

    ys, s5r, s5i = s5_scan(u3, st['s5r'], st['s5i'], l, W['s5_ar'], W['s5_ai'], W['s5_bre'], W['s5_bim'],
                           W['s5_cre'], W['s5_cim'], W['s5_d'], nb=B, tc=grp['s5_tc'], blk=grp['s5_blk'],
                           kb=2 if B <= 16 else 1)

    Lc = grp['ssd_L']
    Tm = grp['ssd_Tpad']
    ym, ssm_fin = ssd_block(_pad_time(xbc, B, T, Tm), _pad_time(z_m, B, T, Tm), _pad_time(dt_m, B, T, Tm),
                            st['conv'], st['ssm'], l, W['conv_w'], W['conv_b'], W['dt_bias'],
                            W['a_log'], W['d_x'], W['m_norm'], W['eh'], acc[1], B=B, T=Tm, L=Lc, G=grp['ssd_G'],
                            t_real=min(T, Lc))
    ym = ym.reshape(B, Tm, M_WIDTH)[:, :T].reshape(n, M_WIDTH)
    conv_new = jnp.concatenate([st['conv'][l], xbc.reshape(B, T, M_CONV_DIM)[:, max(T - 3, 0):]], axis=1)[:, -3:]

    h = mixer_merge(gates, yr, ys if grp['s5_tmajor'] else ys.reshape(n, D), ym, h, W['w_out_rwkv'],
                    W['s5_w_glu'], W['w_out_s5'], W['w_out_mamba'], W['w_out'], W['g_mix_post'],
                    seq=(B, T) if grp['s5_tmajor'] else None)

    if grp['xa_nb'] == 1 and st['mk'].ndim == 3:
        h = cross_attention_block(h, W['g_xa_pre'], W['xa_wq'], st['mk'], st['mv'], W['xa_wo'], W['g_xa_post'],
                                  B=B, T=T, tq=grp['xa_tq'])
    else:
        q = norm_matmul(h, W['g_xa_pre'], W['xa_wq'], name="xa_q")
        o = cross_attention(q, st['mk'], st['mv'], st['kv_index'](l), B=B, T=T, tq=grp['xa_tq'],
                            nb=grp['xa_nb'])
        h = matmul_norm_residual(o, W['xa_wo'], W['g_xa_post'], h, name="xa_out")

    h = mlp_block(h, W['g_mlp_pre'], W['mlp_w1'], W['mlp_w2'], W['g_mlp_post'])
    return h, (s_fin, shift_new, s5r.reshape(B, 64, 64), s5i.reshape(B, 64, 64), conv_new, ssm_fin)


def _group_cfg(T, B):
    cfg = dict(T=T, B=B)
    g = 8 if B % 8 == 0 else 1
    if T % 64 == 0:
        cfg.update(rwkv_C=64, rwkv_Tpad=T, rwkv_G=g, rwkv_NP=2)
    else:
        tp = -(-T // 8) * 8
        cfg.update(rwkv_C=tp, rwkv_Tpad=tp, rwkv_G=16 if B % 16 == 0 else g, rwkv_NP=2)
    if T % 8 == 0 and (1024 // B) >= 8 and T % (1024 // B) == 0:
        tc = 1024 // B
        cfg.update(s5_tmajor=True, s5_tc=tc, s5_blk=(tc, B))
    else:
        cfg.update(s5_tmajor=False, s5_tc=T, s5_blk=(1, B * T))
    if T % 128 == 0:
        cfg.update(ssd_L=128, ssd_Tpad=T, ssd_G=1)
    else:
        tp = -(-T // 8) * 8
        cfg.update(ssd_L=tp, ssd_Tpad=tp, ssd_G=4 if B % 4 == 0 else 1)
    if T >= 64:
        cfg.update(xa_tq=_row_tile(T, 512), xa_nb=1)
    else:
        cfg.update(xa_tq=T, xa_nb=4 if B % 4 == 0 else 1)
    return cfg


def kernel(x_prompt, x_sample, cache_mem_k, cache_mem_v, state_rwkv, state_rwkv_shift, state_s5_re, state_s5_im, state_conv, state_ssm, mem_prompt, norm_mix_pre, norm_mix_post, norm_xa_pre, norm_xa_post, norm_mlp_pre, norm_mlp_post, norm_mem, w_in, w_out, rwkv_mu, rwkv_w0, rwkv_w2, rwkv_a0, rwkv_a2, rwkv_g2, rwkv_k_k, rwkv_k_a, rwkv_r_k, rwkv_ln_w, rwkv_ln_b, w_out_rwkv, s5_a_re, s5_a_im, s5_log_dt, s5_b_re, s5_b_im, s5_c_re, s5_c_im, s5_d, s5_w_glu, w_out_s5, m_conv_w, m_conv_b, m_dt_bias, m_a_log, m_d, m_norm, w_out_mamba, xa_wq, xa_wk, xa_wv, xa_wo, mlp_w1, mlp_w2):
    depth = w_in.shape[0]
    bp, tp, _ = x_prompt.shape
    bs, ts, _ = x_sample.shape
    mlen = mem_prompt.shape[1]
    bf = lambda x: x.astype(_MXU)
    row = lambda x: x.reshape(1, -1).astype(F32)

    eh = (jnp.arange(128)[:, None] == (jnp.arange(M_WIDTH)[None, :] // M_HEAD)).astype(_MXU)

    col = [0]
    for sz in (3 * D, 3 * D + 256, D, M_WIDTH, M_CONV_DIM, 32):
        col.append(col[-1] + sz)
    layers = []
    w_in_b = bf(w_in)
    big = {name: bf(w) for name, w in dict(
        w_out_rwkv=w_out_rwkv, s5_w_glu=s5_w_glu, w_out_s5=w_out_s5, w_out_mamba=w_out_mamba, w_out=w_out,
        xa_wq=xa_wq, xa_wk=xa_wk, xa_wv=xa_wv, xa_wo=xa_wo, mlp_w1=mlp_w1, mlp_w2=mlp_w2).items()}
    for l in range(depth):
        wl = w_in_b[l]
        ar, ai, bre, bim, cre, cim = _s5_discretize(s5_a_re[l], s5_a_im[l], s5_log_dt[l], s5_b_re[l], s5_b_im[l],
                                                    s5_c_re[l], s5_c_im[l])
        zero = jnp.zeros((64, D), F32)
        wa = jnp.concatenate([jnp.concatenate([rwkv_w2[l], zero], axis=1),
                              jnp.concatenate([zero, rwkv_a2[l]], axis=1)], axis=0)
        pad32 = lambda x: jnp.pad(x.reshape(1, -1).astype(F32), ((0, 0), (0, 128 - x.shape[-1])))
        layers.append(dict(
            g_mix_pre=row(norm_mix_pre[l]), g_mix_post=row(norm_mix_post[l]),
            g_xa_pre=row(norm_xa_pre[l]), g_xa_post=row(norm_xa_post[l]),
            g_mlp_pre=row(norm_mlp_pre[l]), g_mlp_post=row(norm_mlp_post[l]),
            w_gates=bf(wl[:, col[0]:col[1]]), w_rwkv=bf(wl[:, col[1]:col[2]]), w_s5=bf(wl[:, col[2]:col[3]]),
            w_z=bf(wl[:, col[3]:col[4]]), w_xbc=bf(wl[:, col[4]:col[5]]),
            w_dt=bf(jnp.pad(wl[:, col[5]:col[6]], ((0, 0), (0, 96)))),
            mu=row(rwkv_mu[l]), wa=bf(wa), w0=row(rwkv_w0[l]), a0=row(rwkv_a0[l]), g2=bf(rwkv_g2[l]),
            k_k=row(rwkv_k_k[l]), k_a=row(rwkv_k_a[l]), r_k=row(rwkv_r_k[l]),
            ln_w=row(rwkv_ln_w[l]), ln_b=row(rwkv_ln_b[l]),
            s5_ar=ar, s5_ai=ai, s5_bre=bre, s5_bim=bim, s5_cre=cre, s5_cim=cim, s5_d=row(s5_d[l]),
            conv_w=m_conv_w[l].astype(F32), conv_b=row(m_conv_b[l]), dt_bias=pad32(m_dt_bias[l]),
            a_log=pad32(m_a_log[l]), d_x=row(jnp.repeat(m_d[l], M_HEAD)), m_norm=row(m_norm[l]), eh=eh,
            g_mem=row(norm_mem[l]), **{name: w[l] for name, w in big.items()},
        ))

    hp = x_prompt.reshape(bp * tp, D).astype(F32)
    hs = x_sample.reshape(bs * ts, D).astype(F32)
    mem2 = mem_prompt.reshape(bp * mlen, D).astype(F32)

    cfg_p = _group_cfg(tp, bp)
    cfg_s = _group_cfg(ts, bs)
    zeros_p = dict(
        rwkv=jnp.zeros((depth, bp) + state_rwkv.shape[2:], F32),
        shift=jnp.zeros((depth, bp, state_rwkv_shift.shape[-1]), F32),
        s5r=jnp.zeros((depth, bp, state_s5_re.shape[2] * state_s5_re.shape[3]), F32),
        s5i=jnp.zeros((depth, bp, state_s5_re.shape[2] * state_s5_re.shape[3]), F32),
        conv=jnp.zeros((depth, bp) + state_conv.shape[2:], F32),
        ssm=jnp.zeros((depth, bp) + state_ssm.shape[2:], F32),
    )
    st_s = dict(
        rwkv=state_rwkv, shift=state_rwkv_shift,
        s5r=state_s5_re.reshape(depth, bs, -1), s5i=state_s5_im.reshape(depth, bs, -1),
        conv=state_conv, ssm=state_ssm,
        mk=cache_mem_k, mv=cache_mem_v,
        kv_index=lambda l: (l,),
    )

    mk_out, mv_out = [], []
    st_p_out = [[] for _ in range(6)]
    st_s_out = [[] for _ in range(6)]
    acc_p = (jnp.zeros_like(zeros_p['rwkv']), jnp.zeros_like(zeros_p['ssm']))
    acc_s = (jnp.zeros_like(state_rwkv, dtype=F32), jnp.zeros_like(state_ssm, dtype=F32))
    for l in range(depth):
        W = layers[l]
        mk = norm_matmul(mem2, W['g_mem'], W['xa_wk'], name="mem_k")
        mv = norm_matmul(mem2, W['g_mem'], W['xa_wv'], name="mem_v")
        st_p = dict(zeros_p, mk=mk.reshape(bp, mlen, D), mv=mv.reshape(bp, mlen, D), kv_index=lambda l: ())
        hp, new_p = _layer(l, hp, cfg_p, W, st_p, acc_p)
        hs, new_s = _layer(l, hs, cfg_s, W, st_s, acc_s)
        acc_p = (new_p[0], new_p[5])
        acc_s = (new_s[0], new_s[5])
        mk_out.append(mk.reshape(bp, mlen, XA_HEADS, XA_HD))
        mv_out.append(mv.reshape(bp, mlen, XA_HEADS, XA_HD))
        for i in range(1, 5):
            st_p_out[i].append(new_p[i])
            st_s_out[i].append(new_s[i])

    stk = lambda xs: jnp.stack(xs).astype(F32)
    y_p = hp.reshape(bp, tp, D).astype(x_prompt.dtype)
    y_s = hs.reshape(bs, ts, D).astype(x_sample.dtype)
    return (y_p, y_s, stk(mk_out), stk(mv_out),
            acc_p[0], *[stk(st_p_out[i]) for i in range(1, 5)], acc_p[1],
            acc_s[0], *[stk(st_s_out[i]) for i in range(1, 5)], acc_s[1])
```

```python
import functools
import math

import numpy as np
import jax
import jax.numpy as jnp
from jax import lax
from jax.experimental import pallas as pl
from jax.experimental.pallas import tpu as pltpu

F32 = jnp.float32
_MXU = jnp.bfloat16
EPS = 1e-6
R_LN_EPS = 64e-5
D = 1024
R_HEAD = 64
R_PAIR = 2 * R_HEAD
S5_BLK_CH = 128
S5_BLK_ST = 512
M_HEAD = 64
M_STATE = 128
M_GROUPS = 4
M_HPG = 8
M_GW = M_HPG * M_HEAD
M_WIDTH = 2048
M_CONV_DIM = 3072
XA_HEADS = 4
XA_HD = 256
VMEM_LIMIT = 56 * 1024 * 1024


def _mm(a, b):
    return jnp.dot(a.astype(_MXU), b.astype(_MXU), preferred_element_type=F32)


def _mm_nt(a, b):
    return lax.dot_general(a.astype(_MXU), b.astype(_MXU), (((1,), (1,)), ((), ())),
                           preferred_element_type=F32)


def _mm_tn(a, b):
    return lax.dot_general(a.astype(_MXU), b.astype(_MXU), (((0,), (0,)), ((), ())),
                           preferred_element_type=F32)


def _split3(x):
    hi = x.astype(_MXU)
    r1 = x - hi.astype(F32)
    mid = r1.astype(_MXU)
    lo = (r1 - mid.astype(F32)).astype(_MXU)
    return hi, mid, lo


def _mm_sel_l(sel, x):
    return sum(jnp.dot(sel, p, preferred_element_type=F32) for p in _split3(x))


def _mm_sel_r(x, sel):
    return sum(jnp.dot(p, sel, preferred_element_type=F32) for p in _split3(x))


def _mm_nt_sel_l(sel, x):
    return sum(lax.dot_general(sel, p, (((1,), (1,)), ((), ())), preferred_element_type=F32)
               for p in _split3(x))


def _sigmoid(x):
    return 1.0 / (1.0 + jnp.exp(-x))


def _softplus(x):
    return jnp.maximum(x, 0.0) + jnp.log(1.0 + jnp.exp(-jnp.abs(x)))


def _rms(x, g):
    return x * lax.rsqrt(jnp.mean(x * x, axis=-1, keepdims=True) + EPS) * g


def _iota(shape, dim):
    return lax.broadcasted_iota(jnp.int32, shape, dim)


def _eye(n, dtype):
    return jnp.where(_iota((n, n), 0) == _iota((n, n), 1), 1.0, 0.0).astype(dtype)


def _cparams(sem):
    return pltpu.CompilerParams(dimension_semantics=sem, vmem_limit_bytes=VMEM_LIMIT)


def _row_tile(n, want):
    t = min(n, want)
    while n % t:
        t //= 2
    return t


def _col_tile(c, cap):
    best = 128
    for t in range(128, min(c, cap) + 1, 128):
        if c % t == 0:
            best = t
    return best


def _const_spec(shape):
    nd = len(shape)
    return pl.BlockSpec(shape, lambda *_: (0,) * nd, pipeline_mode=pl.Buffered(1))


def _norm_mm_kernel(x_ref, g_ref, w_ref, o_ref, xn_ref):
    @pl.when(pl.program_id(1) == 0)
    def _():
        xn_ref[...] = _rms(x_ref[...], g_ref[...]).astype(_MXU)

    o_ref[...] = jnp.dot(xn_ref[...], w_ref[...], preferred_element_type=F32)


def norm_matmul(x, g, w, *, tm=1024, tn=1664, name="norm_mm"):
    n, d = x.shape
    c = w.shape[1]
    tm = _row_tile(n, tm)
    tn = _col_tile(c, tn)
    return pl.pallas_call(
        _norm_mm_kernel,
        out_shape=jax.ShapeDtypeStruct((n, c), F32),
        grid=(n // tm, c // tn),
        in_specs=[pl.BlockSpec((tm, d), lambda i, j: (i, 0)),
                  pl.BlockSpec((1, d), lambda i, j: (0, 0)),
                  pl.BlockSpec((d, tn), lambda i, j: (0, j))],
        out_specs=pl.BlockSpec((tm, tn), lambda i, j: (i, j)),
        scratch_shapes=[pltpu.VMEM((tm, d), _MXU)],
        compiler_params=_cparams(("parallel", "arbitrary")),
        name=name,
    )(x, g, w)


def _mm_norm_res_kernel(x_ref, w_ref, g_ref, h_ref, o_ref):
    y = _mm(x_ref[...], w_ref[...])
    o_ref[...] = h_ref[...] + _rms(y, g_ref[...])


def matmul_norm_residual(x, w, g, h, *, tm=512, name="mm_norm_res"):
    n, k = x.shape
    d = w.shape[1]
    tm = _row_tile(n, tm)
    return pl.pallas_call(
        _mm_norm_res_kernel,
        out_shape=jax.ShapeDtypeStruct((n, d), F32),
        grid=(n // tm,),
        in_specs=[pl.BlockSpec((tm, k), lambda i: (i, 0)),
                  _const_spec((k, d)),
                  _const_spec((1, d)),
                  pl.BlockSpec((tm, d), lambda i: (i, 0))],
        out_specs=pl.BlockSpec((tm, d), lambda i: (i, 0)),
        compiler_params=_cparams(("parallel",)),
        name=name,
    )(x, w, g, h)


def _mlp_kernel(h_ref, g1_ref, w1_ref, w2_ref, g2_ref, o_ref, *, n_chunks, ck):
    h = h_ref[...]
    xn = _rms(h, g1_ref[...]).astype(_MXU)
    acc = jnp.zeros(h.shape, F32)
    for j in range(n_chunks):
        a = jnp.dot(xn, w1_ref[:, j * ck:(j + 1) * ck], preferred_element_type=F32)
        a = jnp.square(jnp.maximum(a, 0.0))
        acc = acc + jnp.dot(a.astype(_MXU), w2_ref[j * ck:(j + 1) * ck, :], preferred_element_type=F32)
    o_ref[...] = h + _rms(acc, g2_ref[...])


def mlp_block(h, g1, w1, w2, g2, *, tm=1024, ck=1024):
    n, d = h.shape
    f = w1.shape[1]
    tm = _row_tile(n, tm)
    return pl.pallas_call(
        functools.partial(_mlp_kernel, n_chunks=f // ck, ck=ck),
        out_shape=jax.ShapeDtypeStruct((n, d), F32),
        grid=(n // tm,),
        in_specs=[pl.BlockSpec((tm, d), lambda i: (i, 0)),
                  _const_spec((1, d)),
                  _const_spec((d, f)),
                  _const_spec((f, d)),
                  _const_spec((1, d))],
        out_specs=pl.BlockSpec((tm, d), lambda i: (i, 0)),
        compiler_params=_cparams(("parallel",)),
        name="mlp",
    )(h, g1, w1, w2, g2)


def _norm_mm_tmajor_kernel(x_ref, g_ref, w_ref, perm_ref, o_ref):
    nb, tt, d = x_ref.shape
    xn = _rms(x_ref[...].reshape(nb * tt, d), g_ref[...]).astype(_MXU)
    xt = jnp.dot(perm_ref[...], xn, preferred_element_type=F32).astype(_MXU)
    o_ref[...] = jnp.dot(xt, w_ref[...], preferred_element_type=F32).reshape(o_ref.shape)


def norm_matmul_tmajor(x3, g, w, *, rows=512, name="norm_mm_t"):
    nb, t, d = x3.shape
    c = w.shape[1]
    tt = rows // nb
    assert t % tt == 0 and tt % 8 == 0
    perm = jnp.asarray(_time_major_perm(nb, tt), _MXU)
    return pl.pallas_call(
        _norm_mm_tmajor_kernel,
        out_shape=jax.ShapeDtypeStruct((t, nb, c), F32),
        grid=(t // tt,),
        in_specs=[pl.BlockSpec((nb, tt, d), lambda i: (0, i, 0)),
                  _const_spec((1, d)), _const_spec((d, c)), _const_spec((nb * tt, nb * tt))],
        out_specs=pl.BlockSpec((tt, nb, c), lambda i: (i, 0, 0)),
        compiler_params=_cparams(("parallel",)),
        name=name,
    )(x3, g, w, perm)


def _merge_kernel(gates_ref, yr_ref, ys_ref, ym_ref, h_ref, wr_ref, wglu_ref, ws_ref, wm_ref, wo_ref, g_ref,
                  *rest, ys_tmajor):
    o_ref = rest[-1]
    rows2d = lambda ref: ref[...].reshape(-1, ref.shape[-1])
    o_r = _mm(rows2d(yr_ref), wr_ref[...])
    ys = rows2d(ys_ref)
    y3 = ys * _sigmoid(_mm(ys, wglu_ref[...]))
    if ys_tmajor:
        y3 = jnp.dot(rest[0][...], y3.astype(_MXU), preferred_element_type=F32)
    o_s = _mm(y3, ws_ref[...])
    o_m = _mm(rows2d(ym_ref), wm_ref[...])
    gates = rows2d(gates_ref)
    merged = (_sigmoid(gates[:, 0:D]) * o_r + _sigmoid(gates[:, D:2 * D]) * o_s
              + _sigmoid(gates[:, 2 * D:3 * D]) * o_m)
    mix = _mm(merged, wo_ref[...])
    o_ref[...] = (rows2d(h_ref) + _rms(mix, g_ref[...])).reshape(o_ref.shape)


def mixer_merge(gates, yr, ys, ym, h, wr, wglu, ws, wm, wo, g, *, tm=256, seq=None):
    n = h.shape[0]
    weights = [_const_spec((D, D)), _const_spec((D, D)), _const_spec((D, D)),
               _const_spec((M_WIDTH, D)), _const_spec((D, D)), _const_spec((1, D))]
    if seq is None:
        tm = _row_tile(n, tm)
        row = lambda w: pl.BlockSpec((tm, w), lambda i: (i, 0))
        in_specs = [row(3 * D), row(D), row(D), row(M_WIDTH), row(D)] + weights
        args = [gates, yr, ys, ym, h, wr, wglu, ws, wm, wo, g]
        out_spec, out_shape, grid = row(D), (n, D), (n // tm,)
    else:
        nb, t = seq
        tt = tm // nb
        assert t % tt == 0 and tt % 8 == 0
        row = lambda w: pl.BlockSpec((nb, tt, w), lambda i: (0, i, 0))
        v3 = lambda x: x.reshape(nb, t, x.shape[-1])
        in_specs = ([row(3 * D), row(D), pl.BlockSpec((tt, nb, D), lambda i: (i, 0, 0)), row(M_WIDTH), row(D)]
                    + weights + [_const_spec((nb * tt, nb * tt))])
        args = [v3(gates), v3(yr), ys, v3(ym), v3(h), wr, wglu, ws, wm, wo, g,
                jnp.asarray(_time_major_perm(nb, tt).T, _MXU)]
        out_spec, out_shape, grid = row(D), (nb, t, D), (t // tt,)
    out = pl.pallas_call(
        functools.partial(_merge_kernel, ys_tmajor=seq is not None),
        out_shape=jax.ShapeDtypeStruct(out_shape, F32),
        grid=grid,
        in_specs=in_specs,
        out_specs=out_spec,
        compiler_params=_cparams(("parallel",)),
        name="mixer_merge",
    )(*args)
    return out.reshape(n, D)


def _rwkv_rec_kernel(pr_ref, pk_ref, pv_ref, pl_ref, shr_ref, shk_ref, shv_ref, shl_ref,
                     mur_ref, muk_ref, muv_ref, mul_ref, w2_ref, a2_ref, g2_ref, w0_ref, a0_ref, kk_ref, ka_ref,
                     s0_ref, rk_ref, lnw_ref, lnb_ref,
                     y_ref, sfin_ref, s_scr, carry_scr, *, C, G, NP, n_steps, t_real):
    step = pl.program_id(2)
    C2 = 2 * C
    W = NP * R_PAIR
    probs = [(gi, pp) for gi in range(G) for pp in range(NP)]
    rng = range(len(probs))
    row_c = _iota((C, 1), 0)

    @pl.when(step == 0)
    def _():
        for j, sh_ref in enumerate((shr_ref, shk_ref, shv_ref, shl_ref)):
            carry_scr[j] = sh_ref[...]

    def shift_mix(j, p_ref, mu_ref):
        out = []
        for gi in range(G):
            x = p_ref[gi]
            prev = jnp.where(row_c == 0, carry_scr[j, gi:gi + 1, :], pltpu.roll(x, 1, axis=0))
            carry_scr[j, gi:gi + 1, :] = x[C - 1:C, :]
            out.append(x + mu_ref[...] * (prev - x))
        return out

    pm_r = shift_mix(0, pr_ref, mur_ref)
    pm_k = shift_mix(1, pk_ref, muk_ref)
    pm_v = shift_mix(2, pv_ref, muv_ref)
    xl = jnp.concatenate(shift_mix(3, pl_ref, mul_ref), axis=0)
    x_wa = xl[:, 0:128]
    lora_in = jnp.where(_iota(x_wa.shape, 1) < 64, jnp.tanh(x_wa), x_wa)
    w_log = -_softplus(-(w0_ref[...] + _mm(lora_in, w2_ref[...]))) - 0.5
    e_all = jnp.exp(w_log)
    a_all = _sigmoid(a0_ref[...] + _mm(lora_in, a2_ref[...]))
    g_all = _mm(_sigmoid(xl[:, 128:256]), g2_ref[...])
    valid = row_c < t_real

    def seq_rows(x, gi, keep_pad=False):
        x = x[gi * C:(gi + 1) * C]
        return x if (keep_pad or t_real >= C) else jnp.where(valid, x, 0.0)

    def pad0(x):
        return x if t_real >= C else jnp.where(valid, x, 0.0)

    lanes = lambda x, pp: x[:, pp * R_PAIR:(pp + 1) * R_PAIR]
    r, e, k, v, kk, a_l, g_l = [], [], [], [], [], [], []
    for gi, pp in probs:
        a_q = lanes(seq_rows(a_all, gi, True), pp)
        k_raw = lanes(pad0(pm_k[gi]), pp)
        r.append(lanes(pad0(pm_r[gi]), pp))
        e.append(lanes(seq_rows(e_all, gi), pp))
        k.append(k_raw * (1.0 + (a_q - 1.0) * lanes(ka_ref[...], pp)))
        v.append(lanes(pad0(pm_v[gi]), pp))
        kk.append(k_raw * lanes(kk_ref[...], pp))
        a_l.append(a_q)
        g_l.append(lanes(seq_rows(g_all, gi, True), pp))

    def par(ref, q):
        pp = probs[q][1]
        return ref[:, pp * R_PAIR:(pp + 1) * R_PAIR]

    lane = _iota((1, R_PAIR), 1)
    m0 = lane < R_HEAD
    ri = _iota((R_PAIR, R_PAIR), 0)
    ci = _iota((R_PAIR, R_PAIR), 1)
    bd = (ri < R_HEAD) == (ci < R_HEAD)
    rt_i = _iota((C2, C2), 0)
    ct_i = _iota((C2, C2), 1)
    same = (rt_i < C) == (ct_i < C)
    tr = jnp.where(rt_i < C, rt_i, rt_i - C)
    ts = jnp.where(ct_i < C, ct_i, ct_i - C)
    m_sl = same & (ts < tr)
    m_li = same & (ts <= tr)
    eye2 = jnp.where(rt_i == ct_i, 1.0, 0.0)
    lvl_masks = []
    m = 1
    while m < C:
        lvl_masks.append(((tr & ~(2 * m - 1)) == (ts & ~(2 * m - 1))) & ((tr & m) != 0) & ((ts & m) == 0))
        m *= 2
    tril = jnp.where(_iota((C, C), 1) <= _iota((C, C), 0), 1.0, 0.0).astype(_MXU)

    @pl.when(step == 0)
    def _():
        z = jnp.zeros((R_HEAD, R_HEAD), F32)
        for q, (gi, pp) in enumerate(probs):
            s_scr[q] = jnp.concatenate([jnp.concatenate([s0_ref[gi, 2 * pp], z], axis=1),
                                        jnp.concatenate([z, s0_ref[gi, 2 * pp + 1]], axis=1)], axis=0)

    def stack(x):
        return jnp.concatenate([jnp.where(m0, x, 0.0), jnp.where(m0, 0.0, x)], axis=0)

    def fold(xs):
        return xs[:C] + xs[C:]

    def each(f, *lists):
        return [f(*xs) for xs in zip(*lists)]

    def head_sum(x):
        s0 = jnp.sum(jnp.where(m0, x, 0.0), axis=-1, keepdims=True)
        s1 = jnp.sum(jnp.where(m0, 0.0, x), axis=-1, keepdims=True)
        return jnp.where(m0, s0, s1)

    an = each(lambda x: -x * lax.rsqrt(head_sum(x * x) + 1e-12), kk)
    bn = each(lambda n_, a_: -n_ * a_, an, a_l)
    cs = each(lambda x: _mm_sel_l(tril, x), e)
    cl = each(lambda x: x[C - 1:C, :], cs)
    at = each(lambda a_, e_, c_: a_ * jnp.exp(e_ - c_), an, e, cs)
    rt = each(lambda r_, c_: r_ * jnp.exp(-c_), r, cs)
    ecs = each(jnp.exp, cs)
    bt_ = each(lambda b_, x: b_ * x, bn, ecs)
    kt = each(lambda k_, x: k_ * x, k, ecs)
    ecl = each(lambda c_, l_: jnp.exp(c_ - l_), cs, cl)
    bh = each(lambda b_, x: b_ * x, bn, ecl)
    kh = each(lambda k_, x: k_ * x, k, ecl)
    As, Rs, Vs = each(stack, at), each(stack, rt), each(stack, v)
    b2 = each(lambda x: jnp.concatenate([x, x], axis=0), bt_)
    k2 = each(lambda x: jnp.concatenate([x, x], axis=0), kt)
    N = each(lambda a_, b_: jnp.where(m_sl, _mm_nt(a_, b_), 0.0), As, b2)
    Ak = each(lambda a_, b_: jnp.where(m_sl, _mm_nt(a_, b_), 0.0), As, k2)
    Arb = each(lambda a_, b_: jnp.where(m_li, _mm_nt(a_, b_), 0.0), Rs, b2)
    Ark = each(lambda a_, b_: jnp.where(m_li, _mm_nt(a_, b_), 0.0), Rs, k2)
    T = each(lambda n_: eye2 + jnp.where(lvl_masks[0], n_, 0.0), N)
    for lm in lvl_masks[1:]:
        LT = each(lambda n_, t_: _mm(jnp.where(lm, n_, 0.0), t_), N, T)
        T = each(lambda t_, x: t_ + _mm(t_, x), T, LT)
    AkV = each(_mm, Ak, Vs)
    X = each(lambda t_, a_, u_: _mm(t_, jnp.concatenate([a_, u_], axis=1)), T, As, AkV)
    Z = each(_mm, Arb, X)
    ArkV = each(_mm, Ark, Vs)
    Rp = each(lambda r_, z_: r_ + fold(z_[:, 0:R_PAIR]), rt, Z)
    Y0 = each(lambda z_, a_: fold(z_[:, R_PAIR:] + a_), Z, ArkV)
    Ap = each(lambda x: fold(x[:, 0:R_PAIR]), X)
    U0 = each(lambda x: fold(x[:, R_PAIR:]), X)
    P = each(lambda a_, b_: jnp.where(bd, _mm_tn(a_, b_), 0.0), Ap, bh)
    Q = each(lambda u_, v_, b_, k_: jnp.where(bd, _mm_tn(jnp.concatenate([u_, v_], axis=0),
                                                         jnp.concatenate([b_, k_], axis=0)), 0.0), U0, v, bh, kh)
    S = [s_scr[q] for q in rng]
    y = each(lambda y_, r_, s_: y_ + _mm_nt(r_, s_), Y0, Rp, S)
    S = each(lambda s_, l_, p_, q_: s_ * jnp.exp(-l_) + _mm(s_, p_) + q_, S, cl, P, Q)
    for q in rng:
        s_scr[q] = S[q]
    dlt = each(lambda y_: y_ - head_sum(y_) * (1.0 / R_HEAD), y)
    var = each(lambda d_: head_sum(d_ * d_) * (1.0 / R_HEAD), dlt)
    bonus = each(lambda r_, k_, v_, q: head_sum(r_ * k_ * par(rk_ref, q)) * v_, r, k, v, rng)
    for q, (gi, pp) in enumerate(probs):
        yn = dlt[q] * lax.rsqrt(var[q] + R_LN_EPS) * par(lnw_ref, q) + par(lnb_ref, q)
        y_ref[gi, :, pp * R_PAIR:(pp + 1) * R_PAIR] = (yn + bonus[q]) * g_l[q]

    @pl.when(step == n_steps - 1)
    def _():
        for q, (gi, pp) in enumerate(probs):
            sfin_ref[gi, 2 * pp] = S[q][0:R_HEAD, 0:R_HEAD]
            sfin_ref[gi, 2 * pp + 1] = S[q][R_HEAD:, R_HEAD:]


def _stacked_out(kernel_fn, in_specs, args, out_index, stack):
    pos = len(args)

    def with_alias(*refs):
        return kernel_fn(*refs[:pos], *refs[pos + 1:])

    return with_alias, in_specs + [pl.BlockSpec(memory_space=pl.ANY)], args + [stack], {pos: out_index}


def rwkv_recurrence(p3, shift_prev, mu, wa, g2, w0, a0, k_k, k_a, s0, l, r_k, ln_w, ln_b, stack, *, C, G, NP,
                    t_real):
    B, T, _ = p3.shape
    n_steps = T // C
    n_pairs = D // R_PAIR
    W = NP * R_PAIR
    assert W == 256 and t_real <= C and (t_real == C or n_steps == 1)
    seg = D // W
    cblk = lambda off: pl.BlockSpec((G, C, W), lambda b, p, s: (b, s, off + p))
    sblk = lambda off: pl.BlockSpec((G, W), lambda b, p, s: (b, off + p))
    mblk = lambda off: pl.BlockSpec((1, W), lambda b, p, s: (0, off + p))
    lora_c = pl.BlockSpec((G, C, W), lambda b, p, s: (b, s, 3 * seg))
    lora_s = pl.BlockSpec((G, W), lambda b, p, s: (b, 3 * seg))
    lora_m = pl.BlockSpec((1, W), lambda b, p, s: (0, 3 * seg))
    wblk = lambda off: pl.BlockSpec((128, W), lambda b, p, s: (0, off + p))
    par = lambda: pl.BlockSpec((1, W), lambda b, p, s: (0, p))
    out_blk = pl.BlockSpec((G, C, W), lambda b, p, s: (b, s, p))
    st_spec = lambda: pl.BlockSpec((None, G, 2 * NP, R_HEAD, R_HEAD), lambda b, p, s: (l, b, p, 0, 0))
    in_specs = ([cblk(0), cblk(seg), cblk(2 * seg), lora_c,
                 sblk(0), sblk(seg), sblk(2 * seg), lora_s,
                 mblk(0), mblk(seg), mblk(2 * seg), lora_m,
                 wblk(0), wblk(seg), wblk(0)]
                + [par() for _ in range(4)] + [st_spec(), par(), par(), par()])
    kern, in_specs, args, aliases = _stacked_out(
        functools.partial(_rwkv_rec_kernel, C=C, G=G, NP=NP, n_steps=n_steps, t_real=t_real), in_specs,
        [p3, p3, p3, p3, shift_prev, shift_prev, shift_prev, shift_prev, mu, mu, mu, mu, wa, wa, g2,
         w0, a0, k_k, k_a, s0, r_k, ln_w, ln_b], 1, stack)
    y, sfin = pl.pallas_call(
        kern,
        out_shape=[jax.ShapeDtypeStruct((B, T, D), F32),
                   jax.ShapeDtypeStruct((s0.shape[0], B, 2 * n_pairs, R_HEAD, R_HEAD), F32)],
        grid=(B // G, n_pairs // NP, n_steps),
        in_specs=in_specs,
        out_specs=[out_blk, st_spec()],
        scratch_shapes=[pltpu.VMEM((G * NP, R_PAIR, R_PAIR), F32), pltpu.VMEM((4, G, W), F32)],
        input_output_aliases=aliases,
        compiler_params=_cparams(("parallel", "parallel", "arbitrary")),
        name="rwkv_rec",
    )(*args)
    return y, sfin


def _s5_kernel(u_ref, h0r_ref, h0i_ref, ar_ref, ai_ref, bre_ref, bim_ref, cre_ref, cim_ref, d_ref, *rest,
               tc, nb, kb, n_steps, permute):
    if permute:
        perm_ref, permt_ref = rest[:2]
        rest = rest[2:]
    y_ref, hr_out, hi_out, inr_scr, ini_scr, hr_scr, hi_scr = rest
    step = pl.program_id(1)
    rows = nb * tc

    @pl.when(step == 0)
    def _():
        hr_scr[...] = h0r_ref[...]
        hi_scr[...] = h0i_ref[...]

    u = u_ref[...].reshape(rows, kb * S5_BLK_CH)
    ch = lambda j: slice(j * S5_BLK_CH, (j + 1) * S5_BLK_CH)
    stt = lambda j: slice(j * S5_BLK_ST, (j + 1) * S5_BLK_ST)
    if permute:
        ut = jnp.dot(perm_ref[...], u.astype(_MXU), preferred_element_type=F32).astype(_MXU)
    else:
        ut = u.astype(_MXU)
    for j in range(kb):
        inr_scr[:, stt(j)] = jnp.dot(ut[:, ch(j)], bre_ref[j], preferred_element_type=F32)
        ini_scr[:, stt(j)] = jnp.dot(ut[:, ch(j)], bim_ref[j], preferred_element_type=F32)
    ar = ar_ref[...]
    ai = ai_ref[...]

    def body(t, carry):
        hr, hi = carry
        sl = pl.ds(pl.multiple_of(t * nb, nb), nb)
        nr = ar * hr - ai * hi + inr_scr[sl, :]
        ni = ar * hi + ai * hr + ini_scr[sl, :]
        inr_scr[sl, :] = nr
        ini_scr[sl, :] = ni
        return nr, ni

    hr, hi = lax.fori_loop(0, tc, body, (hr_scr[...], hi_scr[...]))
    hr_scr[...] = hr
    hi_scr[...] = hi
    yts = [jnp.dot(inr_scr[:, stt(j)].astype(_MXU), cre_ref[j], preferred_element_type=F32)
           - jnp.dot(ini_scr[:, stt(j)].astype(_MXU), cim_ref[j], preferred_element_type=F32) for j in range(kb)]
    yt = yts[0] if kb == 1 else jnp.concatenate(yts, axis=1)
    if permute:
        yt_hi = yt.astype(_MXU)
        yt_lo = (yt - yt_hi.astype(F32)).astype(_MXU)
        yt = (jnp.dot(permt_ref[...], yt_hi, preferred_element_type=F32)
              + jnp.dot(permt_ref[...], yt_lo, preferred_element_type=F32))
    y = yt + d_ref[...] * u
    y = 0.5 * y * (1.0 + jnp.tanh(math.sqrt(2.0 / math.pi) * (y + 0.044715 * (y * y * y))))
    y_ref[...] = y.reshape(y_ref.shape)

    @pl.when(step == n_steps - 1)
    def _():
        hr_out[...] = hr
        hi_out[...] = hi


def _time_major_perm(nb, tc):
    rows = nb * tc
    j = np.arange(rows)
    perm_np = np.zeros((rows, rows), np.float32)
    perm_np[(j % tc) * nb + j // tc, j] = 1.0
    return perm_np


def s5_scan(u3, h0r, h0i, l, ar, ai, bre, bim, cre, cim, d, *, nb, tc, blk, kb):
    A, R, _ = u3.shape
    ga, tr = blk
    assert ga * tr == nb * tc
    tmajor = (ga, tr) == (tc, nb) and R == nb
    nblk = D // S5_BLK_CH
    n_steps = A // ga if tmajor else R // tr
    rows = nb * tc
    u_spec = (pl.BlockSpec((ga, tr, kb * S5_BLK_CH), lambda c, s: (s, 0, c)) if tmajor
              else pl.BlockSpec((ga, tr, kb * S5_BLK_CH), lambda c, s: (0, s, c)))
    if tmajor:
        perm_specs, perm_args = [], []
    else:
        assert A == ga
        perm_np = _time_major_perm(nb, tc)
        perm_specs = [_const_spec((rows, rows)), _const_spec((rows, rows))]
        perm_args = [jnp.asarray(perm_np, _MXU), jnp.asarray(perm_np.T, _MXU)]
    st = lambda: pl.BlockSpec((None, nb, kb * S5_BLK_ST), lambda c, s: (l, 0, c))
    vec = lambda w: pl.BlockSpec((1, kb * w), lambda c, s: (0, c))
    mat = lambda a, b: pl.BlockSpec((kb, a, b), lambda c, s: (c, 0, 0))
    return pl.pallas_call(
        functools.partial(_s5_kernel, tc=tc, nb=nb, kb=kb, n_steps=n_steps, permute=not tmajor),
        out_shape=[jax.ShapeDtypeStruct(u3.shape, F32),
                   jax.ShapeDtypeStruct((nb, nblk * S5_BLK_ST), F32),
                   jax.ShapeDtypeStruct((nb, nblk * S5_BLK_ST), F32)],
        grid=(nblk // kb, n_steps),
        in_specs=[u_spec,
                  st(), st(), vec(S5_BLK_ST), vec(S5_BLK_ST),
                  mat(S5_BLK_CH, S5_BLK_ST), mat(S5_BLK_CH, S5_BLK_ST),
                  mat(S5_BLK_ST, S5_BLK_CH), mat(S5_BLK_ST, S5_BLK_CH),
                  vec(S5_BLK_CH)] + perm_specs,
        out_specs=[u_spec,
                   pl.BlockSpec((nb, kb * S5_BLK_ST), lambda c, s: (0, c)),
                   pl.BlockSpec((nb, kb * S5_BLK_ST), lambda c, s: (0, c))],
        scratch_shapes=[pltpu.VMEM((rows, kb * S5_BLK_ST), F32), pltpu.VMEM((rows, kb * S5_BLK_ST), F32),
                        pltpu.VMEM((nb, kb * S5_BLK_ST), F32), pltpu.VMEM((nb, kb * S5_BLK_ST), F32)],
        compiler_params=_cparams(("parallel", "arbitrary")),
        name="s5_scan",
    )(u3, h0r, h0i, ar, ai, bre, bim, cre, cim, d, *perm_args)


def _s5_discretize(a_re, a_im, log_dt, b_re, b_im, c_re, c_im):
    g, p, hch = b_re.shape
    dt = jnp.exp(log_dt.astype(F32))[:, None]
    mag = jnp.exp(dt * a_re)
    abar_re = mag * jnp.cos(dt * a_im)
    abar_im = mag * jnp.sin(dt * a_im)
    den = a_re * a_re + a_im * a_im
    nr = abar_re - 1.0
    q_re = (nr * a_re + abar_im * a_im) / den
    q_im = (abar_im * a_re - nr * a_im) / den
    bb_re = q_re[..., None] * b_re - q_im[..., None] * b_im
    bb_im = q_re[..., None] * b_im + q_im[..., None] * b_re
    nblk = D // S5_BLK_CH
    gl = g // nblk
    eye = jnp.eye(gl, dtype=F32)

    def in_blocks(bb):
        t = jnp.transpose(bb, (0, 2, 1)).reshape(nblk, gl, hch, p)
        return jnp.einsum('cghp,gk->cghkp', t, eye).reshape(nblk, gl * hch, gl * p).astype(_MXU)

    def out_blocks(cc):
        t = jnp.transpose(cc, (0, 2, 1)).reshape(nblk, gl, p, hch)
        return jnp.einsum('cgph,gk->cgpkh', t, eye).reshape(nblk, gl * p, gl * hch).astype(_MXU)

    return (abar_re.reshape(1, g * p), abar_im.reshape(1, g * p), in_blocks(bb_re), in_blocks(bb_im),
            out_blocks(c_re), out_blocks(c_im))


def _ssd_kernel(xbc_ref, z_ref, dt_ref, cprev_ref, h0_ref, cw_ref, cb_ref, dtb_ref, alog_ref, dx_ref, ng_ref,
                eh_ref, y_ref, hfin_ref, ext_scr, h_scr, *, L, G, t_real, n_steps):
    step = pl.program_id(1)
    rng = range(G)
    NB = M_GROUPS * M_STATE

    def each(f, *lists):
        return [f(*xs) for xs in zip(*lists)]

    @pl.when(step == 0)
    def _():
        for s in rng:
            ext_scr[s] = jnp.zeros((8, M_CONV_DIM), F32)
            ext_scr[s, 5:8, :] = cprev_ref[s]
            for gi in range(M_GROUPS):
                h_scr[s, gi] = h0_ref[s, gi * M_HPG:(gi + 1) * M_HPG].reshape(M_GW, M_STATE).T

    tril = jnp.where(_iota((L, L), 1) <= _iota((L, L), 0), 1.0, 0.0).astype(_MXU)
    causal = _iota((L, L), 1) <= _iota((L, L), 0)
    eye_h = _eye(128, _MXU)
    eh = eh_ref[...]
    lane_head = _iota((1, M_GW), 1) // M_HEAD
    row8 = _iota((8, 1), 0)
    a = -jnp.exp(alog_ref[...])
    conv, dts = [], []
    for s in rng:
        x = xbc_ref[s * L:(s + 1) * L, :]
        prev8 = ext_scr[s]
        c = cb_ref[...] + cw_ref[3:4, :] * x
        for j in range(1, 4):
            xr = pltpu.roll(x, j, axis=0)
            head = jnp.where(row8 < j, pltpu.roll(prev8, j, axis=0), xr[0:8])
            xr = head if L == 8 else jnp.concatenate([head, xr[8:]], axis=0)
            c = c + cw_ref[3 - j:4 - j, :] * xr
        ext_scr[s] = x[L - 8:L]
        conv.append(c * _sigmoid(c))
        dt = _softplus(dt_ref[s * L:(s + 1) * L, :] + dtb_ref[...])
        if t_real < L:
            dt = jnp.where(_iota(dt.shape, 0) < t_real, dt, 0.0)
        dts.append(dt)
    xs = each(lambda c: c[:, 0:M_WIDTH], conv)
    acum = each(lambda d_: _mm_sel_l(tril, d_ * a), dts)
    acum_t = each(lambda x: _mm_nt_sel_l(eye_h, x), acum)
    dt_x = each(lambda d_: _mm_sel_r(d_, eh), dts)
    acum_x = each(lambda x: _mm_sel_r(x, eh), acum)
    acl_x = each(lambda x: x[L - 1:L, :], acum_x)
    xd = each(lambda x, d_: x * d_, xs, dt_x)
    xdd = each(lambda x, l_, c_: x * jnp.exp(l_ - c_), xd, acl_x, acum_x)
    eacum_x = each(jnp.exp, acum_x)
    cdec_x = each(jnp.exp, acl_x)
    ys = [[] for _ in rng]
    for gi in range(M_GROUPS):
        gs = slice(gi * M_GW, (gi + 1) * M_GW)
        bg = each(lambda c: c[:, M_WIDTH + gi * M_STATE:M_WIDTH + (gi + 1) * M_STATE], conv)
        cg = each(lambda c: c[:, M_WIDTH + NB + gi * M_STATE:M_WIDTH + NB + (gi + 1) * M_STATE], conv)
        h = [h_scr[s, gi] for s in rng]
        cbm = each(_mm_nt, cg, bg)
        bgt = each(lambda b_: _mm_nt(eye_h, b_), bg)
        y_off = each(lambda c_, h_, e_: _mm(c_, h_) * e_[:, gs], cg, h, eacum_x)
        ms = []
        for e in range(M_HPG):
            he = gi * M_HPG + e
            ms.append(each(lambda c_, a_, t_: c_ * jnp.exp(jnp.where(causal, a_[:, he:he + 1] - t_[he:he + 1, :],
                                                                     -jnp.inf)), cbm, acum, acum_t))
        if L % 128 == 0:
            mcat = [jnp.concatenate([ms[e][s] for e in range(M_HPG)], axis=1) for s in rng]
            xst = each(lambda x: jnp.concatenate([jnp.where(lane_head == e, x[:, gs], 0.0).astype(_MXU)
                                                  for e in range(M_HPG)], axis=0), xd)
            y_dg = each(_mm, mcat, xst)
        else:
            y_dg = [sum(_mm(ms[e][s], jnp.where(lane_head == e, xd[s][:, gs], 0.0)) for e in range(M_HPG))
                    for s in rng]
        hn = each(lambda h_, d_, b_, x: h_ * d_[:, gs] + _mm(b_, x[:, gs]), h, cdec_x, bgt, xdd)
        for s in rng:
            h_scr[s, gi] = hn[s]
            ys[s].append(y_off[s] + y_dg[s])
    for s in rng:
        y = jnp.concatenate(ys[s], axis=1) + dx_ref[...] * xs[s]
        zz = z_ref[s * L:(s + 1) * L, :]
        y = y * (zz * _sigmoid(zz))
        outs = []
        for gi in range(M_GROUPS):
            yg = y[:, gi * M_GW:(gi + 1) * M_GW]
            outs.append(yg * lax.rsqrt(jnp.mean(yg * yg, axis=-1, keepdims=True) + EPS))
        y_ref[s * L:(s + 1) * L, :] = jnp.concatenate(outs, axis=1) * ng_ref[...]

    @pl.when(step == n_steps - 1)
    def _():
        for s in rng:
            for gi in range(M_GROUPS):
                hfin_ref[s, gi * M_HPG:(gi + 1) * M_HPG] = h_scr[s, gi].T.reshape(M_HPG, M_HEAD, M_STATE)


def ssd_block(xbc, zdt, conv_prev, h0, l, conv_w, conv_b, dt_bias, a_log, d_x, norm_g, eh, stack, *, B, T, L, G,
              t_real):
    n_steps = T // L
    assert G == 1 or n_steps == 1
    heads = M_GROUPS * M_HPG
    seq = lambda w: pl.BlockSpec((G * L, w), lambda b, s: (b * n_steps + s, 0))
    st_spec = lambda: pl.BlockSpec((None, G, heads, M_HEAD, M_STATE), lambda b, s: (l, b, 0, 0, 0))
    dt_spec = pl.BlockSpec((G * L, 128), lambda b, s: (b * n_steps + s, M_WIDTH // 128))
    in_specs = [seq(M_CONV_DIM), seq(M_WIDTH), dt_spec,
                pl.BlockSpec((None, G, 3, M_CONV_DIM), lambda b, s: (l, b, 0, 0)),
                st_spec(),
                _const_spec((4, M_CONV_DIM)), _const_spec((1, M_CONV_DIM)),
                _const_spec((1, 128)), _const_spec((1, 128)),
                _const_spec((1, M_WIDTH)), _const_spec((1, M_WIDTH)),
                _const_spec((128, M_WIDTH))]
    kern, in_specs, args, aliases = _stacked_out(
        functools.partial(_ssd_kernel, L=L, G=G, t_real=t_real, n_steps=n_steps), in_specs,
        [xbc, zdt, zdt, conv_prev, h0, conv_w, conv_b, dt_bias, a_log, d_x, norm_g, eh], 1, stack)
    return pl.pallas_call(
        kern,
        out_shape=[jax.ShapeDtypeStruct((B * T, M_WIDTH), F32),
                   jax.ShapeDtypeStruct((h0.shape[0], B, heads, M_HEAD, M_STATE), F32)],
        grid=(B // G, n_steps),
        in_specs=in_specs,
        out_specs=[seq(M_WIDTH), st_spec()],
        scratch_shapes=[pltpu.VMEM((G, 8, M_CONV_DIM), F32),
                        pltpu.VMEM((G, M_GROUPS, M_STATE, M_GW), F32)],
        input_output_aliases=aliases,
        compiler_params=_cparams(("parallel", "arbitrary")),
        name="ssd",
    )(*args)


def _attn_kernel(q_ref, k_ref, v_ref, o_ref, *, nb, tq, heads_split):
    scale = XA_HD ** -0.5
    for j in range(nb):
        if heads_split:
            k_all = pltpu.einshape("mhd->hmd", k_ref[j])
            v_all = pltpu.einshape("mhd->hmd", v_ref[j])
        outs = []
        for hd in range(XA_HEADS):
            cs = slice(hd * XA_HD, (hd + 1) * XA_HD)
            q = q_ref[j * tq:(j + 1) * tq, cs]
            kh = k_all[hd] if heads_split else k_ref[j, :, cs]
            vh = v_all[hd] if heads_split else v_ref[j, :, cs]
            s = _mm_nt(q, kh) * scale
            s = s - jnp.max(s, axis=-1, keepdims=True)
            p = jnp.exp(s)
            p = p / jnp.sum(p, axis=-1, keepdims=True)
            outs.append(_mm(p, vh))
        o_ref[j * tq:(j + 1) * tq, :] = jnp.concatenate(outs, axis=1)


def cross_attention(q, mk, mv, kv_index, *, B, T, tq, nb):
    nlead = len(kv_index)
    heads_split = mk.ndim - nlead == 4
    kv_blk = mk.shape[nlead + 1:]
    n_t = T // tq
    kv_spec = pl.BlockSpec((None,) * nlead + (nb,) + kv_blk,
                           lambda b, s: tuple(kv_index) + (b,) + (0,) * len(kv_blk))
    q_spec = pl.BlockSpec((nb * tq, D), lambda b, s: (b * n_t + s, 0))
    return pl.pallas_call(
        functools.partial(_attn_kernel, nb=nb, tq=tq, heads_split=heads_split),
        out_shape=jax.ShapeDtypeStruct((B * T, D), F32),
        grid=(B // nb, n_t),
        in_specs=[q_spec, kv_spec, kv_spec],
        out_specs=q_spec,
        compiler_params=_cparams(("parallel", "arbitrary")),
        name="xattn",
    )(q, mk, mv)


def _xattn_block_kernel(h_ref, gq_ref, wq_ref, k_ref, v_ref, wo_ref, go_ref, o_ref):
    h = h_ref[...]
    q = _mm(_rms(h, gq_ref[...]), wq_ref[...])
    scale = XA_HD ** -0.5
    outs = []
    for hd in range(XA_HEADS):
        cs = slice(hd * XA_HD, (hd + 1) * XA_HD)
        s = _mm_nt(q[:, cs], k_ref[0, :, cs]) * scale
        s = s - jnp.max(s, axis=-1, keepdims=True)
        p = jnp.exp(s)
        p = p / jnp.sum(p, axis=-1, keepdims=True)
        outs.append(_mm(p, v_ref[0, :, cs]))
    xa = _mm(jnp.concatenate(outs, axis=1), wo_ref[...])
    o_ref[...] = h + _rms(xa, go_ref[...])


def cross_attention_block(h, gq, wq, mk, mv, wo, go, *, B, T, tq):
    m = mk.shape[1]
    n_t = T // tq
    row = pl.BlockSpec((tq, D), lambda b, s: (b * n_t + s, 0))
    kv = pl.BlockSpec((1, m, D), lambda b, s: (b, 0, 0))
    return pl.pallas_call(
        _xattn_block_kernel,
        out_shape=jax.ShapeDtypeStruct((B * T, D), F32),
        grid=(B, n_t),
        in_specs=[row, _const_spec((1, D)), _const_spec((D, D)), kv, kv, _const_spec((D, D)), _const_spec((1, D))],
        out_specs=row,
        compiler_params=_cparams(("parallel", "arbitrary")),
        name="xattn_block",
    )(h, gq, wq, mk, mv, wo, go)


def _pad_time(x2, B, T, Tp):
    if Tp == T:
        return x2
    w = x2.shape[-1]
    return jnp.pad(x2.reshape(B, T, w), ((0, 0), (0, Tp - T), (0, 0))).reshape(B * Tp, w)


def _layer(l, h, grp, W, st, acc):
    T, B = grp['T'], grp['B']
    n = T * B
    gates = norm_matmul(h, W['g_mix_pre'], W['w_gates'], name="in_gates")
    p_r = norm_matmul(h, W['g_mix_pre'], W['w_rwkv'], name="in_rwkv")
    if grp['s5_tmajor']:
        u3 = norm_matmul_tmajor(h.reshape(B, T, D), W['g_mix_pre'], W['w_s5'], name="in_s5")
    else:
        u3 = norm_matmul(h, W['g_mix_pre'], W['w_s5'], name="in_s5").reshape(1, n, D)
    zdt = norm_matmul(h, W['g_mix_pre'], W['w_zdt'], tn=M_WIDTH + 128, name="in_zdt")
    xbc = norm_matmul(h, W['g_mix_pre'], W['w_xbc'], name="in_xbc")

    Tp = grp['rwkv_Tpad']
    p3 = _pad_time(p_r, B, T, Tp).reshape(B, Tp, p_r.shape[-1])
    yr, s_fin = rwkv_recurrence(p3, st['shift'][l], W['mu'], W['wa'], W['g2'], W['w0'], W['a0'], W['k_k'],
                                W['k_a'], st['rwkv'], l, W['r_k'], W['ln_w'], W['ln_b'], acc[0],
                                C=grp['rwkv_C'], G=grp['rwkv_G'], NP=grp['rwkv_NP'], t_real=min(T, grp['rwkv_C']))
    yr = yr[:, :T].reshape(n, D)
    shift_new = p_r.reshape(B, T, -1)[:, T - 1]

    ys, s5r, s5i = s5_scan(u3, st['s5r'], st['s5i'], l, W['s5_ar'], W['s5_ai'], W['s5_bre'], W['s5_bim'],
                           W['s5_cre'], W['s5_cim'], W['s5_d'], nb=B, tc=grp['s5_tc'], blk=grp['s5_blk'],
                           kb=2 if B <= 16 else 1)

    Lc = grp['ssd_L']
    Tm = grp['ssd_Tpad']
    ym, ssm_fin = ssd_block(_pad_time(xbc, B, T, Tm), _pad_time(zdt, B, T, Tm), st['conv'], st['ssm'], l,
                            W['conv_w'], W['conv_b'], W['dt_bias'], W['a_log'], W['d_x'], W['m_norm'], W['eh'], acc[1], B=B, T=Tm, L=Lc, G=grp['ssd_G'],
                            t_real=min(T, Lc))
    ym = ym.reshape(B, Tm, M_WIDTH)[:, :T].reshape(n, M_WIDTH)
    conv_new = jnp.concatenate([st['conv'][l], xbc.reshape(B, T, M_CONV_DIM)[:, max(T - 3, 0):]], axis=1)[:, -3:]

    h = mixer_merge(gates, yr, ys if grp['s5_tmajor'] else ys.reshape(n, D), ym, h, W['w_out_rwkv'],
                    W['s5_w_glu'], W['w_out_s5'], W['w_out_mamba'], W['w_out'], W['g_mix_post'],
                    seq=(B, T) if grp['s5_tmajor'] else None)

    if grp['xa_nb'] == 1 and st['mk'].ndim == 3:
        h = cross_attention_block(h, W['g_xa_pre'], W['xa_wq'], st['mk'], st['mv'], W['xa_wo'], W['g_xa_post'],
                                  B=B, T=T, tq=grp['xa_tq'])
    else:
        q = norm_matmul(h, W['g_xa_pre'], W['xa_wq'], name="xa_q")
        o = cross_attention(q, st['mk'], st['mv'], st['kv_index'](l), B=B, T=T, tq=grp['xa_tq'],
                            nb=grp['xa_nb'])
        h = matmul_norm_residual(o, W['xa_wo'], W['g_xa_post'], h, name="xa_out")

    h = mlp_block(h, W['g_mlp_pre'], W['mlp_w1'], W['mlp_w2'], W['g_mlp_post'])
    return h, (s_fin, shift_new, s5r.reshape(B, 64, 64), s5i.reshape(B, 64, 64), conv_new, ssm_fin)


def _group_cfg(T, B):
    cfg = dict(T=T, B=B)
    g = 8 if B % 8 == 0 else 1
    if T % 64 == 0:
        cfg.update(rwkv_C=64, rwkv_Tpad=T, rwkv_G=g, rwkv_NP=2)
    else:
        tp = -(-T // 8) * 8
        cfg.update(rwkv_C=tp, rwkv_Tpad=tp, rwkv_G=16 if B % 16 == 0 else g, rwkv_NP=2)
    if T % 8 == 0 and (1024 // B) >= 8 and T % (1024 // B) == 0:
        tc = 1024 // B
        cfg.update(s5_tmajor=True, s5_tc=tc, s5_blk=(tc, B))
    else:
        cfg.update(s5_tmajor=False, s5_tc=T, s5_blk=(1, B * T))
    if T % 128 == 0:
        cfg.update(ssd_L=128, ssd_Tpad=T, ssd_G=1)
    else:
        tp = -(-T // 8) * 8
        cfg.update(ssd_L=tp, ssd_Tpad=tp, ssd_G=4 if B % 4 == 0 else 1)
    if T >= 64:
        cfg.update(xa_tq=_row_tile(T, 512), xa_nb=1)
    else:
        cfg.update(xa_tq=T, xa_nb=4 if B % 4 == 0 else 1)
    return cfg


def kernel(x_prompt, x_sample, cache_mem_k, cache_mem_v, state_rwkv, state_rwkv_shift, state_s5_re, state_s5_im, state_conv, state_ssm, mem_prompt, norm_mix_pre, norm_mix_post, norm_xa_pre, norm_xa_post, norm_mlp_pre, norm_mlp_post, norm_mem, w_in, w_out, rwkv_mu, rwkv_w0, rwkv_w2, rwkv_a0, rwkv_a2, rwkv_g2, rwkv_k_k, rwkv_k_a, rwkv_r_k, rwkv_ln_w, rwkv_ln_b, w_out_rwkv, s5_a_re, s5_a_im, s5_log_dt, s5_b_re, s5_b_im, s5_c_re, s5_c_im, s5_d, s5_w_glu, w_out_s5, m_conv_w, m_conv_b, m_dt_bias, m_a_log, m_d, m_norm, w_out_mamba, xa_wq, xa_wk, xa_wv, xa_wo, mlp_w1, mlp_w2):
    depth = w_in.shape[0]
    bp, tp, _ = x_prompt.shape
    bs, ts, _ = x_sample.shape
    mlen = mem_prompt.shape[1]
    bf = lambda x: x.astype(_MXU)
    row = lambda x: x.reshape(1, -1).astype(F32)

    eh = (jnp.arange(128)[:, None] == (jnp.arange(M_WIDTH)[None, :] // M_HEAD)).astype(_MXU)

    col = [0]
    for sz in (3 * D, 3 * D + 256, D, M_WIDTH, M_CONV_DIM, 32):
        col.append(col[-1] + sz)
    layers = []
    w_in_b = bf(w_in)
    big = {name: bf(w) for name, w in dict(
        w_out_rwkv=w_out_rwkv, s5_w_glu=s5_w_glu, w_out_s5=w_out_s5, w_out_mamba=w_out_mamba, w_out=w_out,
        xa_wq=xa_wq, xa_wk=xa_wk, xa_wv=xa_wv, xa_wo=xa_wo, mlp_w1=mlp_w1, mlp_w2=mlp_w2).items()}
    for l in range(depth):
        wl = w_in_b[l]
        ar, ai, bre, bim, cre, cim = _s5_discretize(s5_a_re[l], s5_a_im[l], s5_log_dt[l], s5_b_re[l], s5_b_im[l],
                                                    s5_c_re[l], s5_c_im[l])
        zero = jnp.zeros((64, D), F32)
        wa = jnp.concatenate([jnp.concatenate([rwkv_w2[l], zero], axis=1),
                              jnp.concatenate([zero, rwkv_a2[l]], axis=1)], axis=0)
        pad32 = lambda x: jnp.pad(x.reshape(1, -1).astype(F32), ((0, 0), (0, 128 - x.shape[-1])))
        layers.append(dict(
            g_mix_pre=row(norm_mix_pre[l]), g_mix_post=row(norm_mix_post[l]),
            g_xa_pre=row(norm_xa_pre[l]), g_xa_post=row(norm_xa_post[l]),
            g_mlp_pre=row(norm_mlp_pre[l]), g_mlp_post=row(norm_mlp_post[l]),
            w_gates=bf(wl[:, col[0]:col[1]]), w_rwkv=bf(wl[:, col[1]:col[2]]), w_s5=bf(wl[:, col[2]:col[3]]),
            w_xbc=bf(wl[:, col[4]:col[5]]),
            w_zdt=jnp.concatenate([wl[:, col[3]:col[4]], jnp.pad(wl[:, col[5]:col[6]], ((0, 0), (0, 96)))], axis=1),
            mu=row(rwkv_mu[l]), wa=bf(wa), w0=row(rwkv_w0[l]), a0=row(rwkv_a0[l]), g2=bf(rwkv_g2[l]),
            k_k=row(rwkv_k_k[l]), k_a=row(rwkv_k_a[l]), r_k=row(rwkv_r_k[l]),
            ln_w=row(rwkv_ln_w[l]), ln_b=row(rwkv_ln_b[l]),
            s5_ar=ar, s5_ai=ai, s5_bre=bre, s5_bim=bim, s5_cre=cre, s5_cim=cim, s5_d=row(s5_d[l]),
            conv_w=m_conv_w[l].astype(F32), conv_b=row(m_conv_b[l]), dt_bias=pad32(m_dt_bias[l]),
            a_log=pad32(m_a_log[l]), d_x=row(jnp.repeat(m_d[l], M_HEAD)), m_norm=row(m_norm[l]), eh=eh,
            g_mem=row(norm_mem[l]), **{name: w[l] for name, w in big.items()},
        ))

    hp = x_prompt.reshape(bp * tp, D).astype(F32)
    hs = x_sample.reshape(bs * ts, D).astype(F32)
    mem2 = mem_prompt.reshape(bp * mlen, D).astype(F32)

    cfg_p = _group_cfg(tp, bp)
    cfg_s = _group_cfg(ts, bs)
    zeros_p = dict(
        rwkv=jnp.zeros((depth, bp) + state_rwkv.shape[2:], F32),
        shift=jnp.zeros((depth, bp, state_rwkv_shift.shape[-1]), F32),
        s5r=jnp.zeros((depth, bp, state_s5_re.shape[2] * state_s5_re.shape[3]), F32),
        s5i=jnp.zeros((depth, bp, state_s5_re.shape[2] * state_s5_re.shape[3]), F32),
        conv=jnp.zeros((depth, bp) + state_conv.shape[2:], F32),
        ssm=jnp.zeros((depth, bp) + state_ssm.shape[2:], F32),
    )
    st_s = dict(
        rwkv=state_rwkv, shift=state_rwkv_shift,
        s5r=state_s5_re.reshape(depth, bs, -1), s5i=state_s5_im.reshape(depth, bs, -1),
        conv=state_conv, ssm=state_ssm,
        mk=cache_mem_k, mv=cache_mem_v,
        kv_index=lambda l: (l,),
    )

    mk_out, mv_out = [], []
    st_p_out = [[] for _ in range(6)]
    st_s_out = [[] for _ in range(6)]
    acc_p = (jnp.zeros_like(zeros_p['rwkv']), jnp.zeros_like(zeros_p['ssm']))
    acc_s = (jnp.zeros_like(state_rwkv, dtype=F32), jnp.zeros_like(state_ssm, dtype=F32))
    for l in range(depth):
        W = layers[l]
        mk = norm_matmul(mem2, W['g_mem'], W['xa_wk'], name="mem_k")
        mv = norm_matmul(mem2, W['g_mem'], W['xa_wv'], name="mem_v")
        st_p = dict(zeros_p, mk=mk.reshape(bp, mlen, D), mv=mv.reshape(bp, mlen, D), kv_index=lambda l: ())
        hp, new_p = _layer(l, hp, cfg_p, W, st_p, acc_p)
        hs, new_s = _layer(l, hs, cfg_s, W, st_s, acc_s)
        acc_p = (new_p[0], new_p[5])
        acc_s = (new_s[0], new_s[5])
        mk_out.append(mk.reshape(bp, mlen, XA_HEADS, XA_HD))
        mv_out.append(mv.reshape(bp, mlen, XA_HEADS, XA_HD))
        for i in range(1, 5):
            st_p_out[i].append(new_p[i])
            st_s_out[i].append(new_s[i])

    stk = lambda xs: jnp.stack(xs).astype(F32)
    y_p = hp.reshape(bp, tp, D).astype(x_prompt.dtype)
    y_s = hs.reshape(bs, ts, D).astype(x_sample.dtype)
    return (y_p, y_s, stk(mk_out), stk(mv_out),
            acc_p[0], *[stk(st_p_out[i]) for i in range(1, 5)], acc_p[1],
            acc_s[0], *[stk(st_s_out[i]) for i in range(1, 5)], acc_s[1])
```
